```python
import jax, jax.numpy as jnp
from jax import lax
import numpy as np

D_MODEL = 1024
BATCH = 32
SEQ = 256
DEPTH = 1
DEC_BATCH = 2
DEC_SEQ = 4096
PAST_LEN = 256

GRID_W = 64
D_RNN = 1024
RG_BLOCKS = 16
RG_BW = D_RNN // RG_BLOCKS
CONV_W = 4
RG_C = 8.0
ML_HEADS = 4
ML_DIM = 1024
ML_HD = ML_DIM // ML_HEADS
CHUNK = 64
D_FF = 4 * D_MODEL
ALPHA = (2 * DEPTH) ** 0.25
BETA = (8 * DEPTH) ** -0.25
LN_EPS = 1e-5
IN_SIZES = [D_RNN, D_RNN, ML_DIM, ML_DIM, ML_DIM, ML_DIM, 2 * D_MODEL, 2 * ML_HEADS, 2 * ML_HEADS]
D_IN = sum(IN_SIZES)

kernel_name = "bidir_rglru_mlstm_prefix_diffusion_step"


def layer_norm(x, g=None, b=None):
    xf = x.astype(jnp.float32)
    mu = jnp.mean(xf, -1, keepdims=True)
    var = jnp.mean(jnp.square(xf - mu), -1, keepdims=True)
    y = (xf - mu) * lax.rsqrt(var + LN_EPS)
    if g is not None:
        y = y * g.astype(jnp.float32) + b.astype(jnp.float32)
    return y.astype(x.dtype)


def centred_dwconv(x, w, b):
    left = CONV_W // 2
    right = CONV_W - 1 - left
    t = x.shape[-2]
    pad = [(0, 0)] * (x.ndim - 2) + [(left, right), (0, 0)]
    xp = jnp.pad(x, pad)
    return sum(w[k] * xp[..., k:k + t, :] for k in range(CONV_W)) + b


def block_diag_linear(x, w, b):
    xb = x.reshape(x.shape[:-1] + (RG_BLOCKS, RG_BW))
    y = jnp.einsum('btnk,nkj->btnj', xb, w)
    return y.reshape(x.shape) + b


def _linear_combine(left, right):
    a1, b1 = left
    a2, b2 = right
    return a1 * a2, a2 * b1 + b2


def rglru_scan(xc, w_a, b_a, w_x, b_x, lam, h0):
    f32 = jnp.float32
    r = jax.nn.sigmoid(block_diag_linear(xc, w_a, b_a).astype(f32))
    i = jax.nn.sigmoid(block_diag_linear(xc, w_x, b_x).astype(f32))
    log_a = RG_C * r * jax.nn.log_sigmoid(lam.astype(f32))
    a = jnp.exp(log_a)
    gain = jnp.sqrt(jnp.clip(-jnp.expm1(2.0 * log_a), 0.0, 1.0))
    bterm = gain * i * xc.astype(f32)
    bterm = bterm.at[:, 0].add(a[:, 0] * h0.astype(f32))
    _, h = lax.associative_scan(_linear_combine, (a, bterm), axis=1)
    return h, h[:, -1]


def mlstm_chunkwise(q, k, v, ig, lf, c0, n0, m0):
    bsz, nh, t, hd = q.shape
    nc = t // CHUNK

    def to_chunks(z):
        return jnp.moveaxis(z.reshape(z.shape[:2] + (nc, CHUNK) + z.shape[3:]), 2, 0)

    causal = jnp.tril(jnp.ones((CHUNK, CHUNK), dtype=bool))

    def step(carry, xs):
        cm, nv, m = carry
        qc, kc, vc, ic, fc = xs
        bcum = jnp.cumsum(fc, axis=-1)
        dmat = bcum[..., :, None] - bcum[..., None, :] + ic[..., None, :]
        dmat = jnp.where(causal, dmat, -jnp.inf)
        inter = bcum + m[..., None]
        m_row = jnp.maximum(jnp.max(dmat, -1), inter)
        wts = jnp.exp(dmat - m_row[..., None])
        w_inter = jnp.exp(inter - m_row)
        s = jnp.einsum('bhid,bhjd->bhij', qc, kc) * wts
        num = jnp.einsum('bhij,bhjd->bhid', s, vc) + w_inter[..., None] * jnp.einsum('bhid,bhde->bhie', qc, cm)
        den = jnp.sum(s, -1) + w_inter * jnp.einsum('bhid,bhd->bhi', qc, nv)
        h = num / jnp.maximum(jnp.abs(den), jnp.exp(-m_row))[..., None]
        btot = bcum[..., -1]
        wk_log = btot[..., None] - bcum + ic
        m_new = jnp.maximum(btot + m, jnp.max(wk_log, -1))
        wk = jnp.exp(wk_log - m_new[..., None])
        decay = jnp.exp(btot + m - m_new)
        c_new = decay[..., None, None] * cm + jnp.einsum('bhj,bhjd,bhje->bhde', wk, kc, vc)
        n_new = decay[..., None] * nv + jnp.einsum('bhj,bhjd->bhd', wk, kc)
        return (c_new, n_new, m_new), h

    (cf, nf, mf), hs = lax.scan(step, (c0, n0, m0), (to_chunks(q), to_chunks(k), to_chunks(v), to_chunks(ig), to_chunks(lf)))
    h = jnp.moveaxis(hs, 0, 2).reshape(bsz, nh, t, hd)
    return h, cf, nf, mf


def mixer(u, lw, h0, c0, n0, m0, rows):
    f32 = jnp.float32
    bsz, t, _ = u.shape
    proj = u @ lw['w_in']
    xr, zr, q, k, v, o, gm, ig, fg = jnp.split(proj, np.cumsum(IN_SIZES)[:-1].tolist(), axis=-1)

    if rows is None:
        xc = centred_dwconv(xr, lw['conv_w'], lw['conv_b'])
    else:
        xc = centred_dwconv(xr.reshape(bsz, rows, GRID_W, D_RNN), lw['conv_w'], lw['conv_b']).reshape(bsz, t, D_RNN)
    hf, hf_fin = rglru_scan(xc, lw['wa'][0], lw['ba'][0], lw['wx'][0], lw['bx'][0], lw['lam'][0], h0[:, 0])
    hb, hb_fin = rglru_scan(jnp.flip(xc, 1), lw['wa'][1], lw['ba'][1], lw['wx'][1], lw['bx'][1], lw['lam'][1], h0[:, 1])
    h_rg = (hf + jnp.flip(hb, 1)).astype(u.dtype) * jax.nn.gelu(zr)
    y_rg = h_rg @ lw['w_rg_proj']

    def heads(z):
        return z.reshape(bsz, t, ML_HEADS, ML_HD).transpose(0, 2, 1, 3).astype(f32)
    qh = heads(q) * (ML_HD ** -0.5)
    kh = heads(k)
    vh = heads(v)
    igp = (ig + lw['b_ig'].reshape(-1)).astype(f32).reshape(bsz, t, 2, ML_HEADS).transpose(2, 0, 3, 1)
    lfp = jax.nn.log_sigmoid((fg + lw['b_fg'].reshape(-1)).astype(f32)).reshape(bsz, t, 2, ML_HEADS).transpose(2, 0, 3, 1)
    f32s = lambda z: z.astype(f32)
    h_f, cf, nf, mf = mlstm_chunkwise(qh, kh, vh, igp[0], lfp[0], f32s(c0[:, 0]), f32s(n0[:, 0]), f32s(m0[:, 0]))
    h_b, cb, nb, mb = mlstm_chunkwise(jnp.flip(qh, 2), jnp.flip(kh, 2), jnp.flip(vh, 2),
                                      jnp.flip(igp[1], -1), jnp.flip(lfp[1], -1),
                                      f32s(c0[:, 1]), f32s(n0[:, 1]), f32s(m0[:, 1]))
    hsum = (h_f + jnp.flip(h_b, 2)).transpose(0, 2, 1, 3)
    h_ml = layer_norm(hsum).reshape(bsz, t, ML_DIM).astype(u.dtype) * lw['gn_g']
    y_ml = (jax.nn.sigmoid(o) * h_ml) @ lw['w_ml_proj']

    g_rg, g_ml = jnp.split(jax.nn.sigmoid(gm + lw['b_merge']), 2, axis=-1)
    out = (g_rg * y_rg + g_ml * y_ml) @ lw['w_out']
    h_fin = jnp.stack([hf_fin, hb_fin], axis=1)
    c_fin = jnp.stack([cf, cb], axis=1)
    n_fin = jnp.stack([nf, nb], axis=1)
    m_fin = jnp.stack([mf, mb], axis=1)
    return out, h_fin, c_fin, n_fin, m_fin


def trunk_layer(x, mod, lw, h0, c0, n0, m0, rows):
    shift1, scale1, gate1, shift2, scale2, gate2 = jnp.split(mod, 6, axis=-1)
    u = layer_norm(x) * (1.0 + scale1) + shift1
    mix, h_fin, c_fin, n_fin, m_fin = mixer(u, lw, h0, c0, n0, m0, rows)
    x = layer_norm(ALPHA * x + gate1 * mix, lw['ln_g'][0], lw['ln_b'][0])
    u = layer_norm(x) * (1.0 + scale2) + shift2
    hid = jnp.square(jax.nn.relu(u @ lw['w_fc'] + lw['b_fc']))
    x = layer_norm(ALPHA * x + gate2 * (hid @ lw['w_proj'] + lw['b_proj']), lw['ln_g'][1], lw['ln_b'][1])
    return x, h_fin, c_fin, n_fin, m_fin


def setup_inputs(seed: int = 0) -> dict:
    key = jax.random.key(seed)
    ks = iter(jax.random.split(key, 40))
    f32 = jnp.float32

    def nrm(shape, scale):
        return jax.random.normal(next(ks), shape, f32) * scale

    d = {}
    d['x_prompt'] = nrm((BATCH, SEQ, D_MODEL), 1.0)
    d['x_sample'] = nrm((DEC_BATCH, DEC_SEQ, D_MODEL), 1.0)
    d['state_rglru_h'] = nrm((DEC_BATCH, DEPTH, 2, D_RNN), 0.5)
    d['state_mlstm_C'] = nrm((DEC_BATCH, DEPTH, 2, ML_HEADS, ML_HD, ML_HD), 1.0)
    d['state_mlstm_n'] = nrm((DEC_BATCH, DEPTH, 2, ML_HEADS, ML_HD), 1.0)
    d['state_mlstm_m'] = nrm((DEC_BATCH, DEPTH, 2, ML_HEADS), 0.5)
    d['c'] = nrm((DEC_BATCH, D_MODEL), 1.0)
    d['c_ctx'] = nrm((D_MODEL,), 1.0)
    d['w_ada'] = nrm((DEPTH, D_MODEL, 6 * D_MODEL), 0.5 * D_MODEL ** -0.5)
    d['b_ada'] = nrm((DEPTH, 6 * D_MODEL), 0.02)
    col_scale = jnp.concatenate([jnp.ones((D_IN - 4 * ML_HEADS,), f32), jnp.full((4 * ML_HEADS,), 0.1, f32)])
    d['w_in'] = nrm((DEPTH, D_MODEL, D_IN), D_MODEL ** -0.5) * col_scale
    d['rg_conv_w'] = nrm((DEPTH, CONV_W, D_RNN), CONV_W ** -0.5)
    d['rg_conv_b'] = nrm((DEPTH, D_RNN), 0.02)
    d['rg_wa'] = nrm((DEPTH, 2, RG_BLOCKS, RG_BW, RG_BW), RG_BW ** -0.5)
    d['rg_ba'] = nrm((DEPTH, 2, D_RNN), 0.02)
    d['rg_wx'] = nrm((DEPTH, 2, RG_BLOCKS, RG_BW, RG_BW), RG_BW ** -0.5)
    d['rg_bx'] = nrm((DEPTH, 2, D_RNN), 0.02)
    a_pow = jax.random.uniform(next(ks), (DEPTH, 2, D_RNN), f32, 0.9, 0.999)
    a_base = a_pow ** (1.0 / RG_C)
    d['rg_lambda'] = jnp.log(a_base) - jnp.log1p(-a_base)
    d['w_rg_proj'] = nrm((DEPTH, D_RNN, D_MODEL), BETA * D_RNN ** -0.5)
    d['ml_b_igate'] = nrm((DEPTH, 2, ML_HEADS), 0.1)
    d['ml_b_fgate'] = jnp.broadcast_to(jnp.linspace(3.0, 6.0, ML_HEADS, dtype=f32), (DEPTH, 2, ML_HEADS)) + nrm((DEPTH, 2, ML_HEADS), 0.1)
    d['ml_gn_g'] = 1.0 + nrm((DEPTH, ML_DIM), 0.02)
    d['w_ml_proj'] = nrm((DEPTH, ML_DIM, D_MODEL), BETA * ML_DIM ** -0.5)
    d['b_merge'] = nrm((DEPTH, 2 * D_MODEL), 0.02)
    d['w_out'] = nrm((DEPTH, D_MODEL, D_MODEL), BETA * D_MODEL ** -0.5)
    d['ln_g'] = 1.0 + nrm((DEPTH, 2, D_MODEL), 0.02)
    d['ln_b'] = nrm((DEPTH, 2, D_MODEL), 0.02)
    d['w_fc'] = nrm((DEPTH, D_MODEL, D_FF), D_MODEL ** -0.5)
    d['b_fc'] = nrm((DEPTH, D_FF), 0.02)
    d['w_proj'] = nrm((DEPTH, D_FF, D_MODEL), BETA * D_FF ** -0.5)
    d['b_proj'] = nrm((DEPTH, D_MODEL), 0.02)
    return d


def reference(x_prompt, x_sample, state_rglru_h, state_mlstm_C, state_mlstm_n, state_mlstm_m, c, c_ctx,
              w_ada, b_ada, w_in, rg_conv_w, rg_conv_b, rg_wa, rg_ba, rg_wx, rg_bx, rg_lambda, w_rg_proj,
              ml_b_igate, ml_b_fgate, ml_gn_g, w_ml_proj, b_merge, w_out, ln_g, ln_b, w_fc, b_fc, w_proj, b_proj):
    f32 = jnp.float32
    bp = x_prompt.shape[0]
    rows = x_sample.shape[1] // GRID_W
    zero_h = jnp.zeros((bp, 2, D_RNN), f32)
    zero_c = jnp.zeros((bp, 2, ML_HEADS, ML_HD, ML_HD), f32)
    zero_n = jnp.zeros((bp, 2, ML_HEADS, ML_HD), f32)
    zero_m = jnp.zeros((bp, 2, ML_HEADS), f32)
    xp = x_prompt
    xs = x_sample
    new_h, new_c, new_n, new_m = [], [], [], []
    for l in range(DEPTH):
        lw = {'w_in': w_in[l], 'conv_w': rg_conv_w[l], 'conv_b': rg_conv_b[l], 'wa': rg_wa[l], 'ba': rg_ba[l],
              'wx': rg_wx[l], 'bx': rg_bx[l], 'lam': rg_lambda[l], 'w_rg_proj': w_rg_proj[l],
              'b_ig': ml_b_igate[l], 'b_fg': ml_b_fgate[l], 'gn_g': ml_gn_g[l], 'w_ml_proj': w_ml_proj[l],
              'b_merge': b_merge[l], 'w_out': w_out[l], 'ln_g': ln_g[l], 'ln_b': ln_b[l],
              'w_fc': w_fc[l], 'b_fc': b_fc[l], 'w_proj': w_proj[l], 'b_proj': b_proj[l]}
        mod_ctx = (jax.nn.silu(c_ctx) @ w_ada[l] + b_ada[l])[None, None, :]
        mod_lat = (jax.nn.silu(c) @ w_ada[l] + b_ada[l])[:, None, :]
        xp, h_fin, c_fin, n_fin, m_fin = trunk_layer(xp, mod_ctx, lw, zero_h, zero_c, zero_n, zero_m, None)
        new_h.append(h_fin)
        new_c.append(c_fin)
        new_n.append(n_fin)
        new_m.append(m_fin)
        xs, _, _, _, _ = trunk_layer(xs, mod_lat, lw, state_rglru_h[:, l], state_mlstm_C[:, l],
                                     state_mlstm_n[:, l], state_mlstm_m[:, l], rows)
    new_rglru_h = jnp.stack(new_h, axis=1).astype(x_prompt.dtype)
    new_mlstm_C = jnp.stack(new_c, axis=1).astype(x_prompt.dtype)
    new_mlstm_n = jnp.stack(new_n, axis=1).astype(x_prompt.dtype)
    new_mlstm_m = jnp.stack(new_m, axis=1).astype(x_prompt.dtype)
    return (xp, xs, new_rglru_h, new_mlstm_C, new_mlstm_n, new_mlstm_m)
```

```python
import functools

import jax
import jax.numpy as jnp
from jax import lax
from jax.experimental import pallas as pl
from jax.experimental.pallas import tpu as pltpu

F32 = jnp.float32
BF16 = jnp.bfloat16

LN_EPS = 1e-5
RG_C = 8.0
RG_BW = 64
CONV_W = 4
GRID_W = 64
ML_HEADS = 4
MLSTM_CHUNK = 256
SCAN_SEG = 256
RG_CT = 128
GATE_LANES = 128
VMEM_LIMIT = 52 * 1024 * 1024


def _cparams(n_axes):
    return pltpu.CompilerParams(
        dimension_semantics=("arbitrary",) * n_axes, vmem_limit_bytes=VMEM_LIMIT)


def _log_sigmoid(x):
    return jnp.minimum(x, 0.0) - jnp.log1p(jnp.exp(-jnp.abs(x)))


def _ln(x):
    mu = jnp.mean(x, -1, keepdims=True)
    xc = x - mu
    var = jnp.mean(xc * xc, -1, keepdims=True)
    return xc * lax.rsqrt(var + LN_EPS)


def _mod_row(i, row_base, tiles_per_row):
    if tiles_per_row is None:
        return row_base
    return row_base + lax.div(i, jnp.int32(tiles_per_row))


def _ada_kernel(ct_ref, w_ref, b_ref, o_ref, *, n_rows):
    w = w_ref[...]
    ct = ct_ref[...]
    s = ct * jax.nn.sigmoid(ct)
    o_ref[...] = jnp.zeros(o_ref.shape, F32)
    for r in range(n_rows):
        o_ref[r:r + 1, :] = jnp.sum(w * s[:, r:r + 1], axis=0, keepdims=True) + b_ref[...]


def _ada(cond, w_ada, b_ada):
    n_rows, d = cond.shape
    assert n_rows <= 8
    ct = jnp.zeros((d, 8), F32).at[:, :n_rows].set(cond.T)
    n_out = w_ada.shape[1]
    tn = 512
    return pl.pallas_call(
        functools.partial(_ada_kernel, n_rows=n_rows),
        grid=(n_out // tn,),
        in_specs=[pl.BlockSpec((d, 8), lambda j: (0, 0)),
                  pl.BlockSpec((d, tn), lambda j: (0, j)),
                  pl.BlockSpec((1, tn), lambda j: (0, j))],
        out_specs=pl.BlockSpec((8, tn), lambda j: (0, j)),
        out_shape=jax.ShapeDtypeStruct((8, n_out), F32),
        compiler_params=_cparams(1),
        name="ada_mod",
    )(ct, w_ada, b_ada.reshape(1, n_out))


def _inproj_kernel(x_ref, mod_ref, w_ref, wg_ref, bg_ref, of_ref, ob_ref, g_ref, gt_ref, u_ref,
                   *, d, tm, row_base, tiles_per_row):
    i = pl.program_id(0)
    j = pl.program_id(1)
    lc = MLSTM_CHUNK

    @pl.when(j == 0)
    def _():
        r = _mod_row(i, row_base, tiles_per_row)
        shift = mod_ref[pl.ds(r, 1), 0:d]
        scale = mod_ref[pl.ds(r, 1), d:2 * d]
        ub = (_ln(x_ref[...]) * (1.0 + scale) + shift).astype(BF16)
        u_ref[...] = ub
        g = jnp.dot(ub, wg_ref[...], preferred_element_type=F32) + bg_ref[...]
        lane = lax.broadcasted_iota(jnp.int32, g.shape, 1)
        g = jnp.where(lane >= 2 * ML_HEADS, _log_sigmoid(g), g)
        ii = lax.broadcasted_iota(jnp.int32, (lc, lc), 0)
        jj = lax.broadcasted_iota(jnp.int32, (lc, lc), 1)
        tril = (jj <= ii).astype(BF16)
        triu = (jj >= ii).astype(BF16)
        lane_c = lax.broadcasted_iota(jnp.int32, (lc, GATE_LANES), 1)
        for c in range(tm // lc):
            gc = g[c * lc:(c + 1) * lc]
            hi = gc.astype(BF16)
            r1 = gc - hi.astype(F32)
            mid = r1.astype(BF16)
            lo = (r1 - mid.astype(F32)).astype(BF16)
            cum_f = (jnp.dot(tril, hi, preferred_element_type=F32)
                     + jnp.dot(tril, mid, preferred_element_type=F32)
                     + jnp.dot(tril, lo, preferred_element_type=F32))
            cum_b = (jnp.dot(triu, hi, preferred_element_type=F32)
                     + jnp.dot(triu, mid, preferred_element_type=F32)
                     + jnp.dot(triu, lo, preferred_element_type=F32))
            gfin = jnp.where(lane_c < 2 * ML_HEADS, gc,
                             jnp.where(lane_c < 3 * ML_HEADS, cum_f, cum_b))
            g_ref[c * lc:(c + 1) * lc, :] = gfin
            gt_ref[:, c * lc:(c + 1) * lc] = gfin.T[0:4 * ML_HEADS, :]

    acc = jnp.dot(u_ref[...], w_ref[...], preferred_element_type=F32)
    is_bf = jnp.logical_and(j >= 2, j <= 4)

    @pl.when(is_bf)
    def _():
        ob_ref[...] = acc.astype(BF16)

    @pl.when(jnp.logical_not(is_bf))
    def _():
        of_ref[...] = acc


def _tiles_per_row(rows_per_mod, tm):
    if rows_per_mod is None:
        return None
    assert rows_per_mod % tm == 0
    return rows_per_mod // tm


def _in_proj(x, mod, w_main, w_gate, b_gate, *, row_base, rows_per_mod, tm):
    n, d = x.shape
    tiles_per_row = _tiles_per_row(rows_per_mod, tm)
    tn = d
    n_j = w_main.shape[1] // tn
    assert n_j == 8
    kern = functools.partial(_inproj_kernel, d=d, tm=tm, row_base=row_base, tiles_per_row=tiles_per_row)
    return pl.pallas_call(
        kern,
        grid=(n // tm, n_j),
        in_specs=[pl.BlockSpec((tm, d), lambda i, j: (i, 0)),
                  pl.BlockSpec(mod.shape, lambda i, j: (0, 0)),
                  pl.BlockSpec((d, tn), lambda i, j: (0, j)),
                  pl.BlockSpec((d, GATE_LANES), lambda i, j: (0, 0)),
                  pl.BlockSpec((1, GATE_LANES), lambda i, j: (0, 0))],
        out_specs=[
            pl.BlockSpec((tm, tn), lambda i, j: (i, jnp.minimum(j, 1) + jnp.maximum(j - 4, 0))),
            pl.BlockSpec((tm, tn), lambda i, j: (i, jnp.clip(j - 2, 0, 2))),
            pl.BlockSpec((tm, GATE_LANES), lambda i, j: (i, 0)),
            pl.BlockSpec((4 * ML_HEADS, tm), lambda i, j: (0, i))],
        out_shape=[jax.ShapeDtypeStruct((n, 5 * d), F32),
                   jax.ShapeDtypeStruct((n, 3 * d), BF16),
                   jax.ShapeDtypeStruct((n, GATE_LANES), F32),
                   jax.ShapeDtypeStruct((4 * ML_HEADS, n), F32)],
        scratch_shapes=[pltpu.VMEM((tm, d), BF16)],
        compiler_params=_cparams(2),
        name="in_proj",
    )(x, mod, w_main, w_gate, b_gate)


def _rglru_kernel(xr_ref, zr_ref, w_ref, b_ref, lam_ref, cw_ref, cb_ref, h0_ref, *rest,
                  n_vseq, conv_len, n_seg, want_final):
    if want_final:
        out_ref, hfin_ref = rest[:2]
        rest = rest[2:]
    else:
        out_ref = rest[0]
        rest = rest[1:]
    a_f, b_f, a_b, b_b = rest[:4]
    seg = SCAN_SEG
    ct = xr_ref.shape[1]
    rows = n_vseq * seg
    ch = 256
    la = RG_C * _log_sigmoid(lam_ref[0])
    la_f, la_b = la[:, :ct], la[:, ct:]
    w = w_ref[0]
    bias = b_ref[0]
    cw = cw_ref[...]
    cb = cb_ref[...]
    pos = jnp.bitwise_and(lax.broadcasted_iota(jnp.int32, (ch, 1), 0), conv_len - 1)

    def gate_chunk(ci, carry):
        r0 = pl.multiple_of(ci * ch, ch)
        x = xr_ref[pl.ds(r0, ch), :]
        xm2 = jnp.where(pos >= 2, pltpu.roll(x, 2, 0), 0.0)
        xm1 = jnp.where(pos >= 1, pltpu.roll(x, 1, 0), 0.0)
        xp1 = jnp.where(pos <= conv_len - 2, pltpu.roll(x, ch - 1, 0), 0.0)
        xc = cw[0:1] * xm2 + cw[1:2] * xm1 + cw[2:3] * x + cw[3:4] * xp1 + cb
        gates = jnp.dot(xc.astype(BF16), w, preferred_element_type=F32) + bias

        def a_and_b(r_pre, i_pre, la_dir):
            log_a = jax.nn.sigmoid(r_pre) * la_dir
            a = jnp.exp(log_a)
            gain = jnp.sqrt(jnp.clip(1.0 - a * a, 0.0, 1.0))
            return a, gain * jax.nn.sigmoid(i_pre) * xc

        af, bf = a_and_b(gates[:, 0:ct], gates[:, ct:2 * ct], la_f)
        ab, bb = a_and_b(gates[:, 2 * ct:3 * ct], gates[:, 3 * ct:4 * ct], la_b)
        a_f[pl.ds(r0, ch), :] = af
        b_f[pl.ds(r0, ch), :] = bf
        a_b[pl.ds(r0, ch), :] = ab
        b_b[pl.ds(r0, ch), :] = bb
        return carry

    lax.fori_loop(0, rows // ch, gate_chunk, 0)

    def slab(ref, t):
        return ref[pl.ds(t, n_vseq, stride=seg), :]

    if n_seg == 1:
        init_f = h0_ref[0]
        init_b = h0_ref[1]
    else:
        e_f, p_f, e_b, p_b, cin_f, cin_b = rest[4:]
        n_real = n_vseq // n_seg
        zero = jnp.zeros((n_vseq, ct), F32)
        one = jnp.ones((n_vseq, ct), F32)

        def local_step(t, carry):
            ef, pf, eb, pb = carry
            tb = seg - 1 - t
            a = slab(a_f, t)
            ef = a * ef + slab(b_f, t)
            pf = a * pf
            a2 = slab(a_b, tb)
            eb = a2 * eb + slab(b_b, tb)
            pb = a2 * pb
            return ef, pf, eb, pb

        ef, pf, eb, pb = lax.fori_loop(0, seg, local_step, (zero, one, zero, one), unroll=8)
        e_f[...] = ef
        p_f[...] = pf
        e_b[...] = eb
        p_b[...] = pb
        carry = h0_ref[0]
        for s in range(n_seg):
            idx = pl.ds(s, n_real, stride=n_seg)
            cin_f[idx, :] = carry
            carry = e_f[idx, :] + p_f[idx, :] * carry
        carry = h0_ref[1]
        for s in reversed(range(n_seg)):
            idx = pl.ds(s, n_real, stride=n_seg)
            cin_b[idx, :] = carry
            carry = e_b[idx, :] + p_b[idx, :] * carry
        init_f = cin_f[...]
        init_b = cin_b[...]

    def scan_step(t, carry):
        hf, hb = carry
        tb = seg - 1 - t
        hf = slab(a_f, t) * hf + slab(b_f, t)
        b_f[pl.ds(t, n_vseq, stride=seg), :] = hf
        hb = slab(a_b, tb) * hb + slab(b_b, tb)
        b_b[pl.ds(tb, n_vseq, stride=seg), :] = hb
        return hf, hb

    hf_last, hb_last = lax.fori_loop(0, seg, scan_step, (init_f, init_b), unroll=8)
    if want_final:
        hfin_ref[0] = hf_last
        hfin_ref[1] = hb_last

    def out_chunk(ci, carry):
        r0 = pl.multiple_of(ci * ch, ch)
        h = b_f[pl.ds(r0, ch), :] + b_b[pl.ds(r0, ch), :]
        out_ref[pl.ds(r0, ch), :] = (h * jax.nn.gelu(zr_ref[pl.ds(r0, ch), :])).astype(BF16)
        return carry

    lax.fori_loop(0, rows // ch, out_chunk, 0)


def _rglru(of32, w_cat, b_cat, lam_cat, conv_w, conv_b, h0, *, n_seq, seq_len, conv_len, want_final):
    n = of32.shape[0]
    d_rnn = conv_w.shape[1]
    ct = RG_CT
    n_ct = d_rnn // ct
    n_seg = seq_len // SCAN_SEG
    n_vseq = n_seq * n_seg
    assert n_vseq * SCAN_SEG == n and SCAN_SEG % conv_len == 0 and (conv_len & (conv_len - 1)) == 0
    kern = functools.partial(_rglru_kernel, n_vseq=n_vseq, conv_len=conv_len, n_seg=n_seg,
                             want_final=want_final)
    out_specs = [pl.BlockSpec((n, ct), lambda c: (0, c))]
    out_shape = [jax.ShapeDtypeStruct((n, d_rnn), BF16)]
    if want_final:
        out_specs.append(pl.BlockSpec((2, n_seq, ct), lambda c: (0, 0, c)))
        out_shape.append(jax.ShapeDtypeStruct((2, n_seq, d_rnn), F32))
    scratch = [pltpu.VMEM((n, ct), F32) for _ in range(4)]
    if n_seg > 1:
        scratch += [pltpu.VMEM((n_vseq, ct), F32) for _ in range(6)]
    return pl.pallas_call(
        kern,
        grid=(n_ct,),
        in_specs=[pl.BlockSpec((n, ct), lambda c: (0, c)),
                  pl.BlockSpec((n, ct), lambda c: (0, n_ct + c)),
                  pl.BlockSpec((1, ct, 4 * ct), lambda c: (c, 0, 0)),
                  pl.BlockSpec((1, 1, 4 * ct), lambda c: (c, 0, 0)),
                  pl.BlockSpec((1, 1, 2 * ct), lambda c: (c, 0, 0)),
                  pl.BlockSpec((CONV_W, ct), lambda c: (0, c)),
                  pl.BlockSpec((1, ct), lambda c: (0, c)),
                  pl.BlockSpec((2, n_seq, ct), lambda c: (0, 0, c))],
        out_specs=out_specs,
        out_shape=out_shape,
        scratch_shapes=scratch,
        compiler_params=_cparams(1),
        name="rglru_ctx" if want_final else "rglru_lat",
    )(of32, of32, w_cat, b_cat, lam_cat, conv_w, conv_b, h0)


def _mlstm_unit(direction, head, q, k, v, gc, gt_ref, r0, state, scale):
    lc = q.shape[0]
    col_i = direction * ML_HEADS + head
    col_c = 2 * ML_HEADS + col_i
    lane = lax.broadcasted_iota(jnp.int32, (1, GATE_LANES), 1)
    ig_col = jnp.sum(jnp.where(lane == col_i, gc, 0.0), -1, keepdims=True)
    cum_col = jnp.sum(jnp.where(lane == col_c, gc, 0.0), -1, keepdims=True)
    ig_row = gt_ref[pl.ds(col_i, 1), pl.ds(r0, lc)]
    cum_row = gt_ref[pl.ds(col_c, 1), pl.ds(r0, lc)]
    ii = lax.broadcasted_iota(jnp.int32, (lc, lc), 0)
    jj = lax.broadcasted_iota(jnp.int32, (lc, lc), 1)
    mask = (jj <= ii) if direction == 0 else (jj >= ii)
    dmat = jnp.where(mask, cum_col + (ig_row - cum_row), -jnp.inf)
    if state is None:
        m_prev = 0.0
    else:
        c_prev, n_prev, m_prev = state
    inter = cum_col + m_prev
    m_row = jnp.maximum(jnp.max(dmat, -1, keepdims=True), inter)
    wts = jnp.exp(dmat - m_row)
    s = lax.dot_general(q, k, (((1,), (1,)), ((), ())), preferred_element_type=F32) * (wts * scale)
    num = jnp.dot(s.astype(BF16), v, preferred_element_type=F32)
    den = jnp.sum(s, -1, keepdims=True)
    if state is not None:
        w_inter = jnp.exp(inter - m_row) * scale
        num = num + w_inter * jnp.dot(q, c_prev.astype(BF16), preferred_element_type=F32)
        den = den + w_inter * jnp.sum(q.astype(F32) * n_prev, -1, keepdims=True)
    h = num / jnp.maximum(jnp.abs(den), jnp.exp(-m_row))
    btot = cum_col[lc - 1:lc] if direction == 0 else cum_col[0:1]
    wk_log = btot - cum_col + ig_col
    m_new = jnp.maximum(btot + m_prev, jnp.max(wk_log, 0, keepdims=True))
    kw = k.astype(F32) * jnp.exp(wk_log - m_new)
    c_new = lax.dot_general(kw.astype(BF16), v, (((0,), (0,)), ((), ())), preferred_element_type=F32)
    n_new = jnp.sum(kw, 0, keepdims=True)
    if state is not None:
        decay = jnp.exp(btot + m_prev - m_new)
        c_new = decay * c_prev + c_new
        n_new = decay * n_prev + n_new
    return h, c_new, n_new, m_new


def _head_out(hsum, o, gn):
    return (jax.nn.sigmoid(o) * (_ln(hsum) * gn)).astype(BF16)


def _mlstm_ctx_kernel(q_ref, k_ref, v_ref, o_ref, g_ref, gt_ref, gn_ref,
                      out_ref, cfin_ref, nfin_ref, mfin_ref, *, n_sub, scale):
    head = pl.program_id(1)
    lc = MLSTM_CHUNK
    gn = gn_ref[...]

    def body(s, carry):
        r0 = pl.multiple_of(s * lc, lc)
        rs = pl.ds(r0, lc)
        q, k, v, gc = q_ref[rs, :], k_ref[rs, :], v_ref[rs, :], g_ref[rs, :]
        hsum = None
        for direction in (0, 1):
            h, c_new, n_new, m_new = _mlstm_unit(direction, head, q, k, v, gc, gt_ref, r0, None, scale)
            cfin_ref[s, direction, 0] = c_new
            nfin_ref[s, direction, 0] = n_new
            mfin_ref[s, direction, 0] = jnp.broadcast_to(m_new, (1, GATE_LANES))
            hsum = h if hsum is None else hsum + h
        out_ref[rs, :] = _head_out(hsum, o_ref[rs, :], gn)
        return carry

    lax.fori_loop(0, n_sub, body, 0)


def _mlstm_lat_kernel(q_ref, k_ref, v_ref, o_ref, g_ref, gt_ref, gn_ref, c0_ref, n0_ref, m0_ref,
                      out_ref, c_st, n_st, m_st, h_acc, *, n_chunks, scale):
    head = pl.program_id(1)
    lc = MLSTM_CHUNK
    gn = gn_ref[...]
    for direction in (0, 1):
        c_st[direction] = c0_ref[0, direction, 0]
        n_st[direction] = n0_ref[0, direction, 0]
        m_st[direction] = m0_ref[0, direction, 0]
    h_acc[...] = jnp.zeros(h_acc.shape, F32)

    def body(c, carry):
        for direction in (0, 1):
            cc = c if direction == 0 else n_chunks - 1 - c
            r0 = pl.multiple_of(cc * lc, lc)
            rs = pl.ds(r0, lc)
            state = (c_st[direction], n_st[direction], m_st[direction][:, 0:1])
            h, c_new, n_new, m_new = _mlstm_unit(direction, head, q_ref[rs, :], k_ref[rs, :], v_ref[rs, :],
                                                 g_ref[rs, :], gt_ref, r0, state, scale)
            c_st[direction] = c_new
            n_st[direction] = n_new
            m_st[direction] = jnp.broadcast_to(m_new, (1, GATE_LANES))
            h_acc[rs, :] += h
        return carry

    lax.fori_loop(0, n_chunks, body, 0)

    def out_chunk(c, carry):
        rs = pl.ds(pl.multiple_of(c * lc, lc), lc)
        out_ref[rs, :] = _head_out(h_acc[rs, :], o_ref[rs, :], gn)
        return carry

    lax.fori_loop(0, n_chunks, out_chunk, 0)


def _mlstm_ctx(obf, of32, g, gt, gn_g, *, n_seq):
    n = obf.shape[0]
    ml_dim = gn_g.shape[1]
    hd = ml_dim // ML_HEADS
    lc = MLSTM_CHUNK
    assert n == n_seq * lc
    n_sub = 4
    rows = n_sub * lc
    o_blk = (2 * ml_dim) // hd
    kern = functools.partial(_mlstm_ctx_kernel, n_sub=n_sub, scale=hd ** -0.5)
    return pl.pallas_call(
        kern,
        grid=(n_seq // n_sub, ML_HEADS),
        in_specs=[pl.BlockSpec((rows, hd), lambda s, h: (s, h)),
                  pl.BlockSpec((rows, hd), lambda s, h: (s, ML_HEADS + h)),
                  pl.BlockSpec((rows, hd), lambda s, h: (s, 2 * ML_HEADS + h)),
                  pl.BlockSpec((rows, hd), lambda s, h: (s, o_blk + h)),
                  pl.BlockSpec((rows, GATE_LANES), lambda s, h: (s, 0)),
                  pl.BlockSpec((4 * ML_HEADS, rows), lambda s, h: (0, s)),
                  pl.BlockSpec((1, hd), lambda s, h: (0, h))],
        out_specs=[pl.BlockSpec((rows, hd), lambda s, h: (s, h)),
                   pl.BlockSpec((n_sub, 2, 1, hd, hd), lambda s, h: (s, 0, h, 0, 0)),
                   pl.BlockSpec((n_sub, 2, 1, 1, hd), lambda s, h: (s, 0, h, 0, 0)),
                   pl.BlockSpec((n_sub, 2, 1, 1, GATE_LANES), lambda s, h: (s, 0, h, 0, 0))],
        out_shape=[jax.ShapeDtypeStruct((n, ml_dim), BF16),
                   jax.ShapeDtypeStruct((n_seq, 2, ML_HEADS, hd, hd), F32),
                   jax.ShapeDtypeStruct((n_seq, 2, ML_HEADS, 1, hd), F32),
                   jax.ShapeDtypeStruct((n_seq, 2, ML_HEADS, 1, GATE_LANES), F32)],
        compiler_params=_cparams(2),
        name="mlstm_ctx",
    )(obf, obf, obf, of32, g, gt, gn_g)


def _mlstm_lat(obf, of32, g, gt, gn_g, c0, n0, m0, *, n_seq):
    n = obf.shape[0]
    ml_dim = gn_g.shape[1]
    hd = ml_dim // ML_HEADS
    lc = MLSTM_CHUNK
    rows = n // n_seq
    n_chunks = rows // lc
    o_blk = (2 * ml_dim) // hd
    kern = functools.partial(_mlstm_lat_kernel, n_chunks=n_chunks, scale=hd ** -0.5)
    return pl.pallas_call(
        kern,
        grid=(n_seq, ML_HEADS),
        in_specs=[pl.BlockSpec((rows, hd), lambda s, h: (s, h)),
                  pl.BlockSpec((rows, hd), lambda s, h: (s, ML_HEADS + h)),
                  pl.BlockSpec((rows, hd), lambda s, h: (s, 2 * ML_HEADS + h)),
                  pl.BlockSpec((rows, hd), lambda s, h: (s, o_blk + h)),
                  pl.BlockSpec((rows, GATE_LANES), lambda s, h: (s, 0)),
                  pl.BlockSpec((4 * ML_HEADS, rows), lambda s, h: (0, s)),
                  pl.BlockSpec((1, hd), lambda s, h: (0, h)),
                  pl.BlockSpec((1, 2, 1, hd, hd), lambda s, h: (s, 0, h, 0, 0)),
                  pl.BlockSpec((1, 2, 1, 1, hd), lambda s, h: (s, 0, h, 0, 0)),
                  pl.BlockSpec((1, 2, 1, 1, GATE_LANES), lambda s, h: (s, 0, h, 0, 0))],
        out_specs=pl.BlockSpec((rows, hd), lambda s, h: (s, h)),
        out_shape=jax.ShapeDtypeStruct((n, ml_dim), BF16),
        scratch_shapes=[pltpu.VMEM((2, hd, hd), F32),
                        pltpu.VMEM((2, 1, hd), F32),
                        pltpu.VMEM((2, 1, GATE_LANES), F32),
                        pltpu.VMEM((rows, hd), F32)],
        compiler_params=_cparams(2),
        name="mlstm_lat",
    )(obf, obf, obf, of32, g, gt, gn_g, c0, n0, m0)


def _merge_kernel(hrg_ref, hml_ref, gma_ref, gmb_ref, x_ref, mod_ref, wrg_ref, wml_ref, wout_ref,
                  bm_ref, lng_ref, lnb_ref, o_ref, *, d, alpha, row_base, tiles_per_row):
    r = _mod_row(pl.program_id(0), row_base, tiles_per_row)
    gate1 = mod_ref[pl.ds(r, 1), 2 * d:3 * d]
    y_rg = jnp.dot(hrg_ref[...], wrg_ref[...], preferred_element_type=F32)
    y_ml = jnp.dot(hml_ref[...], wml_ref[...], preferred_element_type=F32)
    g_rg = jax.nn.sigmoid(gma_ref[...] + bm_ref[:, 0:d])
    g_ml = jax.nn.sigmoid(gmb_ref[...] + bm_ref[:, d:2 * d])
    merged = (g_rg * y_rg + g_ml * y_ml).astype(BF16)
    mix = jnp.dot(merged, wout_ref[...], preferred_element_type=F32)
    o_ref[...] = _ln(alpha * x_ref[...] + gate1 * mix) * lng_ref[...] + lnb_ref[...]


def _merge(hrg, hml, of32, x, mod, w_rg, w_ml, w_out, b_merge, ln_g, ln_b, *, alpha, row_base,
           rows_per_mod, tm):
    n, d = x.shape
    tiles_per_row = _tiles_per_row(rows_per_mod, tm)
    kern = functools.partial(_merge_kernel, d=d, alpha=alpha, row_base=row_base, tiles_per_row=tiles_per_row)
    full = lambda shape: pl.BlockSpec(shape, lambda i: (0,) * len(shape))
    return pl.pallas_call(
        kern,
        grid=(n // tm,),
        in_specs=[pl.BlockSpec((tm, d), lambda i: (i, 0)),
                  pl.BlockSpec((tm, d), lambda i: (i, 0)),
                  pl.BlockSpec((tm, d), lambda i: (i, 3)),
                  pl.BlockSpec((tm, d), lambda i: (i, 4)),
                  pl.BlockSpec((tm, d), lambda i: (i, 0)),
                  full(mod.shape), full((d, d)), full((d, d)), full((d, d)),
                  full((1, 2 * d)), full((1, d)), full((1, d))],
        out_specs=pl.BlockSpec((tm, d), lambda i: (i, 0)),
        out_shape=jax.ShapeDtypeStruct((n, d), F32),
        compiler_params=_cparams(1),
        name="merge",
    )(hrg, hml, of32, of32, x, mod, w_rg, w_ml, w_out, b_merge, ln_g, ln_b)


def _mlp_kernel(x_ref, mod_ref, wfc_ref, bfc_ref, wpj_ref, bpj_ref, lng_ref, lnb_ref, o_ref,
                u_ref, acc_ref, *, d, alpha, row_base, tiles_per_row):
    f = pl.program_id(1)
    r = _mod_row(pl.program_id(0), row_base, tiles_per_row)

    @pl.when(f == 0)
    def _():
        shift = mod_ref[pl.ds(r, 1), 3 * d:4 * d]
        scale = mod_ref[pl.ds(r, 1), 4 * d:5 * d]
        u_ref[...] = (_ln(x_ref[...]) * (1.0 + scale) + shift).astype(BF16)
        acc_ref[...] = jnp.zeros(acc_ref.shape, F32)

    hid = jnp.dot(u_ref[...], wfc_ref[...], preferred_element_type=F32) + bfc_ref[...]
    hid = jnp.square(jnp.maximum(hid, 0.0)).astype(BF16)
    acc_ref[...] += jnp.dot(hid, wpj_ref[...], preferred_element_type=F32)

    @pl.when(f == pl.num_programs(1) - 1)
    def _():
        gate2 = mod_ref[pl.ds(r, 1), 5 * d:6 * d]
        y = alpha * x_ref[...] + gate2 * (acc_ref[...] + bpj_ref[...])
        o_ref[...] = _ln(y) * lng_ref[...] + lnb_ref[...]


def _mlp(x, mod, w_fc, b_fc, w_proj, b_proj, ln_g, ln_b, *, alpha, row_base, rows_per_mod, tm):
    n, d = x.shape
    tiles_per_row = _tiles_per_row(rows_per_mod, tm)
    d_ff = w_fc.shape[1]
    tf = 1024
    kern = functools.partial(_mlp_kernel, d=d, alpha=alpha, row_base=row_base, tiles_per_row=tiles_per_row)
    return pl.pallas_call(
        kern,
        grid=(n // tm, d_ff // tf),
        in_specs=[pl.BlockSpec((tm, d), lambda i, f: (i, 0)),
                  pl.BlockSpec(mod.shape, lambda i, f: (0, 0)),
                  pl.BlockSpec((d, tf), lambda i, f: (0, f)),
                  pl.BlockSpec((1, tf), lambda i, f: (0, f)),
                  pl.BlockSpec((tf, d), lambda i, f: (f, 0)),
                  pl.BlockSpec((1, d), lambda i, f: (0, 0)),
                  pl.BlockSpec((1, d), lambda i, f: (0, 0)),
                  pl.BlockSpec((1, d), lambda i, f: (0, 0))],
        out_specs=pl.BlockSpec((tm, d), lambda i, f: (i, 0)),
        out_shape=jax.ShapeDtypeStruct((n, d), F32),
        scratch_shapes=[pltpu.VMEM((tm, d), BF16), pltpu.VMEM((tm, d), F32)],
        compiler_params=_cparams(2),
        name="mlp",
    )(x, mod, w_fc, b_fc, w_proj, b_proj, ln_g, ln_b)


def _rg_gate_weights(wa, ba, wx, bx, lam):
    ct = RG_CT
    d_rnn = ba.shape[-1]
    n_ct = d_rnn // ct
    per = ct // RG_BW

    def tile_blockdiag(w):
        wt = w.reshape(n_ct, per, RG_BW, RG_BW)
        eye = jnp.eye(per, dtype=w.dtype)
        return jnp.einsum('cpkj,pq->cpkqj', wt, eye).reshape(n_ct, ct, ct)

    w_cat = jnp.concatenate([tile_blockdiag(wa[0]), tile_blockdiag(wx[0]),
                             tile_blockdiag(wa[1]), tile_blockdiag(wx[1])], axis=-1).astype(BF16)
    tiles = lambda b: b.reshape(n_ct, 1, ct)
    b_cat = jnp.concatenate([tiles(ba[0]), tiles(bx[0]), tiles(ba[1]), tiles(bx[1])], axis=-1)
    lam_cat = jnp.concatenate([tiles(lam[0]), tiles(lam[1])], axis=-1)
    return w_cat, b_cat, lam_cat


def kernel(x_prompt, x_sample, state_rglru_h, state_mlstm_C, state_mlstm_n, state_mlstm_m, c, c_ctx, w_ada, b_ada, w_in, rg_conv_w, rg_conv_b, rg_wa, rg_ba, rg_wx, rg_bx, rg_lambda, w_rg_proj, ml_b_igate, ml_b_fgate, ml_gn_g, w_ml_proj, b_merge, w_out, ln_g, ln_b, w_fc, b_fc, w_proj, b_proj):
    bp, seq, d = x_prompt.shape
    bl, dec_seq, _ = x_sample.shape
    depth = w_in.shape[0]
    d_rnn = rg_conv_w.shape[-1]
    ml_dim = ml_gn_g.shape[-1]
    hd = ml_dim // ML_HEADS
    alpha = (2 * depth) ** 0.25
    d_main = 2 * d_rnn + 4 * ml_dim + 2 * d
    assert seq == MLSTM_CHUNK == SCAN_SEG and dec_seq % MLSTM_CHUNK == 0
    assert d_rnn == d and ml_dim == d and w_in.shape[-1] == d_main + 4 * ML_HEADS

    tm = 1024
    xp = x_prompt.reshape(bp * seq, d)
    xs = x_sample.reshape(bl * dec_seq, d)
    cond = jnp.concatenate([c_ctx[None, :], c], axis=0)
    zero_h = jnp.zeros((2, bp, d_rnn), F32)
    new_h, new_c, new_n, new_m = [], [], [], []
    for l in range(depth):
        mod = _ada(cond, w_ada[l], b_ada[l])
        w_main = w_in[l][:, :d_main].astype(BF16)
        w_gate = jnp.zeros((d, GATE_LANES), BF16).at[:, :4 * ML_HEADS].set(w_in[l][:, d_main:].astype(BF16))
        b_gate = jnp.zeros((1, GATE_LANES), F32).at[0, :4 * ML_HEADS].set(
            jnp.concatenate([ml_b_igate[l].reshape(-1), ml_b_fgate[l].reshape(-1)]))
        w_cat, b_cat, lam_cat = _rg_gate_weights(rg_wa[l], rg_ba[l], rg_wx[l], rg_bx[l], rg_lambda[l])
        conv_b = rg_conv_b[l].reshape(1, d_rnn)
        gn_g = ml_gn_g[l].reshape(1, ml_dim)
        w_rg, w_ml, w_o = (w_rg_proj[l].astype(BF16), w_ml_proj[l].astype(BF16), w_out[l].astype(BF16))
        w_fc_b, w_pj_b = w_fc[l].astype(BF16), w_proj[l].astype(BF16)
        bm = b_merge[l].reshape(1, 2 * d)
        lng0, lnb0 = ln_g[l, 0].reshape(1, d), ln_b[l, 0].reshape(1, d)
        lng1, lnb1 = ln_g[l, 1].reshape(1, d), ln_b[l, 1].reshape(1, d)
        bfc, bpj = b_fc[l].reshape(1, -1), b_proj[l].reshape(1, d)

        def tail(x, hrg, hml, of32, row_base, rows_per_mod):
            x1 = _merge(hrg, hml, of32, x, mod, w_rg, w_ml, w_o, bm, lng0, lnb0, alpha=alpha,
                        row_base=row_base, rows_per_mod=rows_per_mod, tm=512)
            return _mlp(x1, mod, w_fc_b, bfc, w_pj_b, bpj, lng1, lnb1, alpha=alpha,
                        row_base=row_base, rows_per_mod=rows_per_mod, tm=tm)

        of32, obf, g, gt = _in_proj(xp, mod, w_main, w_gate, b_gate, row_base=0, rows_per_mod=None, tm=tm)
        hrg, h_fin = _rglru(of32, w_cat, b_cat, lam_cat, rg_conv_w[l], conv_b, zero_h,
                            n_seq=bp, seq_len=seq, conv_len=seq, want_final=True)
        hml, c_fin, n_fin, m_fin = _mlstm_ctx(obf, of32, g, gt, gn_g, n_seq=bp)
        xp = tail(xp, hrg, hml, of32, 0, None)
        new_h.append(jnp.transpose(h_fin, (1, 0, 2)))
        new_c.append(c_fin)
        new_n.append(n_fin.reshape(bp, 2, ML_HEADS, hd))
        new_m.append(m_fin[..., 0, 0])

        of32, obf, g, gt = _in_proj(xs, mod, w_main, w_gate, b_gate, row_base=1,
                                    rows_per_mod=dec_seq, tm=tm)
        h0 = jnp.transpose(state_rglru_h[:, l], (1, 0, 2))
        (hrg,) = _rglru(of32, w_cat, b_cat, lam_cat, rg_conv_w[l], conv_b, h0,
                        n_seq=bl, seq_len=dec_seq, conv_len=GRID_W, want_final=False)
        c0 = state_mlstm_C[:, l]
        n0 = state_mlstm_n[:, l].reshape(bl, 2, ML_HEADS, 1, hd)
        m0 = jnp.broadcast_to(state_mlstm_m[:, l].reshape(bl, 2, ML_HEADS, 1, 1),
                              (bl, 2, ML_HEADS, 1, GATE_LANES))
        hml = _mlstm_lat(obf, of32, g, gt, gn_g, c0, n0, m0, n_seq=bl)
        xs = tail(xs, hrg, hml, of32, 1, dec_seq)

    stack = lambda parts: jnp.stack(parts, axis=1)
    return (xp.reshape(bp, seq, d), xs.reshape(bl, dec_seq, d),
            stack(new_h), stack(new_c), stack(new_n), stack(new_m))
```

```python
import functools

import jax
import jax.numpy as jnp
from jax import lax
from jax.experimental import pallas as pl
from jax.experimental.pallas import tpu as pltpu

F32 = jnp.float32
BF16 = jnp.bfloat16

LN_EPS = 1e-5
RG_C = 8.0
RG_BW = 64
CONV_W = 4
GRID_W = 64
ML_HEADS = 4
MLSTM_CHUNK = 256
SCAN_SEG = 256
SUBLANES = 8
SCAN_PITCH = SCAN_SEG + SUBLANES
RG_CT = 128
GATE_LANES = 128
VMEM_LIMIT = 52 * 1024 * 1024


def _cparams(n_axes):
    return pltpu.CompilerParams(
        dimension_semantics=("arbitrary",) * n_axes, vmem_limit_bytes=VMEM_LIMIT)


def _log_sigmoid(x):
    return jnp.minimum(x, 0.0) - jnp.log1p(jnp.exp(-jnp.abs(x)))


def _ln(x):
    mu = jnp.mean(x, -1, keepdims=True)
    xc = x - mu
    var = jnp.mean(xc * xc, -1, keepdims=True)
    return xc * lax.rsqrt(var + LN_EPS)


def _mod_row(i, row_base, tiles_per_row):
    if tiles_per_row is None:
        return row_base
    return row_base + lax.div(i, jnp.int32(tiles_per_row))


def _ada_kernel(ct_ref, w_ref, b_ref, o_ref, *, n_rows):
    w = w_ref[...]
    ct = ct_ref[...]
    s = ct * jax.nn.sigmoid(ct)
    o_ref[...] = jnp.zeros(o_ref.shape, F32)
    for r in range(n_rows):
        o_ref[r:r + 1, :] = jnp.sum(w * s[:, r:r + 1], axis=0, keepdims=True) + b_ref[...]


def _ada(cond, w_ada, b_ada):
    n_rows, d = cond.shape
    assert n_rows <= 8
    ct = jnp.zeros((d, 8), F32).at[:, :n_rows].set(cond.T)
    n_out = w_ada.shape[1]
    tn = 512
    return pl.pallas_call(
        functools.partial(_ada_kernel, n_rows=n_rows),
        grid=(n_out // tn,),
        in_specs=[pl.BlockSpec((d, 8), lambda j: (0, 0)),
                  pl.BlockSpec((d, tn), lambda j: (0, j)),
                  pl.BlockSpec((1, tn), lambda j: (0, j))],
        out_specs=pl.BlockSpec((8, tn), lambda j: (0, j)),
        out_shape=jax.ShapeDtypeStruct((8, n_out), F32),
        compiler_params=_cparams(1),
        name="ada_mod",
    )(ct, w_ada, b_ada.reshape(1, n_out))


def _inproj_kernel(x_ref, mod_ref, w_ref, wg_ref, bg_ref, of_ref, ob_ref, g_ref, gt_ref, u_ref,
                   *, d, tm, row_base, tiles_per_row):
    i = pl.program_id(0)
    j = pl.program_id(1)
    lc = MLSTM_CHUNK

    @pl.when(j == 0)
    def _():
        r = _mod_row(i, row_base, tiles_per_row)
        shift = mod_ref[pl.ds(r, 1), 0:d]
        scale = mod_ref[pl.ds(r, 1), d:2 * d]
        u_ref[...] = (_ln(x_ref[...]) * (1.0 + scale) + shift).astype(BF16)
        g = jnp.dot(u_ref[...], wg_ref[...], preferred_element_type=F32) + bg_ref[...]
        lane = lax.broadcasted_iota(jnp.int32, g.shape, 1)
        g = jnp.where(lane >= 2 * ML_HEADS, _log_sigmoid(g), g)
        ii = lax.broadcasted_iota(jnp.int32, (lc, lc), 0)
        jj = lax.broadcasted_iota(jnp.int32, (lc, lc), 1)
        tril = (jj <= ii).astype(BF16)
        triu = (jj >= ii).astype(BF16)
        lane_c = lax.broadcasted_iota(jnp.int32, (lc, GATE_LANES), 1)
        for c in range(tm // lc):
            gc = g[c * lc:(c + 1) * lc]
            hi = gc.astype(BF16)
            r1 = gc - hi.astype(F32)
            mid = r1.astype(BF16)
            lo = (r1 - mid.astype(F32)).astype(BF16)
            cum_f = (jnp.dot(tril, hi, preferred_element_type=F32)
                     + jnp.dot(tril, mid, preferred_element_type=F32)
                     + jnp.dot(tril, lo, preferred_element_type=F32))
            cum_b = (jnp.dot(triu, hi, preferred_element_type=F32)
                     + jnp.dot(triu, mid, preferred_element_type=F32)
                     + jnp.dot(triu, lo, preferred_element_type=F32))
            gfin = jnp.where(lane_c < 2 * ML_HEADS, gc,
                             jnp.where(lane_c < 3 * ML_HEADS, cum_f, cum_b))
            g_ref[c * lc:(c + 1) * lc, :] = gfin
            gt_ref[:, c * lc:(c + 1) * lc] = gfin.T[0:4 * ML_HEADS, :]

    is_bf = jnp.logical_and(j >= 2, j <= 4)

    @pl.when(is_bf)
    def _():
        ob_ref[...] = jnp.dot(u_ref[...], w_ref[...], preferred_element_type=F32).astype(BF16)

    @pl.when(jnp.logical_not(is_bf))
    def _():
        of_ref[...] = jnp.dot(u_ref[...], w_ref[...], preferred_element_type=F32)


def _tiles_per_row(rows_per_mod, tm):
    if rows_per_mod is None:
        return None
    assert rows_per_mod % tm == 0
    return rows_per_mod // tm


def _in_proj(x, mod, w_main, w_gate, b_gate, *, row_base, rows_per_mod, tm):
    n, d = x.shape
    tiles_per_row = _tiles_per_row(rows_per_mod, tm)
    tn = d
    n_j = w_main.shape[1] // tn
    assert n_j == 8
    kern = functools.partial(_inproj_kernel, d=d, tm=tm, row_base=row_base, tiles_per_row=tiles_per_row)
    return pl.pallas_call(
        kern,
        grid=(n // tm, n_j),
        in_specs=[pl.BlockSpec((tm, d), lambda i, j: (i, 0)),
                  pl.BlockSpec(mod.shape, lambda i, j: (0, 0)),
                  pl.BlockSpec((d, tn), lambda i, j: (0, j)),
                  pl.BlockSpec((d, GATE_LANES), lambda i, j: (0, 0)),
                  pl.BlockSpec((1, GATE_LANES), lambda i, j: (0, 0))],
        out_specs=[
            pl.BlockSpec((tm, tn), lambda i, j: (i, jnp.minimum(j, 1) + jnp.maximum(j - 4, 0))),
            pl.BlockSpec((tm, tn), lambda i, j: (i, jnp.clip(j - 2, 0, 2))),
            pl.BlockSpec((tm, GATE_LANES), lambda i, j: (i, 0)),
            pl.BlockSpec((4 * ML_HEADS, tm), lambda i, j: (0, i))],
        out_shape=[jax.ShapeDtypeStruct((n, 5 * d), F32),
                   jax.ShapeDtypeStruct((n, 3 * d), BF16),
                   jax.ShapeDtypeStruct((n, GATE_LANES), F32),
                   jax.ShapeDtypeStruct((4 * ML_HEADS, n), F32)],
        scratch_shapes=[pltpu.VMEM((tm, d), BF16)],
        compiler_params=_cparams(2),
        name="in_proj",
    )(x, mod, w_main, w_gate, b_gate)


def _rglru_kernel(xr_ref, zr_ref, w_ref, b_ref, lam_ref, cw_ref, cb_ref, h0_ref, *rest,
                  n_vseq, conv_len, n_seg, want_final):
    if want_final:
        out_ref, hfin_ref = rest[:2]
        rest = rest[2:]
    else:
        out_ref = rest[0]
        rest = rest[1:]
    a_f, b_f, a_b, b_b, cwm_ref = rest[:5]
    rest = rest[5:]
    seg = SCAN_SEG
    pitch = SCAN_PITCH
    ct = xr_ref.shape[1]
    ch = seg
    la = RG_C * _log_sigmoid(lam_ref[0])
    la_f, la_b = la[:, :ct], la[:, ct:]
    w = w_ref[0]
    bias = b_ref[0]
    cw = cw_ref[...]
    cb = cb_ref[...]
    pos = jnp.bitwise_and(lax.broadcasted_iota(jnp.int32, (ch, ct), 0), conv_len - 1)
    cwm_ref[0] = jnp.where(pos >= 2, cw[0:1], 0.0)
    cwm_ref[1] = jnp.where(pos >= 1, cw[1:2], 0.0)
    cwm_ref[2] = jnp.where(pos <= conv_len - 2, cw[3:4], 0.0)

    def gate_chunk(ci, carry):
        r0 = pl.multiple_of(ci * ch, ch)
        x = xr_ref[pl.ds(r0, ch), :]
        xc = (cwm_ref[0] * pltpu.roll(x, 2, 0) + cwm_ref[1] * pltpu.roll(x, 1, 0) + cw[2:3] * x
              + cwm_ref[2] * pltpu.roll(x, ch - 1, 0) + cb)
        gates = jnp.dot(xc.astype(BF16), w, preferred_element_type=F32) + bias

        def a_and_b(r_pre, i_pre, la_dir):
            log_a = jax.nn.sigmoid(r_pre) * la_dir
            a = jnp.exp(log_a)
            y = jnp.clip(1.0 - a * a, 0.0, 1.0)
            gain = jnp.where(y > 0.0, y * lax.rsqrt(y), 0.0)
            return a, gain * jax.nn.sigmoid(i_pre) * xc

        af, bf = a_and_b(gates[:, 0:ct], gates[:, ct:2 * ct], la_f)
        ab, bb = a_and_b(gates[:, 2 * ct:3 * ct], gates[:, 3 * ct:4 * ct], la_b)
        s0 = pl.ds(pl.multiple_of(ci * pitch, SUBLANES), ch)
        a_f[s0, :] = af
        b_f[s0, :] = bf
        a_b[s0, :] = ab
        b_b[s0, :] = bb
        return carry

    lax.fori_loop(0, n_vseq, gate_chunk, 0)

    def slab_idx(t):
        return pl.ds(t, n_vseq, stride=pitch)

    def slab(ref, t):
        return ref[slab_idx(t), :]

    if n_seg == 1:
        init_f = h0_ref[0]
        init_b = h0_ref[1]
    else:
        e_f, p_f, e_b, p_b, cin_f, cin_b = rest
        n_real = n_vseq // n_seg
        zero = jnp.zeros((n_vseq, ct), F32)
        one = jnp.ones((n_vseq, ct), F32)

        def local_step(t, carry):
            ef, pf, eb, pb = carry
            tb = seg - 1 - t
            a = slab(a_f, t)
            ef = a * ef + slab(b_f, t)
            pf = a * pf
            a2 = slab(a_b, tb)
            eb = a2 * eb + slab(b_b, tb)
            pb = a2 * pb
            return ef, pf, eb, pb

        ef, pf, eb, pb = lax.fori_loop(0, seg, local_step, (zero, one, zero, one), unroll=8)
        e_f[...] = ef
        p_f[...] = pf
        e_b[...] = eb
        p_b[...] = pb
        carry = h0_ref[0]
        for s in range(n_seg):
            idx = pl.ds(s, n_real, stride=n_seg)
            cin_f[idx, :] = carry
            carry = e_f[idx, :] + p_f[idx, :] * carry
        carry = h0_ref[1]
        for s in reversed(range(n_seg)):
            idx = pl.ds(s, n_real, stride=n_seg)
            cin_b[idx, :] = carry
            carry = e_b[idx, :] + p_b[idx, :] * carry
        init_f = cin_f[...]
        init_b = cin_b[...]

    def scan_step(t, carry):
        hf, hb = carry
        tb = seg - 1 - t
        hf = slab(a_f, t) * hf + slab(b_f, t)
        b_f[slab_idx(t), :] = hf
        hb = slab(a_b, tb) * hb + slab(b_b, tb)
        b_b[slab_idx(tb), :] = hb
        return hf, hb

    hf_last, hb_last = lax.fori_loop(0, seg, scan_step, (init_f, init_b), unroll=8)
    if want_final:
        hfin_ref[0] = hf_last
        hfin_ref[1] = hb_last

    def out_chunk(ci, carry):
        r0 = pl.multiple_of(ci * ch, ch)
        s0 = pl.ds(pl.multiple_of(ci * pitch, SUBLANES), ch)
        h = b_f[s0, :] + b_b[s0, :]
        out_ref[pl.ds(r0, ch), :] = (h * jax.nn.gelu(zr_ref[pl.ds(r0, ch), :])).astype(BF16)
        return carry

    lax.fori_loop(0, n_vseq, out_chunk, 0)


def _rglru(of32, w_cat, b_cat, lam_cat, conv_w, conv_b, h0, *, n_seq, seq_len, conv_len, want_final):
    n = of32.shape[0]
    d_rnn = conv_w.shape[1]
    ct = RG_CT
    n_ct = d_rnn // ct
    n_seg = seq_len // SCAN_SEG
    n_vseq = n_seq * n_seg
    assert n_vseq * SCAN_SEG == n and SCAN_SEG % conv_len == 0 and (conv_len & (conv_len - 1)) == 0
    kern = functools.partial(_rglru_kernel, n_vseq=n_vseq, conv_len=conv_len, n_seg=n_seg,
                             want_final=want_final)
    out_specs = [pl.BlockSpec((n, ct), lambda c: (0, c))]
    out_shape = [jax.ShapeDtypeStruct((n, d_rnn), BF16)]
    if want_final:
        out_specs.append(pl.BlockSpec((2, n_seq, ct), lambda c: (0, 0, c)))
        out_shape.append(jax.ShapeDtypeStruct((2, n_seq, d_rnn), F32))
    scratch = [pltpu.VMEM((n_vseq * SCAN_PITCH, ct), F32) for _ in range(4)]
    scratch.append(pltpu.VMEM((CONV_W - 1, SCAN_SEG, ct), F32))
    if n_seg > 1:
        scratch += [pltpu.VMEM((n_vseq, ct), F32) for _ in range(6)]
    return pl.pallas_call(
        kern,
        grid=(n_ct,),
        in_specs=[pl.BlockSpec((n, ct), lambda c: (0, c)),
                  pl.BlockSpec((n, ct), lambda c: (0, n_ct + c)),
                  pl.BlockSpec((1, ct, 4 * ct), lambda c: (c, 0, 0)),
                  pl.BlockSpec((1, 1, 4 * ct), lambda c: (c, 0, 0)),
                  pl.BlockSpec((1, 1, 2 * ct), lambda c: (c, 0, 0)),
                  pl.BlockSpec((CONV_W, ct), lambda c: (0, c)),
                  pl.BlockSpec((1, ct), lambda c: (0, c)),
                  pl.BlockSpec((2, n_seq, ct), lambda c: (0, 0, c))],
        out_specs=out_specs,
        out_shape=out_shape,
        scratch_shapes=scratch,
        compiler_params=_cparams(1),
        name="rglru_ctx" if want_final else "rglru_lat",
    )(of32, of32, w_cat, b_cat, lam_cat, conv_w, conv_b, h0)


def _mlstm_unit(direction, head, q, k, v, gc, gt_ref, r0, state, scale):
    lc = q.shape[0]
    col_i = direction * ML_HEADS + head
    col_c = 2 * ML_HEADS + col_i
    lane = lax.broadcasted_iota(jnp.int32, (1, GATE_LANES), 1)
    ig_col = jnp.sum(jnp.where(lane == col_i, gc, 0.0), -1, keepdims=True)
    cum_col = jnp.sum(jnp.where(lane == col_c, gc, 0.0), -1, keepdims=True)
    ig_row = gt_ref[pl.ds(col_i, 1), pl.ds(r0, lc)]
    cum_row = gt_ref[pl.ds(col_c, 1), pl.ds(r0, lc)]
    ii = lax.broadcasted_iota(jnp.int32, (lc, lc), 0)
    jj = lax.broadcasted_iota(jnp.int32, (lc, lc), 1)
    mask = (jj <= ii) if direction == 0 else (jj >= ii)
    dmat = jnp.where(mask, cum_col + (ig_row - cum_row), -jnp.inf)
    if state is None:
        m_prev = 0.0
    else:
        c_prev, n_prev, m_prev = state
    inter = cum_col + m_prev
    m_row = jnp.maximum(jnp.max(dmat, -1, keepdims=True), inter)
    wts = jnp.exp(dmat - m_row)
    s = lax.dot_general(q, k, (((1,), (1,)), ((), ())), preferred_element_type=F32) * (wts * scale)
    num = jnp.dot(s.astype(BF16), v, preferred_element_type=F32)
    den = jnp.sum(s, -1, keepdims=True)
    if state is not None:
        w_inter = jnp.exp(inter - m_row) * scale
        num = num + w_inter * jnp.dot(q, c_prev.astype(BF16), preferred_element_type=F32)
        den = den + w_inter * jnp.sum(q.astype(F32) * n_prev, -1, keepdims=True)
    h = num * (1.0 / jnp.maximum(jnp.abs(den), jnp.exp(-m_row)))
    btot = cum_col[lc - 1:lc] if direction == 0 else cum_col[0:1]
    wk_log = btot - cum_col + ig_col
    m_new = jnp.maximum(btot + m_prev, jnp.max(wk_log, 0, keepdims=True))
    kw = k.astype(F32) * jnp.exp(wk_log - m_new)
    c_new = lax.dot_general(kw.astype(BF16), v, (((0,), (0,)), ((), ())), preferred_element_type=F32)
    n_new = jnp.sum(kw, 0, keepdims=True)
    if state is not None:
        decay = jnp.exp(btot + m_prev - m_new)
        c_new = decay * c_prev + c_new
        n_new = decay * n_prev + n_new
    return h, c_new, n_new, m_new


def _head_out(hsum, o, gn):
    return (jax.nn.sigmoid(o) * (_ln(hsum) * gn)).astype(BF16)


def _mlstm_ctx_kernel(q_ref, k_ref, v_ref, o_ref, g_ref, gt_ref, gn_ref,
                      out_ref, cfin_ref, nfin_ref, mfin_ref, *, n_sub, scale):
    head = pl.program_id(1)
    lc = MLSTM_CHUNK
    gn = gn_ref[...]

    def body(s, carry):
        r0 = pl.multiple_of(s * lc, lc)
        rs = pl.ds(r0, lc)
        q, k, v, gc = q_ref[rs, :], k_ref[rs, :], v_ref[rs, :], g_ref[rs, :]
        hsum = None
        for direction in (0, 1):
            h, c_new, n_new, m_new = _mlstm_unit(direction, head, q, k, v, gc, gt_ref, r0, None, scale)
            cfin_ref[s, direction, 0] = c_new
            nfin_ref[s, direction, 0] = n_new
            mfin_ref[s, direction, 0] = jnp.broadcast_to(m_new, (1, GATE_LANES))
            hsum = h if hsum is None else hsum + h
        out_ref[rs, :] = _head_out(hsum, o_ref[rs, :], gn)
        return carry

    lax.fori_loop(0, n_sub, body, 0)


def _mlstm_lat_kernel(q_ref, k_ref, v_ref, o_ref, g_ref, gt_ref, gn_ref, c0_ref, n0_ref, m0_ref,
                      out_ref, c_st, n_st, m_st, h_acc, *, n_chunks, scale):
    head = pl.program_id(1)
    lc = MLSTM_CHUNK
    gn = gn_ref[...]
    for direction in (0, 1):
        c_st[direction] = c0_ref[0, direction, 0]
        n_st[direction] = n0_ref[0, direction, 0]
        m_st[direction] = m0_ref[0, direction, 0]
    h_acc[...] = jnp.zeros(h_acc.shape, F32)

    def body(c, carry):
        for direction in (0, 1):
            cc = c if direction == 0 else n_chunks - 1 - c
            r0 = pl.multiple_of(cc * lc, lc)
            rs = pl.ds(r0, lc)
            state = (c_st[direction], n_st[direction], m_st[direction][:, 0:1])
            h, c_new, n_new, m_new = _mlstm_unit(direction, head, q_ref[rs, :], k_ref[rs, :], v_ref[rs, :],
                                                 g_ref[rs, :], gt_ref, r0, state, scale)
            c_st[direction] = c_new
            n_st[direction] = n_new
            m_st[direction] = jnp.broadcast_to(m_new, (1, GATE_LANES))
            h_acc[rs, :] += h
        return carry

    lax.fori_loop(0, n_chunks, body, 0, unroll=2)

    def out_chunk(c, carry):
        rs = pl.ds(pl.multiple_of(c * lc, lc), lc)
        out_ref[rs, :] = _head_out(h_acc[rs, :], o_ref[rs, :], gn)
        return carry

    lax.fori_loop(0, n_chunks, out_chunk, 0)


def _mlstm_ctx(obf, of32, g, gt, gn_g, *, n_seq):
    n = obf.shape[0]
    ml_dim = gn_g.shape[1]
    hd = ml_dim // ML_HEADS
    lc = MLSTM_CHUNK
    assert n == n_seq * lc
    n_sub = 4
    rows = n_sub * lc
    o_blk = (2 * ml_dim) // hd
    kern = functools.partial(_mlstm_ctx_kernel, n_sub=n_sub, scale=hd ** -0.5)
    return pl.pallas_call(
        kern,
        grid=(n_seq // n_sub, ML_HEADS),
        in_specs=[pl.BlockSpec((rows, hd), lambda s, h: (s, h)),
                  pl.BlockSpec((rows, hd), lambda s, h: (s, ML_HEADS + h)),
                  pl.BlockSpec((rows, hd), lambda s, h: (s, 2 * ML_HEADS + h)),
                  pl.BlockSpec((rows, hd), lambda s, h: (s, o_blk + h)),
                  pl.BlockSpec((rows, GATE_LANES), lambda s, h: (s, 0)),
                  pl.BlockSpec((4 * ML_HEADS, rows), lambda s, h: (0, s)),
                  pl.BlockSpec((1, hd), lambda s, h: (0, h))],
        out_specs=[pl.BlockSpec((rows, hd), lambda s, h: (s, h)),
                   pl.BlockSpec((n_sub, 2, 1, hd, hd), lambda s, h: (s, 0, h, 0, 0)),
                   pl.BlockSpec((n_sub, 2, 1, 1, hd), lambda s, h: (s, 0, h, 0, 0)),
                   pl.BlockSpec((n_sub, 2, 1, 1, GATE_LANES), lambda s, h: (s, 0, h, 0, 0))],
        out_shape=[jax.ShapeDtypeStruct((n, ml_dim), BF16),
                   jax.ShapeDtypeStruct((n_seq, 2, ML_HEADS, hd, hd), F32),
                   jax.ShapeDtypeStruct((n_seq, 2, ML_HEADS, 1, hd), F32),
                   jax.ShapeDtypeStruct((n_seq, 2, ML_HEADS, 1, GATE_LANES), F32)],
        compiler_params=_cparams(2),
        name="mlstm_ctx",
    )(obf, obf, obf, of32, g, gt, gn_g)


def _mlstm_lat(obf, of32, g, gt, gn_g, c0, n0, m0, *, n_seq):
    n = obf.shape[0]
    ml_dim = gn_g.shape[1]
    hd = ml_dim // ML_HEADS
    lc = MLSTM_CHUNK
    rows = n // n_seq
    n_chunks = rows // lc
    o_blk = (2 * ml_dim) // hd
    kern = functools.partial(_mlstm_lat_kernel, n_chunks=n_chunks, scale=hd ** -0.5)
    return pl.pallas_call(
        kern,
        grid=(n_seq, ML_HEADS),
        in_specs=[pl.BlockSpec((rows, hd), lambda s, h: (s, h)),
                  pl.BlockSpec((rows, hd), lambda s, h: (s, ML_HEADS + h)),
                  pl.BlockSpec((rows, hd), lambda s, h: (s, 2 * ML_HEADS + h)),
                  pl.BlockSpec((rows, hd), lambda s, h: (s, o_blk + h)),
                  pl.BlockSpec((rows, GATE_LANES), lambda s, h: (s, 0)),
                  pl.BlockSpec((4 * ML_HEADS, rows), lambda s, h: (0, s)),
                  pl.BlockSpec((1, hd), lambda s, h: (0, h)),
                  pl.BlockSpec((1, 2, 1, hd, hd), lambda s, h: (s, 0, h, 0, 0)),
                  pl.BlockSpec((1, 2, 1, 1, hd), lambda s, h: (s, 0, h, 0, 0)),
                  pl.BlockSpec((1, 2, 1, 1, GATE_LANES), lambda s, h: (s, 0, h, 0, 0))],
        out_specs=pl.BlockSpec((rows, hd), lambda s, h: (s, h)),
        out_shape=jax.ShapeDtypeStruct((n, ml_dim), BF16),
        scratch_shapes=[pltpu.VMEM((2, hd, hd), F32),
                        pltpu.VMEM((2, 1, hd), F32),
                        pltpu.VMEM((2, 1, GATE_LANES), F32),
                        pltpu.VMEM((rows, hd), F32)],
        compiler_params=_cparams(2),
        name="mlstm_lat",
    )(obf, obf, obf, of32, g, gt, gn_g, c0, n0, m0)


def _merge_kernel(hrg_ref, hml_ref, gma_ref, gmb_ref, x_ref, mod_ref, wrg_ref, wml_ref, wout_ref,
                  bm_ref, lng_ref, lnb_ref, o_ref, *, d, alpha, row_base, tiles_per_row):
    r = _mod_row(pl.program_id(0), row_base, tiles_per_row)
    gate1 = mod_ref[pl.ds(r, 1), 2 * d:3 * d]
    y_rg = jnp.dot(hrg_ref[...], wrg_ref[...], preferred_element_type=F32)
    y_ml = jnp.dot(hml_ref[...], wml_ref[...], preferred_element_type=F32)
    g_rg = jax.nn.sigmoid(gma_ref[...] + bm_ref[:, 0:d])
    g_ml = jax.nn.sigmoid(gmb_ref[...] + bm_ref[:, d:2 * d])
    merged = (g_rg * y_rg + g_ml * y_ml).astype(BF16)
    mix = jnp.dot(merged, wout_ref[...], preferred_element_type=F32)
    o_ref[...] = _ln(alpha * x_ref[...] + gate1 * mix) * lng_ref[...] + lnb_ref[...]


def _merge(hrg, hml, of32, x, mod, w_rg, w_ml, w_out, b_merge, ln_g, ln_b, *, alpha, row_base,
           rows_per_mod, tm):
    n, d = x.shape
    tiles_per_row = _tiles_per_row(rows_per_mod, tm)
    kern = functools.partial(_merge_kernel, d=d, alpha=alpha, row_base=row_base, tiles_per_row=tiles_per_row)
    full = lambda shape: pl.BlockSpec(shape, lambda i: (0,) * len(shape))
    return pl.pallas_call(
        kern,
        grid=(n // tm,),
        in_specs=[pl.BlockSpec((tm, d), lambda i: (i, 0)),
                  pl.BlockSpec((tm, d), lambda i: (i, 0)),
                  pl.BlockSpec((tm, d), lambda i: (i, 3)),
                  pl.BlockSpec((tm, d), lambda i: (i, 4)),
                  pl.BlockSpec((tm, d), lambda i: (i, 0)),
                  full(mod.shape), full((d, d)), full((d, d)), full((d, d)),
                  full((1, 2 * d)), full((1, d)), full((1, d))],
        out_specs=pl.BlockSpec((tm, d), lambda i: (i, 0)),
        out_shape=jax.ShapeDtypeStruct((n, d), F32),
        compiler_params=_cparams(1),
        name="merge",
    )(hrg, hml, of32, of32, x, mod, w_rg, w_ml, w_out, b_merge, ln_g, ln_b)


def _mlp_kernel(x_ref, mod_ref, wfc_ref, bfc_ref, wpj_ref, bpj_ref, lng_ref, lnb_ref, o_ref,
                u_ref, acc_ref, *, d, alpha, row_base, tiles_per_row):
    f = pl.program_id(1)
    r = _mod_row(pl.program_id(0), row_base, tiles_per_row)

    @pl.when(f == 0)
    def _():
        shift = mod_ref[pl.ds(r, 1), 3 * d:4 * d]
        scale = mod_ref[pl.ds(r, 1), 4 * d:5 * d]
        u_ref[...] = (_ln(x_ref[...]) * (1.0 + scale) + shift).astype(BF16)
        acc_ref[...] = jnp.zeros(acc_ref.shape, F32)

    hid = jnp.dot(u_ref[...], wfc_ref[...], preferred_element_type=F32) + bfc_ref[...]
    hid = jnp.square(jnp.maximum(hid, 0.0)).astype(BF16)
    acc_ref[...] += jnp.dot(hid, wpj_ref[...], preferred_element_type=F32)

    @pl.when(f == pl.num_programs(1) - 1)
    def _():
        gate2 = mod_ref[pl.ds(r, 1), 5 * d:6 * d]
        y = alpha * x_ref[...] + gate2 * (acc_ref[...] + bpj_ref[...])
        o_ref[...] = _ln(y) * lng_ref[...] + lnb_ref[...]


def _mlp(x, mod, w_fc, b_fc, w_proj, b_proj, ln_g, ln_b, *, alpha, row_base, rows_per_mod, tm):
    n, d = x.shape
    tiles_per_row = _tiles_per_row(rows_per_mod, tm)
    d_ff = w_fc.shape[1]
    tf = 1024
    kern = functools.partial(_mlp_kernel, d=d, alpha=alpha, row_base=row_base, tiles_per_row=tiles_per_row)
    return pl.pallas_call(
        kern,
        grid=(n // tm, d_ff // tf),
        in_specs=[pl.BlockSpec((tm, d), lambda i, f: (i, 0)),
                  pl.BlockSpec(mod.shape, lambda i, f: (0, 0)),
                  pl.BlockSpec((d, tf), lambda i, f: (0, f)),
                  pl.BlockSpec((1, tf), lambda i, f: (0, f)),
                  pl.BlockSpec((tf, d), lambda i, f: (f, 0)),
                  pl.BlockSpec((1, d), lambda i, f: (0, 0)),
                  pl.BlockSpec((1, d), lambda i, f: (0, 0)),
                  pl.BlockSpec((1, d), lambda i, f: (0, 0))],
        out_specs=pl.BlockSpec((tm, d), lambda i, f: (i, 0)),
        out_shape=jax.ShapeDtypeStruct((n, d), F32),
        scratch_shapes=[pltpu.VMEM((tm, d), BF16), pltpu.VMEM((tm, d), F32)],
        compiler_params=_cparams(2),
        name="mlp",
    )(x, mod, w_fc, b_fc, w_proj, b_proj, ln_g, ln_b)


def _rg_gate_weights(wa, ba, wx, bx, lam):
    ct = RG_CT
    d_rnn = ba.shape[-1]
    n_ct = d_rnn // ct
    per = ct // RG_BW

    def tile_blockdiag(w):
        wt = w.reshape(n_ct, per, RG_BW, RG_BW)
        eye = jnp.eye(per, dtype=w.dtype)
        return jnp.einsum('cpkj,pq->cpkqj', wt, eye).reshape(n_ct, ct, ct)

    w_cat = jnp.concatenate([tile_blockdiag(wa[0]), tile_blockdiag(wx[0]),
                             tile_blockdiag(wa[1]), tile_blockdiag(wx[1])], axis=-1).astype(BF16)
    tiles = lambda b: b.reshape(n_ct, 1, ct)
    b_cat = jnp.concatenate([tiles(ba[0]), tiles(bx[0]), tiles(ba[1]), tiles(bx[1])], axis=-1)
    lam_cat = jnp.concatenate([tiles(lam[0]), tiles(lam[1])], axis=-1)
    return w_cat, b_cat, lam_cat


def kernel(x_prompt, x_sample, state_rglru_h, state_mlstm_C, state_mlstm_n, state_mlstm_m, c, c_ctx, w_ada, b_ada, w_in, rg_conv_w, rg_conv_b, rg_wa, rg_ba, rg_wx, rg_bx, rg_lambda, w_rg_proj, ml_b_igate, ml_b_fgate, ml_gn_g, w_ml_proj, b_merge, w_out, ln_g, ln_b, w_fc, b_fc, w_proj, b_proj):
    bp, seq, d = x_prompt.shape
    bl, dec_seq, _ = x_sample.shape
    depth = w_in.shape[0]
    d_rnn = rg_conv_w.shape[-1]
    ml_dim = ml_gn_g.shape[-1]
    hd = ml_dim // ML_HEADS
    alpha = (2 * depth) ** 0.25
    d_main = 2 * d_rnn + 4 * ml_dim + 2 * d
    assert seq == MLSTM_CHUNK == SCAN_SEG and dec_seq % MLSTM_CHUNK == 0
    assert d_rnn == d and ml_dim == d and w_in.shape[-1] == d_main + 4 * ML_HEADS

    tm = 1024
    xp = x_prompt.reshape(bp * seq, d)
    xs = x_sample.reshape(bl * dec_seq, d)
    cond = jnp.concatenate([c_ctx[None, :], c], axis=0)
    zero_h = jnp.zeros((2, bp, d_rnn), F32)
    new_h, new_c, new_n, new_m = [], [], [], []
    for l in range(depth):
        mod = _ada(cond, w_ada[l], b_ada[l])
        w_main = w_in[l][:, :d_main].astype(BF16)
        w_gate = jnp.zeros((d, GATE_LANES), BF16).at[:, :4 * ML_HEADS].set(w_in[l][:, d_main:].astype(BF16))
        b_gate = jnp.zeros((1, GATE_LANES), F32).at[0, :4 * ML_HEADS].set(
            jnp.concatenate([ml_b_igate[l].reshape(-1), ml_b_fgate[l].reshape(-1)]))
        w_cat, b_cat, lam_cat = _rg_gate_weights(rg_wa[l], rg_ba[l], rg_wx[l], rg_bx[l], rg_lambda[l])
        conv_b = rg_conv_b[l].reshape(1, d_rnn)
        gn_g = ml_gn_g[l].reshape(1, ml_dim)
        w_rg, w_ml, w_o = (w_rg_proj[l].astype(BF16), w_ml_proj[l].astype(BF16), w_out[l].astype(BF16))
        w_fc_b, w_pj_b = w_fc[l].astype(BF16), w_proj[l].astype(BF16)
        bm = b_merge[l].reshape(1, 2 * d)
        lng0, lnb0 = ln_g[l, 0].reshape(1, d), ln_b[l, 0].reshape(1, d)
        lng1, lnb1 = ln_g[l, 1].reshape(1, d), ln_b[l, 1].reshape(1, d)
        bfc, bpj = b_fc[l].reshape(1, -1), b_proj[l].reshape(1, d)

        def tail(x, hrg, hml, of32, row_base, rows_per_mod):
            x1 = _merge(hrg, hml, of32, x, mod, w_rg, w_ml, w_o, bm, lng0, lnb0, alpha=alpha,
                        row_base=row_base, rows_per_mod=rows_per_mod, tm=512)
            return _mlp(x1, mod, w_fc_b, bfc, w_pj_b, bpj, lng1, lnb1, alpha=alpha,
                        row_base=row_base, rows_per_mod=rows_per_mod, tm=tm)

        of32, obf, g, gt = _in_proj(xp, mod, w_main, w_gate, b_gate, row_base=0, rows_per_mod=None, tm=tm)
        hrg, h_fin = _rglru(of32, w_cat, b_cat, lam_cat, rg_conv_w[l], conv_b, zero_h,
                            n_seq=bp, seq_len=seq, conv_len=seq, want_final=True)
        hml, c_fin, n_fin, m_fin = _mlstm_ctx(obf, of32, g, gt, gn_g, n_seq=bp)
        xp = tail(xp, hrg, hml, of32, 0, None)
        new_h.append(jnp.transpose(h_fin, (1, 0, 2)))
        new_c.append(c_fin)
        new_n.append(n_fin.reshape(bp, 2, ML_HEADS, hd))
        new_m.append(m_fin[..., 0, 0])

        of32, obf, g, gt = _in_proj(xs, mod, w_main, w_gate, b_gate, row_base=1,
                                    rows_per_mod=dec_seq, tm=tm)
        h0 = jnp.transpose(state_rglru_h[:, l], (1, 0, 2))
        (hrg,) = _rglru(of32, w_cat, b_cat, lam_cat, rg_conv_w[l], conv_b, h0,
                        n_seq=bl, seq_len=dec_seq, conv_len=GRID_W, want_final=False)
        c0 = state_mlstm_C[:, l]
        n0 = state_mlstm_n[:, l].reshape(bl, 2, ML_HEADS, 1, hd)
        m0 = jnp.broadcast_to(state_mlstm_m[:, l].reshape(bl, 2, ML_HEADS, 1, 1),
                              (bl, 2, ML_HEADS, 1, GATE_LANES))
        hml = _mlstm_lat(obf, of32, g, gt, gn_g, c0, n0, m0, n_seq=bl)
        xs = tail(xs, hrg, hml, of32, 1, dec_seq)

    def stack(parts):
        return parts[0][:, None] if len(parts) == 1 else jnp.stack(parts, axis=1)

    return (xp.reshape(bp, seq, d), xs.reshape(bl, dec_seq, d),
            stack(new_h), stack(new_c), stack(new_n), stack(new_m))
```

```python
import functools

import jax
import jax.numpy as jnp
from jax import lax
from jax.experimental import pallas as pl
from jax.experimental.pallas import tpu as pltpu

F32 = jnp.float32
BF16 = jnp.bfloat16

LN_EPS = 1e-5
LOG2E = 1.4426950408889634
RG_C = 8.0
RG_BW = 64
CONV_W = 4
GRID_W = 64
ML_HEADS = 4
MLSTM_CHUNK = 256
SCAN_SEG = 256
SUBLANES = 8
SCAN_PITCH = SCAN_SEG + SUBLANES
RG_CT = 128
GATE_LANES = 128
VMEM_LIMIT = 52 * 1024 * 1024


def _cparams(n_axes):
    return pltpu.CompilerParams(
        dimension_semantics=("arbitrary",) * n_axes, vmem_limit_bytes=VMEM_LIMIT)


def _log_sigmoid(x):
    return jnp.minimum(x, 0.0) - jnp.log1p(jnp.exp(-jnp.abs(x)))


def _ln(x):
    mu = jnp.mean(x, -1, keepdims=True)
    xc = x - mu
    var = jnp.mean(xc * xc, -1, keepdims=True)
    return xc * lax.rsqrt(var + LN_EPS)


def _mod_row(i, row_base, tiles_per_row):
    if tiles_per_row is None:
        return row_base
    return row_base + lax.div(i, jnp.int32(tiles_per_row))


def _ada_kernel(ct_ref, w_ref, b_ref, o_ref, *, n_rows):
    w = w_ref[...]
    ct = ct_ref[...]
    s = ct * jax.nn.sigmoid(ct)
    o_ref[...] = jnp.zeros(o_ref.shape, F32)
    for r in range(n_rows):
        o_ref[r:r + 1, :] = jnp.sum(w * s[:, r:r + 1], axis=0, keepdims=True) + b_ref[...]


def _ada(cond, w_ada, b_ada):
    n_rows, d = cond.shape
    assert n_rows <= 8
    ct = jnp.zeros((d, 8), F32).at[:, :n_rows].set(cond.T)
    n_out = w_ada.shape[1]
    tn = 512
    return pl.pallas_call(
        functools.partial(_ada_kernel, n_rows=n_rows),
        grid=(n_out // tn,),
        in_specs=[pl.BlockSpec((d, 8), lambda j: (0, 0)),
                  pl.BlockSpec((d, tn), lambda j: (0, j)),
                  pl.BlockSpec((1, tn), lambda j: (0, j))],
        out_specs=pl.BlockSpec((8, tn), lambda j: (0, j)),
        out_shape=jax.ShapeDtypeStruct((8, n_out), F32),
        compiler_params=_cparams(1),
        name="ada_mod",
    )(ct, w_ada, b_ada.reshape(1, n_out))


def _inproj_kernel(x_ref, mod_ref, w_ref, wg_ref, bg_ref, of_ref, ob_ref, g_ref, gt_ref, u_ref,
                   *, d, tm, row_base, tiles_per_row):
    i = pl.program_id(0)
    j = pl.program_id(1)
    lc = MLSTM_CHUNK

    @pl.when(j == 0)
    def _():
        r = _mod_row(i, row_base, tiles_per_row)
        shift = mod_ref[pl.ds(r, 1), 0:d]
        scale = mod_ref[pl.ds(r, 1), d:2 * d]
        u_ref[...] = (_ln(x_ref[...]) * (1.0 + scale) + shift).astype(BF16)
        g = jnp.dot(u_ref[...], wg_ref[...], preferred_element_type=F32) + bg_ref[...]
        lane = lax.broadcasted_iota(jnp.int32, g.shape, 1)
        g = jnp.where(lane >= 2 * ML_HEADS, _log_sigmoid(g), g)
        ii = lax.broadcasted_iota(jnp.int32, (lc, lc), 0)
        jj = lax.broadcasted_iota(jnp.int32, (lc, lc), 1)
        tril = (jj <= ii).astype(BF16)
        triu = (jj >= ii).astype(BF16)
        lane_c = lax.broadcasted_iota(jnp.int32, (lc, GATE_LANES), 1)
        for c in range(tm // lc):
            gc = g[c * lc:(c + 1) * lc]
            hi = gc.astype(BF16)
            r1 = gc - hi.astype(F32)
            mid = r1.astype(BF16)
            lo = (r1 - mid.astype(F32)).astype(BF16)
            cum_f = (jnp.dot(tril, hi, preferred_element_type=F32)
                     + jnp.dot(tril, mid, preferred_element_type=F32)
                     + jnp.dot(tril, lo, preferred_element_type=F32))
            cum_b = (jnp.dot(triu, hi, preferred_element_type=F32)
                     + jnp.dot(triu, mid, preferred_element_type=F32)
                     + jnp.dot(triu, lo, preferred_element_type=F32))
            gfin = jnp.where(lane_c < 2 * ML_HEADS, gc,
                             jnp.where(lane_c < 3 * ML_HEADS, cum_f, cum_b))
            g_ref[c * lc:(c + 1) * lc, :] = gfin
            gt_ref[:, c * lc:(c + 1) * lc] = gfin.T[0:4 * ML_HEADS, :]

    is_bf = jnp.logical_and(j >= 2, j <= 4)

    @pl.when(is_bf)
    def _():
        ob_ref[...] = jnp.dot(u_ref[...], w_ref[...], preferred_element_type=F32).astype(BF16)

    @pl.when(jnp.logical_not(is_bf))
    def _():
        of_ref[...] = jnp.dot(u_ref[...], w_ref[...], preferred_element_type=F32)


def _tiles_per_row(rows_per_mod, tm):
    if rows_per_mod is None:
        return None
    assert rows_per_mod % tm == 0
    return rows_per_mod // tm


def _in_proj(x, mod, w_main, w_gate, b_gate, *, row_base, rows_per_mod, tm):
    n, d = x.shape
    tiles_per_row = _tiles_per_row(rows_per_mod, tm)
    tn = d
    n_j = w_main.shape[1] // tn
    assert n_j == 8
    kern = functools.partial(_inproj_kernel, d=d, tm=tm, row_base=row_base, tiles_per_row=tiles_per_row)
    return pl.pallas_call(
        kern,
        grid=(n // tm, n_j),
        in_specs=[pl.BlockSpec((tm, d), lambda i, j: (i, 0)),
                  pl.BlockSpec(mod.shape, lambda i, j: (0, 0)),
                  pl.BlockSpec((d, tn), lambda i, j: (0, j)),
                  pl.BlockSpec((d, GATE_LANES), lambda i, j: (0, 0)),
                  pl.BlockSpec((1, GATE_LANES), lambda i, j: (0, 0))],
        out_specs=[
            pl.BlockSpec((tm, tn), lambda i, j: (i, jnp.minimum(j, 1) + jnp.maximum(j - 4, 0))),
            pl.BlockSpec((tm, tn), lambda i, j: (i, jnp.clip(j - 2, 0, 2))),
            pl.BlockSpec((tm, GATE_LANES), lambda i, j: (i, 0)),
            pl.BlockSpec((4 * ML_HEADS, tm), lambda i, j: (0, i))],
        out_shape=[jax.ShapeDtypeStruct((n, 5 * d), F32),
                   jax.ShapeDtypeStruct((n, 3 * d), BF16),
                   jax.ShapeDtypeStruct((n, GATE_LANES), F32),
                   jax.ShapeDtypeStruct((4 * ML_HEADS, n), F32)],
        scratch_shapes=[pltpu.VMEM((tm, d), BF16)],
        compiler_params=_cparams(2),
        name="in_proj",
    )(x, mod, w_main, w_gate, b_gate)


def _rglru_kernel(xr_ref, zr_ref, w_ref, b_ref, lam_ref, cw_ref, cb_ref, h0_ref, *rest,
                  n_vseq, conv_len, n_seg, want_final):
    if want_final:
        out_ref, hfin_ref = rest[:2]
        rest = rest[2:]
    else:
        out_ref = rest[0]
        rest = rest[1:]
    a_f, b_f, a_b, b_b, cwm_ref = rest[:5]
    rest = rest[5:]
    seg = SCAN_SEG
    pitch = SCAN_PITCH
    ct = xr_ref.shape[1]
    ch = seg
    la = (RG_C * LOG2E) * _log_sigmoid(lam_ref[0])
    la_f, la_b = la[:, :ct], la[:, ct:]
    w = w_ref[0]
    bias = b_ref[0]
    cw = cw_ref[...]
    cb = cb_ref[...]
    pos = jnp.bitwise_and(lax.broadcasted_iota(jnp.int32, (ch, ct), 0), conv_len - 1)
    cwm_ref[0] = jnp.where(pos >= 2, cw[0:1], 0.0)
    cwm_ref[1] = jnp.where(pos >= 1, cw[1:2], 0.0)
    cwm_ref[2] = jnp.where(pos <= conv_len - 2, cw[3:4], 0.0)

    def gate_chunk(ci, carry):
        r0 = pl.multiple_of(ci * ch, ch)
        x = xr_ref[pl.ds(r0, ch), :]
        xc = (cwm_ref[0] * pltpu.roll(x, 2, 0) + cwm_ref[1] * pltpu.roll(x, 1, 0) + cw[2:3] * x
              + cwm_ref[2] * pltpu.roll(x, ch - 1, 0) + cb)
        gates = jnp.dot(xc.astype(BF16), w, preferred_element_type=F32) + bias

        def a_and_b(r_pre, i_pre, la_dir):
            a = jnp.exp2(la_dir / (1.0 + jnp.exp2(r_pre * (-LOG2E))))
            y = jnp.clip(1.0 - a * a, 0.0, 1.0)
            gain = jnp.where(y > 0.0, y * lax.rsqrt(y), 0.0)
            return a, (gain * xc) / (1.0 + jnp.exp2(i_pre * (-LOG2E)))

        af, bf = a_and_b(gates[:, 0:ct], gates[:, ct:2 * ct], la_f)
        ab, bb = a_and_b(gates[:, 2 * ct:3 * ct], gates[:, 3 * ct:4 * ct], la_b)
        s0 = pl.ds(pl.multiple_of(ci * pitch, SUBLANES), ch)
        a_f[s0, :] = af
        b_f[s0, :] = bf
        a_b[s0, :] = ab
        b_b[s0, :] = bb
        return carry

    lax.fori_loop(0, n_vseq, gate_chunk, 0, unroll=2)

    def slab_idx(t):
        return pl.ds(t, n_vseq, stride=pitch)

    def slab(ref, t):
        return ref[slab_idx(t), :]

    if n_seg == 1:
        init_f = h0_ref[0]
        init_b = h0_ref[1]
    else:
        e_f, p_f, e_b, p_b, cin_f, cin_b = rest
        n_real = n_vseq // n_seg
        zero = jnp.zeros((n_vseq, ct), F32)
        one = jnp.ones((n_vseq, ct), F32)

        def local_step(t, carry):
            ef, pf, eb, pb = carry
            tb = seg - 1 - t
            a = slab(a_f, t)
            ef = a * ef + slab(b_f, t)
            pf = a * pf
            a2 = slab(a_b, tb)
            eb = a2 * eb + slab(b_b, tb)
            pb = a2 * pb
            return ef, pf, eb, pb

        ef, pf, eb, pb = lax.fori_loop(0, seg, local_step, (zero, one, zero, one), unroll=8)
        e_f[...] = ef
        p_f[...] = pf
        e_b[...] = eb
        p_b[...] = pb
        carry = h0_ref[0]
        for s in range(n_seg):
            idx = pl.ds(s, n_real, stride=n_seg)
            cin_f[idx, :] = carry
            carry = e_f[idx, :] + p_f[idx, :] * carry
        carry = h0_ref[1]
        for s in reversed(range(n_seg)):
            idx = pl.ds(s, n_real, stride=n_seg)
            cin_b[idx, :] = carry
            carry = e_b[idx, :] + p_b[idx, :] * carry
        init_f = cin_f[...]
        init_b = cin_b[...]

    def scan_step(t, carry):
        hf, hb = carry
        tb = seg - 1 - t
        hf = slab(a_f, t) * hf + slab(b_f, t)
        b_f[slab_idx(t), :] = hf
        hb = slab(a_b, tb) * hb + slab(b_b, tb)
        b_b[slab_idx(tb), :] = hb
        return hf, hb

    hf_last, hb_last = lax.fori_loop(0, seg, scan_step, (init_f, init_b), unroll=8)
    if want_final:
        hfin_ref[0] = hf_last
        hfin_ref[1] = hb_last

    def out_chunk(ci, carry):
        r0 = pl.multiple_of(ci * ch, ch)
        s0 = pl.ds(pl.multiple_of(ci * pitch, SUBLANES), ch)
        h = b_f[s0, :] + b_b[s0, :]
        out_ref[pl.ds(r0, ch), :] = (h * jax.nn.gelu(zr_ref[pl.ds(r0, ch), :])).astype(BF16)
        return carry

    lax.fori_loop(0, n_vseq, out_chunk, 0)


def _rglru(of32, w_cat, b_cat, lam_cat, conv_w, conv_b, h0, *, n_seq, seq_len, conv_len, want_final):
    n = of32.shape[0]
    d_rnn = conv_w.shape[1]
    ct = RG_CT
    n_ct = d_rnn // ct
    n_seg = seq_len // SCAN_SEG
    n_vseq = n_seq * n_seg
    assert n_vseq * SCAN_SEG == n and SCAN_SEG % conv_len == 0 and (conv_len & (conv_len - 1)) == 0
    kern = functools.partial(_rglru_kernel, n_vseq=n_vseq, conv_len=conv_len, n_seg=n_seg,
                             want_final=want_final)
    out_specs = [pl.BlockSpec((n, ct), lambda c: (0, c))]
    out_shape = [jax.ShapeDtypeStruct((n, d_rnn), BF16)]
    if want_final:
        out_specs.append(pl.BlockSpec((2, n_seq, ct), lambda c: (0, 0, c)))
        out_shape.append(jax.ShapeDtypeStruct((2, n_seq, d_rnn), F32))
    scratch = [pltpu.VMEM((n_vseq * SCAN_PITCH, ct), F32) for _ in range(4)]
    scratch.append(pltpu.VMEM((CONV_W - 1, SCAN_SEG, ct), F32))
    if n_seg > 1:
        scratch += [pltpu.VMEM((n_vseq, ct), F32) for _ in range(6)]
    return pl.pallas_call(
        kern,
        grid=(n_ct,),
        in_specs=[pl.BlockSpec((n, ct), lambda c: (0, c)),
                  pl.BlockSpec((n, ct), lambda c: (0, n_ct + c)),
                  pl.BlockSpec((1, ct, 4 * ct), lambda c: (c, 0, 0)),
                  pl.BlockSpec((1, 1, 4 * ct), lambda c: (c, 0, 0)),
                  pl.BlockSpec((1, 1, 2 * ct), lambda c: (c, 0, 0)),
                  pl.BlockSpec((CONV_W, ct), lambda c: (0, c)),
                  pl.BlockSpec((1, ct), lambda c: (0, c)),
                  pl.BlockSpec((2, n_seq, ct), lambda c: (0, 0, c))],
        out_specs=out_specs,
        out_shape=out_shape,
        scratch_shapes=scratch,
        compiler_params=_cparams(1),
        name="rglru_ctx" if want_final else "rglru_lat",
    )(of32, of32, w_cat, b_cat, lam_cat, conv_w, conv_b, h0)


def _causal_bias(mask_ref):
    lc = mask_ref.shape[1]
    ii = lax.broadcasted_iota(jnp.int32, (lc, lc), 0)
    jj = lax.broadcasted_iota(jnp.int32, (lc, lc), 1)
    mask_ref[0] = jnp.where(jj <= ii, 0.0, -jnp.inf)
    mask_ref[1] = jnp.where(jj >= ii, 0.0, -jnp.inf)


def _with_ones(v):
    return jnp.concatenate([v, jnp.ones((v.shape[0], GATE_LANES), v.dtype)], axis=1)


def _mlstm_unit(direction, head, q, k, v, v1, gc, gt_ref, r0, mask_ref, state, scale):
    lc, hd = q.shape
    col_i = direction * ML_HEADS + head
    col_c = 2 * ML_HEADS + col_i
    lane = lax.broadcasted_iota(jnp.int32, (1, GATE_LANES), 1)
    ig_col = jnp.sum(jnp.where(lane == col_i, gc, 0.0), -1, keepdims=True)
    cum_col = jnp.sum(jnp.where(lane == col_c, gc, 0.0), -1, keepdims=True)
    ig_row = gt_ref[pl.ds(col_i, 1), pl.ds(r0, lc)]
    cum_row = gt_ref[pl.ds(col_c, 1), pl.ds(r0, lc)]
    dmat = (cum_col + (ig_row - cum_row)) + mask_ref[direction]
    if state is None:
        m_prev = 0.0
    else:
        s_prev, m_prev = state
    inter = cum_col + m_prev
    m_row = jnp.maximum(jnp.max(dmat, -1, keepdims=True), inter)
    wts = jnp.exp(dmat - m_row)
    s = lax.dot_general(q, k, (((1,), (1,)), ((), ())), preferred_element_type=F32) * (wts * scale)
    nd = jnp.dot(s.astype(BF16), v1, preferred_element_type=F32)
    if state is not None:
        w_inter = jnp.exp(inter - m_row) * scale
        nd = nd + w_inter * jnp.dot(q, s_prev.astype(BF16), preferred_element_type=F32)
    rden = 1.0 / jnp.maximum(jnp.abs(nd[:, hd:]), jnp.exp(-m_row))
    h = nd[:, :hd] * jnp.concatenate([rden] * (hd // GATE_LANES), axis=1)
    btot = cum_col[lc - 1:lc] if direction == 0 else cum_col[0:1]
    wk_log = btot - cum_col + ig_col
    m_new = jnp.maximum(btot + m_prev, jnp.max(wk_log, 0, keepdims=True))
    kw = k.astype(F32) * jnp.exp(wk_log - m_new)
    t_lhs = (((0,), (0,)), ((), ()))
    if state is None:
        c_new = lax.dot_general(kw.astype(BF16), v, t_lhs, preferred_element_type=F32)
        return h, c_new, jnp.sum(kw, 0, keepdims=True), m_new
    decay = jnp.exp(btot + m_prev - m_new)
    s_new = decay * s_prev + lax.dot_general(kw.astype(BF16), v1, t_lhs, preferred_element_type=F32)
    return h, s_new, None, m_new


def _head_out(hsum, o, gn):
    return (jax.nn.sigmoid(o) * (_ln(hsum) * gn)).astype(BF16)


def _mlstm_ctx_kernel(q_ref, k_ref, v_ref, o_ref, g_ref, gt_ref, gn_ref,
                      out_ref, cfin_ref, nfin_ref, mfin_ref, mask_ref, *, n_sub, scale):
    head = pl.program_id(1)
    lc = MLSTM_CHUNK
    gn = gn_ref[...]
    _causal_bias(mask_ref)

    def body(s, carry):
        r0 = pl.multiple_of(s * lc, lc)
        rs = pl.ds(r0, lc)
        q, k, v, gc = q_ref[rs, :], k_ref[rs, :], v_ref[rs, :], g_ref[rs, :]
        v1 = _with_ones(v)
        hsum = None
        for direction in (0, 1):
            h, c_new, n_new, m_new = _mlstm_unit(direction, head, q, k, v, v1, gc, gt_ref, r0, mask_ref,
                                                 None, scale)
            cfin_ref[s, direction, 0] = c_new
            nfin_ref[s, direction, 0] = n_new
            mfin_ref[s, direction, 0] = jnp.broadcast_to(m_new, (1, GATE_LANES))
            hsum = h if hsum is None else hsum + h
        out_ref[rs, :] = _head_out(hsum, o_ref[rs, :], gn)
        return carry

    lax.fori_loop(0, n_sub, body, 0)


def _mlstm_lat_kernel(q_ref, k_ref, v_ref, o_ref, g_ref, gt_ref, gn_ref, c0_ref, n0_ref, m0_ref,
                      out_ref, s_st, m_st, h_acc, mask_ref, *, n_chunks, scale):
    head = pl.program_id(1)
    lc = MLSTM_CHUNK
    hd = q_ref.shape[1]
    gn = gn_ref[...]
    _causal_bias(mask_ref)
    for direction in (0, 1):
        s_st[direction, :, 0:hd] = c0_ref[0, direction, 0]
        s_st[direction, :, hd:] = n0_ref[0, direction, 0]
        m_st[direction] = m0_ref[0, direction, 0]
    h_acc[...] = jnp.zeros(h_acc.shape, F32)

    def body(c, carry):
        for direction in (0, 1):
            cc = c if direction == 0 else n_chunks - 1 - c
            r0 = pl.multiple_of(cc * lc, lc)
            rs = pl.ds(r0, lc)
            state = (s_st[direction], m_st[direction][:, 0:1])
            v = v_ref[rs, :]
            h, s_new, _, m_new = _mlstm_unit(direction, head, q_ref[rs, :], k_ref[rs, :], v, _with_ones(v),
                                             g_ref[rs, :], gt_ref, r0, mask_ref, state, scale)
            s_st[direction] = s_new
            m_st[direction] = jnp.broadcast_to(m_new, (1, GATE_LANES))
            h_acc[rs, :] += h
        return carry

    lax.fori_loop(0, n_chunks, body, 0, unroll=2)

    def out_chunk(c, carry):
        rs = pl.ds(pl.multiple_of(c * lc, lc), lc)
        out_ref[rs, :] = _head_out(h_acc[rs, :], o_ref[rs, :], gn)
        return carry

    lax.fori_loop(0, n_chunks, out_chunk, 0)


def _mlstm_ctx(obf, of32, g, gt, gn_g, *, n_seq):
    n = obf.shape[0]
    ml_dim = gn_g.shape[1]
    hd = ml_dim // ML_HEADS
    lc = MLSTM_CHUNK
    assert n == n_seq * lc
    n_sub = 4
    rows = n_sub * lc
    o_blk = (2 * ml_dim) // hd
    kern = functools.partial(_mlstm_ctx_kernel, n_sub=n_sub, scale=hd ** -0.5)
    return pl.pallas_call(
        kern,
        grid=(n_seq // n_sub, ML_HEADS),
        in_specs=[pl.BlockSpec((rows, hd), lambda s, h: (s, h)),
                  pl.BlockSpec((rows, hd), lambda s, h: (s, ML_HEADS + h)),
                  pl.BlockSpec((rows, hd), lambda s, h: (s, 2 * ML_HEADS + h)),
                  pl.BlockSpec((rows, hd), lambda s, h: (s, o_blk + h)),
                  pl.BlockSpec((rows, GATE_LANES), lambda s, h: (s, 0)),
                  pl.BlockSpec((4 * ML_HEADS, rows), lambda s, h: (0, s)),
                  pl.BlockSpec((1, hd), lambda s, h: (0, h))],
        out_specs=[pl.BlockSpec((rows, hd), lambda s, h: (s, h)),
                   pl.BlockSpec((n_sub, 2, 1, hd, hd), lambda s, h: (s, 0, h, 0, 0)),
                   pl.BlockSpec((n_sub, 2, 1, 1, hd), lambda s, h: (s, 0, h, 0, 0)),
                   pl.BlockSpec((n_sub, 2, 1, 1, GATE_LANES), lambda s, h: (s, 0, h, 0, 0))],
        out_shape=[jax.ShapeDtypeStruct((n, ml_dim), BF16),
                   jax.ShapeDtypeStruct((n_seq, 2, ML_HEADS, hd, hd), F32),
                   jax.ShapeDtypeStruct((n_seq, 2, ML_HEADS, 1, hd), F32),
                   jax.ShapeDtypeStruct((n_seq, 2, ML_HEADS, 1, GATE_LANES), F32)],
        scratch_shapes=[pltpu.VMEM((2, lc, lc), F32)],
        compiler_params=_cparams(2),
        name="mlstm_ctx",
    )(obf, obf, obf, of32, g, gt, gn_g)


def _mlstm_lat(obf, of32, g, gt, gn_g, c0, n0, m0, *, n_seq):
    n = obf.shape[0]
    ml_dim = gn_g.shape[1]
    hd = ml_dim // ML_HEADS
    lc = MLSTM_CHUNK
    rows = n // n_seq
    n_chunks = rows // lc
    o_blk = (2 * ml_dim) // hd
    kern = functools.partial(_mlstm_lat_kernel, n_chunks=n_chunks, scale=hd ** -0.5)
    return pl.pallas_call(
        kern,
        grid=(n_seq, ML_HEADS),
        in_specs=[pl.BlockSpec((rows, hd), lambda s, h: (s, h)),
                  pl.BlockSpec((rows, hd), lambda s, h: (s, ML_HEADS + h)),
                  pl.BlockSpec((rows, hd), lambda s, h: (s, 2 * ML_HEADS + h)),
                  pl.BlockSpec((rows, hd), lambda s, h: (s, o_blk + h)),
                  pl.BlockSpec((rows, GATE_LANES), lambda s, h: (s, 0)),
                  pl.BlockSpec((4 * ML_HEADS, rows), lambda s, h: (0, s)),
                  pl.BlockSpec((1, hd), lambda s, h: (0, h)),
                  pl.BlockSpec((1, 2, 1, hd, hd), lambda s, h: (s, 0, h, 0, 0)),
                  pl.BlockSpec((1, 2, 1, hd, GATE_LANES), lambda s, h: (s, 0, h, 0, 0)),
                  pl.BlockSpec((1, 2, 1, 1, GATE_LANES), lambda s, h: (s, 0, h, 0, 0))],
        out_specs=pl.BlockSpec((rows, hd), lambda s, h: (s, h)),
        out_shape=jax.ShapeDtypeStruct((n, ml_dim), BF16),
        scratch_shapes=[pltpu.VMEM((2, hd, hd + GATE_LANES), F32),
                        pltpu.VMEM((2, 1, GATE_LANES), F32),
                        pltpu.VMEM((rows, hd), F32),
                        pltpu.VMEM((2, lc, lc), F32)],
        compiler_params=_cparams(2),
        name="mlstm_lat",
    )(obf, obf, obf, of32, g, gt, gn_g, c0, n0, m0)


def _merge_kernel(hrg_ref, hml_ref, gma_ref, gmb_ref, x_ref, mod_ref, wrg_ref, wml_ref, wout_ref,
                  bm_ref, lng_ref, lnb_ref, o_ref, *, d, alpha, row_base, tiles_per_row):
    r = _mod_row(pl.program_id(0), row_base, tiles_per_row)
    gate1 = mod_ref[pl.ds(r, 1), 2 * d:3 * d]
    y_rg = jnp.dot(hrg_ref[...], wrg_ref[...], preferred_element_type=F32)
    y_ml = jnp.dot(hml_ref[...], wml_ref[...], preferred_element_type=F32)
    g_rg = jax.nn.sigmoid(gma_ref[...] + bm_ref[:, 0:d])
    g_ml = jax.nn.sigmoid(gmb_ref[...] + bm_ref[:, d:2 * d])
    merged = (g_rg * y_rg + g_ml * y_ml).astype(BF16)
    mix = jnp.dot(merged, wout_ref[...], preferred_element_type=F32)
    o_ref[...] = _ln(alpha * x_ref[...] + gate1 * mix) * lng_ref[...] + lnb_ref[...]


def _merge(hrg, hml, of32, x, mod, w_rg, w_ml, w_out, b_merge, ln_g, ln_b, *, alpha, row_base,
           rows_per_mod, tm):
    n, d = x.shape
    tiles_per_row = _tiles_per_row(rows_per_mod, tm)
    kern = functools.partial(_merge_kernel, d=d, alpha=alpha, row_base=row_base, tiles_per_row=tiles_per_row)
    full = lambda shape: pl.BlockSpec(shape, lambda i: (0,) * len(shape))
    return pl.pallas_call(
        kern,
        grid=(n // tm,),
        in_specs=[pl.BlockSpec((tm, d), lambda i: (i, 0)),
                  pl.BlockSpec((tm, d), lambda i: (i, 0)),
                  pl.BlockSpec((tm, d), lambda i: (i, 3)),
                  pl.BlockSpec((tm, d), lambda i: (i, 4)),
                  pl.BlockSpec((tm, d), lambda i: (i, 0)),
                  full(mod.shape), full((d, d)), full((d, d)), full((d, d)),
                  full((1, 2 * d)), full((1, d)), full((1, d))],
        out_specs=pl.BlockSpec((tm, d), lambda i: (i, 0)),
        out_shape=jax.ShapeDtypeStruct((n, d), F32),
        compiler_params=_cparams(1),
        name="merge",
    )(hrg, hml, of32, of32, x, mod, w_rg, w_ml, w_out, b_merge, ln_g, ln_b)


def _mlp_kernel(x_ref, mod_ref, wfc_ref, bfc_ref, wpj_ref, bpj_ref, lng_ref, lnb_ref, o_ref,
                u_ref, acc_ref, *, d, alpha, row_base, tiles_per_row):
    f = pl.program_id(1)
    r = _mod_row(pl.program_id(0), row_base, tiles_per_row)

    @pl.when(f == 0)
    def _():
        shift = mod_ref[pl.ds(r, 1), 3 * d:4 * d]
        scale = mod_ref[pl.ds(r, 1), 4 * d:5 * d]
        u_ref[...] = (_ln(x_ref[...]) * (1.0 + scale) + shift).astype(BF16)
        acc_ref[...] = jnp.zeros(acc_ref.shape, F32)

    hid = jnp.dot(u_ref[...], wfc_ref[...], preferred_element_type=F32) + bfc_ref[...]
    hid = jnp.square(jnp.maximum(hid, 0.0)).astype(BF16)
    acc_ref[...] += jnp.dot(hid, wpj_ref[...], preferred_element_type=F32)

    @pl.when(f == pl.num_programs(1) - 1)
    def _():
        gate2 = mod_ref[pl.ds(r, 1), 5 * d:6 * d]
        y = alpha * x_ref[...] + gate2 * (acc_ref[...] + bpj_ref[...])
        o_ref[...] = _ln(y) * lng_ref[...] + lnb_ref[...]


def _mlp(x, mod, w_fc, b_fc, w_proj, b_proj, ln_g, ln_b, *, alpha, row_base, rows_per_mod, tm):
    n, d = x.shape
    tiles_per_row = _tiles_per_row(rows_per_mod, tm)
    d_ff = w_fc.shape[1]
    tf = 1024
    kern = functools.partial(_mlp_kernel, d=d, alpha=alpha, row_base=row_base, tiles_per_row=tiles_per_row)
    return pl.pallas_call(
        kern,
        grid=(n // tm, d_ff // tf),
        in_specs=[pl.BlockSpec((tm, d), lambda i, f: (i, 0)),
                  pl.BlockSpec(mod.shape, lambda i, f: (0, 0)),
                  pl.BlockSpec((d, tf), lambda i, f: (0, f)),
                  pl.BlockSpec((1, tf), lambda i, f: (0, f)),
                  pl.BlockSpec((tf, d), lambda i, f: (f, 0)),
                  pl.BlockSpec((1, d), lambda i, f: (0, 0)),
                  pl.BlockSpec((1, d), lambda i, f: (0, 0)),
                  pl.BlockSpec((1, d), lambda i, f: (0, 0))],
        out_specs=pl.BlockSpec((tm, d), lambda i, f: (i, 0)),
        out_shape=jax.ShapeDtypeStruct((n, d), F32),
        scratch_shapes=[pltpu.VMEM((tm, d), BF16), pltpu.VMEM((tm, d), F32)],
        compiler_params=_cparams(2),
        name="mlp",
    )(x, mod, w_fc, b_fc, w_proj, b_proj, ln_g, ln_b)


def _rg_gate_weights(wa, ba, wx, bx, lam):
    ct = RG_CT
    d_rnn = ba.shape[-1]
    n_ct = d_rnn // ct
    per = ct // RG_BW

    def tile_blockdiag(w):
        wt = w.reshape(n_ct, per, RG_BW, RG_BW)
        eye = jnp.eye(per, dtype=w.dtype)
        return jnp.einsum('cpkj,pq->cpkqj', wt, eye).reshape(n_ct, ct, ct)

    w_cat = jnp.concatenate([tile_blockdiag(wa[0]), tile_blockdiag(wx[0]),
                             tile_blockdiag(wa[1]), tile_blockdiag(wx[1])], axis=-1).astype(BF16)
    tiles = lambda b: b.reshape(n_ct, 1, ct)
    b_cat = jnp.concatenate([tiles(ba[0]), tiles(bx[0]), tiles(ba[1]), tiles(bx[1])], axis=-1)
    lam_cat = jnp.concatenate([tiles(lam[0]), tiles(lam[1])], axis=-1)
    return w_cat, b_cat, lam_cat


def kernel(x_prompt, x_sample, state_rglru_h, state_mlstm_C, state_mlstm_n, state_mlstm_m, c, c_ctx, w_ada, b_ada, w_in, rg_conv_w, rg_conv_b, rg_wa, rg_ba, rg_wx, rg_bx, rg_lambda, w_rg_proj, ml_b_igate, ml_b_fgate, ml_gn_g, w_ml_proj, b_merge, w_out, ln_g, ln_b, w_fc, b_fc, w_proj, b_proj):
    bp, seq, d = x_prompt.shape
    bl, dec_seq, _ = x_sample.shape
    depth = w_in.shape[0]
    d_rnn = rg_conv_w.shape[-1]
    ml_dim = ml_gn_g.shape[-1]
    hd = ml_dim // ML_HEADS
    alpha = (2 * depth) ** 0.25
    d_main = 2 * d_rnn + 4 * ml_dim + 2 * d
    assert seq == MLSTM_CHUNK == SCAN_SEG and dec_seq % MLSTM_CHUNK == 0
    assert d_rnn == d and ml_dim == d and w_in.shape[-1] == d_main + 4 * ML_HEADS

    tm = 1024
    xp = x_prompt.reshape(bp * seq, d)
    xs = x_sample.reshape(bl * dec_seq, d)
    cond = jnp.concatenate([c_ctx[None, :], c], axis=0)
    zero_h = jnp.zeros((2, bp, d_rnn), F32)
    new_h, new_c, new_n, new_m = [], [], [], []
    for l in range(depth):
        mod = _ada(cond, w_ada[l], b_ada[l])
        w_main = w_in[l][:, :d_main].astype(BF16)
        w_gate = jnp.zeros((d, GATE_LANES), BF16).at[:, :4 * ML_HEADS].set(w_in[l][:, d_main:].astype(BF16))
        b_gate = jnp.zeros((1, GATE_LANES), F32).at[0, :4 * ML_HEADS].set(
            jnp.concatenate([ml_b_igate[l].reshape(-1), ml_b_fgate[l].reshape(-1)]))
        w_cat, b_cat, lam_cat = _rg_gate_weights(rg_wa[l], rg_ba[l], rg_wx[l], rg_bx[l], rg_lambda[l])
        conv_b = rg_conv_b[l].reshape(1, d_rnn)
        gn_g = ml_gn_g[l].reshape(1, ml_dim)
        w_rg, w_ml, w_o = (w_rg_proj[l].astype(BF16), w_ml_proj[l].astype(BF16), w_out[l].astype(BF16))
        w_fc_b, w_pj_b = w_fc[l].astype(BF16), w_proj[l].astype(BF16)
        bm = b_merge[l].reshape(1, 2 * d)
        lng0, lnb0 = ln_g[l, 0].reshape(1, d), ln_b[l, 0].reshape(1, d)
        lng1, lnb1 = ln_g[l, 1].reshape(1, d), ln_b[l, 1].reshape(1, d)
        bfc, bpj = b_fc[l].reshape(1, -1), b_proj[l].reshape(1, d)

        def tail(x, hrg, hml, of32, row_base, rows_per_mod):
            x1 = _merge(hrg, hml, of32, x, mod, w_rg, w_ml, w_o, bm, lng0, lnb0, alpha=alpha,
                        row_base=row_base, rows_per_mod=rows_per_mod, tm=512)
            return _mlp(x1, mod, w_fc_b, bfc, w_pj_b, bpj, lng1, lnb1, alpha=alpha,
                        row_base=row_base, rows_per_mod=rows_per_mod, tm=tm)

        of32, obf, g, gt = _in_proj(xp, mod, w_main, w_gate, b_gate, row_base=0, rows_per_mod=None, tm=tm)
        hrg, h_fin = _rglru(of32, w_cat, b_cat, lam_cat, rg_conv_w[l], conv_b, zero_h,
                            n_seq=bp, seq_len=seq, conv_len=seq, want_final=True)
        hml, c_fin, n_fin, m_fin = _mlstm_ctx(obf, of32, g, gt, gn_g, n_seq=bp)
        xp = tail(xp, hrg, hml, of32, 0, None)
        new_h.append(jnp.transpose(h_fin, (1, 0, 2)))
        new_c.append(c_fin)
        new_n.append(n_fin.reshape(bp, 2, ML_HEADS, hd))
        new_m.append(m_fin[..., 0, 0])

        of32, obf, g, gt = _in_proj(xs, mod, w_main, w_gate, b_gate, row_base=1,
                                    rows_per_mod=dec_seq, tm=tm)
        h0 = jnp.transpose(state_rglru_h[:, l], (1, 0, 2))
        (hrg,) = _rglru(of32, w_cat, b_cat, lam_cat, rg_conv_w[l], conv_b, h0,
                        n_seq=bl, seq_len=dec_seq, conv_len=GRID_W, want_final=False)
        c0 = state_mlstm_C[:, l]
        n0 = jnp.broadcast_to(state_mlstm_n[:, l][..., None], (bl, 2, ML_HEADS, hd, GATE_LANES))
        m0 = jnp.broadcast_to(state_mlstm_m[:, l].reshape(bl, 2, ML_HEADS, 1, 1),
                              (bl, 2, ML_HEADS, 1, GATE_LANES))
        hml = _mlstm_lat(obf, of32, g, gt, gn_g, c0, n0, m0, n_seq=bl)
        xs = tail(xs, hrg, hml, of32, 1, dec_seq)

    def stack(parts):
        return parts[0][:, None] if len(parts) == 1 else jnp.stack(parts, axis=1)

    return (xp.reshape(bp, seq, d), xs.reshape(bl, dec_seq, d),
            stack(new_h), stack(new_c), stack(new_n), stack(new_m))
```

```python
import functools

import jax
import jax.numpy as jnp
from jax import lax
from jax.experimental import pallas as pl
from jax.experimental.pallas import tpu as pltpu

F32 = jnp.float32
BF16 = jnp.bfloat16

LN_EPS = 1e-5
LOG2E = 1.4426950408889634
RG_C = 8.0
RG_BW = 64
CONV_W = 4
GRID_W = 64
ML_HEADS = 4
MLSTM_CHUNK = 256
SCAN_SEG = 256
SUBLANES = 8
SCAN_PITCH = SCAN_SEG + SUBLANES
RG_CT = 128
GATE_LANES = 128
VMEM_LIMIT = 52 * 1024 * 1024


def _cparams(n_axes):
    return pltpu.CompilerParams(
        dimension_semantics=("arbitrary",) * n_axes, vmem_limit_bytes=VMEM_LIMIT)


def _log_sigmoid(x):
    return jnp.minimum(x, 0.0) - jnp.log1p(jnp.exp(-jnp.abs(x)))


def _ln(x):
    mu = jnp.mean(x, -1, keepdims=True)
    xc = x - mu
    var = jnp.mean(xc * xc, -1, keepdims=True)
    return xc * lax.rsqrt(var + LN_EPS)


def _mod_row(i, row_base, tiles_per_row):
    if tiles_per_row is None:
        return row_base
    return row_base + lax.div(i, jnp.int32(tiles_per_row))


def _ada_kernel(ct_ref, w_ref, b_ref, o_ref, *, n_rows):
    w = w_ref[...]
    ct = ct_ref[...]
    s = ct * jax.nn.sigmoid(ct)
    o_ref[...] = jnp.zeros(o_ref.shape, F32)
    for r in range(n_rows):
        o_ref[r:r + 1, :] = jnp.sum(w * s[:, r:r + 1], axis=0, keepdims=True) + b_ref[...]


def _ada(cond, w_ada, b_ada):
    n_rows, d = cond.shape
    assert n_rows <= 8
    ct = jnp.zeros((d, 8), F32).at[:, :n_rows].set(cond.T)
    n_out = w_ada.shape[1]
    tn = 512
    return pl.pallas_call(
        functools.partial(_ada_kernel, n_rows=n_rows),
        grid=(n_out // tn,),
        in_specs=[pl.BlockSpec((d, 8), lambda j: (0, 0)),
                  pl.BlockSpec((d, tn), lambda j: (0, j)),
                  pl.BlockSpec((1, tn), lambda j: (0, j))],
        out_specs=pl.BlockSpec((8, tn), lambda j: (0, j)),
        out_shape=jax.ShapeDtypeStruct((8, n_out), F32),
        compiler_params=_cparams(1),
        name="ada_mod",
    )(ct, w_ada, b_ada.reshape(1, n_out))


def _inproj_kernel(x_ref, mod_ref, w_ref, wg_ref, bg_ref, of_ref, ob_ref, g_ref, gt_ref, u_ref,
                   *, d, tm, row_base, tiles_per_row):
    i = pl.program_id(0)
    j = pl.program_id(1)
    lc = MLSTM_CHUNK

    @pl.when(j == 0)
    def _():
        r = _mod_row(i, row_base, tiles_per_row)
        shift = mod_ref[pl.ds(r, 1), 0:d]
        scale = mod_ref[pl.ds(r, 1), d:2 * d]
        u_ref[...] = (_ln(x_ref[...]) * (1.0 + scale) + shift).astype(BF16)
        n_gate = 4 * ML_HEADS
        g = jnp.dot(u_ref[...], wg_ref[...], preferred_element_type=F32)
        gt = g.T[0:n_gate, :] + bg_ref[...]
        gt = jnp.where(lax.broadcasted_iota(jnp.int32, gt.shape, 0) >= 2 * ML_HEADS, _log_sigmoid(gt), gt)
        row = lax.broadcasted_iota(jnp.int32, (n_gate, lc), 0)
        ii = lax.broadcasted_iota(jnp.int32, (lc, lc), 0)
        jj = lax.broadcasted_iota(jnp.int32, (lc, lc), 1)
        incl_before = (ii <= jj).astype(BF16)
        incl_after = (ii >= jj).astype(BF16)
        parts = []
        for c in range(tm // lc):
            x = gt[:, c * lc:(c + 1) * lc]
            hi = x.astype(BF16)
            r1 = x - hi.astype(F32)
            mid = r1.astype(BF16)
            lo = (r1 - mid.astype(F32)).astype(BF16)
            split = jnp.concatenate([hi, mid, lo], axis=0)

            def cumsum(tri):
                r = jnp.dot(split, tri, preferred_element_type=F32)
                return r[0:n_gate] + r[n_gate:2 * n_gate] + r[2 * n_gate:3 * n_gate]

            parts.append(jnp.where(row < 2 * ML_HEADS, x,
                                   jnp.where(row < 3 * ML_HEADS, cumsum(incl_before), cumsum(incl_after))))
        gfin = jnp.concatenate(parts, axis=1)
        gt_ref[...] = gfin
        pad = jnp.zeros((GATE_LANES - n_gate, tm), F32)
        g_ref[...] = jnp.concatenate([gfin, pad], axis=0).T

    is_bf = jnp.logical_and(j >= 2, j <= 4)
    tn = of_ref.shape[1]
    w_cols = pl.ds(pl.multiple_of(j * tn, tn), tn)

    @pl.when(is_bf)
    def _():
        ob_ref[...] = jnp.dot(u_ref[...], w_ref[:, w_cols], preferred_element_type=F32).astype(BF16)

    @pl.when(jnp.logical_not(is_bf))
    def _():
        of_ref[...] = jnp.dot(u_ref[...], w_ref[:, w_cols], preferred_element_type=F32)


def _tiles_per_row(rows_per_mod, tm):
    if rows_per_mod is None:
        return None
    assert rows_per_mod % tm == 0
    return rows_per_mod // tm


def _in_proj(x, mod, w_main, w_gate, b_gate, *, row_base, rows_per_mod, tm):
    n, d = x.shape
    tiles_per_row = _tiles_per_row(rows_per_mod, tm)
    tn = d
    n_j = w_main.shape[1] // tn
    assert n_j == 8
    kern = functools.partial(_inproj_kernel, d=d, tm=tm, row_base=row_base, tiles_per_row=tiles_per_row)
    return pl.pallas_call(
        kern,
        grid=(n // tm, n_j),
        in_specs=[pl.BlockSpec((tm, d), lambda i, j: (i, 0)),
                  pl.BlockSpec(mod.shape, lambda i, j: (0, 0)),
                  pl.BlockSpec(w_main.shape, lambda i, j: (0, 0), pipeline_mode=pl.Buffered(1)),
                  pl.BlockSpec((d, GATE_LANES), lambda i, j: (0, 0)),
                  pl.BlockSpec((4 * ML_HEADS, 1), lambda i, j: (0, 0))],
        out_specs=[
            pl.BlockSpec((tm, tn), lambda i, j: (i, jnp.minimum(j, 1) + jnp.maximum(j - 4, 0))),
            pl.BlockSpec((tm, tn), lambda i, j: (i, jnp.clip(j - 2, 0, 2))),
            pl.BlockSpec((tm, GATE_LANES), lambda i, j: (i, 0)),
            pl.BlockSpec((4 * ML_HEADS, tm), lambda i, j: (0, i))],
        out_shape=[jax.ShapeDtypeStruct((n, 5 * d), F32),
                   jax.ShapeDtypeStruct((n, 3 * d), BF16),
                   jax.ShapeDtypeStruct((n, GATE_LANES), F32),
                   jax.ShapeDtypeStruct((4 * ML_HEADS, n), F32)],
        scratch_shapes=[pltpu.VMEM((tm, d), BF16)],
        compiler_params=_cparams(2),
        name="in_proj",
    )(x, mod, w_main, w_gate, b_gate)


def _rglru_kernel(xr_ref, zr_ref, w_ref, b_ref, lam_ref, cw_ref, cb_ref, h0_ref, *rest,
                  n_vseq, conv_len, n_seg, want_final):
    if want_final:
        out_ref, hfin_ref = rest[:2]
        rest = rest[2:]
    else:
        out_ref = rest[0]
        rest = rest[1:]
    a_f, b_f, a_b, b_b, cwm_ref = rest[:5]
    rest = rest[5:]
    seg = SCAN_SEG
    pitch = SCAN_PITCH
    ct = xr_ref.shape[1]
    ch = seg
    la = (RG_C * LOG2E) * _log_sigmoid(lam_ref[0])
    la_f, la_b = la[:, :ct], la[:, ct:]
    w = w_ref[0]
    bias = b_ref[0]
    cw = cw_ref[...]
    cb = cb_ref[...]
    pos = jnp.bitwise_and(lax.broadcasted_iota(jnp.int32, (ch, ct), 0), conv_len - 1)
    cwm_ref[0] = jnp.where(pos >= 2, cw[0:1], 0.0)
    cwm_ref[1] = jnp.where(pos >= 1, cw[1:2], 0.0)
    cwm_ref[2] = jnp.where(pos <= conv_len - 2, cw[3:4], 0.0)

    def gate_chunk(ci, carry):
        r0 = pl.multiple_of(ci * ch, ch)
        x = xr_ref[pl.ds(r0, ch), :]
        xc = (cwm_ref[0] * pltpu.roll(x, 2, 0) + cwm_ref[1] * pltpu.roll(x, 1, 0) + cw[2:3] * x
              + cwm_ref[2] * pltpu.roll(x, ch - 1, 0) + cb)
        gates = jnp.dot(xc.astype(BF16), w, preferred_element_type=F32) + bias

        def a_and_b(r_pre, i_pre, la_dir):
            a = jnp.exp2(la_dir / (1.0 + jnp.exp2(r_pre * (-LOG2E))))
            y = jnp.clip(1.0 - a * a, 0.0, 1.0)
            gain = jnp.where(y > 0.0, y * lax.rsqrt(y), 0.0)
            return a, (gain * xc) / (1.0 + jnp.exp2(i_pre * (-LOG2E)))

        af, bf = a_and_b(gates[:, 0:ct], gates[:, ct:2 * ct], la_f)
        ab, bb = a_and_b(gates[:, 2 * ct:3 * ct], gates[:, 3 * ct:4 * ct], la_b)
        s0 = pl.ds(pl.multiple_of(ci * pitch, SUBLANES), ch)
        a_f[s0, :] = af
        b_f[s0, :] = bf
        a_b[s0, :] = ab
        b_b[s0, :] = bb
        return carry

    lax.fori_loop(0, n_vseq, gate_chunk, 0, unroll=2)

    def slab_idx(t):
        return pl.ds(t, n_vseq, stride=pitch)

    def slab(ref, t):
        return ref[slab_idx(t), :]

    if n_seg == 1:
        init_f = h0_ref[0]
        init_b = h0_ref[1]
    else:
        e_f, p_f, e_b, p_b, cin_f, cin_b = rest
        n_real = n_vseq // n_seg
        zero = jnp.zeros((n_vseq, ct), F32)
        one = jnp.ones((n_vseq, ct), F32)

        def local_step(t, carry):
            ef, pf, eb, pb = carry
            tb = seg - 1 - t
            a = slab(a_f, t)
            ef = a * ef + slab(b_f, t)
            pf = a * pf
            a2 = slab(a_b, tb)
            eb = a2 * eb + slab(b_b, tb)
            pb = a2 * pb
            return ef, pf, eb, pb

        ef, pf, eb, pb = lax.fori_loop(0, seg, local_step, (zero, one, zero, one), unroll=8)
        e_f[...] = ef
        p_f[...] = pf
        e_b[...] = eb
        p_b[...] = pb
        carry = h0_ref[0]
        for s in range(n_seg):
            idx = pl.ds(s, n_real, stride=n_seg)
            cin_f[idx, :] = carry
            carry = e_f[idx, :] + p_f[idx, :] * carry
        carry = h0_ref[1]
        for s in reversed(range(n_seg)):
            idx = pl.ds(s, n_real, stride=n_seg)
            cin_b[idx, :] = carry
            carry = e_b[idx, :] + p_b[idx, :] * carry
        init_f = cin_f[...]
        init_b = cin_b[...]

    def scan_step(t, carry):
        hf, hb = carry
        tb = seg - 1 - t
        hf = slab(a_f, t) * hf + slab(b_f, t)
        b_f[slab_idx(t), :] = hf
        hb = slab(a_b, tb) * hb + slab(b_b, tb)
        b_b[slab_idx(tb), :] = hb
        return hf, hb

    hf_last, hb_last = lax.fori_loop(0, seg, scan_step, (init_f, init_b), unroll=8)
    if want_final:
        hfin_ref[0] = hf_last
        hfin_ref[1] = hb_last

    def out_chunk(ci, carry):
        r0 = pl.multiple_of(ci * ch, ch)
        s0 = pl.ds(pl.multiple_of(ci * pitch, SUBLANES), ch)
        h = b_f[s0, :] + b_b[s0, :]
        out_ref[pl.ds(r0, ch), :] = (h * jax.nn.gelu(zr_ref[pl.ds(r0, ch), :])).astype(BF16)
        return carry

    lax.fori_loop(0, n_vseq, out_chunk, 0)


def _rglru(of32, w_cat, b_cat, lam_cat, conv_w, conv_b, h0, *, n_seq, seq_len, conv_len, want_final):
    n = of32.shape[0]
    d_rnn = conv_w.shape[1]
    ct = RG_CT
    n_ct = d_rnn // ct
    n_seg = seq_len // SCAN_SEG
    n_vseq = n_seq * n_seg
    assert n_vseq * SCAN_SEG == n and SCAN_SEG % conv_len == 0 and (conv_len & (conv_len - 1)) == 0
    kern = functools.partial(_rglru_kernel, n_vseq=n_vseq, conv_len=conv_len, n_seg=n_seg,
                             want_final=want_final)
    out_specs = [pl.BlockSpec((n, ct), lambda c: (0, c))]
    out_shape = [jax.ShapeDtypeStruct((n, d_rnn), BF16)]
    if want_final:
        out_specs.append(pl.BlockSpec((2, n_seq, ct), lambda c: (0, 0, c)))
        out_shape.append(jax.ShapeDtypeStruct((2, n_seq, d_rnn), F32))
    scratch = [pltpu.VMEM((n_vseq * SCAN_PITCH, ct), F32) for _ in range(4)]
    scratch.append(pltpu.VMEM((CONV_W - 1, SCAN_SEG, ct), F32))
    if n_seg > 1:
        scratch += [pltpu.VMEM((n_vseq, ct), F32) for _ in range(6)]
    return pl.pallas_call(
        kern,
        grid=(n_ct,),
        in_specs=[pl.BlockSpec((n, ct), lambda c: (0, c)),
                  pl.BlockSpec((n, ct), lambda c: (0, n_ct + c)),
                  pl.BlockSpec((1, ct, 4 * ct), lambda c: (c, 0, 0)),
                  pl.BlockSpec((1, 1, 4 * ct), lambda c: (c, 0, 0)),
                  pl.BlockSpec((1, 1, 2 * ct), lambda c: (c, 0, 0)),
                  pl.BlockSpec((CONV_W, ct), lambda c: (0, c)),
                  pl.BlockSpec((1, ct), lambda c: (0, c)),
                  pl.BlockSpec((2, n_seq, ct), lambda c: (0, 0, c))],
        out_specs=out_specs,
        out_shape=out_shape,
        scratch_shapes=scratch,
        compiler_params=_cparams(1),
        name="rglru_ctx" if want_final else "rglru_lat",
    )(of32, of32, w_cat, b_cat, lam_cat, conv_w, conv_b, h0)


def _causal_bias(mask_ref):
    lc = mask_ref.shape[1]
    ii = lax.broadcasted_iota(jnp.int32, (lc, lc), 0)
    jj = lax.broadcasted_iota(jnp.int32, (lc, lc), 1)
    mask_ref[0] = jnp.where(jj <= ii, 0.0, -jnp.inf)
    mask_ref[1] = jnp.where(jj >= ii, 0.0, -jnp.inf)


def _with_ones(v):
    return jnp.concatenate([v, jnp.ones((v.shape[0], GATE_LANES), v.dtype)], axis=1)


def _mlstm_unit(direction, head, q, k, v, v1, gc, gt_ref, r0, mask_ref, state, scale):
    lc, hd = q.shape
    col_i = direction * ML_HEADS + head
    col_c = 2 * ML_HEADS + col_i
    lane = lax.broadcasted_iota(jnp.int32, (1, GATE_LANES), 1)
    ig_col = jnp.sum(jnp.where(lane == col_i, gc, 0.0), -1, keepdims=True)
    cum_col = jnp.sum(jnp.where(lane == col_c, gc, 0.0), -1, keepdims=True)
    ig_row = gt_ref[pl.ds(col_i, 1), pl.ds(r0, lc)]
    cum_row = gt_ref[pl.ds(col_c, 1), pl.ds(r0, lc)]
    dmat = (cum_col + (ig_row - cum_row)) + mask_ref[direction]
    if state is None:
        m_prev = 0.0
    else:
        s_prev, m_prev = state
    inter = cum_col + m_prev
    m_row = jnp.maximum(jnp.max(dmat, -1, keepdims=True), inter)
    wts = jnp.exp(dmat - m_row)
    s = lax.dot_general(q, k, (((1,), (1,)), ((), ())), preferred_element_type=F32) * (wts * scale)
    if state is None:
        num = jnp.dot(s.astype(BF16), v, preferred_element_type=F32)
        h = num * (1.0 / jnp.maximum(jnp.abs(jnp.sum(s, -1, keepdims=True)), jnp.exp(-m_row)))
    else:
        w_inter = jnp.exp(inter - m_row) * scale
        nd = (jnp.dot(s.astype(BF16), v1, preferred_element_type=F32)
              + w_inter * jnp.dot(q, s_prev.astype(BF16), preferred_element_type=F32))
        rden = 1.0 / jnp.maximum(jnp.abs(nd[:, hd:]), jnp.exp(-m_row))
        h = nd[:, :hd] * jnp.concatenate([rden] * (hd // GATE_LANES), axis=1)
    btot = cum_col[lc - 1:lc] if direction == 0 else cum_col[0:1]
    wk_log = btot - cum_col + ig_col
    m_new = jnp.maximum(btot + m_prev, jnp.max(wk_log, 0, keepdims=True))
    kw = k.astype(F32) * jnp.exp(wk_log - m_new)
    t_lhs = (((0,), (0,)), ((), ()))
    if state is None:
        c_new = lax.dot_general(kw.astype(BF16), v, t_lhs, preferred_element_type=F32)
        return h, c_new, jnp.sum(kw, 0, keepdims=True), m_new
    decay = jnp.exp(btot + m_prev - m_new)
    s_new = decay * s_prev + lax.dot_general(kw.astype(BF16), v1, t_lhs, preferred_element_type=F32)
    return h, s_new, None, m_new


def _head_out(hsum, o, gn):
    return (jax.nn.sigmoid(o) * (_ln(hsum) * gn)).astype(BF16)


def _mlstm_ctx_kernel(q_ref, k_ref, v_ref, o_ref, g_ref, gt_ref, gn_ref,
                      out_ref, cfin_ref, nfin_ref, mfin_ref, mask_ref, *, n_sub, scale):
    head = pl.program_id(1)
    lc = MLSTM_CHUNK
    gn = gn_ref[...]
    _causal_bias(mask_ref)

    def body(s, carry):
        r0 = pl.multiple_of(s * lc, lc)
        rs = pl.ds(r0, lc)
        q, k, v, gc = q_ref[rs, :], k_ref[rs, :], v_ref[rs, :], g_ref[rs, :]
        hsum = None
        for direction in (0, 1):
            h, c_new, n_new, m_new = _mlstm_unit(direction, head, q, k, v, None, gc, gt_ref, r0, mask_ref,
                                                 None, scale)
            cfin_ref[s, direction, 0] = c_new
            nfin_ref[s, direction, 0] = n_new
            mfin_ref[s, direction, 0] = jnp.broadcast_to(m_new, (1, GATE_LANES))
            hsum = h if hsum is None else hsum + h
        out_ref[rs, :] = _head_out(hsum, o_ref[rs, :], gn)
        return carry

    lax.fori_loop(0, n_sub, body, 0)


def _mlstm_lat_kernel(q_ref, k_ref, v_ref, o_ref, g_ref, gt_ref, gn_ref, c0_ref, n0_ref, m0_ref,
                      out_ref, s_st, m_st, h_acc, mask_ref, *, n_chunks, scale):
    head = pl.program_id(1)
    lc = MLSTM_CHUNK
    hd = q_ref.shape[1]
    gn = gn_ref[...]
    _causal_bias(mask_ref)
    for direction in (0, 1):
        s_st[direction, :, 0:hd] = c0_ref[0, direction, 0]
        s_st[direction, :, hd:] = n0_ref[0, direction, 0]
        m_st[direction] = m0_ref[0, direction, 0]
    h_acc[...] = jnp.zeros(h_acc.shape, F32)

    def body(c, carry):
        for direction in (0, 1):
            cc = c if direction == 0 else n_chunks - 1 - c
            r0 = pl.multiple_of(cc * lc, lc)
            rs = pl.ds(r0, lc)
            state = (s_st[direction], m_st[direction][:, 0:1])
            v = v_ref[rs, :]
            h, s_new, _, m_new = _mlstm_unit(direction, head, q_ref[rs, :], k_ref[rs, :], v, _with_ones(v),
                                             g_ref[rs, :], gt_ref, r0, mask_ref, state, scale)
            s_st[direction] = s_new
            m_st[direction] = jnp.broadcast_to(m_new, (1, GATE_LANES))
            h_acc[rs, :] += h
        return carry

    lax.fori_loop(0, n_chunks, body, 0, unroll=2)

    def out_chunk(c, carry):
        rs = pl.ds(pl.multiple_of(c * lc, lc), lc)
        out_ref[rs, :] = _head_out(h_acc[rs, :], o_ref[rs, :], gn)
        return carry

    lax.fori_loop(0, n_chunks, out_chunk, 0)


def _mlstm_ctx(obf, of32, g, gt, gn_g, *, n_seq):
    n = obf.shape[0]
    ml_dim = gn_g.shape[1]
    hd = ml_dim // ML_HEADS
    lc = MLSTM_CHUNK
    assert n == n_seq * lc
    n_sub = 4
    rows = n_sub * lc
    o_blk = (2 * ml_dim) // hd
    kern = functools.partial(_mlstm_ctx_kernel, n_sub=n_sub, scale=hd ** -0.5)
    return pl.pallas_call(
        kern,
        grid=(n_seq // n_sub, ML_HEADS),
        in_specs=[pl.BlockSpec((rows, hd), lambda s, h: (s, h)),
                  pl.BlockSpec((rows, hd), lambda s, h: (s, ML_HEADS + h)),
                  pl.BlockSpec((rows, hd), lambda s, h: (s, 2 * ML_HEADS + h)),
                  pl.BlockSpec((rows, hd), lambda s, h: (s, o_blk + h)),
                  pl.BlockSpec((rows, GATE_LANES), lambda s, h: (s, 0)),
                  pl.BlockSpec((4 * ML_HEADS, rows), lambda s, h: (0, s)),
                  pl.BlockSpec((1, hd), lambda s, h: (0, h))],
        out_specs=[pl.BlockSpec((rows, hd), lambda s, h: (s, h)),
                   pl.BlockSpec((n_sub, 2, 1, hd, hd), lambda s, h: (s, 0, h, 0, 0)),
                   pl.BlockSpec((n_sub, 2, 1, 1, hd), lambda s, h: (s, 0, h, 0, 0)),
                   pl.BlockSpec((n_sub, 2, 1, 1, GATE_LANES), lambda s, h: (s, 0, h, 0, 0))],
        out_shape=[jax.ShapeDtypeStruct((n, ml_dim), BF16),
                   jax.ShapeDtypeStruct((n_seq, 2, ML_HEADS, hd, hd), F32),
                   jax.ShapeDtypeStruct((n_seq, 2, ML_HEADS, 1, hd), F32),
                   jax.ShapeDtypeStruct((n_seq, 2, ML_HEADS, 1, GATE_LANES), F32)],
        scratch_shapes=[pltpu.VMEM((2, lc, lc), F32)],
        compiler_params=_cparams(2),
        name="mlstm_ctx",
    )(obf, obf, obf, of32, g, gt, gn_g)


def _mlstm_lat(obf, of32, g, gt, gn_g, c0, n0, m0, *, n_seq):
    n = obf.shape[0]
    ml_dim = gn_g.shape[1]
    hd = ml_dim // ML_HEADS
    lc = MLSTM_CHUNK
    rows = n // n_seq
    n_chunks = rows // lc
    o_blk = (2 * ml_dim) // hd
    kern = functools.partial(_mlstm_lat_kernel, n_chunks=n_chunks, scale=hd ** -0.5)
    return pl.pallas_call(
        kern,
        grid=(n_seq, ML_HEADS),
        in_specs=[pl.BlockSpec((rows, hd), lambda s, h: (s, h)),
                  pl.BlockSpec((rows, hd), lambda s, h: (s, ML_HEADS + h)),
                  pl.BlockSpec((rows, hd), lambda s, h: (s, 2 * ML_HEADS + h)),
                  pl.BlockSpec((rows, hd), lambda s, h: (s, o_blk + h)),
                  pl.BlockSpec((rows, GATE_LANES), lambda s, h: (s, 0)),
                  pl.BlockSpec((4 * ML_HEADS, rows), lambda s, h: (0, s)),
                  pl.BlockSpec((1, hd), lambda s, h: (0, h)),
                  pl.BlockSpec((1, 2, 1, hd, hd), lambda s, h: (s, 0, h, 0, 0)),
                  pl.BlockSpec((1, 2, 1, hd, GATE_LANES), lambda s, h: (s, 0, h, 0, 0)),
                  pl.BlockSpec((1, 2, 1, 1, GATE_LANES), lambda s, h: (s, 0, h, 0, 0))],
        out_specs=pl.BlockSpec((rows, hd), lambda s, h: (s, h)),
        out_shape=jax.ShapeDtypeStruct((n, ml_dim), BF16),
        scratch_shapes=[pltpu.VMEM((2, hd, hd + GATE_LANES), F32),
                        pltpu.VMEM((2, 1, GATE_LANES), F32),
                        pltpu.VMEM((rows, hd), F32),
                        pltpu.VMEM((2, lc, lc), F32)],
        compiler_params=_cparams(2),
        name="mlstm_lat",
    )(obf, obf, obf, of32, g, gt, gn_g, c0, n0, m0)


def _merge_kernel(hrg_ref, hml_ref, gma_ref, gmb_ref, x_ref, mod_ref, wrg_ref, wml_ref, wout_ref,
                  bm_ref, lng_ref, lnb_ref, o_ref, *, d, alpha, row_base, tiles_per_row):
    r = _mod_row(pl.program_id(0), row_base, tiles_per_row)
    gate1 = mod_ref[pl.ds(r, 1), 2 * d:3 * d]
    y_rg = jnp.dot(hrg_ref[...], wrg_ref[...], preferred_element_type=F32)
    y_ml = jnp.dot(hml_ref[...], wml_ref[...], preferred_element_type=F32)
    g_rg = jax.nn.sigmoid(gma_ref[...] + bm_ref[:, 0:d])
    g_ml = jax.nn.sigmoid(gmb_ref[...] + bm_ref[:, d:2 * d])
    merged = (g_rg * y_rg + g_ml * y_ml).astype(BF16)
    mix = jnp.dot(merged, wout_ref[...], preferred_element_type=F32)
    o_ref[...] = _ln(alpha * x_ref[...] + gate1 * mix) * lng_ref[...] + lnb_ref[...]


def _merge(hrg, hml, of32, x, mod, w_rg, w_ml, w_out, b_merge, ln_g, ln_b, *, alpha, row_base,
           rows_per_mod, tm):
    n, d = x.shape
    tiles_per_row = _tiles_per_row(rows_per_mod, tm)
    kern = functools.partial(_merge_kernel, d=d, alpha=alpha, row_base=row_base, tiles_per_row=tiles_per_row)
    full = lambda shape: pl.BlockSpec(shape, lambda i: (0,) * len(shape))
    return pl.pallas_call(
        kern,
        grid=(n // tm,),
        in_specs=[pl.BlockSpec((tm, d), lambda i: (i, 0)),
                  pl.BlockSpec((tm, d), lambda i: (i, 0)),
                  pl.BlockSpec((tm, d), lambda i: (i, 3)),
                  pl.BlockSpec((tm, d), lambda i: (i, 4)),
                  pl.BlockSpec((tm, d), lambda i: (i, 0)),
                  full(mod.shape), full((d, d)), full((d, d)), full((d, d)),
                  full((1, 2 * d)), full((1, d)), full((1, d))],
        out_specs=pl.BlockSpec((tm, d), lambda i: (i, 0)),
        out_shape=jax.ShapeDtypeStruct((n, d), F32),
        compiler_params=_cparams(1),
        name="merge",
    )(hrg, hml, of32, of32, x, mod, w_rg, w_ml, w_out, b_merge, ln_g, ln_b)


def _mlp_kernel(x_ref, mod_ref, wfc_ref, bfc_ref, wpj_ref, bpj_ref, lng_ref, lnb_ref, o_ref,
                u_ref, acc_ref, *, d, alpha, row_base, tiles_per_row):
    f = pl.program_id(1)
    r = _mod_row(pl.program_id(0), row_base, tiles_per_row)

    @pl.when(f == 0)
    def _():
        shift = mod_ref[pl.ds(r, 1), 3 * d:4 * d]
        scale = mod_ref[pl.ds(r, 1), 4 * d:5 * d]
        u_ref[...] = (_ln(x_ref[...]) * (1.0 + scale) + shift).astype(BF16)
        acc_ref[...] = jnp.zeros(acc_ref.shape, F32)

    hid = jnp.dot(u_ref[...], wfc_ref[...], preferred_element_type=F32) + bfc_ref[...]
    hid = jnp.square(jnp.maximum(hid, 0.0)).astype(BF16)
    acc_ref[...] += jnp.dot(hid, wpj_ref[...], preferred_element_type=F32)

    @pl.when(f == pl.num_programs(1) - 1)
    def _():
        gate2 = mod_ref[pl.ds(r, 1), 5 * d:6 * d]
        y = alpha * x_ref[...] + gate2 * (acc_ref[...] + bpj_ref[...])
        o_ref[...] = _ln(y) * lng_ref[...] + lnb_ref[...]


def _mlp(x, mod, w_fc, b_fc, w_proj, b_proj, ln_g, ln_b, *, alpha, row_base, rows_per_mod, tm):
    n, d = x.shape
    tiles_per_row = _tiles_per_row(rows_per_mod, tm)
    d_ff = w_fc.shape[1]
    tf = 1024
    kern = functools.partial(_mlp_kernel, d=d, alpha=alpha, row_base=row_base, tiles_per_row=tiles_per_row)
    return pl.pallas_call(
        kern,
        grid=(n // tm, d_ff // tf),
        in_specs=[pl.BlockSpec((tm, d), lambda i, f: (i, 0)),
                  pl.BlockSpec(mod.shape, lambda i, f: (0, 0)),
                  pl.BlockSpec((d, tf), lambda i, f: (0, f)),
                  pl.BlockSpec((1, tf), lambda i, f: (0, f)),
                  pl.BlockSpec((tf, d), lambda i, f: (f, 0)),
                  pl.BlockSpec((1, d), lambda i, f: (0, 0)),
                  pl.BlockSpec((1, d), lambda i, f: (0, 0)),
                  pl.BlockSpec((1, d), lambda i, f: (0, 0))],
        out_specs=pl.BlockSpec((tm, d), lambda i, f: (i, 0)),
        out_shape=jax.ShapeDtypeStruct((n, d), F32),
        scratch_shapes=[pltpu.VMEM((tm, d), BF16), pltpu.VMEM((tm, d), F32)],
        compiler_params=_cparams(2),
        name="mlp",
    )(x, mod, w_fc, b_fc, w_proj, b_proj, ln_g, ln_b)


def _rg_gate_weights(wa, ba, wx, bx, lam):
    ct = RG_CT
    d_rnn = ba.shape[-1]
    n_ct = d_rnn // ct
    per = ct // RG_BW

    def tile_blockdiag(w):
        wt = w.reshape(n_ct, per, RG_BW, RG_BW)
        eye = jnp.eye(per, dtype=w.dtype)
        return jnp.einsum('cpkj,pq->cpkqj', wt, eye).reshape(n_ct, ct, ct)

    w_cat = jnp.concatenate([tile_blockdiag(wa[0]), tile_blockdiag(wx[0]),
                             tile_blockdiag(wa[1]), tile_blockdiag(wx[1])], axis=-1).astype(BF16)
    tiles = lambda b: b.reshape(n_ct, 1, ct)
    b_cat = jnp.concatenate([tiles(ba[0]), tiles(bx[0]), tiles(ba[1]), tiles(bx[1])], axis=-1)
    lam_cat = jnp.concatenate([tiles(lam[0]), tiles(lam[1])], axis=-1)
    return w_cat, b_cat, lam_cat


def kernel(x_prompt, x_sample, state_rglru_h, state_mlstm_C, state_mlstm_n, state_mlstm_m, c, c_ctx, w_ada, b_ada, w_in, rg_conv_w, rg_conv_b, rg_wa, rg_ba, rg_wx, rg_bx, rg_lambda, w_rg_proj, ml_b_igate, ml_b_fgate, ml_gn_g, w_ml_proj, b_merge, w_out, ln_g, ln_b, w_fc, b_fc, w_proj, b_proj):
    bp, seq, d = x_prompt.shape
    bl, dec_seq, _ = x_sample.shape
    depth = w_in.shape[0]
    d_rnn = rg_conv_w.shape[-1]
    ml_dim = ml_gn_g.shape[-1]
    hd = ml_dim // ML_HEADS
    alpha = (2 * depth) ** 0.25
    d_main = 2 * d_rnn + 4 * ml_dim + 2 * d
    assert seq == MLSTM_CHUNK == SCAN_SEG and dec_seq % MLSTM_CHUNK == 0
    assert d_rnn == d and ml_dim == d and w_in.shape[-1] == d_main + 4 * ML_HEADS

    tm = 1024
    xp = x_prompt.reshape(bp * seq, d)
    xs = x_sample.reshape(bl * dec_seq, d)
    cond = jnp.concatenate([c_ctx[None, :], c], axis=0)
    zero_h = jnp.zeros((2, bp, d_rnn), F32)
    new_h, new_c, new_n, new_m = [], [], [], []
    for l in range(depth):
        mod = _ada(cond, w_ada[l], b_ada[l])
        w_main = w_in[l][:, :d_main].astype(BF16)
        w_gate = jnp.zeros((d, GATE_LANES), BF16).at[:, :4 * ML_HEADS].set(w_in[l][:, d_main:].astype(BF16))
        b_gate = jnp.concatenate([ml_b_igate[l].reshape(-1), ml_b_fgate[l].reshape(-1)]).reshape(-1, 1)
        w_cat, b_cat, lam_cat = _rg_gate_weights(rg_wa[l], rg_ba[l], rg_wx[l], rg_bx[l], rg_lambda[l])
        conv_b = rg_conv_b[l].reshape(1, d_rnn)
        gn_g = ml_gn_g[l].reshape(1, ml_dim)
        w_rg, w_ml, w_o = (w_rg_proj[l].astype(BF16), w_ml_proj[l].astype(BF16), w_out[l].astype(BF16))
        w_fc_b, w_pj_b = w_fc[l].astype(BF16), w_proj[l].astype(BF16)
        bm = b_merge[l].reshape(1, 2 * d)
        lng0, lnb0 = ln_g[l, 0].reshape(1, d), ln_b[l, 0].reshape(1, d)
        lng1, lnb1 = ln_g[l, 1].reshape(1, d), ln_b[l, 1].reshape(1, d)
        bfc, bpj = b_fc[l].reshape(1, -1), b_proj[l].reshape(1, d)

        def tail(x, hrg, hml, of32, row_base, rows_per_mod):
            x1 = _merge(hrg, hml, of32, x, mod, w_rg, w_ml, w_o, bm, lng0, lnb0, alpha=alpha,
                        row_base=row_base, rows_per_mod=rows_per_mod, tm=512)
            return _mlp(x1, mod, w_fc_b, bfc, w_pj_b, bpj, lng1, lnb1, alpha=alpha,
                        row_base=row_base, rows_per_mod=rows_per_mod, tm=tm)

        of32, obf, g, gt = _in_proj(xp, mod, w_main, w_gate, b_gate, row_base=0, rows_per_mod=None, tm=tm)
        hrg, h_fin = _rglru(of32, w_cat, b_cat, lam_cat, rg_conv_w[l], conv_b, zero_h,
                            n_seq=bp, seq_len=seq, conv_len=seq, want_final=True)
        hml, c_fin, n_fin, m_fin = _mlstm_ctx(obf, of32, g, gt, gn_g, n_seq=bp)
        xp = tail(xp, hrg, hml, of32, 0, None)
        new_h.append(jnp.transpose(h_fin, (1, 0, 2)))
        new_c.append(c_fin)
        new_n.append(n_fin.reshape(bp, 2, ML_HEADS, hd))
        new_m.append(m_fin[..., 0, 0])

        of32, obf, g, gt = _in_proj(xs, mod, w_main, w_gate, b_gate, row_base=1,
                                    rows_per_mod=dec_seq, tm=tm)
        h0 = jnp.transpose(state_rglru_h[:, l], (1, 0, 2))
        (hrg,) = _rglru(of32, w_cat, b_cat, lam_cat, rg_conv_w[l], conv_b, h0,
                        n_seq=bl, seq_len=dec_seq, conv_len=GRID_W, want_final=False)
        c0 = state_mlstm_C[:, l]
        n0 = jnp.broadcast_to(state_mlstm_n[:, l][..., None], (bl, 2, ML_HEADS, hd, GATE_LANES))
        m0 = jnp.broadcast_to(state_mlstm_m[:, l].reshape(bl, 2, ML_HEADS, 1, 1),
                              (bl, 2, ML_HEADS, 1, GATE_LANES))
        hml = _mlstm_lat(obf, of32, g, gt, gn_g, c0, n0, m0, n_seq=bl)
        xs = tail(xs, hrg, hml, of32, 1, dec_seq)

    def stack(parts):
        return parts[0][:, None] if len(parts) == 1 else jnp.stack(parts, axis=1)

    return (xp.reshape(bp, seq, d), xs.reshape(bl, dec_seq, d),
            stack(new_h), stack(new_c), stack(new_n), stack(new_m))
```

```python
import functools

import jax
import jax.numpy as jnp
from jax import lax
from jax.experimental import pallas as pl
from jax.experimental.pallas import tpu as pltpu

F32 = jnp.float32
BF16 = jnp.bfloat16

LN_EPS = 1e-5
LOG2E = 1.4426950408889634
RG_C = 8.0
RG_BW = 64
CONV_W = 4
GRID_W = 64
ML_HEADS = 4
MLSTM_CHUNK = 256
SCAN_SEG = 256
SUBLANES = 8
SCAN_PITCH = SCAN_SEG + SUBLANES
RG_CT = 128
GATE_LANES = 128
VMEM_LIMIT = 52 * 1024 * 1024


def _cparams(n_axes):
    return pltpu.CompilerParams(
        dimension_semantics=("arbitrary",) * n_axes, vmem_limit_bytes=VMEM_LIMIT)


def _log_sigmoid(x):
    return jnp.minimum(x, 0.0) - jnp.log1p(jnp.exp(-jnp.abs(x)))


def _ln(x):
    mu = jnp.mean(x, -1, keepdims=True)
    xc = x - mu
    var = jnp.mean(xc * xc, -1, keepdims=True)
    return xc * lax.rsqrt(var + LN_EPS)


def _mod_row(i, row_base, tiles_per_row):
    if tiles_per_row is None:
        return row_base
    return row_base + lax.div(i, jnp.int32(tiles_per_row))


def _ada_kernel(ct_ref, w_ref, b_ref, o_ref, *, n_rows):
    w = w_ref[...]
    ct = ct_ref[...]
    s = ct * jax.nn.sigmoid(ct)
    o_ref[...] = jnp.zeros(o_ref.shape, F32)
    for r in range(n_rows):
        o_ref[r:r + 1, :] = jnp.sum(w * s[:, r:r + 1], axis=0, keepdims=True) + b_ref[...]


def _ada(cond, w_ada, b_ada):
    n_rows, d = cond.shape
    assert n_rows <= 8
    ct = jnp.zeros((d, 8), F32).at[:, :n_rows].set(cond.T)
    n_out = w_ada.shape[1]
    tn = 512
    return pl.pallas_call(
        functools.partial(_ada_kernel, n_rows=n_rows),
        grid=(n_out // tn,),
        in_specs=[pl.BlockSpec((d, 8), lambda j: (0, 0)),
                  pl.BlockSpec((d, tn), lambda j: (0, j)),
                  pl.BlockSpec((1, tn), lambda j: (0, j))],
        out_specs=pl.BlockSpec((8, tn), lambda j: (0, j)),
        out_shape=jax.ShapeDtypeStruct((8, n_out), F32),
        compiler_params=_cparams(1),
        name="ada_mod",
    )(ct, w_ada, b_ada.reshape(1, n_out))


def _inproj_kernel(x_ref, mod_ref, w_ref, wg_ref, bg_ref, of_ref, ob_ref, g_ref, gt_ref, u_ref,
                   *, d, tm, row_base, tiles_per_row):
    i = pl.program_id(0)
    j = pl.program_id(1)
    lc = MLSTM_CHUNK

    @pl.when(j == 0)
    def _():
        r = _mod_row(i, row_base, tiles_per_row)
        shift = mod_ref[pl.ds(r, 1), 0:d]
        scale = mod_ref[pl.ds(r, 1), d:2 * d]
        u_ref[...] = (_ln(x_ref[...]) * (1.0 + scale) + shift).astype(BF16)
        n_gate = 4 * ML_HEADS
        g = jnp.dot(u_ref[...], wg_ref[...], preferred_element_type=F32)
        gt = g.T[0:n_gate, :] + bg_ref[...]
        gt = jnp.where(lax.broadcasted_iota(jnp.int32, gt.shape, 0) >= 2 * ML_HEADS, _log_sigmoid(gt), gt)
        row = lax.broadcasted_iota(jnp.int32, (n_gate, lc), 0)
        ii = lax.broadcasted_iota(jnp.int32, (lc, lc), 0)
        jj = lax.broadcasted_iota(jnp.int32, (lc, lc), 1)
        incl_before = (ii <= jj).astype(BF16)
        incl_after = (ii >= jj).astype(BF16)
        parts = []
        for c in range(tm // lc):
            x = gt[:, c * lc:(c + 1) * lc]
            hi = x.astype(BF16)
            r1 = x - hi.astype(F32)
            mid = r1.astype(BF16)
            lo = (r1 - mid.astype(F32)).astype(BF16)
            split = jnp.concatenate([hi, mid, lo], axis=0)

            def cumsum(tri):
                r = jnp.dot(split, tri, preferred_element_type=F32)
                return r[0:n_gate] + r[n_gate:2 * n_gate] + r[2 * n_gate:3 * n_gate]

            parts.append(jnp.where(row < 2 * ML_HEADS, x,
                                   jnp.where(row < 3 * ML_HEADS, cumsum(incl_before), cumsum(incl_after))))
        gfin = jnp.concatenate(parts, axis=1)
        gt_ref[...] = gfin
        pad = jnp.zeros((GATE_LANES - n_gate, tm), F32)
        g_ref[...] = jnp.concatenate([gfin, pad], axis=0).T

    is_bf = jnp.logical_and(jnp.bitwise_and(j, 1) == 1, j < 7)
    tn = of_ref.shape[1]
    swap = lambda a, b: (j == a).astype(jnp.int32) * (b - a) + (j == b).astype(jnp.int32) * (a - b)
    w_tile = j + swap(1, 2) + swap(4, 5)
    w_cols = pl.ds(pl.multiple_of(w_tile * tn, tn), tn)

    @pl.when(is_bf)
    def _():
        ob_ref[...] = jnp.dot(u_ref[...], w_ref[:, w_cols], preferred_element_type=F32).astype(BF16)

    @pl.when(jnp.logical_not(is_bf))
    def _():
        of_ref[...] = jnp.dot(u_ref[...], w_ref[:, w_cols], preferred_element_type=F32)


def _tiles_per_row(rows_per_mod, tm):
    if rows_per_mod is None:
        return None
    assert rows_per_mod % tm == 0
    return rows_per_mod // tm


def _in_proj(x, mod, w_main, w_gate, b_gate, *, row_base, rows_per_mod, tm):
    n, d = x.shape
    tiles_per_row = _tiles_per_row(rows_per_mod, tm)
    tn = d
    n_j = w_main.shape[1] // tn
    assert n_j == 8
    kern = functools.partial(_inproj_kernel, d=d, tm=tm, row_base=row_base, tiles_per_row=tiles_per_row)
    return pl.pallas_call(
        kern,
        grid=(n // tm, n_j),
        in_specs=[pl.BlockSpec((tm, d), lambda i, j: (i, 0)),
                  pl.BlockSpec(mod.shape, lambda i, j: (0, 0)),
                  pl.BlockSpec(w_main.shape, lambda i, j: (0, 0), pipeline_mode=pl.Buffered(1)),
                  pl.BlockSpec((d, GATE_LANES), lambda i, j: (0, 0)),
                  pl.BlockSpec((4 * ML_HEADS, 1), lambda i, j: (0, 0))],
        out_specs=[
            pl.BlockSpec((None, tm, tn), lambda i, j: (j // 2 + j // 7, i, 0)),
            pl.BlockSpec((None, tm, tn), lambda i, j: (jnp.minimum(jnp.maximum(j - 1, 0) // 2, 2), i, 0)),
            pl.BlockSpec((tm, GATE_LANES), lambda i, j: (i, 0)),
            pl.BlockSpec((4 * ML_HEADS, tm), lambda i, j: (0, i))],
        out_shape=[jax.ShapeDtypeStruct((5, n, d), F32),
                   jax.ShapeDtypeStruct((3, n, d), BF16),
                   jax.ShapeDtypeStruct((n, GATE_LANES), F32),
                   jax.ShapeDtypeStruct((4 * ML_HEADS, n), F32)],
        scratch_shapes=[pltpu.VMEM((tm, d), BF16)],
        compiler_params=_cparams(2),
        name="in_proj",
    )(x, mod, w_main, w_gate, b_gate)


def _rglru_kernel(xr_ref, zr_ref, w_ref, b_ref, lam_ref, cw_ref, cb_ref, h0_ref, *rest,
                  n_vseq, conv_len, n_seg, want_final):
    if want_final:
        out_ref, hfin_ref = rest[:2]
        rest = rest[2:]
    else:
        out_ref = rest[0]
        rest = rest[1:]
    a_f, b_f, a_b, b_b, cwm_ref = rest[:5]
    rest = rest[5:]
    seg = SCAN_SEG
    pitch = SCAN_PITCH
    ct = xr_ref.shape[1]
    ch = seg
    la = (RG_C * LOG2E) * _log_sigmoid(lam_ref[0])
    la_f, la_b = la[:, :ct], la[:, ct:]
    w = w_ref[0]
    bias = b_ref[0]
    cw = cw_ref[...]
    cb = cb_ref[...]
    pos = jnp.bitwise_and(lax.broadcasted_iota(jnp.int32, (ch, ct), 0), conv_len - 1)
    cwm_ref[0] = jnp.where(pos >= 2, cw[0:1], 0.0)
    cwm_ref[1] = jnp.where(pos >= 1, cw[1:2], 0.0)
    cwm_ref[2] = jnp.where(pos <= conv_len - 2, cw[3:4], 0.0)

    def gate_chunk(ci, carry):
        r0 = pl.multiple_of(ci * ch, ch)
        x = xr_ref[pl.ds(r0, ch), :]
        xc = (cwm_ref[0] * pltpu.roll(x, 2, 0) + cwm_ref[1] * pltpu.roll(x, 1, 0) + cw[2:3] * x
              + cwm_ref[2] * pltpu.roll(x, ch - 1, 0) + cb)
        gates = jnp.dot(xc.astype(BF16), w, preferred_element_type=F32) + bias

        def a_and_b(r_pre, i_pre, la_dir):
            a = jnp.exp2(la_dir / (1.0 + jnp.exp2(r_pre * (-LOG2E))))
            y = jnp.clip(1.0 - a * a, 0.0, 1.0)
            gain = jnp.where(y > 0.0, y * lax.rsqrt(y), 0.0)
            return a, (gain * xc) / (1.0 + jnp.exp2(i_pre * (-LOG2E)))

        af, bf = a_and_b(gates[:, 0:ct], gates[:, ct:2 * ct], la_f)
        ab, bb = a_and_b(gates[:, 2 * ct:3 * ct], gates[:, 3 * ct:4 * ct], la_b)
        s0 = pl.ds(pl.multiple_of(ci * pitch, SUBLANES), ch)
        a_f[s0, :] = af
        b_f[s0, :] = bf
        a_b[s0, :] = ab
        b_b[s0, :] = bb
        return carry

    lax.fori_loop(0, n_vseq, gate_chunk, 0, unroll=2)

    def slab_idx(t):
        return pl.ds(t, n_vseq, stride=pitch)

    def slab(ref, t):
        return ref[slab_idx(t), :]

    if n_seg == 1:
        init_f = h0_ref[0]
        init_b = h0_ref[1]
    else:
        e_f, p_f, e_b, p_b, cin_f, cin_b = rest
        n_real = n_vseq // n_seg
        zero = jnp.zeros((n_vseq, ct), F32)
        one = jnp.ones((n_vseq, ct), F32)

        def local_step(t, carry):
            ef, pf, eb, pb = carry
            tb = seg - 1 - t
            a = slab(a_f, t)
            ef = a * ef + slab(b_f, t)
            pf = a * pf
            a2 = slab(a_b, tb)
            eb = a2 * eb + slab(b_b, tb)
            pb = a2 * pb
            return ef, pf, eb, pb

        ef, pf, eb, pb = lax.fori_loop(0, seg, local_step, (zero, one, zero, one), unroll=8)
        e_f[...] = ef
        p_f[...] = pf
        e_b[...] = eb
        p_b[...] = pb
        carry = h0_ref[0]
        for s in range(n_seg):
            idx = pl.ds(s, n_real, stride=n_seg)
            cin_f[idx, :] = carry
            carry = e_f[idx, :] + p_f[idx, :] * carry
        carry = h0_ref[1]
        for s in reversed(range(n_seg)):
            idx = pl.ds(s, n_real, stride=n_seg)
            cin_b[idx, :] = carry
            carry = e_b[idx, :] + p_b[idx, :] * carry
        init_f = cin_f[...]
        init_b = cin_b[...]

    def scan_step(t, carry):
        hf, hb = carry
        tb = seg - 1 - t
        hf = slab(a_f, t) * hf + slab(b_f, t)
        b_f[slab_idx(t), :] = hf
        hb = slab(a_b, tb) * hb + slab(b_b, tb)
        b_b[slab_idx(tb), :] = hb
        return hf, hb

    hf_last, hb_last = lax.fori_loop(0, seg, scan_step, (init_f, init_b), unroll=8)
    if want_final:
        hfin_ref[0] = hf_last
        hfin_ref[1] = hb_last

    def out_chunk(ci, carry):
        r0 = pl.multiple_of(ci * ch, ch)
        s0 = pl.ds(pl.multiple_of(ci * pitch, SUBLANES), ch)
        h = b_f[s0, :] + b_b[s0, :]
        out_ref[pl.ds(r0, ch), :] = (h * jax.nn.gelu(zr_ref[pl.ds(r0, ch), :])).astype(BF16)
        return carry

    lax.fori_loop(0, n_vseq, out_chunk, 0)


def _rglru(of32, w_cat, b_cat, lam_cat, conv_w, conv_b, h0, *, n_seq, seq_len, conv_len, want_final):
    n = of32.shape[1]
    d_rnn = conv_w.shape[1]
    ct = RG_CT
    n_ct = d_rnn // ct
    n_seg = seq_len // SCAN_SEG
    n_vseq = n_seq * n_seg
    assert n_vseq * SCAN_SEG == n and SCAN_SEG % conv_len == 0 and (conv_len & (conv_len - 1)) == 0
    kern = functools.partial(_rglru_kernel, n_vseq=n_vseq, conv_len=conv_len, n_seg=n_seg,
                             want_final=want_final)
    out_specs = [pl.BlockSpec((n, ct), lambda c: (0, c))]
    out_shape = [jax.ShapeDtypeStruct((n, d_rnn), BF16)]
    if want_final:
        out_specs.append(pl.BlockSpec((2, n_seq, ct), lambda c: (0, 0, c)))
        out_shape.append(jax.ShapeDtypeStruct((2, n_seq, d_rnn), F32))
    scratch = [pltpu.VMEM((n_vseq * SCAN_PITCH, ct), F32) for _ in range(4)]
    scratch.append(pltpu.VMEM((CONV_W - 1, SCAN_SEG, ct), F32))
    if n_seg > 1:
        scratch += [pltpu.VMEM((n_vseq, ct), F32) for _ in range(6)]
    return pl.pallas_call(
        kern,
        grid=(n_ct,),
        in_specs=[pl.BlockSpec((None, n, ct), lambda c: (0, 0, c)),
                  pl.BlockSpec((None, n, ct), lambda c: (1, 0, c)),
                  pl.BlockSpec((1, ct, 4 * ct), lambda c: (c, 0, 0)),
                  pl.BlockSpec((1, 1, 4 * ct), lambda c: (c, 0, 0)),
                  pl.BlockSpec((1, 1, 2 * ct), lambda c: (c, 0, 0)),
                  pl.BlockSpec((CONV_W, ct), lambda c: (0, c)),
                  pl.BlockSpec((1, ct), lambda c: (0, c)),
                  pl.BlockSpec((2, n_seq, ct), lambda c: (0, 0, c))],
        out_specs=out_specs,
        out_shape=out_shape,
        scratch_shapes=scratch,
        compiler_params=_cparams(1),
        name="rglru_ctx" if want_final else "rglru_lat",
    )(of32, of32, w_cat, b_cat, lam_cat, conv_w, conv_b, h0)


def _causal_bias(mask_ref):
    lc = mask_ref.shape[1]
    ii = lax.broadcasted_iota(jnp.int32, (lc, lc), 0)
    jj = lax.broadcasted_iota(jnp.int32, (lc, lc), 1)
    mask_ref[0] = jnp.where(jj <= ii, 0.0, -jnp.inf)
    mask_ref[1] = jnp.where(jj >= ii, 0.0, -jnp.inf)


def _with_ones(v):
    return jnp.concatenate([v, jnp.ones((v.shape[0], GATE_LANES), v.dtype)], axis=1)


def _mlstm_unit(direction, head, q, k, v, v1, gc, gt_ref, r0, mask_ref, state, scale):
    lc, hd = q.shape
    col_i = direction * ML_HEADS + head
    col_c = 2 * ML_HEADS + col_i
    lane = lax.broadcasted_iota(jnp.int32, (1, GATE_LANES), 1)
    ig_col = jnp.sum(jnp.where(lane == col_i, gc, 0.0), -1, keepdims=True)
    cum_col = jnp.sum(jnp.where(lane == col_c, gc, 0.0), -1, keepdims=True)
    ig_row = gt_ref[pl.ds(col_i, 1), pl.ds(r0, lc)]
    cum_row = gt_ref[pl.ds(col_c, 1), pl.ds(r0, lc)]
    dmat = (cum_col + (ig_row - cum_row)) + mask_ref[direction]
    if state is None:
        m_prev = 0.0
    else:
        s_prev, m_prev = state
    inter = cum_col + m_prev
    m_row = jnp.maximum(jnp.max(dmat, -1, keepdims=True), inter)
    wts = jnp.exp(dmat - m_row)
    s = lax.dot_general(q, k, (((1,), (1,)), ((), ())), preferred_element_type=F32) * (wts * scale)
    if state is None:
        num = jnp.dot(s.astype(BF16), v, preferred_element_type=F32)
        h = num * (1.0 / jnp.maximum(jnp.abs(jnp.sum(s, -1, keepdims=True)), jnp.exp(-m_row)))
    else:
        w_inter = jnp.exp(inter - m_row) * scale
        nd = (jnp.dot(s.astype(BF16), v1, preferred_element_type=F32)
              + w_inter * jnp.dot(q, s_prev.astype(BF16), preferred_element_type=F32))
        rden = 1.0 / jnp.maximum(jnp.abs(nd[:, hd:]), jnp.exp(-m_row))
        h = nd[:, :hd] * jnp.concatenate([rden] * (hd // GATE_LANES), axis=1)
    btot = cum_col[lc - 1:lc] if direction == 0 else cum_col[0:1]
    wk_log = btot - cum_col + ig_col
    m_new = jnp.maximum(btot + m_prev, jnp.max(wk_log, 0, keepdims=True))
    kw = k.astype(F32) * jnp.exp(wk_log - m_new)
    t_lhs = (((0,), (0,)), ((), ()))
    if state is None:
        c_new = lax.dot_general(kw.astype(BF16), v, t_lhs, preferred_element_type=F32)
        return h, c_new, jnp.sum(kw, 0, keepdims=True), m_new
    decay = jnp.exp(btot + m_prev - m_new)
    s_new = decay * s_prev + lax.dot_general(kw.astype(BF16), v1, t_lhs, preferred_element_type=F32)
    return h, s_new, None, m_new


def _head_out(hsum, o, gn):
    return (jax.nn.sigmoid(o) * (_ln(hsum) * gn)).astype(BF16)


def _mlstm_ctx_kernel(q_ref, k_ref, v_ref, o_ref, g_ref, gt_ref, gn_ref,
                      out_ref, cfin_ref, nfin_ref, mfin_ref, mask_ref, *, n_sub, scale):
    head = pl.program_id(1)
    lc = MLSTM_CHUNK
    gn = gn_ref[...]
    _causal_bias(mask_ref)

    def body(s, carry):
        r0 = pl.multiple_of(s * lc, lc)
        rs = pl.ds(r0, lc)
        q, k, v, gc = q_ref[rs, :], k_ref[rs, :], v_ref[rs, :], g_ref[rs, :]
        hsum = None
        for direction in (0, 1):
            h, c_new, n_new, m_new = _mlstm_unit(direction, head, q, k, v, None, gc, gt_ref, r0, mask_ref,
                                                 None, scale)
            cfin_ref[s, direction, 0] = c_new
            nfin_ref[s, direction, 0] = n_new
            mfin_ref[s, direction, 0] = jnp.broadcast_to(m_new, (1, GATE_LANES))
            hsum = h if hsum is None else hsum + h
        out_ref[rs, :] = _head_out(hsum, o_ref[rs, :], gn)
        return carry

    lax.fori_loop(0, n_sub, body, 0)


def _mlstm_lat_kernel(q_ref, k_ref, v_ref, o_ref, g_ref, gt_ref, gn_ref, c0_ref, n0_ref, m0_ref,
                      out_ref, s_st, m_st, h_acc, mask_ref, *, n_chunks, scale):
    head = pl.program_id(1)
    lc = MLSTM_CHUNK
    hd = q_ref.shape[1]
    gn = gn_ref[...]
    _causal_bias(mask_ref)
    for direction in (0, 1):
        s_st[direction, :, 0:hd] = c0_ref[0, direction, 0]
        s_st[direction, :, hd:] = n0_ref[0, direction, 0]
        m_st[direction] = m0_ref[0, direction, 0]
    h_acc[...] = jnp.zeros(h_acc.shape, F32)

    def body(c, carry):
        for direction in (0, 1):
            cc = c if direction == 0 else n_chunks - 1 - c
            r0 = pl.multiple_of(cc * lc, lc)
            rs = pl.ds(r0, lc)
            state = (s_st[direction], m_st[direction][:, 0:1])
            v = v_ref[rs, :]
            h, s_new, _, m_new = _mlstm_unit(direction, head, q_ref[rs, :], k_ref[rs, :], v, _with_ones(v),
                                             g_ref[rs, :], gt_ref, r0, mask_ref, state, scale)
            s_st[direction] = s_new
            m_st[direction] = jnp.broadcast_to(m_new, (1, GATE_LANES))
            h_acc[rs, :] += h
        return carry

    lax.fori_loop(0, n_chunks, body, 0, unroll=2)

    def out_chunk(c, carry):
        rs = pl.ds(pl.multiple_of(c * lc, lc), lc)
        out_ref[rs, :] = _head_out(h_acc[rs, :], o_ref[rs, :], gn)
        return carry

    lax.fori_loop(0, n_chunks, out_chunk, 0)


def _mlstm_in_specs(rows, hd):
    slab = lambda idx: pl.BlockSpec((None, rows, hd), lambda s, h: (idx, s, h))
    return [slab(0), slab(1), slab(2), slab(2),
            pl.BlockSpec((rows, GATE_LANES), lambda s, h: (s, 0)),
            pl.BlockSpec((4 * ML_HEADS, rows), lambda s, h: (0, s)),
            pl.BlockSpec((1, hd), lambda s, h: (0, h))]


def _mlstm_ctx(obf, of32, g, gt, gn_g, *, n_seq):
    n = obf.shape[1]
    ml_dim = gn_g.shape[1]
    hd = ml_dim // ML_HEADS
    lc = MLSTM_CHUNK
    assert n == n_seq * lc
    n_sub = 4
    rows = n_sub * lc
    kern = functools.partial(_mlstm_ctx_kernel, n_sub=n_sub, scale=hd ** -0.5)
    return pl.pallas_call(
        kern,
        grid=(n_seq // n_sub, ML_HEADS),
        in_specs=_mlstm_in_specs(rows, hd),
        out_specs=[pl.BlockSpec((rows, hd), lambda s, h: (s, h)),
                   pl.BlockSpec((n_sub, 2, 1, hd, hd), lambda s, h: (s, 0, h, 0, 0)),
                   pl.BlockSpec((n_sub, 2, 1, 1, hd), lambda s, h: (s, 0, h, 0, 0)),
                   pl.BlockSpec((n_sub, 2, 1, 1, GATE_LANES), lambda s, h: (s, 0, h, 0, 0))],
        out_shape=[jax.ShapeDtypeStruct((n, ml_dim), BF16),
                   jax.ShapeDtypeStruct((n_seq, 2, ML_HEADS, hd, hd), F32),
                   jax.ShapeDtypeStruct((n_seq, 2, ML_HEADS, 1, hd), F32),
                   jax.ShapeDtypeStruct((n_seq, 2, ML_HEADS, 1, GATE_LANES), F32)],
        scratch_shapes=[pltpu.VMEM((2, lc, lc), F32)],
        compiler_params=_cparams(2),
        name="mlstm_ctx",
    )(obf, obf, obf, of32, g, gt, gn_g)


def _mlstm_lat(obf, of32, g, gt, gn_g, c0, n0, m0, *, n_seq):
    n = obf.shape[1]
    ml_dim = gn_g.shape[1]
    hd = ml_dim // ML_HEADS
    lc = MLSTM_CHUNK
    rows = n // n_seq
    n_chunks = rows // lc
    kern = functools.partial(_mlstm_lat_kernel, n_chunks=n_chunks, scale=hd ** -0.5)
    return pl.pallas_call(
        kern,
        grid=(n_seq, ML_HEADS),
        in_specs=_mlstm_in_specs(rows, hd) + [
                  pl.BlockSpec((1, 2, 1, hd, hd), lambda s, h: (s, 0, h, 0, 0)),
                  pl.BlockSpec((1, 2, 1, hd, GATE_LANES), lambda s, h: (s, 0, h, 0, 0)),
                  pl.BlockSpec((1, 2, 1, 1, GATE_LANES), lambda s, h: (s, 0, h, 0, 0))],
        out_specs=pl.BlockSpec((rows, hd), lambda s, h: (s, h)),
        out_shape=jax.ShapeDtypeStruct((n, ml_dim), BF16),
        scratch_shapes=[pltpu.VMEM((2, hd, hd + GATE_LANES), F32),
                        pltpu.VMEM((2, 1, GATE_LANES), F32),
                        pltpu.VMEM((rows, hd), F32),
                        pltpu.VMEM((2, lc, lc), F32)],
        compiler_params=_cparams(2),
        name="mlstm_lat",
    )(obf, obf, obf, of32, g, gt, gn_g, c0, n0, m0)


def _merge_kernel(hrg_ref, hml_ref, gma_ref, gmb_ref, x_ref, mod_ref, wrg_ref, wml_ref, wout_ref,
                  bm_ref, lng_ref, lnb_ref, o_ref, *, d, alpha, row_base, tiles_per_row):
    r = _mod_row(pl.program_id(0), row_base, tiles_per_row)
    gate1 = mod_ref[pl.ds(r, 1), 2 * d:3 * d]
    y_rg = jnp.dot(hrg_ref[...], wrg_ref[...], preferred_element_type=F32)
    y_ml = jnp.dot(hml_ref[...], wml_ref[...], preferred_element_type=F32)
    g_rg = jax.nn.sigmoid(gma_ref[...] + bm_ref[:, 0:d])
    g_ml = jax.nn.sigmoid(gmb_ref[...] + bm_ref[:, d:2 * d])
    merged = (g_rg * y_rg + g_ml * y_ml).astype(BF16)
    mix = jnp.dot(merged, wout_ref[...], preferred_element_type=F32)
    o_ref[...] = _ln(alpha * x_ref[...] + gate1 * mix) * lng_ref[...] + lnb_ref[...]


def _merge(hrg, hml, of32, x, mod, w_rg, w_ml, w_out, b_merge, ln_g, ln_b, *, alpha, row_base,
           rows_per_mod, tm):
    n, d = x.shape
    tiles_per_row = _tiles_per_row(rows_per_mod, tm)
    kern = functools.partial(_merge_kernel, d=d, alpha=alpha, row_base=row_base, tiles_per_row=tiles_per_row)
    full = lambda shape: pl.BlockSpec(shape, lambda i: (0,) * len(shape))
    return pl.pallas_call(
        kern,
        grid=(n // tm,),
        in_specs=[pl.BlockSpec((tm, d), lambda i: (i, 0)),
                  pl.BlockSpec((tm, d), lambda i: (i, 0)),
                  pl.BlockSpec((None, tm, d), lambda i: (3, i, 0)),
                  pl.BlockSpec((None, tm, d), lambda i: (4, i, 0)),
                  pl.BlockSpec((tm, d), lambda i: (i, 0)),
                  full(mod.shape), full((d, d)), full((d, d)), full((d, d)),
                  full((1, 2 * d)), full((1, d)), full((1, d))],
        out_specs=pl.BlockSpec((tm, d), lambda i: (i, 0)),
        out_shape=jax.ShapeDtypeStruct((n, d), F32),
        compiler_params=_cparams(1),
        name="merge",
    )(hrg, hml, of32, of32, x, mod, w_rg, w_ml, w_out, b_merge, ln_g, ln_b)


def _mlp_kernel(x_ref, mod_ref, wfc_ref, bfc_ref, wpj_ref, bpj_ref, lng_ref, lnb_ref, o_ref,
                u_ref, acc_ref, *, d, alpha, row_base, tiles_per_row):
    f = pl.program_id(1)
    r = _mod_row(pl.program_id(0), row_base, tiles_per_row)

    @pl.when(f == 0)
    def _():
        shift = mod_ref[pl.ds(r, 1), 3 * d:4 * d]
        scale = mod_ref[pl.ds(r, 1), 4 * d:5 * d]
        u_ref[...] = (_ln(x_ref[...]) * (1.0 + scale) + shift).astype(BF16)
        acc_ref[...] = jnp.zeros(acc_ref.shape, F32)

    hid = jnp.dot(u_ref[...], wfc_ref[...], preferred_element_type=F32) + bfc_ref[...]
    hid = jnp.square(jnp.maximum(hid, 0.0)).astype(BF16)
    acc_ref[...] += jnp.dot(hid, wpj_ref[...], preferred_element_type=F32)

    @pl.when(f == pl.num_programs(1) - 1)
    def _():
        gate2 = mod_ref[pl.ds(r, 1), 5 * d:6 * d]
        y = alpha * x_ref[...] + gate2 * (acc_ref[...] + bpj_ref[...])
        o_ref[...] = _ln(y) * lng_ref[...] + lnb_ref[...]


def _mlp(x, mod, w_fc, b_fc, w_proj, b_proj, ln_g, ln_b, *, alpha, row_base, rows_per_mod, tm):
    n, d = x.shape
    tiles_per_row = _tiles_per_row(rows_per_mod, tm)
    d_ff = w_fc.shape[1]
    tf = 1024
    kern = functools.partial(_mlp_kernel, d=d, alpha=alpha, row_base=row_base, tiles_per_row=tiles_per_row)
    return pl.pallas_call(
        kern,
        grid=(n // tm, d_ff // tf),
        in_specs=[pl.BlockSpec((tm, d), lambda i, f: (i, 0)),
                  pl.BlockSpec(mod.shape, lambda i, f: (0, 0)),
                  pl.BlockSpec((d, tf), lambda i, f: (0, f)),
                  pl.BlockSpec((1, tf), lambda i, f: (0, f)),
                  pl.BlockSpec((tf, d), lambda i, f: (f, 0)),
                  pl.BlockSpec((1, d), lambda i, f: (0, 0)),
                  pl.BlockSpec((1, d), lambda i, f: (0, 0)),
                  pl.BlockSpec((1, d), lambda i, f: (0, 0))],
        out_specs=pl.BlockSpec((tm, d), lambda i, f: (i, 0)),
        out_shape=jax.ShapeDtypeStruct((n, d), F32),
        scratch_shapes=[pltpu.VMEM((tm, d), BF16), pltpu.VMEM((tm, d), F32)],
        compiler_params=_cparams(2),
        name="mlp",
    )(x, mod, w_fc, b_fc, w_proj, b_proj, ln_g, ln_b)


def _rg_gate_weights(wa, ba, wx, bx, lam):
    ct = RG_CT
    d_rnn = ba.shape[-1]
    n_ct = d_rnn // ct
    per = ct // RG_BW

    def tile_blockdiag(w):
        wt = w.reshape(n_ct, per, RG_BW, RG_BW)
        eye = jnp.eye(per, dtype=w.dtype)
        return jnp.einsum('cpkj,pq->cpkqj', wt, eye).reshape(n_ct, ct, ct)

    w_cat = jnp.concatenate([tile_blockdiag(wa[0]), tile_blockdiag(wx[0]),
                             tile_blockdiag(wa[1]), tile_blockdiag(wx[1])], axis=-1).astype(BF16)
    tiles = lambda b: b.reshape(n_ct, 1, ct)
    b_cat = jnp.concatenate([tiles(ba[0]), tiles(bx[0]), tiles(ba[1]), tiles(bx[1])], axis=-1)
    lam_cat = jnp.concatenate([tiles(lam[0]), tiles(lam[1])], axis=-1)
    return w_cat, b_cat, lam_cat


def kernel(x_prompt, x_sample, state_rglru_h, state_mlstm_C, state_mlstm_n, state_mlstm_m, c, c_ctx, w_ada, b_ada, w_in, rg_conv_w, rg_conv_b, rg_wa, rg_ba, rg_wx, rg_bx, rg_lambda, w_rg_proj, ml_b_igate, ml_b_fgate, ml_gn_g, w_ml_proj, b_merge, w_out, ln_g, ln_b, w_fc, b_fc, w_proj, b_proj):
    bp, seq, d = x_prompt.shape
    bl, dec_seq, _ = x_sample.shape
    depth = w_in.shape[0]
    d_rnn = rg_conv_w.shape[-1]
    ml_dim = ml_gn_g.shape[-1]
    hd = ml_dim // ML_HEADS
    alpha = (2 * depth) ** 0.25
    d_main = 2 * d_rnn + 4 * ml_dim + 2 * d
    assert seq == MLSTM_CHUNK == SCAN_SEG and dec_seq % MLSTM_CHUNK == 0
    assert d_rnn == d and ml_dim == d and w_in.shape[-1] == d_main + 4 * ML_HEADS

    tm = 1024
    xp = x_prompt.reshape(bp * seq, d)
    xs = x_sample.reshape(bl * dec_seq, d)
    cond = jnp.concatenate([c_ctx[None, :], c], axis=0)
    zero_h = jnp.zeros((2, bp, d_rnn), F32)
    new_h, new_c, new_n, new_m = [], [], [], []
    for l in range(depth):
        mod = _ada(cond, w_ada[l], b_ada[l])
        w_main = w_in[l][:, :d_main].astype(BF16)
        w_gate = jnp.zeros((d, GATE_LANES), BF16).at[:, :4 * ML_HEADS].set(w_in[l][:, d_main:].astype(BF16))
        b_gate = jnp.concatenate([ml_b_igate[l].reshape(-1), ml_b_fgate[l].reshape(-1)]).reshape(-1, 1)
        w_cat, b_cat, lam_cat = _rg_gate_weights(rg_wa[l], rg_ba[l], rg_wx[l], rg_bx[l], rg_lambda[l])
        conv_b = rg_conv_b[l].reshape(1, d_rnn)
        gn_g = ml_gn_g[l].reshape(1, ml_dim)
        w_rg, w_ml, w_o = (w_rg_proj[l].astype(BF16), w_ml_proj[l].astype(BF16), w_out[l].astype(BF16))
        w_fc_b, w_pj_b = w_fc[l].astype(BF16), w_proj[l].astype(BF16)
        bm = b_merge[l].reshape(1, 2 * d)
        lng0, lnb0 = ln_g[l, 0].reshape(1, d), ln_b[l, 0].reshape(1, d)
        lng1, lnb1 = ln_g[l, 1].reshape(1, d), ln_b[l, 1].reshape(1, d)
        bfc, bpj = b_fc[l].reshape(1, -1), b_proj[l].reshape(1, d)

        def tail(x, hrg, hml, of32, row_base, rows_per_mod):
            x1 = _merge(hrg, hml, of32, x, mod, w_rg, w_ml, w_o, bm, lng0, lnb0, alpha=alpha,
                        row_base=row_base, rows_per_mod=rows_per_mod, tm=512)
            return _mlp(x1, mod, w_fc_b, bfc, w_pj_b, bpj, lng1, lnb1, alpha=alpha,
                        row_base=row_base, rows_per_mod=rows_per_mod, tm=tm)

        of32, obf, g, gt = _in_proj(xp, mod, w_main, w_gate, b_gate, row_base=0, rows_per_mod=None, tm=tm)
        hrg, h_fin = _rglru(of32, w_cat, b_cat, lam_cat, rg_conv_w[l], conv_b, zero_h,
                            n_seq=bp, seq_len=seq, conv_len=seq, want_final=True)
        hml, c_fin, n_fin, m_fin = _mlstm_ctx(obf, of32, g, gt, gn_g, n_seq=bp)
        xp = tail(xp, hrg, hml, of32, 0, None)
        new_h.append(jnp.transpose(h_fin, (1, 0, 2)))
        new_c.append(c_fin)
        new_n.append(n_fin.reshape(bp, 2, ML_HEADS, hd))
        new_m.append(m_fin[..., 0, 0])

        of32, obf, g, gt = _in_proj(xs, mod, w_main, w_gate, b_gate, row_base=1,
                                    rows_per_mod=dec_seq, tm=tm)
        h0 = jnp.transpose(state_rglru_h[:, l], (1, 0, 2))
        (hrg,) = _rglru(of32, w_cat, b_cat, lam_cat, rg_conv_w[l], conv_b, h0,
                        n_seq=bl, seq_len=dec_seq, conv_len=GRID_W, want_final=False)
        c0 = state_mlstm_C[:, l]
        n0 = jnp.broadcast_to(state_mlstm_n[:, l][..., None], (bl, 2, ML_HEADS, hd, GATE_LANES))
        m0 = jnp.broadcast_to(state_mlstm_m[:, l].reshape(bl, 2, ML_HEADS, 1, 1),
                              (bl, 2, ML_HEADS, 1, GATE_LANES))
        hml = _mlstm_lat(obf, of32, g, gt, gn_g, c0, n0, m0, n_seq=bl)
        xs = tail(xs, hrg, hml, of32, 1, dec_seq)

    def stack(parts):
        return parts[0][:, None] if len(parts) == 1 else jnp.stack(parts, axis=1)

    return (xp.reshape(bp, seq, d), xs.reshape(bl, dec_seq, d),
            stack(new_h), stack(new_c), stack(new_n), stack(new_m))
```

```python
import functools

import jax
import jax.numpy as jnp
from jax import lax
from jax.experimental import pallas as pl
from jax.experimental.pallas import tpu as pltpu

F32 = jnp.float32
BF16 = jnp.bfloat16

LN_EPS = 1e-5
LOG2E = 1.4426950408889634
RG_C = 8.0
RG_BW = 64
CONV_W = 4
GRID_W = 64
ML_HEADS = 4
MLSTM_CHUNK = 256
SCAN_SEG = 256
SUBLANES = 8
SCAN_PITCH = SCAN_SEG + SUBLANES
RG_CT = 128
GATE_LANES = 128
VMEM_LIMIT = 52 * 1024 * 1024
IN_PROJ_VMEM_LIMIT = 58 * 1024 * 1024


def _cparams(n_axes):
    return pltpu.CompilerParams(
        dimension_semantics=("arbitrary",) * n_axes, vmem_limit_bytes=VMEM_LIMIT)


def _log_sigmoid(x):
    return jnp.minimum(x, 0.0) - jnp.log1p(jnp.exp(-jnp.abs(x)))


def _ln(x):
    mu = jnp.mean(x, -1, keepdims=True)
    xc = x - mu
    var = jnp.mean(xc * xc, -1, keepdims=True)
    return xc * lax.rsqrt(var + LN_EPS)


def _mod_row(i, row_base, tiles_per_row):
    if tiles_per_row is None:
        return row_base
    return row_base + lax.div(i, jnp.int32(tiles_per_row))


def _ada_kernel(ct_ref, w_ref, b_ref, o_ref, *, n_rows):
    w = w_ref[...]
    ct = ct_ref[...]
    s = ct * jax.nn.sigmoid(ct)
    o_ref[...] = jnp.zeros(o_ref.shape, F32)
    for r in range(n_rows):
        o_ref[r:r + 1, :] = jnp.sum(w * s[:, r:r + 1], axis=0, keepdims=True) + b_ref[...]


def _ada(cond, w_ada, b_ada):
    n_rows, d = cond.shape
    assert n_rows <= 8
    ct = jnp.zeros((d, 8), F32).at[:, :n_rows].set(cond.T)
    n_out = w_ada.shape[1]
    tn = 512
    return pl.pallas_call(
        functools.partial(_ada_kernel, n_rows=n_rows),
        grid=(n_out // tn,),
        in_specs=[pl.BlockSpec((d, 8), lambda j: (0, 0)),
                  pl.BlockSpec((d, tn), lambda j: (0, j)),
                  pl.BlockSpec((1, tn), lambda j: (0, j))],
        out_specs=pl.BlockSpec((8, tn), lambda j: (0, j)),
        out_shape=jax.ShapeDtypeStruct((8, n_out), F32),
        compiler_params=_cparams(1),
        name="ada_mod",
    )(ct, w_ada, b_ada.reshape(1, n_out))


def _inproj_kernel(x_ref, mod_ref, w_ref, wg_ref, bg_ref, of_ref, ob_ref, g_ref, gt_ref, u_ref,
                   *, d, tm, row_base, tiles_per_row):
    lc = MLSTM_CHUNK
    r = _mod_row(pl.program_id(0), row_base, tiles_per_row)
    shift = mod_ref[pl.ds(r, 1), 0:d]
    scale = mod_ref[pl.ds(r, 1), d:2 * d]
    u_ref[...] = (_ln(x_ref[...]) * (1.0 + scale) + shift).astype(BF16)
    n_gate = 4 * ML_HEADS
    g = jnp.dot(u_ref[...], wg_ref[...], preferred_element_type=F32)
    gt = g.T[0:n_gate, :] + bg_ref[...]
    gt = jnp.where(lax.broadcasted_iota(jnp.int32, gt.shape, 0) >= 2 * ML_HEADS, _log_sigmoid(gt), gt)
    row = lax.broadcasted_iota(jnp.int32, (n_gate, lc), 0)
    ii = lax.broadcasted_iota(jnp.int32, (lc, lc), 0)
    jj = lax.broadcasted_iota(jnp.int32, (lc, lc), 1)
    incl_before = (ii <= jj).astype(BF16)
    incl_after = (ii >= jj).astype(BF16)
    parts = []
    for c in range(tm // lc):
        x = gt[:, c * lc:(c + 1) * lc]
        hi = x.astype(BF16)
        r1 = x - hi.astype(F32)
        mid = r1.astype(BF16)
        lo = (r1 - mid.astype(F32)).astype(BF16)
        split = jnp.concatenate([hi, mid, lo], axis=0)

        def cumsum(tri):
            res = jnp.dot(split, tri, preferred_element_type=F32)
            return res[0:n_gate] + res[n_gate:2 * n_gate] + res[2 * n_gate:3 * n_gate]

        parts.append(jnp.where(row < 2 * ML_HEADS, x,
                               jnp.where(row < 3 * ML_HEADS, cumsum(incl_before), cumsum(incl_after))))
    gfin = jnp.concatenate(parts, axis=1)
    gt_ref[...] = gfin
    pad = jnp.zeros((GATE_LANES - n_gate, tm), F32)
    g_ref[...] = jnp.concatenate([gfin, pad], axis=0).T

    def col_tile(t):
        return jnp.dot(u_ref[...], w_ref[:, t * d:(t + 1) * d], preferred_element_type=F32)

    for slab, t in enumerate((0, 1, 5, 6, 7)):
        of_ref[slab] = col_tile(t)
    for slab, t in enumerate((2, 3, 4)):
        ob_ref[slab] = col_tile(t).astype(BF16)


def _tiles_per_row(rows_per_mod, tm):
    if rows_per_mod is None:
        return None
    assert rows_per_mod % tm == 0
    return rows_per_mod // tm


def _in_proj(x, mod, w_main, w_gate, b_gate, *, row_base, rows_per_mod, tm):
    n, d = x.shape
    tiles_per_row = _tiles_per_row(rows_per_mod, tm)
    assert w_main.shape[1] == 8 * d
    kern = functools.partial(_inproj_kernel, d=d, tm=tm, row_base=row_base, tiles_per_row=tiles_per_row)
    return pl.pallas_call(
        kern,
        grid=(n // tm,),
        in_specs=[pl.BlockSpec((tm, d), lambda i: (i, 0)),
                  pl.BlockSpec(mod.shape, lambda i: (0, 0)),
                  pl.BlockSpec(w_main.shape, lambda i: (0, 0), pipeline_mode=pl.Buffered(1)),
                  pl.BlockSpec((d, GATE_LANES), lambda i: (0, 0)),
                  pl.BlockSpec((4 * ML_HEADS, 1), lambda i: (0, 0))],
        out_specs=[pl.BlockSpec((5, tm, d), lambda i: (0, i, 0)),
                   pl.BlockSpec((3, tm, d), lambda i: (0, i, 0)),
                   pl.BlockSpec((tm, GATE_LANES), lambda i: (i, 0)),
                   pl.BlockSpec((4 * ML_HEADS, tm), lambda i: (0, i))],
        out_shape=[jax.ShapeDtypeStruct((5, n, d), F32),
                   jax.ShapeDtypeStruct((3, n, d), BF16),
                   jax.ShapeDtypeStruct((n, GATE_LANES), F32),
                   jax.ShapeDtypeStruct((4 * ML_HEADS, n), F32)],
        scratch_shapes=[pltpu.VMEM((tm, d), BF16)],
        compiler_params=pltpu.CompilerParams(dimension_semantics=("arbitrary",),
                                             vmem_limit_bytes=IN_PROJ_VMEM_LIMIT),
        name="in_proj",
    )(x, mod, w_main, w_gate, b_gate)


def _rglru_kernel(xr_ref, zr_ref, w_ref, b_ref, lam_ref, cw_ref, cb_ref, h0_ref, *rest,
                  n_vseq, conv_len, n_seg, want_final):
    if want_final:
        out_ref, hfin_ref = rest[:2]
        rest = rest[2:]
    else:
        out_ref = rest[0]
        rest = rest[1:]
    a_f, b_f, a_b, b_b, cwm_ref = rest[:5]
    rest = rest[5:]
    seg = SCAN_SEG
    pitch = SCAN_PITCH
    ct = xr_ref.shape[1]
    ch = seg
    la = (RG_C * LOG2E) * _log_sigmoid(lam_ref[0])
    la_f, la_b = la[:, :ct], la[:, ct:]
    w = w_ref[0]
    bias = b_ref[0]
    cw = cw_ref[...]
    cb = cb_ref[...]
    pos = jnp.bitwise_and(lax.broadcasted_iota(jnp.int32, (ch, ct), 0), conv_len - 1)
    cwm_ref[0] = jnp.where(pos >= 2, cw[0:1], 0.0)
    cwm_ref[1] = jnp.where(pos >= 1, cw[1:2], 0.0)
    cwm_ref[2] = jnp.where(pos <= conv_len - 2, cw[3:4], 0.0)

    def gate_chunk(ci, carry):
        r0 = pl.multiple_of(ci * ch, ch)
        x = xr_ref[pl.ds(r0, ch), :]
        xc = (cwm_ref[0] * pltpu.roll(x, 2, 0) + cwm_ref[1] * pltpu.roll(x, 1, 0) + cw[2:3] * x
              + cwm_ref[2] * pltpu.roll(x, ch - 1, 0) + cb)
        gates = jnp.dot(xc.astype(BF16), w, preferred_element_type=F32) + bias

        def a_and_b(r_pre, i_pre, la_dir):
            a = jnp.exp2(la_dir / (1.0 + jnp.exp2(r_pre * (-LOG2E))))
            y = jnp.clip(1.0 - a * a, 0.0, 1.0)
            gain = jnp.where(y > 0.0, y * lax.rsqrt(y), 0.0)
            return a, (gain * xc) / (1.0 + jnp.exp2(i_pre * (-LOG2E)))

        af, bf = a_and_b(gates[:, 0:ct], gates[:, ct:2 * ct], la_f)
        ab, bb = a_and_b(gates[:, 2 * ct:3 * ct], gates[:, 3 * ct:4 * ct], la_b)
        s0 = pl.ds(pl.multiple_of(ci * pitch, SUBLANES), ch)
        a_f[s0, :] = af
        b_f[s0, :] = bf
        a_b[s0, :] = ab
        b_b[s0, :] = bb
        return carry

    lax.fori_loop(0, n_vseq, gate_chunk, 0, unroll=2)

    def slab_idx(t):
        return pl.ds(t, n_vseq, stride=pitch)

    def slab(ref, t):
        return ref[slab_idx(t), :]

    if n_seg == 1:
        init_f = h0_ref[0]
        init_b = h0_ref[1]
    else:
        e_f, p_f, e_b, p_b, cin_f, cin_b = rest
        n_real = n_vseq // n_seg
        zero = jnp.zeros((n_vseq, ct), F32)
        one = jnp.ones((n_vseq, ct), F32)

        def local_step(t, carry):
            ef, pf, eb, pb = carry
            tb = seg - 1 - t
            a = slab(a_f, t)
            ef = a * ef + slab(b_f, t)
            pf = a * pf
            a2 = slab(a_b, tb)
            eb = a2 * eb + slab(b_b, tb)
            pb = a2 * pb
            return ef, pf, eb, pb

        ef, pf, eb, pb = lax.fori_loop(0, seg, local_step, (zero, one, zero, one), unroll=8)
        e_f[...] = ef
        p_f[...] = pf
        e_b[...] = eb
        p_b[...] = pb
        carry = h0_ref[0]
        for s in range(n_seg):
            idx = pl.ds(s, n_real, stride=n_seg)
            cin_f[idx, :] = carry
            carry = e_f[idx, :] + p_f[idx, :] * carry
        carry = h0_ref[1]
        for s in reversed(range(n_seg)):
            idx = pl.ds(s, n_real, stride=n_seg)
            cin_b[idx, :] = carry
            carry = e_b[idx, :] + p_b[idx, :] * carry
        init_f = cin_f[...]
        init_b = cin_b[...]

    def scan_step(t, carry):
        hf, hb = carry
        tb = seg - 1 - t
        hf = slab(a_f, t) * hf + slab(b_f, t)
        b_f[slab_idx(t), :] = hf
        hb = slab(a_b, tb) * hb + slab(b_b, tb)
        b_b[slab_idx(tb), :] = hb
        return hf, hb

    hf_last, hb_last = lax.fori_loop(0, seg, scan_step, (init_f, init_b), unroll=8)
    if want_final:
        hfin_ref[0] = hf_last
        hfin_ref[1] = hb_last

    def out_chunk(ci, carry):
        r0 = pl.multiple_of(ci * ch, ch)
        s0 = pl.ds(pl.multiple_of(ci * pitch, SUBLANES), ch)
        h = b_f[s0, :] + b_b[s0, :]
        out_ref[pl.ds(r0, ch), :] = (h * jax.nn.gelu(zr_ref[pl.ds(r0, ch), :])).astype(BF16)
        return carry

    lax.fori_loop(0, n_vseq, out_chunk, 0)


def _rglru(of32, w_cat, b_cat, lam_cat, conv_w, conv_b, h0, *, n_seq, seq_len, conv_len, want_final):
    n = of32.shape[1]
    d_rnn = conv_w.shape[1]
    ct = RG_CT
    n_ct = d_rnn // ct
    n_seg = seq_len // SCAN_SEG
    n_vseq = n_seq * n_seg
    assert n_vseq * SCAN_SEG == n and SCAN_SEG % conv_len == 0 and (conv_len & (conv_len - 1)) == 0
    kern = functools.partial(_rglru_kernel, n_vseq=n_vseq, conv_len=conv_len, n_seg=n_seg,
                             want_final=want_final)
    out_specs = [pl.BlockSpec((n, ct), lambda c: (0, c))]
    out_shape = [jax.ShapeDtypeStruct((n, d_rnn), BF16)]
    if want_final:
        out_specs.append(pl.BlockSpec((2, n_seq, ct), lambda c: (0, 0, c)))
        out_shape.append(jax.ShapeDtypeStruct((2, n_seq, d_rnn), F32))
    scratch = [pltpu.VMEM((n_vseq * SCAN_PITCH, ct), F32) for _ in range(4)]
    scratch.append(pltpu.VMEM((CONV_W - 1, SCAN_SEG, ct), F32))
    if n_seg > 1:
        scratch += [pltpu.VMEM((n_vseq, ct), F32) for _ in range(6)]
    return pl.pallas_call(
        kern,
        grid=(n_ct,),
        in_specs=[pl.BlockSpec((None, n, ct), lambda c: (0, 0, c)),
                  pl.BlockSpec((None, n, ct), lambda c: (1, 0, c)),
                  pl.BlockSpec((1, ct, 4 * ct), lambda c: (c, 0, 0)),
                  pl.BlockSpec((1, 1, 4 * ct), lambda c: (c, 0, 0)),
                  pl.BlockSpec((1, 1, 2 * ct), lambda c: (c, 0, 0)),
                  pl.BlockSpec((CONV_W, ct), lambda c: (0, c)),
                  pl.BlockSpec((1, ct), lambda c: (0, c)),
                  pl.BlockSpec((2, n_seq, ct), lambda c: (0, 0, c))],
        out_specs=out_specs,
        out_shape=out_shape,
        scratch_shapes=scratch,
        compiler_params=_cparams(1),
        name="rglru_ctx" if want_final else "rglru_lat",
    )(of32, of32, w_cat, b_cat, lam_cat, conv_w, conv_b, h0)


def _causal_bias(mask_ref):
    lc = mask_ref.shape[1]
    ii = lax.broadcasted_iota(jnp.int32, (lc, lc), 0)
    jj = lax.broadcasted_iota(jnp.int32, (lc, lc), 1)
    mask_ref[0] = jnp.where(jj <= ii, 0.0, -jnp.inf)
    mask_ref[1] = jnp.where(jj >= ii, 0.0, -jnp.inf)


def _with_ones(v):
    return jnp.concatenate([v, jnp.ones((v.shape[0], GATE_LANES), v.dtype)], axis=1)


def _mlstm_unit(direction, head, q, k, v, v1, gc, gt_ref, r0, mask_ref, state, scale):
    lc, hd = q.shape
    col_i = direction * ML_HEADS + head
    col_c = 2 * ML_HEADS + col_i
    lane = lax.broadcasted_iota(jnp.int32, (1, GATE_LANES), 1)
    ig_col = jnp.sum(jnp.where(lane == col_i, gc, 0.0), -1, keepdims=True)
    cum_col = jnp.sum(jnp.where(lane == col_c, gc, 0.0), -1, keepdims=True)
    ig_row = gt_ref[pl.ds(col_i, 1), pl.ds(r0, lc)]
    cum_row = gt_ref[pl.ds(col_c, 1), pl.ds(r0, lc)]
    dmat = (cum_col + (ig_row - cum_row)) + mask_ref[direction]
    if state is None:
        m_prev = 0.0
    else:
        s_prev, m_prev = state
    inter = cum_col + m_prev
    m_row = jnp.maximum(jnp.max(dmat, -1, keepdims=True), inter)
    wts = jnp.exp(dmat - m_row)
    s = lax.dot_general(q, k, (((1,), (1,)), ((), ())), preferred_element_type=F32) * (wts * scale)
    if state is None:
        num = jnp.dot(s.astype(BF16), v, preferred_element_type=F32)
        h = num * (1.0 / jnp.maximum(jnp.abs(jnp.sum(s, -1, keepdims=True)), jnp.exp(-m_row)))
    else:
        w_inter = jnp.exp(inter - m_row) * scale
        nd = (jnp.dot(s.astype(BF16), v1, preferred_element_type=F32)
              + w_inter * jnp.dot(q, s_prev.astype(BF16), preferred_element_type=F32))
        rden = 1.0 / jnp.maximum(jnp.abs(nd[:, hd:]), jnp.exp(-m_row))
        h = nd[:, :hd] * jnp.concatenate([rden] * (hd // GATE_LANES), axis=1)
    btot = cum_col[lc - 1:lc] if direction == 0 else cum_col[0:1]
    wk_log = btot - cum_col + ig_col
    m_new = jnp.maximum(btot + m_prev, jnp.max(wk_log, 0, keepdims=True))
    kw = k.astype(F32) * jnp.exp(wk_log - m_new)
    t_lhs = (((0,), (0,)), ((), ()))
    if state is None:
        c_new = lax.dot_general(kw.astype(BF16), v, t_lhs, preferred_element_type=F32)
        return h, c_new, jnp.sum(kw, 0, keepdims=True), m_new
    decay = jnp.exp(btot + m_prev - m_new)
    s_new = decay * s_prev + lax.dot_general(kw.astype(BF16), v1, t_lhs, preferred_element_type=F32)
    return h, s_new, None, m_new


def _head_out(hsum, o, gn):
    return (jax.nn.sigmoid(o) * (_ln(hsum) * gn)).astype(BF16)


def _mlstm_ctx_kernel(q_ref, k_ref, v_ref, o_ref, g_ref, gt_ref, gn_ref,
                      out_ref, cfin_ref, nfin_ref, mfin_ref, mask_ref, *, n_sub, scale):
    head = pl.program_id(1)
    lc = MLSTM_CHUNK
    gn = gn_ref[...]
    _causal_bias(mask_ref)

    def body(s, carry):
        r0 = pl.multiple_of(s * lc, lc)
        rs = pl.ds(r0, lc)
        q, k, v, gc = q_ref[rs, :], k_ref[rs, :], v_ref[rs, :], g_ref[rs, :]
        hsum = None
        for direction in (0, 1):
            h, c_new, n_new, m_new = _mlstm_unit(direction, head, q, k, v, None, gc, gt_ref, r0, mask_ref,
                                                 None, scale)
            cfin_ref[s, direction, 0] = c_new
            nfin_ref[s, direction, 0] = n_new
            mfin_ref[s, direction, 0] = jnp.broadcast_to(m_new, (1, GATE_LANES))
            hsum = h if hsum is None else hsum + h
        out_ref[rs, :] = _head_out(hsum, o_ref[rs, :], gn)
        return carry

    lax.fori_loop(0, n_sub, body, 0)


def _mlstm_lat_kernel(q_ref, k_ref, v_ref, o_ref, g_ref, gt_ref, gn_ref, c0_ref, n0_ref, m0_ref,
                      out_ref, s_st, m_st, h_acc, mask_ref, *, n_chunks, scale):
    head = pl.program_id(1)
    lc = MLSTM_CHUNK
    hd = q_ref.shape[1]
    gn = gn_ref[...]
    _causal_bias(mask_ref)
    for direction in (0, 1):
        s_st[direction, :, 0:hd] = c0_ref[0, direction, 0]
        s_st[direction, :, hd:] = n0_ref[0, direction, 0]
        m_st[direction] = m0_ref[0, direction, 0]
    h_acc[...] = jnp.zeros(h_acc.shape, F32)

    def body(c, carry):
        for direction in (0, 1):
            cc = c if direction == 0 else n_chunks - 1 - c
            r0 = pl.multiple_of(cc * lc, lc)
            rs = pl.ds(r0, lc)
            state = (s_st[direction], m_st[direction][:, 0:1])
            v = v_ref[rs, :]
            h, s_new, _, m_new = _mlstm_unit(direction, head, q_ref[rs, :], k_ref[rs, :], v, _with_ones(v),
                                             g_ref[rs, :], gt_ref, r0, mask_ref, state, scale)
            s_st[direction] = s_new
            m_st[direction] = jnp.broadcast_to(m_new, (1, GATE_LANES))
            h_acc[rs, :] += h
        return carry

    lax.fori_loop(0, n_chunks, body, 0, unroll=2)

    def out_chunk(c, carry):
        rs = pl.ds(pl.multiple_of(c * lc, lc), lc)
        out_ref[rs, :] = _head_out(h_acc[rs, :], o_ref[rs, :], gn)
        return carry

    lax.fori_loop(0, n_chunks, out_chunk, 0)


def _mlstm_in_specs(rows, hd):
    slab = lambda idx: pl.BlockSpec((None, rows, hd), lambda s, h: (idx, s, h))
    return [slab(0), slab(1), slab(2), slab(2),
            pl.BlockSpec((rows, GATE_LANES), lambda s, h: (s, 0)),
            pl.BlockSpec((4 * ML_HEADS, rows), lambda s, h: (0, s)),
            pl.BlockSpec((1, hd), lambda s, h: (0, h))]


def _mlstm_ctx(obf, of32, g, gt, gn_g, *, n_seq):
    n = obf.shape[1]
    ml_dim = gn_g.shape[1]
    hd = ml_dim // ML_HEADS
    lc = MLSTM_CHUNK
    assert n == n_seq * lc
    n_sub = 4
    rows = n_sub * lc
    kern = functools.partial(_mlstm_ctx_kernel, n_sub=n_sub, scale=hd ** -0.5)
    return pl.pallas_call(
        kern,
        grid=(n_seq // n_sub, ML_HEADS),
        in_specs=_mlstm_in_specs(rows, hd),
        out_specs=[pl.BlockSpec((rows, hd), lambda s, h: (s, h)),
                   pl.BlockSpec((n_sub, 2, 1, hd, hd), lambda s, h: (s, 0, h, 0, 0)),
                   pl.BlockSpec((n_sub, 2, 1, 1, hd), lambda s, h: (s, 0, h, 0, 0)),
                   pl.BlockSpec((n_sub, 2, 1, 1, GATE_LANES), lambda s, h: (s, 0, h, 0, 0))],
        out_shape=[jax.ShapeDtypeStruct((n, ml_dim), BF16),
                   jax.ShapeDtypeStruct((n_seq, 2, ML_HEADS, hd, hd), F32),
                   jax.ShapeDtypeStruct((n_seq, 2, ML_HEADS, 1, hd), F32),
                   jax.ShapeDtypeStruct((n_seq, 2, ML_HEADS, 1, GATE_LANES), F32)],
        scratch_shapes=[pltpu.VMEM((2, lc, lc), F32)],
        compiler_params=_cparams(2),
        name="mlstm_ctx",
    )(obf, obf, obf, of32, g, gt, gn_g)


def _mlstm_lat(obf, of32, g, gt, gn_g, c0, n0, m0, *, n_seq):
    n = obf.shape[1]
    ml_dim = gn_g.shape[1]
    hd = ml_dim // ML_HEADS
    lc = MLSTM_CHUNK
    rows = n // n_seq
    n_chunks = rows // lc
    kern = functools.partial(_mlstm_lat_kernel, n_chunks=n_chunks, scale=hd ** -0.5)
    return pl.pallas_call(
        kern,
        grid=(n_seq, ML_HEADS),
        in_specs=_mlstm_in_specs(rows, hd) + [
                  pl.BlockSpec((1, 2, 1, hd, hd), lambda s, h: (s, 0, h, 0, 0)),
                  pl.BlockSpec((1, 2, 1, hd, GATE_LANES), lambda s, h: (s, 0, h, 0, 0)),
                  pl.BlockSpec((1, 2, 1, 1, GATE_LANES), lambda s, h: (s, 0, h, 0, 0))],
        out_specs=pl.BlockSpec((rows, hd), lambda s, h: (s, h)),
        out_shape=jax.ShapeDtypeStruct((n, ml_dim), BF16),
        scratch_shapes=[pltpu.VMEM((2, hd, hd + GATE_LANES), F32),
                        pltpu.VMEM((2, 1, GATE_LANES), F32),
                        pltpu.VMEM((rows, hd), F32),
                        pltpu.VMEM((2, lc, lc), F32)],
        compiler_params=_cparams(2),
        name="mlstm_lat",
    )(obf, obf, obf, of32, g, gt, gn_g, c0, n0, m0)


def _merge_kernel(hrg_ref, hml_ref, gma_ref, gmb_ref, x_ref, mod_ref, wrg_ref, wml_ref, wout_ref,
                  bm_ref, lng_ref, lnb_ref, o_ref, *, d, alpha, row_base, tiles_per_row):
    r = _mod_row(pl.program_id(0), row_base, tiles_per_row)
    gate1 = mod_ref[pl.ds(r, 1), 2 * d:3 * d]
    y_rg = jnp.dot(hrg_ref[...], wrg_ref[...], preferred_element_type=F32)
    y_ml = jnp.dot(hml_ref[...], wml_ref[...], preferred_element_type=F32)
    g_rg = jax.nn.sigmoid(gma_ref[...] + bm_ref[:, 0:d])
    g_ml = jax.nn.sigmoid(gmb_ref[...] + bm_ref[:, d:2 * d])
    merged = (g_rg * y_rg + g_ml * y_ml).astype(BF16)
    mix = jnp.dot(merged, wout_ref[...], preferred_element_type=F32)
    o_ref[...] = _ln(alpha * x_ref[...] + gate1 * mix) * lng_ref[...] + lnb_ref[...]


def _merge(hrg, hml, of32, x, mod, w_rg, w_ml, w_out, b_merge, ln_g, ln_b, *, alpha, row_base,
           rows_per_mod, tm):
    n, d = x.shape
    tiles_per_row = _tiles_per_row(rows_per_mod, tm)
    kern = functools.partial(_merge_kernel, d=d, alpha=alpha, row_base=row_base, tiles_per_row=tiles_per_row)
    full = lambda shape: pl.BlockSpec(shape, lambda i: (0,) * len(shape))
    return pl.pallas_call(
        kern,
        grid=(n // tm,),
        in_specs=[pl.BlockSpec((tm, d), lambda i: (i, 0)),
                  pl.BlockSpec((tm, d), lambda i: (i, 0)),
                  pl.BlockSpec((None, tm, d), lambda i: (3, i, 0)),
                  pl.BlockSpec((None, tm, d), lambda i: (4, i, 0)),
                  pl.BlockSpec((tm, d), lambda i: (i, 0)),
                  full(mod.shape), full((d, d)), full((d, d)), full((d, d)),
                  full((1, 2 * d)), full((1, d)), full((1, d))],
        out_specs=pl.BlockSpec((tm, d), lambda i: (i, 0)),
        out_shape=jax.ShapeDtypeStruct((n, d), F32),
        compiler_params=_cparams(1),
        name="merge",
    )(hrg, hml, of32, of32, x, mod, w_rg, w_ml, w_out, b_merge, ln_g, ln_b)


def _mlp_kernel(x_ref, mod_ref, wfc_ref, bfc_ref, wpj_ref, bpj_ref, lng_ref, lnb_ref, o_ref,
                u_ref, *, d, tf, alpha, row_base, tiles_per_row):
    r = _mod_row(pl.program_id(0), row_base, tiles_per_row)
    shift = mod_ref[pl.ds(r, 1), 3 * d:4 * d]
    scale = mod_ref[pl.ds(r, 1), 4 * d:5 * d]
    gate2 = mod_ref[pl.ds(r, 1), 5 * d:6 * d]
    u_ref[...] = (_ln(x_ref[...]) * (1.0 + scale) + shift).astype(BF16)
    acc = None
    for f in range(wfc_ref.shape[1] // tf):
        cols = slice(f * tf, (f + 1) * tf)
        hid = jnp.dot(u_ref[...], wfc_ref[:, cols], preferred_element_type=F32) + bfc_ref[:, cols]
        hid = jnp.square(jnp.maximum(hid, 0.0)).astype(BF16)
        part = jnp.dot(hid, wpj_ref[cols, :], preferred_element_type=F32)
        acc = part if acc is None else acc + part
    y = alpha * x_ref[...] + gate2 * (acc + bpj_ref[...])
    o_ref[...] = _ln(y) * lng_ref[...] + lnb_ref[...]


def _mlp(x, mod, w_fc, b_fc, w_proj, b_proj, ln_g, ln_b, *, alpha, row_base, rows_per_mod, tm):
    n, d = x.shape
    tiles_per_row = _tiles_per_row(rows_per_mod, tm)
    d_ff = w_fc.shape[1]
    tf = 1024
    assert d_ff % tf == 0
    kern = functools.partial(_mlp_kernel, d=d, tf=tf, alpha=alpha, row_base=row_base,
                             tiles_per_row=tiles_per_row)
    resident = lambda shape: pl.BlockSpec(shape, lambda i: (0, 0), pipeline_mode=pl.Buffered(1))
    return pl.pallas_call(
        kern,
        grid=(n // tm,),
        in_specs=[pl.BlockSpec((tm, d), lambda i: (i, 0)),
                  pl.BlockSpec(mod.shape, lambda i: (0, 0)),
                  resident((d, d_ff)),
                  pl.BlockSpec((1, d_ff), lambda i: (0, 0)),
                  resident((d_ff, d)),
                  pl.BlockSpec((1, d), lambda i: (0, 0)),
                  pl.BlockSpec((1, d), lambda i: (0, 0)),
                  pl.BlockSpec((1, d), lambda i: (0, 0))],
        out_specs=pl.BlockSpec((tm, d), lambda i: (i, 0)),
        out_shape=jax.ShapeDtypeStruct((n, d), F32),
        scratch_shapes=[pltpu.VMEM((tm, d), BF16)],
        compiler_params=_cparams(1),
        name="mlp",
    )(x, mod, w_fc, b_fc, w_proj, b_proj, ln_g, ln_b)


def _rg_gate_weights(wa, ba, wx, bx, lam):
    ct = RG_CT
    d_rnn = ba.shape[-1]
    n_ct = d_rnn // ct
    per = ct // RG_BW

    def tile_blockdiag(w):
        wt = w.reshape(n_ct, per, RG_BW, RG_BW)
        eye = jnp.eye(per, dtype=w.dtype)
        return jnp.einsum('cpkj,pq->cpkqj', wt, eye).reshape(n_ct, ct, ct)

    w_cat = jnp.concatenate([tile_blockdiag(wa[0]), tile_blockdiag(wx[0]),
                             tile_blockdiag(wa[1]), tile_blockdiag(wx[1])], axis=-1).astype(BF16)
    tiles = lambda b: b.reshape(n_ct, 1, ct)
    b_cat = jnp.concatenate([tiles(ba[0]), tiles(bx[0]), tiles(ba[1]), tiles(bx[1])], axis=-1)
    lam_cat = jnp.concatenate([tiles(lam[0]), tiles(lam[1])], axis=-1)
    return w_cat, b_cat, lam_cat


def kernel(x_prompt, x_sample, state_rglru_h, state_mlstm_C, state_mlstm_n, state_mlstm_m, c, c_ctx, w_ada, b_ada, w_in, rg_conv_w, rg_conv_b, rg_wa, rg_ba, rg_wx, rg_bx, rg_lambda, w_rg_proj, ml_b_igate, ml_b_fgate, ml_gn_g, w_ml_proj, b_merge, w_out, ln_g, ln_b, w_fc, b_fc, w_proj, b_proj):
    bp, seq, d = x_prompt.shape
    bl, dec_seq, _ = x_sample.shape
    depth = w_in.shape[0]
    d_rnn = rg_conv_w.shape[-1]
    ml_dim = ml_gn_g.shape[-1]
    hd = ml_dim // ML_HEADS
    alpha = (2 * depth) ** 0.25
    d_main = 2 * d_rnn + 4 * ml_dim + 2 * d
    assert seq == MLSTM_CHUNK == SCAN_SEG and dec_seq % MLSTM_CHUNK == 0
    assert d_rnn == d and ml_dim == d and w_in.shape[-1] == d_main + 4 * ML_HEADS

    tm = 1024
    tm_proj = 512
    xp = x_prompt.reshape(bp * seq, d)
    xs = x_sample.reshape(bl * dec_seq, d)
    cond = jnp.concatenate([c_ctx[None, :], c], axis=0)
    zero_h = jnp.zeros((2, bp, d_rnn), F32)
    new_h, new_c, new_n, new_m = [], [], [], []
    for l in range(depth):
        mod = _ada(cond, w_ada[l], b_ada[l])
        w_main = w_in[l][:, :d_main].astype(BF16)
        w_gate = jnp.zeros((d, GATE_LANES), BF16).at[:, :4 * ML_HEADS].set(w_in[l][:, d_main:].astype(BF16))
        b_gate = jnp.concatenate([ml_b_igate[l].reshape(-1), ml_b_fgate[l].reshape(-1)]).reshape(-1, 1)
        w_cat, b_cat, lam_cat = _rg_gate_weights(rg_wa[l], rg_ba[l], rg_wx[l], rg_bx[l], rg_lambda[l])
        conv_b = rg_conv_b[l].reshape(1, d_rnn)
        gn_g = ml_gn_g[l].reshape(1, ml_dim)
        w_rg, w_ml, w_o = (w_rg_proj[l].astype(BF16), w_ml_proj[l].astype(BF16), w_out[l].astype(BF16))
        w_fc_b, w_pj_b = w_fc[l].astype(BF16), w_proj[l].astype(BF16)
        bm = b_merge[l].reshape(1, 2 * d)
        lng0, lnb0 = ln_g[l, 0].reshape(1, d), ln_b[l, 0].reshape(1, d)
        lng1, lnb1 = ln_g[l, 1].reshape(1, d), ln_b[l, 1].reshape(1, d)
        bfc, bpj = b_fc[l].reshape(1, -1), b_proj[l].reshape(1, d)

        def tail(x, hrg, hml, of32, row_base, rows_per_mod):
            x1 = _merge(hrg, hml, of32, x, mod, w_rg, w_ml, w_o, bm, lng0, lnb0, alpha=alpha,
                        row_base=row_base, rows_per_mod=rows_per_mod, tm=512)
            return _mlp(x1, mod, w_fc_b, bfc, w_pj_b, bpj, lng1, lnb1, alpha=alpha,
                        row_base=row_base, rows_per_mod=rows_per_mod, tm=tm)

        of32, obf, g, gt = _in_proj(xp, mod, w_main, w_gate, b_gate, row_base=0, rows_per_mod=None, tm=tm_proj)
        hrg, h_fin = _rglru(of32, w_cat, b_cat, lam_cat, rg_conv_w[l], conv_b, zero_h,
                            n_seq=bp, seq_len=seq, conv_len=seq, want_final=True)
        hml, c_fin, n_fin, m_fin = _mlstm_ctx(obf, of32, g, gt, gn_g, n_seq=bp)
        xp = tail(xp, hrg, hml, of32, 0, None)
        new_h.append(jnp.transpose(h_fin, (1, 0, 2)))
        new_c.append(c_fin)
        new_n.append(n_fin.reshape(bp, 2, ML_HEADS, hd))
        new_m.append(m_fin[..., 0, 0])

        of32, obf, g, gt = _in_proj(xs, mod, w_main, w_gate, b_gate, row_base=1,
                                    rows_per_mod=dec_seq, tm=tm_proj)
        h0 = jnp.transpose(state_rglru_h[:, l], (1, 0, 2))
        (hrg,) = _rglru(of32, w_cat, b_cat, lam_cat, rg_conv_w[l], conv_b, h0,
                        n_seq=bl, seq_len=dec_seq, conv_len=GRID_W, want_final=False)
        c0 = state_mlstm_C[:, l]
        n0 = jnp.broadcast_to(state_mlstm_n[:, l][..., None], (bl, 2, ML_HEADS, hd, GATE_LANES))
        m0 = jnp.broadcast_to(state_mlstm_m[:, l].reshape(bl, 2, ML_HEADS, 1, 1),
                              (bl, 2, ML_HEADS, 1, GATE_LANES))
        hml = _mlstm_lat(obf, of32, g, gt, gn_g, c0, n0, m0, n_seq=bl)
        xs = tail(xs, hrg, hml, of32, 1, dec_seq)

    def stack(parts):
        return parts[0][:, None] if len(parts) == 1 else jnp.stack(parts, axis=1)

    return (xp.reshape(bp, seq, d), xs.reshape(bl, dec_seq, d),
            stack(new_h), stack(new_c), stack(new_n), stack(new_m))
```

```python
import functools

import jax
import jax.numpy as jnp
from jax import lax
from jax.experimental import pallas as pl
from jax.experimental.pallas import tpu as pltpu

F32 = jnp.float32
BF16 = jnp.bfloat16

LN_EPS = 1e-5
LOG2E = 1.4426950408889634
RG_C = 8.0
RG_BW = 64
CONV_W = 4
GRID_W = 64
ML_HEADS = 4
MLSTM_CHUNK = 256
SCAN_SEG = 256
SUBLANES = 8
SCAN_PITCH = SCAN_SEG + SUBLANES
RG_CT = 128
GATE_LANES = 128
VMEM_LIMIT = 52 * 1024 * 1024
IN_PROJ_VMEM_LIMIT = 58 * 1024 * 1024


def _cparams(n_axes):
    return pltpu.CompilerParams(
        dimension_semantics=("arbitrary",) * n_axes, vmem_limit_bytes=VMEM_LIMIT)


def _log_sigmoid(x):
    return jnp.minimum(x, 0.0) - jnp.log1p(jnp.exp(-jnp.abs(x)))


def _ln(x):
    mu = jnp.mean(x, -1, keepdims=True)
    xc = x - mu
    var = jnp.mean(xc * xc, -1, keepdims=True)
    return xc * lax.rsqrt(var + LN_EPS)


def _mod_row(i, row_base, tiles_per_row):
    if tiles_per_row is None:
        return row_base
    return row_base + lax.div(i, jnp.int32(tiles_per_row))


def _ada_kernel(ct_ref, w_ref, b_ref, o_ref, *, n_rows):
    w = w_ref[...]
    ct = ct_ref[...]
    s = ct * jax.nn.sigmoid(ct)
    o_ref[...] = jnp.zeros(o_ref.shape, F32)
    for r in range(n_rows):
        o_ref[r:r + 1, :] = jnp.sum(w * s[:, r:r + 1], axis=0, keepdims=True) + b_ref[...]


def _ada(cond, w_ada, b_ada):
    n_rows, d = cond.shape
    assert n_rows <= 8
    ct = jnp.zeros((d, 8), F32).at[:, :n_rows].set(cond.T)
    n_out = w_ada.shape[1]
    tn = 512
    return pl.pallas_call(
        functools.partial(_ada_kernel, n_rows=n_rows),
        grid=(n_out // tn,),
        in_specs=[pl.BlockSpec((d, 8), lambda j: (0, 0)),
                  pl.BlockSpec((d, tn), lambda j: (0, j)),
                  pl.BlockSpec((1, tn), lambda j: (0, j))],
        out_specs=pl.BlockSpec((8, tn), lambda j: (0, j)),
        out_shape=jax.ShapeDtypeStruct((8, n_out), F32),
        compiler_params=_cparams(1),
        name="ada_mod",
    )(ct, w_ada, b_ada.reshape(1, n_out))


def _inproj_kernel(x_ref, mod_ref, w_ref, wg_ref, bg_ref, of_ref, ob_ref, g_ref, gt_ref, u_ref,
                   *, d, tm, row_base, tiles_per_row):
    lc = MLSTM_CHUNK
    r = _mod_row(pl.program_id(0), row_base, tiles_per_row)
    shift = mod_ref[pl.ds(r, 1), 0:d]
    scale = mod_ref[pl.ds(r, 1), d:2 * d]
    u_ref[...] = (_ln(x_ref[...]) * (1.0 + scale) + shift).astype(BF16)
    n_gate = 4 * ML_HEADS
    g = jnp.dot(u_ref[...], wg_ref[...], preferred_element_type=F32)
    gt = g.T[0:n_gate, :] + bg_ref[...]
    gt = jnp.where(lax.broadcasted_iota(jnp.int32, gt.shape, 0) >= 2 * ML_HEADS, _log_sigmoid(gt), gt)
    row = lax.broadcasted_iota(jnp.int32, (n_gate, lc), 0)
    ii = lax.broadcasted_iota(jnp.int32, (lc, lc), 0)
    jj = lax.broadcasted_iota(jnp.int32, (lc, lc), 1)
    incl_before = (ii <= jj).astype(BF16)
    incl_after = (ii >= jj).astype(BF16)
    parts = []
    for c in range(tm // lc):
        x = gt[:, c * lc:(c + 1) * lc]
        hi = x.astype(BF16)
        r1 = x - hi.astype(F32)
        mid = r1.astype(BF16)
        lo = (r1 - mid.astype(F32)).astype(BF16)
        split = jnp.concatenate([hi, mid, lo], axis=0)

        def cumsum(tri):
            res = jnp.dot(split, tri, preferred_element_type=F32)
            return res[0:n_gate] + res[n_gate:2 * n_gate] + res[2 * n_gate:3 * n_gate]

        parts.append(jnp.where(row < 2 * ML_HEADS, x,
                               jnp.where(row < 3 * ML_HEADS, cumsum(incl_before), cumsum(incl_after))))
    gfin = jnp.concatenate(parts, axis=1)
    gt_ref[...] = gfin
    pad = jnp.zeros((GATE_LANES - n_gate, tm), F32)
    g_ref[...] = jnp.concatenate([gfin, pad], axis=0).T

    def col_tile(t):
        return jnp.dot(u_ref[...], w_ref[:, t * d:(t + 1) * d], preferred_element_type=F32)

    for slab, t in enumerate((0, 1, 5, 6, 7)):
        of_ref[slab] = col_tile(t)
    for slab, t in enumerate((2, 3, 4)):
        ob_ref[slab] = col_tile(t).astype(BF16)


def _tiles_per_row(rows_per_mod, tm):
    if rows_per_mod is None:
        return None
    assert rows_per_mod % tm == 0
    return rows_per_mod // tm


def _in_proj(x, mod, w_main, w_gate, b_gate, *, row_base, rows_per_mod, tm):
    n, d = x.shape
    tiles_per_row = _tiles_per_row(rows_per_mod, tm)
    assert w_main.shape[1] >= 8 * d
    kern = functools.partial(_inproj_kernel, d=d, tm=tm, row_base=row_base, tiles_per_row=tiles_per_row)
    return pl.pallas_call(
        kern,
        grid=(n // tm,),
        in_specs=[pl.BlockSpec((tm, d), lambda i: (i, 0)),
                  pl.BlockSpec(mod.shape, lambda i: (0, 0)),
                  pl.BlockSpec(w_main.shape, lambda i: (0, 0), pipeline_mode=pl.Buffered(1)),
                  pl.BlockSpec((d, GATE_LANES), lambda i: (0, 0)),
                  pl.BlockSpec((4 * ML_HEADS, 1), lambda i: (0, 0))],
        out_specs=[pl.BlockSpec((5, tm, d), lambda i: (0, i, 0)),
                   pl.BlockSpec((3, tm, d), lambda i: (0, i, 0)),
                   pl.BlockSpec((tm, GATE_LANES), lambda i: (i, 0)),
                   pl.BlockSpec((4 * ML_HEADS, tm), lambda i: (0, i))],
        out_shape=[jax.ShapeDtypeStruct((5, n, d), F32),
                   jax.ShapeDtypeStruct((3, n, d), BF16),
                   jax.ShapeDtypeStruct((n, GATE_LANES), F32),
                   jax.ShapeDtypeStruct((4 * ML_HEADS, n), F32)],
        scratch_shapes=[pltpu.VMEM((tm, d), BF16)],
        compiler_params=pltpu.CompilerParams(dimension_semantics=("arbitrary",),
                                             vmem_limit_bytes=IN_PROJ_VMEM_LIMIT),
        name="in_proj",
    )(x, mod, w_main, w_gate, b_gate)


def _rglru_kernel(xr_ref, zr_ref, w_ref, b_ref, lam_ref, cw_ref, cb_ref, h0_ref, *rest,
                  n_vseq, conv_len, n_seg, want_final):
    if want_final:
        out_ref, hfin_ref = rest[:2]
        rest = rest[2:]
    else:
        out_ref = rest[0]
        rest = rest[1:]
    a_f, b_f, a_b, b_b, cwm_ref = rest[:5]
    rest = rest[5:]
    seg = SCAN_SEG
    pitch = SCAN_PITCH
    ct = xr_ref.shape[1]
    ch = seg
    la = (RG_C * LOG2E) * _log_sigmoid(lam_ref[0])
    la_f, la_b = la[:, :ct], la[:, ct:]
    w = w_ref[0]
    bias = b_ref[0]
    cw = cw_ref[...]
    cb = cb_ref[...]
    pos = jnp.bitwise_and(lax.broadcasted_iota(jnp.int32, (ch, ct), 0), conv_len - 1)
    cwm_ref[0] = jnp.where(pos >= 2, cw[0:1], 0.0)
    cwm_ref[1] = jnp.where(pos >= 1, cw[1:2], 0.0)
    cwm_ref[2] = jnp.where(pos <= conv_len - 2, cw[3:4], 0.0)

    def gate_chunk(ci, carry):
        r0 = pl.multiple_of(ci * ch, ch)
        x = xr_ref[pl.ds(r0, ch), :]
        xc = (cwm_ref[0] * pltpu.roll(x, 2, 0) + cwm_ref[1] * pltpu.roll(x, 1, 0) + cw[2:3] * x
              + cwm_ref[2] * pltpu.roll(x, ch - 1, 0) + cb)
        gates = jnp.dot(xc.astype(BF16), w, preferred_element_type=F32) + bias

        def a_and_b(r_pre, i_pre, la_dir):
            a = jnp.exp2(la_dir / (1.0 + jnp.exp2(r_pre * (-LOG2E))))
            y = jnp.clip(1.0 - a * a, 0.0, 1.0)
            gain = jnp.where(y > 0.0, y * lax.rsqrt(y), 0.0)
            return a, (gain * xc) / (1.0 + jnp.exp2(i_pre * (-LOG2E)))

        af, bf = a_and_b(gates[:, 0:ct], gates[:, ct:2 * ct], la_f)
        ab, bb = a_and_b(gates[:, 2 * ct:3 * ct], gates[:, 3 * ct:4 * ct], la_b)
        s0 = pl.ds(pl.multiple_of(ci * pitch, SUBLANES), ch)
        a_f[s0, :] = af
        b_f[s0, :] = bf
        a_b[s0, :] = ab
        b_b[s0, :] = bb
        return carry

    lax.fori_loop(0, n_vseq, gate_chunk, 0, unroll=2)

    def slab_idx(t):
        return pl.ds(t, n_vseq, stride=pitch)

    def slab(ref, t):
        return ref[slab_idx(t), :]

    if n_seg == 1:
        init_f = h0_ref[0]
        init_b = h0_ref[1]
    else:
        e_f, p_f, e_b, p_b, cin_f, cin_b = rest
        n_real = n_vseq // n_seg
        zero = jnp.zeros((n_vseq, ct), F32)
        one = jnp.ones((n_vseq, ct), F32)

        def local_step(t, carry):
            ef, pf, eb, pb = carry
            tb = seg - 1 - t
            a = slab(a_f, t)
            ef = a * ef + slab(b_f, t)
            pf = a * pf
            a2 = slab(a_b, tb)
            eb = a2 * eb + slab(b_b, tb)
            pb = a2 * pb
            return ef, pf, eb, pb

        ef, pf, eb, pb = lax.fori_loop(0, seg, local_step, (zero, one, zero, one), unroll=8)
        e_f[...] = ef
        p_f[...] = pf
        e_b[...] = eb
        p_b[...] = pb
        carry = h0_ref[0]
        for s in range(n_seg):
            idx = pl.ds(s, n_real, stride=n_seg)
            cin_f[idx, :] = carry
            carry = e_f[idx, :] + p_f[idx, :] * carry
        carry = h0_ref[1]
        for s in reversed(range(n_seg)):
            idx = pl.ds(s, n_real, stride=n_seg)
            cin_b[idx, :] = carry
            carry = e_b[idx, :] + p_b[idx, :] * carry
        init_f = cin_f[...]
        init_b = cin_b[...]

    def scan_step(t, carry):
        hf, hb = carry
        tb = seg - 1 - t
        hf = slab(a_f, t) * hf + slab(b_f, t)
        b_f[slab_idx(t), :] = hf
        hb = slab(a_b, tb) * hb + slab(b_b, tb)
        b_b[slab_idx(tb), :] = hb
        return hf, hb

    hf_last, hb_last = lax.fori_loop(0, seg, scan_step, (init_f, init_b), unroll=8)
    if want_final:
        hfin_ref[0] = hf_last
        hfin_ref[1] = hb_last

    def out_chunk(ci, carry):
        r0 = pl.multiple_of(ci * ch, ch)
        s0 = pl.ds(pl.multiple_of(ci * pitch, SUBLANES), ch)
        h = b_f[s0, :] + b_b[s0, :]
        out_ref[pl.ds(r0, ch), :] = (h * jax.nn.gelu(zr_ref[pl.ds(r0, ch), :])).astype(BF16)
        return carry

    lax.fori_loop(0, n_vseq, out_chunk, 0)


def _rglru(of32, w_cat, b_cat, lam_cat, conv_w, conv_b, h0, *, n_seq, seq_len, conv_len, want_final):
    n = of32.shape[1]
    d_rnn = conv_w.shape[1]
    ct = RG_CT
    n_ct = d_rnn // ct
    n_seg = seq_len // SCAN_SEG
    n_vseq = n_seq * n_seg
    assert n_vseq * SCAN_SEG == n and SCAN_SEG % conv_len == 0 and (conv_len & (conv_len - 1)) == 0
    kern = functools.partial(_rglru_kernel, n_vseq=n_vseq, conv_len=conv_len, n_seg=n_seg,
                             want_final=want_final)
    out_specs = [pl.BlockSpec((n, ct), lambda c: (0, c))]
    out_shape = [jax.ShapeDtypeStruct((n, d_rnn), BF16)]
    if want_final:
        out_specs.append(pl.BlockSpec((2, n_seq, ct), lambda c: (0, 0, c)))
        out_shape.append(jax.ShapeDtypeStruct((2, n_seq, d_rnn), F32))
    scratch = [pltpu.VMEM((n_vseq * SCAN_PITCH, ct), F32) for _ in range(4)]
    scratch.append(pltpu.VMEM((CONV_W - 1, SCAN_SEG, ct), F32))
    if n_seg > 1:
        scratch += [pltpu.VMEM((n_vseq, ct), F32) for _ in range(6)]
    return pl.pallas_call(
        kern,
        grid=(n_ct,),
        in_specs=[pl.BlockSpec((None, n, ct), lambda c: (0, 0, c)),
                  pl.BlockSpec((None, n, ct), lambda c: (1, 0, c)),
                  pl.BlockSpec((1, ct, 4 * ct), lambda c: (c, 0, 0)),
                  pl.BlockSpec((1, 1, 4 * ct), lambda c: (c, 0, 0)),
                  pl.BlockSpec((1, 1, 2 * ct), lambda c: (c, 0, 0)),
                  pl.BlockSpec((CONV_W, ct), lambda c: (0, c)),
                  pl.BlockSpec((1, ct), lambda c: (0, c)),
                  pl.BlockSpec((2, n_seq, ct), lambda c: (0, 0, c))],
        out_specs=out_specs,
        out_shape=out_shape,
        scratch_shapes=scratch,
        compiler_params=_cparams(1),
        name="rglru_ctx" if want_final else "rglru_lat",
    )(of32, of32, w_cat, b_cat, lam_cat, conv_w, conv_b, h0)


def _causal_bias(mask_ref):
    lc = mask_ref.shape[1]
    ii = lax.broadcasted_iota(jnp.int32, (lc, lc), 0)
    jj = lax.broadcasted_iota(jnp.int32, (lc, lc), 1)
    mask_ref[0] = jnp.where(jj <= ii, 0.0, -jnp.inf)
    mask_ref[1] = jnp.where(jj >= ii, 0.0, -jnp.inf)


def _with_ones(v):
    return jnp.concatenate([v, jnp.ones((v.shape[0], GATE_LANES), v.dtype)], axis=1)


def _rep(col):
    return jnp.broadcast_to(col, (col.shape[0], GATE_LANES))


def _wide(rep, width):
    return jnp.concatenate([rep] * (width // GATE_LANES), axis=1)


def _mlstm_unit(direction, head, q, k, v, v1, gc, gt_ref, r0, mask_ref, state, scale):
    lc, hd = q.shape
    col_i = direction * ML_HEADS + head
    col_c = 2 * ML_HEADS + col_i
    lane = lax.broadcasted_iota(jnp.int32, (1, GATE_LANES), 1)
    ig_c = _rep(jnp.sum(jnp.where(lane == col_i, gc, 0.0), -1, keepdims=True))
    cum_c = _rep(jnp.sum(jnp.where(lane == col_c, gc, 0.0), -1, keepdims=True))
    ig_row = gt_ref[pl.ds(col_i, 1), pl.ds(r0, lc)]
    cum_row = gt_ref[pl.ds(col_c, 1), pl.ds(r0, lc)]
    dmat = (_wide(cum_c, lc) + (ig_row - cum_row)) + mask_ref[direction]
    if state is None:
        m_prev = 0.0
    else:
        s_prev, m_prev = state
    inter = cum_c + m_prev
    m_row = jnp.maximum(_rep(jnp.max(dmat, -1, keepdims=True)), inter)
    wts = jnp.exp(dmat - _wide(m_row, lc))
    s = lax.dot_general(q, k, (((1,), (1,)), ((), ())), preferred_element_type=F32) * (wts * scale)
    if state is None:
        num = jnp.dot(s.astype(BF16), v, preferred_element_type=F32)
        den = _rep(jnp.sum(s, -1, keepdims=True))
        h = num * _wide(1.0 / jnp.maximum(jnp.abs(den), jnp.exp(-m_row)), hd)
    else:
        w_inter = jnp.exp(inter - m_row) * scale
        nd = (jnp.dot(s.astype(BF16), v1, preferred_element_type=F32)
              + _wide(w_inter, hd + GATE_LANES) * jnp.dot(q, s_prev.astype(BF16), preferred_element_type=F32))
        rden = 1.0 / jnp.maximum(jnp.abs(nd[:, hd:]), jnp.exp(-m_row))
        h = nd[:, :hd] * _wide(rden, hd)
    btot = cum_c[lc - 1:lc] if direction == 0 else cum_c[0:1]
    wk_log = btot - cum_c + ig_c
    m_new = jnp.maximum(btot + m_prev, jnp.max(wk_log, 0, keepdims=True))
    kw = k.astype(F32) * _wide(jnp.exp(wk_log - m_new), hd)
    t_lhs = (((0,), (0,)), ((), ()))
    if state is None:
        c_new = lax.dot_general(kw.astype(BF16), v, t_lhs, preferred_element_type=F32)
        return h, c_new, jnp.sum(kw, 0, keepdims=True), m_new
    decay = _wide(jnp.exp(btot + m_prev - m_new), hd + GATE_LANES)
    s_new = decay * s_prev + lax.dot_general(kw.astype(BF16), v1, t_lhs, preferred_element_type=F32)
    return h, s_new, None, m_new


def _head_out(hsum, o, gn):
    return (jax.nn.sigmoid(o) * (_ln(hsum) * gn)).astype(BF16)


def _mlstm_ctx_kernel(q_ref, k_ref, v_ref, o_ref, g_ref, gt_ref, gn_ref,
                      out_ref, cfin_ref, nfin_ref, mfin_ref, mask_ref, *, n_sub, scale):
    head = pl.program_id(1)
    lc = MLSTM_CHUNK
    gn = gn_ref[...]
    _causal_bias(mask_ref)

    def body(s, carry):
        r0 = pl.multiple_of(s * lc, lc)
        rs = pl.ds(r0, lc)
        q, k, v, gc = q_ref[rs, :], k_ref[rs, :], v_ref[rs, :], g_ref[rs, :]
        hsum = None
        for direction in (0, 1):
            h, c_new, n_new, m_new = _mlstm_unit(direction, head, q, k, v, None, gc, gt_ref, r0, mask_ref,
                                                 None, scale)
            cfin_ref[s, direction, 0] = c_new
            nfin_ref[s, direction, 0] = n_new
            mfin_ref[s, direction, 0] = m_new
            hsum = h if hsum is None else hsum + h
        out_ref[rs, :] = _head_out(hsum, o_ref[rs, :], gn)
        return carry

    lax.fori_loop(0, n_sub, body, 0)


def _mlstm_lat_kernel(q_ref, k_ref, v_ref, o_ref, g_ref, gt_ref, gn_ref, c0_ref, n0_ref, m0_ref,
                      out_ref, s_st, m_st, h_acc, mask_ref, *, n_chunks, scale):
    head = pl.program_id(1)
    lc = MLSTM_CHUNK
    hd = q_ref.shape[1]
    gn = gn_ref[...]
    _causal_bias(mask_ref)
    for direction in (0, 1):
        s_st[direction, :, 0:hd] = c0_ref[0, direction, 0]
        s_st[direction, :, hd:] = n0_ref[0, direction, 0]
        m_st[direction] = m0_ref[0, direction, 0]
    h_acc[...] = jnp.zeros(h_acc.shape, F32)

    def body(c, carry):
        for direction in (0, 1):
            cc = c if direction == 0 else n_chunks - 1 - c
            r0 = pl.multiple_of(cc * lc, lc)
            rs = pl.ds(r0, lc)
            state = (s_st[direction], m_st[direction])
            v = v_ref[rs, :]
            h, s_new, _, m_new = _mlstm_unit(direction, head, q_ref[rs, :], k_ref[rs, :], v, _with_ones(v),
                                             g_ref[rs, :], gt_ref, r0, mask_ref, state, scale)
            s_st[direction] = s_new
            m_st[direction] = m_new
            h_acc[rs, :] += h
        return carry

    lax.fori_loop(0, n_chunks, body, 0, unroll=2)

    def out_chunk(c, carry):
        rs = pl.ds(pl.multiple_of(c * lc, lc), lc)
        out_ref[rs, :] = _head_out(h_acc[rs, :], o_ref[rs, :], gn)
        return carry

    lax.fori_loop(0, n_chunks, out_chunk, 0)


def _mlstm_in_specs(rows, hd):
    slab = lambda idx: pl.BlockSpec((None, rows, hd), lambda s, h: (idx, s, h))
    return [slab(0), slab(1), slab(2), slab(2),
            pl.BlockSpec((rows, GATE_LANES), lambda s, h: (s, 0)),
            pl.BlockSpec((4 * ML_HEADS, rows), lambda s, h: (0, s)),
            pl.BlockSpec((1, hd), lambda s, h: (0, h))]


def _mlstm_ctx(obf, of32, g, gt, gn_g, *, n_seq):
    n = obf.shape[1]
    ml_dim = gn_g.shape[1]
    hd = ml_dim // ML_HEADS
    lc = MLSTM_CHUNK
    assert n == n_seq * lc
    n_sub = 4
    rows = n_sub * lc
    kern = functools.partial(_mlstm_ctx_kernel, n_sub=n_sub, scale=hd ** -0.5)
    return pl.pallas_call(
        kern,
        grid=(n_seq // n_sub, ML_HEADS),
        in_specs=_mlstm_in_specs(rows, hd),
        out_specs=[pl.BlockSpec((rows, hd), lambda s, h: (s, h)),
                   pl.BlockSpec((n_sub, 2, 1, hd, hd), lambda s, h: (s, 0, h, 0, 0)),
                   pl.BlockSpec((n_sub, 2, 1, 1, hd), lambda s, h: (s, 0, h, 0, 0)),
                   pl.BlockSpec((n_sub, 2, 1, 1, GATE_LANES), lambda s, h: (s, 0, h, 0, 0))],
        out_shape=[jax.ShapeDtypeStruct((n, ml_dim), BF16),
                   jax.ShapeDtypeStruct((n_seq, 2, ML_HEADS, hd, hd), F32),
                   jax.ShapeDtypeStruct((n_seq, 2, ML_HEADS, 1, hd), F32),
                   jax.ShapeDtypeStruct((n_seq, 2, ML_HEADS, 1, GATE_LANES), F32)],
        scratch_shapes=[pltpu.VMEM((2, lc, lc), F32)],
        compiler_params=_cparams(2),
        name="mlstm_ctx",
    )(obf, obf, obf, of32, g, gt, gn_g)


def _mlstm_lat(obf, of32, g, gt, gn_g, c0, n0, m0, *, n_seq):
    n = obf.shape[1]
    ml_dim = gn_g.shape[1]
    hd = ml_dim // ML_HEADS
    lc = MLSTM_CHUNK
    rows = n // n_seq
    n_chunks = rows // lc
    kern = functools.partial(_mlstm_lat_kernel, n_chunks=n_chunks, scale=hd ** -0.5)
    return pl.pallas_call(
        kern,
        grid=(n_seq, ML_HEADS),
        in_specs=_mlstm_in_specs(rows, hd) + [
                  pl.BlockSpec((1, 2, 1, hd, hd), lambda s, h: (s, 0, h, 0, 0)),
                  pl.BlockSpec((1, 2, 1, hd, GATE_LANES), lambda s, h: (s, 0, h, 0, 0)),
                  pl.BlockSpec((1, 2, 1, 1, GATE_LANES), lambda s, h: (s, 0, h, 0, 0))],
        out_specs=pl.BlockSpec((rows, hd), lambda s, h: (s, h)),
        out_shape=jax.ShapeDtypeStruct((n, ml_dim), BF16),
        scratch_shapes=[pltpu.VMEM((2, hd, hd + GATE_LANES), F32),
                        pltpu.VMEM((2, 1, GATE_LANES), F32),
                        pltpu.VMEM((rows, hd), F32),
                        pltpu.VMEM((2, lc, lc), F32)],
        compiler_params=_cparams(2),
        name="mlstm_lat",
    )(obf, obf, obf, of32, g, gt, gn_g, c0, n0, m0)


def _merge_kernel(hrg_ref, hml_ref, gma_ref, gmb_ref, x_ref, mod_ref, wrg_ref, wml_ref, wout_ref,
                  bm_ref, lng_ref, lnb_ref, o_ref, *, d, alpha, row_base, tiles_per_row):
    r = _mod_row(pl.program_id(0), row_base, tiles_per_row)
    gate1 = mod_ref[pl.ds(r, 1), 2 * d:3 * d]
    y_rg = jnp.dot(hrg_ref[...], wrg_ref[...], preferred_element_type=F32)
    y_ml = jnp.dot(hml_ref[...], wml_ref[...], preferred_element_type=F32)
    g_rg = jax.nn.sigmoid(gma_ref[...] + bm_ref[:, 0:d])
    g_ml = jax.nn.sigmoid(gmb_ref[...] + bm_ref[:, d:2 * d])
    merged = (g_rg * y_rg + g_ml * y_ml).astype(BF16)
    mix = jnp.dot(merged, wout_ref[...], preferred_element_type=F32)
    o_ref[...] = _ln(alpha * x_ref[...] + gate1 * mix) * lng_ref[...] + lnb_ref[...]


def _merge(hrg, hml, of32, x, mod, w_rg, w_ml, w_out, b_merge, ln_g, ln_b, *, alpha, row_base,
           rows_per_mod, tm):
    n, d = x.shape
    tiles_per_row = _tiles_per_row(rows_per_mod, tm)
    kern = functools.partial(_merge_kernel, d=d, alpha=alpha, row_base=row_base, tiles_per_row=tiles_per_row)
    full = lambda shape: pl.BlockSpec(shape, lambda i: (0,) * len(shape))
    return pl.pallas_call(
        kern,
        grid=(n // tm,),
        in_specs=[pl.BlockSpec((tm, d), lambda i: (i, 0)),
                  pl.BlockSpec((tm, d), lambda i: (i, 0)),
                  pl.BlockSpec((None, tm, d), lambda i: (3, i, 0)),
                  pl.BlockSpec((None, tm, d), lambda i: (4, i, 0)),
                  pl.BlockSpec((tm, d), lambda i: (i, 0)),
                  full(mod.shape), full((d, d)), full((d, d)), full((d, d)),
                  full((1, 2 * d)), full((1, d)), full((1, d))],
        out_specs=pl.BlockSpec((tm, d), lambda i: (i, 0)),
        out_shape=jax.ShapeDtypeStruct((n, d), F32),
        compiler_params=_cparams(1),
        name="merge",
    )(hrg, hml, of32, of32, x, mod, w_rg, w_ml, w_out, b_merge, ln_g, ln_b)


def _mlp_kernel(x_ref, mod_ref, wfc_ref, bfc_ref, wpj_ref, bpj_ref, lng_ref, lnb_ref, o_ref,
                u_ref, *, d, tf, alpha, row_base, tiles_per_row):
    r = _mod_row(pl.program_id(0), row_base, tiles_per_row)
    shift = mod_ref[pl.ds(r, 1), 3 * d:4 * d]
    scale = mod_ref[pl.ds(r, 1), 4 * d:5 * d]
    gate2 = mod_ref[pl.ds(r, 1), 5 * d:6 * d]
    u_ref[...] = (_ln(x_ref[...]) * (1.0 + scale) + shift).astype(BF16)
    acc = None
    for f in range(wfc_ref.shape[1] // tf):
        cols = slice(f * tf, (f + 1) * tf)
        hid = jnp.dot(u_ref[...], wfc_ref[:, cols], preferred_element_type=F32) + bfc_ref[:, cols]
        hid = jnp.square(jnp.maximum(hid, 0.0)).astype(BF16)
        part = jnp.dot(hid, wpj_ref[cols, :], preferred_element_type=F32)
        acc = part if acc is None else acc + part
    y = alpha * x_ref[...] + gate2 * (acc + bpj_ref[...])
    o_ref[...] = _ln(y) * lng_ref[...] + lnb_ref[...]


def _mlp(x, mod, w_fc, b_fc, w_proj, b_proj, ln_g, ln_b, *, alpha, row_base, rows_per_mod, tm):
    n, d = x.shape
    tiles_per_row = _tiles_per_row(rows_per_mod, tm)
    d_ff = w_fc.shape[1]
    tf = 1024
    assert d_ff % tf == 0
    kern = functools.partial(_mlp_kernel, d=d, tf=tf, alpha=alpha, row_base=row_base,
                             tiles_per_row=tiles_per_row)
    resident = lambda shape: pl.BlockSpec(shape, lambda i: (0, 0), pipeline_mode=pl.Buffered(1))
    return pl.pallas_call(
        kern,
        grid=(n // tm,),
        in_specs=[pl.BlockSpec((tm, d), lambda i: (i, 0)),
                  pl.BlockSpec(mod.shape, lambda i: (0, 0)),
                  resident((d, d_ff)),
                  pl.BlockSpec((1, d_ff), lambda i: (0, 0)),
                  resident((d_ff, d)),
                  pl.BlockSpec((1, d), lambda i: (0, 0)),
                  pl.BlockSpec((1, d), lambda i: (0, 0)),
                  pl.BlockSpec((1, d), lambda i: (0, 0))],
        out_specs=pl.BlockSpec((tm, d), lambda i: (i, 0)),
        out_shape=jax.ShapeDtypeStruct((n, d), F32),
        scratch_shapes=[pltpu.VMEM((tm, d), BF16)],
        compiler_params=_cparams(1),
        name="mlp",
    )(x, mod, w_fc, b_fc, w_proj, b_proj, ln_g, ln_b)


def _rg_gate_weights(wa, ba, wx, bx, lam):
    ct = RG_CT
    d_rnn = ba.shape[-1]
    n_ct = d_rnn // ct
    per = ct // RG_BW

    def tile_blockdiag(w):
        wt = w.reshape(n_ct, per, RG_BW, RG_BW)
        eye = jnp.eye(per, dtype=w.dtype)
        return jnp.einsum('cpkj,pq->cpkqj', wt, eye).reshape(n_ct, ct, ct)

    w_cat = jnp.concatenate([tile_blockdiag(wa[0]), tile_blockdiag(wx[0]),
                             tile_blockdiag(wa[1]), tile_blockdiag(wx[1])], axis=-1).astype(BF16)
    tiles = lambda b: b.reshape(n_ct, 1, ct)
    b_cat = jnp.concatenate([tiles(ba[0]), tiles(bx[0]), tiles(ba[1]), tiles(bx[1])], axis=-1)
    lam_cat = jnp.concatenate([tiles(lam[0]), tiles(lam[1])], axis=-1)
    return w_cat, b_cat, lam_cat


def kernel(x_prompt, x_sample, state_rglru_h, state_mlstm_C, state_mlstm_n, state_mlstm_m, c, c_ctx, w_ada, b_ada, w_in, rg_conv_w, rg_conv_b, rg_wa, rg_ba, rg_wx, rg_bx, rg_lambda, w_rg_proj, ml_b_igate, ml_b_fgate, ml_gn_g, w_ml_proj, b_merge, w_out, ln_g, ln_b, w_fc, b_fc, w_proj, b_proj):
    bp, seq, d = x_prompt.shape
    bl, dec_seq, _ = x_sample.shape
    depth = w_in.shape[0]
    d_rnn = rg_conv_w.shape[-1]
    ml_dim = ml_gn_g.shape[-1]
    hd = ml_dim // ML_HEADS
    alpha = (2 * depth) ** 0.25
    d_main = 2 * d_rnn + 4 * ml_dim + 2 * d
    assert seq == MLSTM_CHUNK == SCAN_SEG and dec_seq % MLSTM_CHUNK == 0
    assert d_rnn == d and ml_dim == d and w_in.shape[-1] == d_main + 4 * ML_HEADS

    tm = 1024
    tm_proj = 512
    xp = x_prompt.reshape(bp * seq, d)
    xs = x_sample.reshape(bl * dec_seq, d)
    cond = jnp.concatenate([c_ctx[None, :], c], axis=0)
    zero_h = jnp.zeros((2, bp, d_rnn), F32)
    new_h, new_c, new_n, new_m = [], [], [], []
    for l in range(depth):
        mod = _ada(cond, w_ada[l], b_ada[l])
        w_main = w_in[l].astype(BF16)
        w_gate = jnp.zeros((d, GATE_LANES), BF16).at[:, :4 * ML_HEADS].set(w_in[l][:, d_main:].astype(BF16))
        b_gate = jnp.concatenate([ml_b_igate[l].reshape(-1), ml_b_fgate[l].reshape(-1)]).reshape(-1, 1)
        w_cat, b_cat, lam_cat = _rg_gate_weights(rg_wa[l], rg_ba[l], rg_wx[l], rg_bx[l], rg_lambda[l])
        conv_b = rg_conv_b[l].reshape(1, d_rnn)
        gn_g = ml_gn_g[l].reshape(1, ml_dim)
        w_rg, w_ml, w_o = (w_rg_proj[l].astype(BF16), w_ml_proj[l].astype(BF16), w_out[l].astype(BF16))
        w_fc_b, w_pj_b = w_fc[l].astype(BF16), w_proj[l].astype(BF16)
        bm = b_merge[l].reshape(1, 2 * d)
        lng0, lnb0 = ln_g[l, 0].reshape(1, d), ln_b[l, 0].reshape(1, d)
        lng1, lnb1 = ln_g[l, 1].reshape(1, d), ln_b[l, 1].reshape(1, d)
        bfc, bpj = b_fc[l].reshape(1, -1), b_proj[l].reshape(1, d)

        def tail(x, hrg, hml, of32, row_base, rows_per_mod):
            x1 = _merge(hrg, hml, of32, x, mod, w_rg, w_ml, w_o, bm, lng0, lnb0, alpha=alpha,
                        row_base=row_base, rows_per_mod=rows_per_mod, tm=512)
            return _mlp(x1, mod, w_fc_b, bfc, w_pj_b, bpj, lng1, lnb1, alpha=alpha,
                        row_base=row_base, rows_per_mod=rows_per_mod, tm=tm)

        of32, obf, g, gt = _in_proj(xp, mod, w_main, w_gate, b_gate, row_base=0, rows_per_mod=None, tm=tm_proj)
        hrg, h_fin = _rglru(of32, w_cat, b_cat, lam_cat, rg_conv_w[l], conv_b, zero_h,
                            n_seq=bp, seq_len=seq, conv_len=seq, want_final=True)
        hml, c_fin, n_fin, m_fin = _mlstm_ctx(obf, of32, g, gt, gn_g, n_seq=bp)
        xp = tail(xp, hrg, hml, of32, 0, None)
        new_h.append(jnp.transpose(h_fin, (1, 0, 2)))
        new_c.append(c_fin)
        new_n.append(n_fin.reshape(bp, 2, ML_HEADS, hd))
        new_m.append(m_fin[..., 0, 0])

        of32, obf, g, gt = _in_proj(xs, mod, w_main, w_gate, b_gate, row_base=1,
                                    rows_per_mod=dec_seq, tm=tm_proj)
        h0 = jnp.transpose(state_rglru_h[:, l], (1, 0, 2))
        (hrg,) = _rglru(of32, w_cat, b_cat, lam_cat, rg_conv_w[l], conv_b, h0,
                        n_seq=bl, seq_len=dec_seq, conv_len=GRID_W, want_final=False)
        c0 = state_mlstm_C[:, l]
        n0 = jnp.broadcast_to(state_mlstm_n[:, l][..., None], (bl, 2, ML_HEADS, hd, GATE_LANES))
        m0 = jnp.broadcast_to(state_mlstm_m[:, l].reshape(bl, 2, ML_HEADS, 1, 1),
                              (bl, 2, ML_HEADS, 1, GATE_LANES))
        hml = _mlstm_lat(obf, of32, g, gt, gn_g, c0, n0, m0, n_seq=bl)
        xs = tail(xs, hrg, hml, of32, 1, dec_seq)

    def stack(parts):
        return parts[0][:, None] if len(parts) == 1 else jnp.stack(parts, axis=1)

    return (xp.reshape(bp, seq, d), xs.reshape(bl, dec_seq, d),
            stack(new_h), stack(new_c), stack(new_n), stack(new_m))
```

```python
import functools

import jax
import jax.numpy as jnp
from jax import lax
from jax.experimental import pallas as pl
from jax.experimental.pallas import tpu as pltpu

F32 = jnp.float32
BF16 = jnp.bfloat16

LN_EPS = 1e-5
LOG2E = 1.4426950408889634
RG_C = 8.0
RG_BW = 64
CONV_W = 4
GRID_W = 64
ML_HEADS = 4
MLSTM_CHUNK = 256
SCAN_SEG = 256
SUBLANES = 8
SCAN_PITCH = SCAN_SEG + SUBLANES
RG_CT = 128
GATE_LANES = 128
VMEM_LIMIT = 52 * 1024 * 1024
MERGE_ROW_GROUPS = 4
MLP_ROW_GROUPS = 2
IN_PROJ_VMEM_LIMIT = 58 * 1024 * 1024


def _cparams(n_axes):
    return pltpu.CompilerParams(
        dimension_semantics=("arbitrary",) * n_axes, vmem_limit_bytes=VMEM_LIMIT)


def _log_sigmoid(x):
    return jnp.minimum(x, 0.0) - jnp.log1p(jnp.exp(-jnp.abs(x)))


def _ln(x):
    mu = jnp.mean(x, -1, keepdims=True)
    xc = x - mu
    var = jnp.mean(xc * xc, -1, keepdims=True)
    return xc * lax.rsqrt(var + LN_EPS)


def _mod_row(i, row_base, tiles_per_row):
    if tiles_per_row is None:
        return row_base
    return row_base + lax.div(i, jnp.int32(tiles_per_row))


def _ada_kernel(ct_ref, w_ref, b_ref, o_ref, *, n_rows):
    w = w_ref[...]
    ct = ct_ref[...]
    s = ct * jax.nn.sigmoid(ct)
    o_ref[...] = jnp.zeros(o_ref.shape, F32)
    for r in range(n_rows):
        o_ref[r:r + 1, :] = jnp.sum(w * s[:, r:r + 1], axis=0, keepdims=True) + b_ref[...]


def _ada(cond, w_ada, b_ada):
    n_rows, d = cond.shape
    assert n_rows <= 8
    ct = jnp.zeros((d, 8), F32).at[:, :n_rows].set(cond.T)
    n_out = w_ada.shape[1]
    tn = 512
    return pl.pallas_call(
        functools.partial(_ada_kernel, n_rows=n_rows),
        grid=(n_out // tn,),
        in_specs=[pl.BlockSpec((d, 8), lambda j: (0, 0)),
                  pl.BlockSpec((d, tn), lambda j: (0, j)),
                  pl.BlockSpec((1, tn), lambda j: (0, j))],
        out_specs=pl.BlockSpec((8, tn), lambda j: (0, j)),
        out_shape=jax.ShapeDtypeStruct((8, n_out), F32),
        compiler_params=_cparams(1),
        name="ada_mod",
    )(ct, w_ada, b_ada.reshape(1, n_out))


def _inproj_tile(u_ref, w_ref, wg_ref, bg_ref, of_ref, ob_ref, g_ref, gt_ref, *, d, tm):
    lc = MLSTM_CHUNK
    n_gate = 4 * ML_HEADS
    g = jnp.dot(u_ref[...], wg_ref[...], preferred_element_type=F32)
    gt = g.T[0:n_gate, :] + bg_ref[...]
    gt = jnp.where(lax.broadcasted_iota(jnp.int32, gt.shape, 0) >= 2 * ML_HEADS, _log_sigmoid(gt), gt)
    row = lax.broadcasted_iota(jnp.int32, (n_gate, lc), 0)
    ii = lax.broadcasted_iota(jnp.int32, (lc, lc), 0)
    jj = lax.broadcasted_iota(jnp.int32, (lc, lc), 1)
    incl_before = (ii <= jj).astype(BF16)
    incl_after = (ii >= jj).astype(BF16)
    parts = []
    for c in range(tm // lc):
        x = gt[:, c * lc:(c + 1) * lc]
        hi = x.astype(BF16)
        r1 = x - hi.astype(F32)
        mid = r1.astype(BF16)
        lo = (r1 - mid.astype(F32)).astype(BF16)
        split = jnp.concatenate([hi, mid, lo], axis=0)

        def cumsum(tri):
            res = jnp.dot(split, tri, preferred_element_type=F32)
            return res[0:n_gate] + res[n_gate:2 * n_gate] + res[2 * n_gate:3 * n_gate]

        parts.append(jnp.where(row < 2 * ML_HEADS, x,
                               jnp.where(row < 3 * ML_HEADS, cumsum(incl_before), cumsum(incl_after))))
    gfin = jnp.concatenate(parts, axis=1)
    gt_ref[...] = gfin
    pad = jnp.zeros((GATE_LANES - n_gate, tm), F32)
    g_ref[...] = jnp.concatenate([gfin, pad], axis=0).T

    def col_tile(t):
        return jnp.dot(u_ref[...], w_ref[:, t * d:(t + 1) * d], preferred_element_type=F32)

    for slab, t in enumerate((0, 1, 5, 6, 7)):
        of_ref[slab] = col_tile(t)
    for slab, t in enumerate((2, 3, 4)):
        ob_ref[slab] = col_tile(t).astype(BF16)


def _inproj_kernel(x0_ref, xn_ref, mod_ref, w_ref, wg_ref, bg_ref, of_ref, ob_ref, g_ref, gt_ref,
                   ua_ref, ub_ref, *, d, tm, row_base, tiles_per_row):
    i = pl.program_id(0)
    last = pl.num_programs(0) - 1

    def ln_mod(x_blk, tile):
        r = _mod_row(tile, row_base, tiles_per_row)
        shift = mod_ref[pl.ds(r, 1), 0:d]
        scale = mod_ref[pl.ds(r, 1), d:2 * d]
        return (_ln(x_blk[...]) * (1.0 + scale) + shift).astype(BF16)

    @pl.when(i == 0)
    def _():
        ua_ref[...] = ln_mod(x0_ref, i)

    def step(u_ref, u_next_ref):
        u_next_ref[...] = ln_mod(xn_ref, jnp.minimum(i + 1, last))
        _inproj_tile(u_ref, w_ref, wg_ref, bg_ref, of_ref, ob_ref, g_ref, gt_ref, d=d, tm=tm)

    parity = jnp.bitwise_and(i, 1)
    pl.when(parity == 0)(lambda: step(ua_ref, ub_ref))
    pl.when(parity == 1)(lambda: step(ub_ref, ua_ref))


def _tiles_per_row(rows_per_mod, tm):
    if rows_per_mod is None:
        return None
    assert rows_per_mod % tm == 0
    return rows_per_mod // tm


def _in_proj(x, mod, w_main, w_gate, b_gate, *, row_base, rows_per_mod, tm):
    n, d = x.shape
    tiles_per_row = _tiles_per_row(rows_per_mod, tm)
    assert w_main.shape[1] >= 8 * d
    n_tiles = n // tm
    kern = functools.partial(_inproj_kernel, d=d, tm=tm, row_base=row_base, tiles_per_row=tiles_per_row)
    return pl.pallas_call(
        kern,
        grid=(n_tiles,),
        in_specs=[pl.BlockSpec((tm, d), lambda i: (0, 0)),
                  pl.BlockSpec((tm, d), lambda i: (jnp.minimum(i + 1, n_tiles - 1), 0)),
                  pl.BlockSpec(mod.shape, lambda i: (0, 0)),
                  pl.BlockSpec(w_main.shape, lambda i: (0, 0), pipeline_mode=pl.Buffered(1)),
                  pl.BlockSpec((d, GATE_LANES), lambda i: (0, 0)),
                  pl.BlockSpec((4 * ML_HEADS, 1), lambda i: (0, 0))],
        out_specs=[pl.BlockSpec((5, tm, d), lambda i: (0, i, 0)),
                   pl.BlockSpec((3, tm, d), lambda i: (0, i, 0)),
                   pl.BlockSpec((tm, GATE_LANES), lambda i: (i, 0)),
                   pl.BlockSpec((4 * ML_HEADS, tm), lambda i: (0, i))],
        out_shape=[jax.ShapeDtypeStruct((5, n, d), F32),
                   jax.ShapeDtypeStruct((3, n, d), BF16),
                   jax.ShapeDtypeStruct((n, GATE_LANES), F32),
                   jax.ShapeDtypeStruct((4 * ML_HEADS, n), F32)],
        scratch_shapes=[pltpu.VMEM((tm, d), BF16), pltpu.VMEM((tm, d), BF16)],
        compiler_params=pltpu.CompilerParams(dimension_semantics=("arbitrary",),
                                             vmem_limit_bytes=IN_PROJ_VMEM_LIMIT),
        name="in_proj",
    )(x, x, mod, w_main, w_gate, b_gate)


def _rglru_kernel(xr_ref, zr_ref, w_ref, b_ref, lam_ref, cw_ref, cb_ref, h0_ref, *rest,
                  n_vseq, conv_len, n_seg, want_final):
    if want_final:
        out_ref, hfin_ref = rest[:2]
        rest = rest[2:]
    else:
        out_ref = rest[0]
        rest = rest[1:]
    a_f, b_f, a_b, b_b, cwm_ref = rest[:5]
    rest = rest[5:]
    seg = SCAN_SEG
    pitch = SCAN_PITCH
    ct = xr_ref.shape[1]
    ch = seg
    la = (RG_C * LOG2E) * _log_sigmoid(lam_ref[0])
    la_f, la_b = la[:, :ct], la[:, ct:]
    w = w_ref[0]
    bias = b_ref[0]
    cw = cw_ref[...]
    cb = cb_ref[...]
    pos = jnp.bitwise_and(lax.broadcasted_iota(jnp.int32, (ch, ct), 0), conv_len - 1)
    cwm_ref[0] = jnp.where(pos >= 2, cw[0:1], 0.0)
    cwm_ref[1] = jnp.where(pos >= 1, cw[1:2], 0.0)
    cwm_ref[2] = jnp.where(pos <= conv_len - 2, cw[3:4], 0.0)

    def gate_chunk(ci, carry):
        r0 = pl.multiple_of(ci * ch, ch)
        x = xr_ref[pl.ds(r0, ch), :]
        xc = (cwm_ref[0] * pltpu.roll(x, 2, 0) + cwm_ref[1] * pltpu.roll(x, 1, 0) + cw[2:3] * x
              + cwm_ref[2] * pltpu.roll(x, ch - 1, 0) + cb)
        gates = jnp.dot(xc.astype(BF16), w, preferred_element_type=F32) + bias

        def a_and_b(r_pre, i_pre, la_dir):
            a = jnp.exp2(la_dir / (1.0 + jnp.exp2(r_pre * (-LOG2E))))
            y = jnp.clip(1.0 - a * a, 0.0, 1.0)
            gain = jnp.where(y > 0.0, y * lax.rsqrt(y), 0.0)
            return a, (gain * xc) / (1.0 + jnp.exp2(i_pre * (-LOG2E)))

        af, bf = a_and_b(gates[:, 0:ct], gates[:, ct:2 * ct], la_f)
        ab, bb = a_and_b(gates[:, 2 * ct:3 * ct], gates[:, 3 * ct:4 * ct], la_b)
        s0 = pl.ds(pl.multiple_of(ci * pitch, SUBLANES), ch)
        a_f[s0, :] = af
        b_f[s0, :] = bf
        a_b[s0, :] = ab
        b_b[s0, :] = bb
        return carry

    lax.fori_loop(0, n_vseq, gate_chunk, 0, unroll=2)

    def slab_idx(t):
        return pl.ds(t, n_vseq, stride=pitch)

    def slab(ref, t):
        return ref[slab_idx(t), :]

    if n_seg == 1:
        init_f = h0_ref[0]
        init_b = h0_ref[1]
    else:
        e_f, p_f, e_b, p_b, cin_f, cin_b = rest
        n_real = n_vseq // n_seg
        zero = jnp.zeros((n_vseq, ct), F32)
        one = jnp.ones((n_vseq, ct), F32)

        def local_step(t, carry):
            ef, pf, eb, pb = carry
            tb = seg - 1 - t
            a = slab(a_f, t)
            ef = a * ef + slab(b_f, t)
            pf = a * pf
            a2 = slab(a_b, tb)
            eb = a2 * eb + slab(b_b, tb)
            pb = a2 * pb
            return ef, pf, eb, pb

        ef, pf, eb, pb = lax.fori_loop(0, seg, local_step, (zero, one, zero, one), unroll=8)
        e_f[...] = ef
        p_f[...] = pf
        e_b[...] = eb
        p_b[...] = pb
        carry = h0_ref[0]
        for s in range(n_seg):
            idx = pl.ds(s, n_real, stride=n_seg)
            cin_f[idx, :] = carry
            carry = e_f[idx, :] + p_f[idx, :] * carry
        carry = h0_ref[1]
        for s in reversed(range(n_seg)):
            idx = pl.ds(s, n_real, stride=n_seg)
            cin_b[idx, :] = carry
            carry = e_b[idx, :] + p_b[idx, :] * carry
        init_f = cin_f[...]
        init_b = cin_b[...]

    def scan_step(t, carry):
        hf, hb = carry
        tb = seg - 1 - t
        hf = slab(a_f, t) * hf + slab(b_f, t)
        b_f[slab_idx(t), :] = hf
        hb = slab(a_b, tb) * hb + slab(b_b, tb)
        b_b[slab_idx(tb), :] = hb
        return hf, hb

    hf_last, hb_last = lax.fori_loop(0, seg, scan_step, (init_f, init_b), unroll=8)
    if want_final:
        hfin_ref[0] = hf_last
        hfin_ref[1] = hb_last

    def out_chunk(ci, carry):
        r0 = pl.multiple_of(ci * ch, ch)
        s0 = pl.ds(pl.multiple_of(ci * pitch, SUBLANES), ch)
        h = b_f[s0, :] + b_b[s0, :]
        out_ref[pl.ds(r0, ch), :] = (h * jax.nn.gelu(zr_ref[pl.ds(r0, ch), :])).astype(BF16)
        return carry

    lax.fori_loop(0, n_vseq, out_chunk, 0)


def _rglru(of32, w_cat, b_cat, lam_cat, conv_w, conv_b, h0, *, n_seq, seq_len, conv_len, want_final):
    n = of32.shape[1]
    d_rnn = conv_w.shape[1]
    ct = RG_CT
    n_ct = d_rnn // ct
    n_seg = seq_len // SCAN_SEG
    n_vseq = n_seq * n_seg
    assert n_vseq * SCAN_SEG == n and SCAN_SEG % conv_len == 0 and (conv_len & (conv_len - 1)) == 0
    kern = functools.partial(_rglru_kernel, n_vseq=n_vseq, conv_len=conv_len, n_seg=n_seg,
                             want_final=want_final)
    out_specs = [pl.BlockSpec((n, ct), lambda c: (0, c))]
    out_shape = [jax.ShapeDtypeStruct((n, d_rnn), BF16)]
    if want_final:
        out_specs.append(pl.BlockSpec((2, n_seq, ct), lambda c: (0, 0, c)))
        out_shape.append(jax.ShapeDtypeStruct((2, n_seq, d_rnn), F32))
    scratch = [pltpu.VMEM((n_vseq * SCAN_PITCH, ct), F32) for _ in range(4)]
    scratch.append(pltpu.VMEM((CONV_W - 1, SCAN_SEG, ct), F32))
    if n_seg > 1:
        scratch += [pltpu.VMEM((n_vseq, ct), F32) for _ in range(6)]
    return pl.pallas_call(
        kern,
        grid=(n_ct,),
        in_specs=[pl.BlockSpec((None, n, ct), lambda c: (0, 0, c)),
                  pl.BlockSpec((None, n, ct), lambda c: (1, 0, c)),
                  pl.BlockSpec((1, ct, 4 * ct), lambda c: (c, 0, 0)),
                  pl.BlockSpec((1, 1, 4 * ct), lambda c: (c, 0, 0)),
                  pl.BlockSpec((1, 1, 2 * ct), lambda c: (c, 0, 0)),
                  pl.BlockSpec((CONV_W, ct), lambda c: (0, c)),
                  pl.BlockSpec((1, ct), lambda c: (0, c)),
                  pl.BlockSpec((2, n_seq, ct), lambda c: (0, 0, c))],
        out_specs=out_specs,
        out_shape=out_shape,
        scratch_shapes=scratch,
        compiler_params=_cparams(1),
        name="rglru_ctx" if want_final else "rglru_lat",
    )(of32, of32, w_cat, b_cat, lam_cat, conv_w, conv_b, h0)


def _causal_bias(mask_ref):
    lc = mask_ref.shape[1]
    ii = lax.broadcasted_iota(jnp.int32, (lc, lc), 0)
    jj = lax.broadcasted_iota(jnp.int32, (lc, lc), 1)
    mask_ref[0] = jnp.where(jj <= ii, 0.0, -jnp.inf)
    mask_ref[1] = jnp.where(jj >= ii, 0.0, -jnp.inf)


def _with_ones(v):
    return jnp.concatenate([v, jnp.ones((v.shape[0], GATE_LANES), v.dtype)], axis=1)


def _rep(col):
    return jnp.broadcast_to(col, (col.shape[0], GATE_LANES))


def _wide(rep, width):
    return jnp.concatenate([rep] * (width // GATE_LANES), axis=1)


def _mlstm_unit(direction, head, q, k, v, v1, gc, gt_ref, r0, mask_ref, state, scale):
    lc, hd = q.shape
    col_i = direction * ML_HEADS + head
    col_c = 2 * ML_HEADS + col_i
    lane = lax.broadcasted_iota(jnp.int32, (1, GATE_LANES), 1)
    ig_c = _rep(jnp.sum(jnp.where(lane == col_i, gc, 0.0), -1, keepdims=True))
    cum_c = _rep(jnp.sum(jnp.where(lane == col_c, gc, 0.0), -1, keepdims=True))
    ig_row = gt_ref[pl.ds(col_i, 1), pl.ds(r0, lc)]
    cum_row = gt_ref[pl.ds(col_c, 1), pl.ds(r0, lc)]
    dmat = (_wide(cum_c, lc) + (ig_row - cum_row)) + mask_ref[direction]
    if state is None:
        m_prev = 0.0
    else:
        s_prev, m_prev = state
    inter = cum_c + m_prev
    m_row = jnp.maximum(_rep(jnp.max(dmat, -1, keepdims=True)), inter)
    wts = jnp.exp(dmat - _wide(m_row, lc))
    s = lax.dot_general(q, k, (((1,), (1,)), ((), ())), preferred_element_type=F32) * (wts * scale)
    if state is None:
        num = jnp.dot(s.astype(BF16), v, preferred_element_type=F32)
        den = _rep(jnp.sum(s, -1, keepdims=True))
        h = num * _wide(1.0 / jnp.maximum(jnp.abs(den), jnp.exp(-m_row)), hd)
    else:
        w_inter = jnp.exp(inter - m_row) * scale
        nd = (jnp.dot(s.astype(BF16), v1, preferred_element_type=F32)
              + _wide(w_inter, hd + GATE_LANES) * jnp.dot(q, s_prev.astype(BF16), preferred_element_type=F32))
        rden = 1.0 / jnp.maximum(jnp.abs(nd[:, hd:]), jnp.exp(-m_row))
        h = nd[:, :hd] * _wide(rden, hd)
    btot = cum_c[lc - 1:lc] if direction == 0 else cum_c[0:1]
    wk_log = btot - cum_c + ig_c
    m_new = jnp.maximum(btot + m_prev, jnp.max(wk_log, 0, keepdims=True))
    kw = k.astype(F32) * _wide(jnp.exp(wk_log - m_new), hd)
    t_lhs = (((0,), (0,)), ((), ()))
    if state is None:
        c_new = lax.dot_general(kw.astype(BF16), v, t_lhs, preferred_element_type=F32)
        return h, c_new, jnp.sum(kw, 0, keepdims=True), m_new
    decay = _wide(jnp.exp(btot + m_prev - m_new), hd + GATE_LANES)
    s_new = decay * s_prev + lax.dot_general(kw.astype(BF16), v1, t_lhs, preferred_element_type=F32)
    return h, s_new, None, m_new


def _head_out(hsum, o, gn):
    return (jax.nn.sigmoid(o) * (_ln(hsum) * gn)).astype(BF16)


def _mlstm_ctx_kernel(q_ref, k_ref, v_ref, o_ref, g_ref, gt_ref, gn_ref,
                      out_ref, cfin_ref, nfin_ref, mfin_ref, mask_ref, *, n_sub, scale):
    head = pl.program_id(1)
    lc = MLSTM_CHUNK
    gn = gn_ref[...]
    _causal_bias(mask_ref)

    def body(s, carry):
        r0 = pl.multiple_of(s * lc, lc)
        rs = pl.ds(r0, lc)
        q, k, v, gc = q_ref[rs, :], k_ref[rs, :], v_ref[rs, :], g_ref[rs, :]
        hsum = None
        for direction in (0, 1):
            h, c_new, n_new, m_new = _mlstm_unit(direction, head, q, k, v, None, gc, gt_ref, r0, mask_ref,
                                                 None, scale)
            cfin_ref[s, direction, 0] = c_new
            nfin_ref[s, direction, 0] = n_new
            mfin_ref[s, direction, 0] = m_new
            hsum = h if hsum is None else hsum + h
        out_ref[rs, :] = _head_out(hsum, o_ref[rs, :], gn)
        return carry

    lax.fori_loop(0, n_sub, body, 0)


def _mlstm_lat_kernel(q_ref, k_ref, v_ref, o_ref, g_ref, gt_ref, gn_ref, c0_ref, n0_ref, m0_ref,
                      out_ref, s_st, m_st, h_acc, mask_ref, *, n_chunks, scale):
    head = pl.program_id(1)
    lc = MLSTM_CHUNK
    hd = q_ref.shape[1]
    gn = gn_ref[...]
    _causal_bias(mask_ref)
    for direction in (0, 1):
        s_st[direction, :, 0:hd] = c0_ref[0, direction, 0]
        s_st[direction, :, hd:] = n0_ref[0, direction, 0]
        m_st[direction] = m0_ref[0, direction, 0]
    h_acc[...] = jnp.zeros(h_acc.shape, F32)

    def body(c, carry):
        for direction in (0, 1):
            cc = c if direction == 0 else n_chunks - 1 - c
            r0 = pl.multiple_of(cc * lc, lc)
            rs = pl.ds(r0, lc)
            state = (s_st[direction], m_st[direction])
            v = v_ref[rs, :]
            h, s_new, _, m_new = _mlstm_unit(direction, head, q_ref[rs, :], k_ref[rs, :], v, _with_ones(v),
                                             g_ref[rs, :], gt_ref, r0, mask_ref, state, scale)
            s_st[direction] = s_new
            m_st[direction] = m_new
            h_acc[rs, :] += h
        return carry

    lax.fori_loop(0, n_chunks, body, 0, unroll=2)

    def out_chunk(c, carry):
        rs = pl.ds(pl.multiple_of(c * lc, lc), lc)
        out_ref[rs, :] = _head_out(h_acc[rs, :], o_ref[rs, :], gn)
        return carry

    lax.fori_loop(0, n_chunks, out_chunk, 0)


def _mlstm_in_specs(rows, hd):
    slab = lambda idx: pl.BlockSpec((None, rows, hd), lambda s, h: (idx, s, h))
    return [slab(0), slab(1), slab(2), slab(2),
            pl.BlockSpec((rows, GATE_LANES), lambda s, h: (s, 0)),
            pl.BlockSpec((4 * ML_HEADS, rows), lambda s, h: (0, s)),
            pl.BlockSpec((1, hd), lambda s, h: (0, h))]


def _mlstm_ctx(obf, of32, g, gt, gn_g, *, n_seq):
    n = obf.shape[1]
    ml_dim = gn_g.shape[1]
    hd = ml_dim // ML_HEADS
    lc = MLSTM_CHUNK
    assert n == n_seq * lc
    n_sub = 4
    rows = n_sub * lc
    kern = functools.partial(_mlstm_ctx_kernel, n_sub=n_sub, scale=hd ** -0.5)
    return pl.pallas_call(
        kern,
        grid=(n_seq // n_sub, ML_HEADS),
        in_specs=_mlstm_in_specs(rows, hd),
        out_specs=[pl.BlockSpec((rows, hd), lambda s, h: (s, h)),
                   pl.BlockSpec((n_sub, 2, 1, hd, hd), lambda s, h: (s, 0, h, 0, 0)),
                   pl.BlockSpec((n_sub, 2, 1, 1, hd), lambda s, h: (s, 0, h, 0, 0)),
                   pl.BlockSpec((n_sub, 2, 1, 1, GATE_LANES), lambda s, h: (s, 0, h, 0, 0))],
        out_shape=[jax.ShapeDtypeStruct((n, ml_dim), BF16),
                   jax.ShapeDtypeStruct((n_seq, 2, ML_HEADS, hd, hd), F32),
                   jax.ShapeDtypeStruct((n_seq, 2, ML_HEADS, 1, hd), F32),
                   jax.ShapeDtypeStruct((n_seq, 2, ML_HEADS, 1, GATE_LANES), F32)],
        scratch_shapes=[pltpu.VMEM((2, lc, lc), F32)],
        compiler_params=_cparams(2),
        name="mlstm_ctx",
    )(obf, obf, obf, of32, g, gt, gn_g)


def _mlstm_lat(obf, of32, g, gt, gn_g, c0, n0, m0, *, n_seq):
    n = obf.shape[1]
    ml_dim = gn_g.shape[1]
    hd = ml_dim // ML_HEADS
    lc = MLSTM_CHUNK
    rows = n // n_seq
    n_chunks = rows // lc
    kern = functools.partial(_mlstm_lat_kernel, n_chunks=n_chunks, scale=hd ** -0.5)
    return pl.pallas_call(
        kern,
        grid=(n_seq, ML_HEADS),
        in_specs=_mlstm_in_specs(rows, hd) + [
                  pl.BlockSpec((1, 2, 1, hd, hd), lambda s, h: (s, 0, h, 0, 0)),
                  pl.BlockSpec((1, 2, 1, hd, GATE_LANES), lambda s, h: (s, 0, h, 0, 0)),
                  pl.BlockSpec((1, 2, 1, 1, GATE_LANES), lambda s, h: (s, 0, h, 0, 0))],
        out_specs=pl.BlockSpec((rows, hd), lambda s, h: (s, h)),
        out_shape=jax.ShapeDtypeStruct((n, ml_dim), BF16),
        scratch_shapes=[pltpu.VMEM((2, hd, hd + GATE_LANES), F32),
                        pltpu.VMEM((2, 1, GATE_LANES), F32),
                        pltpu.VMEM((rows, hd), F32),
                        pltpu.VMEM((2, lc, lc), F32)],
        compiler_params=_cparams(2),
        name="mlstm_lat",
    )(obf, obf, obf, of32, g, gt, gn_g, c0, n0, m0)


def _merge_kernel(hrg_ref, hml_ref, gma_ref, gmb_ref, x_ref, mod_ref, wrg_ref, wml_ref, wout_ref,
                  bm_ref, lng_ref, lnb_ref, o_ref, *, d, alpha, row_base, tiles_per_row):
    r = _mod_row(pl.program_id(0), row_base, tiles_per_row)
    gate1 = mod_ref[pl.ds(r, 1), 2 * d:3 * d]
    rows_per_group = x_ref.shape[0] // MERGE_ROW_GROUPS
    for grp in range(MERGE_ROW_GROUPS):
        rows = slice(grp * rows_per_group, (grp + 1) * rows_per_group)
        y_rg = jnp.dot(hrg_ref[rows, :], wrg_ref[...], preferred_element_type=F32)
        y_ml = jnp.dot(hml_ref[rows, :], wml_ref[...], preferred_element_type=F32)
        g_rg = jax.nn.sigmoid(gma_ref[rows, :] + bm_ref[:, 0:d])
        g_ml = jax.nn.sigmoid(gmb_ref[rows, :] + bm_ref[:, d:2 * d])
        merged = (g_rg * y_rg + g_ml * y_ml).astype(BF16)
        mix = jnp.dot(merged, wout_ref[...], preferred_element_type=F32)
        o_ref[rows, :] = _ln(alpha * x_ref[rows, :] + gate1 * mix) * lng_ref[...] + lnb_ref[...]


def _merge(hrg, hml, of32, x, mod, w_rg, w_ml, w_out, b_merge, ln_g, ln_b, *, alpha, row_base,
           rows_per_mod, tm):
    n, d = x.shape
    tiles_per_row = _tiles_per_row(rows_per_mod, tm)
    kern = functools.partial(_merge_kernel, d=d, alpha=alpha, row_base=row_base, tiles_per_row=tiles_per_row)
    full = lambda shape: pl.BlockSpec(shape, lambda i: (0,) * len(shape))
    resident = lambda shape: pl.BlockSpec(shape, lambda i: (0, 0), pipeline_mode=pl.Buffered(1))
    return pl.pallas_call(
        kern,
        grid=(n // tm,),
        in_specs=[pl.BlockSpec((tm, d), lambda i: (i, 0)),
                  pl.BlockSpec((tm, d), lambda i: (i, 0)),
                  pl.BlockSpec((None, tm, d), lambda i: (3, i, 0)),
                  pl.BlockSpec((None, tm, d), lambda i: (4, i, 0)),
                  pl.BlockSpec((tm, d), lambda i: (i, 0)),
                  full(mod.shape), resident((d, d)), resident((d, d)), resident((d, d)),
                  full((1, 2 * d)), full((1, d)), full((1, d))],
        out_specs=pl.BlockSpec((tm, d), lambda i: (i, 0)),
        out_shape=jax.ShapeDtypeStruct((n, d), F32),
        compiler_params=_cparams(1),
        name="merge",
    )(hrg, hml, of32, of32, x, mod, w_rg, w_ml, w_out, b_merge, ln_g, ln_b)


def _mlp_kernel(x_ref, mod_ref, wfc_ref, bfc_ref, wpj_ref, bpj_ref, lng_ref, lnb_ref, o_ref,
                u_ref, *, d, tf, alpha, row_base, tiles_per_row):
    r = _mod_row(pl.program_id(0), row_base, tiles_per_row)
    shift = mod_ref[pl.ds(r, 1), 3 * d:4 * d]
    scale = mod_ref[pl.ds(r, 1), 4 * d:5 * d]
    gate2 = mod_ref[pl.ds(r, 1), 5 * d:6 * d]
    rows_per_group = x_ref.shape[0] // MLP_ROW_GROUPS
    for grp in range(MLP_ROW_GROUPS):
        rows = slice(grp * rows_per_group, (grp + 1) * rows_per_group)
        u_ref[rows, :] = (_ln(x_ref[rows, :]) * (1.0 + scale) + shift).astype(BF16)
        acc = None
        for f in range(wfc_ref.shape[1] // tf):
            cols = slice(f * tf, (f + 1) * tf)
            hid = jnp.dot(u_ref[rows, :], wfc_ref[:, cols], preferred_element_type=F32) + bfc_ref[:, cols]
            hid = jnp.square(jnp.maximum(hid, 0.0)).astype(BF16)
            part = jnp.dot(hid, wpj_ref[cols, :], preferred_element_type=F32)
            acc = part if acc is None else acc + part
        y = alpha * x_ref[rows, :] + gate2 * (acc + bpj_ref[...])
        o_ref[rows, :] = _ln(y) * lng_ref[...] + lnb_ref[...]


def _mlp(x, mod, w_fc, b_fc, w_proj, b_proj, ln_g, ln_b, *, alpha, row_base, rows_per_mod, tm):
    n, d = x.shape
    tiles_per_row = _tiles_per_row(rows_per_mod, tm)
    d_ff = w_fc.shape[1]
    tf = 1024
    assert d_ff % tf == 0
    kern = functools.partial(_mlp_kernel, d=d, tf=tf, alpha=alpha, row_base=row_base,
                             tiles_per_row=tiles_per_row)
    resident = lambda shape: pl.BlockSpec(shape, lambda i: (0, 0), pipeline_mode=pl.Buffered(1))
    return pl.pallas_call(
        kern,
        grid=(n // tm,),
        in_specs=[pl.BlockSpec((tm, d), lambda i: (i, 0)),
                  pl.BlockSpec(mod.shape, lambda i: (0, 0)),
                  resident((d, d_ff)),
                  pl.BlockSpec((1, d_ff), lambda i: (0, 0)),
                  resident((d_ff, d)),
                  pl.BlockSpec((1, d), lambda i: (0, 0)),
                  pl.BlockSpec((1, d), lambda i: (0, 0)),
                  pl.BlockSpec((1, d), lambda i: (0, 0))],
        out_specs=pl.BlockSpec((tm, d), lambda i: (i, 0)),
        out_shape=jax.ShapeDtypeStruct((n, d), F32),
        scratch_shapes=[pltpu.VMEM((tm, d), BF16)],
        compiler_params=_cparams(1),
        name="mlp",
    )(x, mod, w_fc, b_fc, w_proj, b_proj, ln_g, ln_b)


def _rg_gate_weights(wa, ba, wx, bx, lam):
    ct = RG_CT
    d_rnn = ba.shape[-1]
    n_ct = d_rnn // ct
    per = ct // RG_BW

    def tile_blockdiag(w):
        wt = w.reshape(n_ct, per, RG_BW, RG_BW)
        eye = jnp.eye(per, dtype=w.dtype)
        return jnp.einsum('cpkj,pq->cpkqj', wt, eye).reshape(n_ct, ct, ct)

    w_cat = jnp.concatenate([tile_blockdiag(wa[0]), tile_blockdiag(wx[0]),
                             tile_blockdiag(wa[1]), tile_blockdiag(wx[1])], axis=-1).astype(BF16)
    tiles = lambda b: b.reshape(n_ct, 1, ct)
    b_cat = jnp.concatenate([tiles(ba[0]), tiles(bx[0]), tiles(ba[1]), tiles(bx[1])], axis=-1)
    lam_cat = jnp.concatenate([tiles(lam[0]), tiles(lam[1])], axis=-1)
    return w_cat, b_cat, lam_cat


def kernel(x_prompt, x_sample, state_rglru_h, state_mlstm_C, state_mlstm_n, state_mlstm_m, c, c_ctx, w_ada, b_ada, w_in, rg_conv_w, rg_conv_b, rg_wa, rg_ba, rg_wx, rg_bx, rg_lambda, w_rg_proj, ml_b_igate, ml_b_fgate, ml_gn_g, w_ml_proj, b_merge, w_out, ln_g, ln_b, w_fc, b_fc, w_proj, b_proj):
    bp, seq, d = x_prompt.shape
    bl, dec_seq, _ = x_sample.shape
    depth = w_in.shape[0]
    d_rnn = rg_conv_w.shape[-1]
    ml_dim = ml_gn_g.shape[-1]
    hd = ml_dim // ML_HEADS
    alpha = (2 * depth) ** 0.25
    d_main = 2 * d_rnn + 4 * ml_dim + 2 * d
    assert seq == MLSTM_CHUNK == SCAN_SEG and dec_seq % MLSTM_CHUNK == 0
    assert d_rnn == d and ml_dim == d and w_in.shape[-1] == d_main + 4 * ML_HEADS

    tm = 1024
    tm_proj = 512
    xp = x_prompt.reshape(bp * seq, d)
    xs = x_sample.reshape(bl * dec_seq, d)
    cond = jnp.concatenate([c_ctx[None, :], c], axis=0)
    zero_h = jnp.zeros((2, bp, d_rnn), F32)
    new_h, new_c, new_n, new_m = [], [], [], []
    for l in range(depth):
        mod = _ada(cond, w_ada[l], b_ada[l])
        w_main = w_in[l].astype(BF16)
        w_gate = jnp.zeros((d, GATE_LANES), BF16).at[:, :4 * ML_HEADS].set(w_in[l][:, d_main:].astype(BF16))
        b_gate = jnp.concatenate([ml_b_igate[l].reshape(-1), ml_b_fgate[l].reshape(-1)]).reshape(-1, 1)
        w_cat, b_cat, lam_cat = _rg_gate_weights(rg_wa[l], rg_ba[l], rg_wx[l], rg_bx[l], rg_lambda[l])
        conv_b = rg_conv_b[l].reshape(1, d_rnn)
        gn_g = ml_gn_g[l].reshape(1, ml_dim)
        w_rg, w_ml, w_o = (w_rg_proj[l].astype(BF16), w_ml_proj[l].astype(BF16), w_out[l].astype(BF16))
        w_fc_b, w_pj_b = w_fc[l].astype(BF16), w_proj[l].astype(BF16)
        bm = b_merge[l].reshape(1, 2 * d)
        lng0, lnb0 = ln_g[l, 0].reshape(1, d), ln_b[l, 0].reshape(1, d)
        lng1, lnb1 = ln_g[l, 1].reshape(1, d), ln_b[l, 1].reshape(1, d)
        bfc, bpj = b_fc[l].reshape(1, -1), b_proj[l].reshape(1, d)

        def tail(x, hrg, hml, of32, row_base, rows_per_mod):
            x1 = _merge(hrg, hml, of32, x, mod, w_rg, w_ml, w_o, bm, lng0, lnb0, alpha=alpha,
                        row_base=row_base, rows_per_mod=rows_per_mod, tm=tm)
            return _mlp(x1, mod, w_fc_b, bfc, w_pj_b, bpj, lng1, lnb1, alpha=alpha,
                        row_base=row_base, rows_per_mod=rows_per_mod, tm=tm)

        of32, obf, g, gt = _in_proj(xp, mod, w_main, w_gate, b_gate, row_base=0, rows_per_mod=None, tm=tm_proj)
        hrg, h_fin = _rglru(of32, w_cat, b_cat, lam_cat, rg_conv_w[l], conv_b, zero_h,
                            n_seq=bp, seq_len=seq, conv_len=seq, want_final=True)
        hml, c_fin, n_fin, m_fin = _mlstm_ctx(obf, of32, g, gt, gn_g, n_seq=bp)
        xp = tail(xp, hrg, hml, of32, 0, None)
        new_h.append(jnp.transpose(h_fin, (1, 0, 2)))
        new_c.append(c_fin)
        new_n.append(n_fin.reshape(bp, 2, ML_HEADS, hd))
        new_m.append(m_fin[..., 0, 0])

        of32, obf, g, gt = _in_proj(xs, mod, w_main, w_gate, b_gate, row_base=1,
                                    rows_per_mod=dec_seq, tm=tm_proj)
        h0 = jnp.transpose(state_rglru_h[:, l], (1, 0, 2))
        (hrg,) = _rglru(of32, w_cat, b_cat, lam_cat, rg_conv_w[l], conv_b, h0,
                        n_seq=bl, seq_len=dec_seq, conv_len=GRID_W, want_final=False)
        c0 = state_mlstm_C[:, l]
        n0 = jnp.broadcast_to(state_mlstm_n[:, l][..., None], (bl, 2, ML_HEADS, hd, GATE_LANES))
        m0 = jnp.broadcast_to(state_mlstm_m[:, l].reshape(bl, 2, ML_HEADS, 1, 1),
                              (bl, 2, ML_HEADS, 1, GATE_LANES))
        hml = _mlstm_lat(obf, of32, g, gt, gn_g, c0, n0, m0, n_seq=bl)
        xs = tail(xs, hrg, hml, of32, 1, dec_seq)

    def stack(parts):
        return parts[0][:, None] if len(parts) == 1 else jnp.stack(parts, axis=1)

    return (xp.reshape(bp, seq, d), xs.reshape(bl, dec_seq, d),
            stack(new_h), stack(new_c), stack(new_n), stack(new_m))
```

```python
import functools

import jax
import jax.numpy as jnp
from jax import lax
from jax.experimental import pallas as pl
from jax.experimental.pallas import tpu as pltpu

F32 = jnp.float32
BF16 = jnp.bfloat16

LN_EPS = 1e-5
LOG2E = 1.4426950408889634
RG_C = 8.0
RG_BW = 64
CONV_W = 4
GRID_W = 64
ML_HEADS = 4
MLSTM_CHUNK = 256
SCAN_SEG = 256
SUBLANES = 8
SCAN_PITCH = SCAN_SEG + SUBLANES
RG_CT = 128
GATE_LANES = 128
VMEM_LIMIT = 52 * 1024 * 1024
MERGE_ROW_GROUPS = 4
MLP_ROW_GROUPS = 2
IN_PROJ_VMEM_LIMIT = 58 * 1024 * 1024


def _cparams(n_axes):
    return pltpu.CompilerParams(
        dimension_semantics=("arbitrary",) * n_axes, vmem_limit_bytes=VMEM_LIMIT)


def _log_sigmoid(x):
    return jnp.minimum(x, 0.0) - jnp.log1p(jnp.exp(-jnp.abs(x)))


def _ln(x):
    mu = jnp.mean(x, -1, keepdims=True)
    xc = x - mu
    var = jnp.mean(xc * xc, -1, keepdims=True)
    return xc * lax.rsqrt(var + LN_EPS)


def _mod_row(i, row_base, tiles_per_row):
    if tiles_per_row is None:
        return row_base
    return row_base + lax.div(i, jnp.int32(tiles_per_row))


def _ada_kernel(ct_ref, w_ref, b_ref, o_ref, *, n_rows):
    w = w_ref[...]
    ct = ct_ref[...]
    s = ct * jax.nn.sigmoid(ct)
    o_ref[...] = jnp.zeros(o_ref.shape, F32)
    for r in range(n_rows):
        o_ref[r:r + 1, :] = jnp.sum(w * s[:, r:r + 1], axis=0, keepdims=True) + b_ref[...]


def _ada(cond, w_ada, b_ada):
    n_rows, d = cond.shape
    assert n_rows <= 8
    ct = jnp.zeros((d, 8), F32).at[:, :n_rows].set(cond.T)
    n_out = w_ada.shape[1]
    tn = 2048
    assert n_out % tn == 0
    return pl.pallas_call(
        functools.partial(_ada_kernel, n_rows=n_rows),
        grid=(n_out // tn,),
        in_specs=[pl.BlockSpec((d, 8), lambda j: (0, 0)),
                  pl.BlockSpec((d, tn), lambda j: (0, j)),
                  pl.BlockSpec((1, tn), lambda j: (0, j))],
        out_specs=pl.BlockSpec((8, tn), lambda j: (0, j)),
        out_shape=jax.ShapeDtypeStruct((8, n_out), F32),
        compiler_params=_cparams(1),
        name="ada_mod",
    )(ct, w_ada, b_ada.reshape(1, n_out))


def _inproj_tile(u_ref, w_ref, wg_ref, bg_ref, cw_ref, cb_ref, of_ref, ob_ref, g_ref, gt_ref, *, d, tm,
                 conv_len):
    lc = MLSTM_CHUNK
    n_gate = 4 * ML_HEADS
    g = jnp.dot(u_ref[...], wg_ref[...], preferred_element_type=F32)
    gt = g.T[0:n_gate, :] + bg_ref[...]
    gt = jnp.where(lax.broadcasted_iota(jnp.int32, gt.shape, 0) >= 2 * ML_HEADS, _log_sigmoid(gt), gt)
    row = lax.broadcasted_iota(jnp.int32, (n_gate, lc), 0)
    ii = lax.broadcasted_iota(jnp.int32, (lc, lc), 0)
    jj = lax.broadcasted_iota(jnp.int32, (lc, lc), 1)
    incl_before = (ii <= jj).astype(BF16)
    incl_after = (ii >= jj).astype(BF16)
    parts = []
    for c in range(tm // lc):
        x = gt[:, c * lc:(c + 1) * lc]
        hi = x.astype(BF16)
        r1 = x - hi.astype(F32)
        mid = r1.astype(BF16)
        lo = (r1 - mid.astype(F32)).astype(BF16)
        split = jnp.concatenate([hi, mid, lo], axis=0)

        def cumsum(tri):
            res = jnp.dot(split, tri, preferred_element_type=F32)
            return res[0:n_gate] + res[n_gate:2 * n_gate] + res[2 * n_gate:3 * n_gate]

        parts.append(jnp.where(row < 2 * ML_HEADS, x,
                               jnp.where(row < 3 * ML_HEADS, cumsum(incl_before), cumsum(incl_after))))
    gfin = jnp.concatenate(parts, axis=1)
    gt_ref[...] = gfin
    pad = jnp.zeros((GATE_LANES - n_gate, tm), F32)
    g_ref[...] = jnp.concatenate([gfin, pad], axis=0).T

    def col_tile(t):
        return jnp.dot(u_ref[...], w_ref[:, t * d:(t + 1) * d], preferred_element_type=F32)

    xr = col_tile(0)
    pos = jnp.bitwise_and(lax.broadcasted_iota(jnp.int32, (tm, 1), 0), conv_len - 1)
    cw = cw_ref[...]
    of_ref[0] = (cw[0:1] * jnp.where(pos >= 2, pltpu.roll(xr, 2, 0), 0.0)
                 + cw[1:2] * jnp.where(pos >= 1, pltpu.roll(xr, 1, 0), 0.0)
                 + cw[2:3] * xr
                 + cw[3:4] * jnp.where(pos <= conv_len - 2, pltpu.roll(xr, tm - 1, 0), 0.0)
                 + cb_ref[...])
    of_ref[1] = jax.nn.gelu(col_tile(1))
    of_ref[2] = jax.nn.sigmoid(col_tile(5))
    of_ref[3] = col_tile(6)
    of_ref[4] = col_tile(7)
    for slab, t in enumerate((2, 3, 4)):
        ob_ref[slab] = col_tile(t).astype(BF16)


def _inproj_kernel(x0_ref, xn_ref, mod_ref, w_ref, wg_ref, bg_ref, cw_ref, cb_ref,
                   of_ref, ob_ref, g_ref, gt_ref, ua_ref, ub_ref, *, d, tm, conv_len, row_base, tiles_per_row):
    i = pl.program_id(0)
    last = pl.num_programs(0) - 1

    def ln_mod(x_blk, tile):
        r = _mod_row(tile, row_base, tiles_per_row)
        shift = mod_ref[pl.ds(r, 1), 0:d]
        scale = mod_ref[pl.ds(r, 1), d:2 * d]
        return (_ln(x_blk[...]) * (1.0 + scale) + shift).astype(BF16)

    @pl.when(i == 0)
    def _():
        ua_ref[...] = ln_mod(x0_ref, i)

    def step(u_ref, u_next_ref):
        u_next_ref[...] = ln_mod(xn_ref, jnp.minimum(i + 1, last))
        _inproj_tile(u_ref, w_ref, wg_ref, bg_ref, cw_ref, cb_ref, of_ref, ob_ref, g_ref, gt_ref,
                     d=d, tm=tm, conv_len=conv_len)

    parity = jnp.bitwise_and(i, 1)
    pl.when(parity == 0)(lambda: step(ua_ref, ub_ref))
    pl.when(parity == 1)(lambda: step(ub_ref, ua_ref))


def _tiles_per_row(rows_per_mod, tm):
    if rows_per_mod is None:
        return None
    assert rows_per_mod % tm == 0
    return rows_per_mod // tm


def _in_proj(x, mod, w_main, w_gate, b_gate, conv_w, conv_b, *, conv_len, row_base, rows_per_mod, tm):
    n, d = x.shape
    tiles_per_row = _tiles_per_row(rows_per_mod, tm)
    assert w_main.shape[1] >= 8 * d and conv_w.shape == (CONV_W, d)
    assert tm % conv_len == 0 and (conv_len & (conv_len - 1)) == 0
    n_tiles = n // tm
    kern = functools.partial(_inproj_kernel, d=d, tm=tm, conv_len=conv_len, row_base=row_base,
                             tiles_per_row=tiles_per_row)
    return pl.pallas_call(
        kern,
        grid=(n_tiles,),
        in_specs=[pl.BlockSpec((tm, d), lambda i: (0, 0)),
                  pl.BlockSpec((tm, d), lambda i: (jnp.minimum(i + 1, n_tiles - 1), 0)),
                  pl.BlockSpec(mod.shape, lambda i: (0, 0)),
                  pl.BlockSpec(w_main.shape, lambda i: (0, 0), pipeline_mode=pl.Buffered(1)),
                  pl.BlockSpec((d, GATE_LANES), lambda i: (0, 0)),
                  pl.BlockSpec((4 * ML_HEADS, 1), lambda i: (0, 0)),
                  pl.BlockSpec((CONV_W, d), lambda i: (0, 0)),
                  pl.BlockSpec((1, d), lambda i: (0, 0))],
        out_specs=[pl.BlockSpec((5, tm, d), lambda i: (0, i, 0)),
                   pl.BlockSpec((3, tm, d), lambda i: (0, i, 0)),
                   pl.BlockSpec((tm, GATE_LANES), lambda i: (i, 0)),
                   pl.BlockSpec((4 * ML_HEADS, tm), lambda i: (0, i))],
        out_shape=[jax.ShapeDtypeStruct((5, n, d), F32),
                   jax.ShapeDtypeStruct((3, n, d), BF16),
                   jax.ShapeDtypeStruct((n, GATE_LANES), F32),
                   jax.ShapeDtypeStruct((4 * ML_HEADS, n), F32)],
        scratch_shapes=[pltpu.VMEM((tm, d), BF16), pltpu.VMEM((tm, d), BF16)],
        compiler_params=pltpu.CompilerParams(dimension_semantics=("arbitrary",),
                                             vmem_limit_bytes=IN_PROJ_VMEM_LIMIT),
        name="in_proj",
    )(x, x, mod, w_main, w_gate, b_gate, conv_w, conv_b)


def _rglru_kernel(xc_ref, gz_ref, w_ref, b_ref, lam_ref, h0_ref, *rest, n_vseq, n_seg, want_final):
    if want_final:
        out_ref, hfin_ref = rest[:2]
        rest = rest[2:]
    else:
        out_ref = rest[0]
        rest = rest[1:]
    a_f, b_f, a_b, b_b = rest[:4]
    rest = rest[4:]
    seg = SCAN_SEG
    pitch = SCAN_PITCH
    ct = xc_ref.shape[1]
    ch = seg
    la = (0.5 * RG_C * LOG2E) * _log_sigmoid(lam_ref[0])
    la_f, la_b = la[:, :ct], la[:, ct:]
    w = w_ref[0]
    bias = b_ref[0]

    def gate_chunk(ci, carry):
        r0 = pl.multiple_of(ci * ch, ch)
        xc = xc_ref[pl.ds(r0, ch), :]
        gates = jnp.dot(xc.astype(BF16), w, preferred_element_type=F32) + bias

        def a_and_b(r_half, i_half, la_half):
            a = jnp.exp2(la_half + la_half * jnp.tanh(r_half))
            y = jnp.clip(1.0 - a * a, 0.0, 1.0)
            gain = jnp.where(y > 0.0, y * lax.rsqrt(y), 0.0)
            half_gx = (0.5 * gain) * xc
            return a, half_gx + half_gx * jnp.tanh(i_half)

        af, bf = a_and_b(gates[:, 0:ct], gates[:, ct:2 * ct], la_f)
        ab, bb = a_and_b(gates[:, 2 * ct:3 * ct], gates[:, 3 * ct:4 * ct], la_b)
        s0 = pl.ds(pl.multiple_of(ci * pitch, SUBLANES), ch)
        a_f[s0, :] = af
        b_f[s0, :] = bf
        a_b[s0, :] = ab
        b_b[s0, :] = bb
        return carry

    lax.fori_loop(0, n_vseq, gate_chunk, 0, unroll=8)

    def slab_idx(t):
        return pl.ds(t, n_vseq, stride=pitch)

    def slab(ref, t):
        return ref[slab_idx(t), :]

    if n_seg == 1:
        init_f = h0_ref[0]
        init_b = h0_ref[1]
    else:
        e_f, p_f, e_b, p_b, cin_f, cin_b = rest
        n_real = n_vseq // n_seg
        zero = jnp.zeros((n_vseq, ct), F32)
        one = jnp.ones((n_vseq, ct), F32)

        def local_step(t, carry):
            ef, pf, eb, pb = carry
            tb = seg - 1 - t
            a = slab(a_f, t)
            ef = a * ef + slab(b_f, t)
            pf = a * pf
            a2 = slab(a_b, tb)
            eb = a2 * eb + slab(b_b, tb)
            pb = a2 * pb
            return ef, pf, eb, pb

        ef, pf, eb, pb = lax.fori_loop(0, seg, local_step, (zero, one, zero, one), unroll=8)
        e_f[...] = ef
        p_f[...] = pf
        e_b[...] = eb
        p_b[...] = pb
        carry = h0_ref[0]
        for s in range(n_seg):
            idx = pl.ds(s, n_real, stride=n_seg)
            cin_f[idx, :] = carry
            carry = e_f[idx, :] + p_f[idx, :] * carry
        carry = h0_ref[1]
        for s in reversed(range(n_seg)):
            idx = pl.ds(s, n_real, stride=n_seg)
            cin_b[idx, :] = carry
            carry = e_b[idx, :] + p_b[idx, :] * carry
        init_f = cin_f[...]
        init_b = cin_b[...]

    def scan_step(t, carry):
        hf, hb = carry
        tb = seg - 1 - t
        hf = slab(a_f, t) * hf + slab(b_f, t)
        b_f[slab_idx(t), :] = hf
        hb = slab(a_b, tb) * hb + slab(b_b, tb)
        b_b[slab_idx(tb), :] = hb
        return hf, hb

    hf_last, hb_last = lax.fori_loop(0, seg, scan_step, (init_f, init_b), unroll=8)
    if want_final:
        hfin_ref[0] = hf_last
        hfin_ref[1] = hb_last

    def out_chunk(ci, carry):
        r0 = pl.multiple_of(ci * ch, ch)
        s0 = pl.ds(pl.multiple_of(ci * pitch, SUBLANES), ch)
        h = b_f[s0, :] + b_b[s0, :]
        out_ref[pl.ds(r0, ch), :] = (h * gz_ref[pl.ds(r0, ch), :]).astype(BF16)
        return carry

    lax.fori_loop(0, n_vseq, out_chunk, 0, unroll=2)


def _rglru(of32, w_cat, b_cat, lam_cat, h0, *, n_seq, seq_len, want_final):
    n, d_rnn = of32.shape[1:]
    ct = RG_CT
    n_ct = d_rnn // ct
    n_seg = seq_len // SCAN_SEG
    n_vseq = n_seq * n_seg
    assert n_vseq * SCAN_SEG == n
    kern = functools.partial(_rglru_kernel, n_vseq=n_vseq, n_seg=n_seg, want_final=want_final)
    out_specs = [pl.BlockSpec((n, ct), lambda c: (0, c))]
    out_shape = [jax.ShapeDtypeStruct((n, d_rnn), BF16)]
    if want_final:
        out_specs.append(pl.BlockSpec((2, n_seq, ct), lambda c: (0, 0, c)))
        out_shape.append(jax.ShapeDtypeStruct((2, n_seq, d_rnn), F32))
    scratch = [pltpu.VMEM((n_vseq * SCAN_PITCH, ct), F32) for _ in range(4)]
    if n_seg > 1:
        scratch += [pltpu.VMEM((n_vseq, ct), F32) for _ in range(6)]
    return pl.pallas_call(
        kern,
        grid=(n_ct,),
        in_specs=[pl.BlockSpec((None, n, ct), lambda c: (0, 0, c)),
                  pl.BlockSpec((None, n, ct), lambda c: (1, 0, c)),
                  pl.BlockSpec((1, ct, 4 * ct), lambda c: (c, 0, 0)),
                  pl.BlockSpec((1, 1, 4 * ct), lambda c: (c, 0, 0)),
                  pl.BlockSpec((1, 1, 2 * ct), lambda c: (c, 0, 0)),
                  pl.BlockSpec((2, n_seq, ct), lambda c: (0, 0, c))],
        out_specs=out_specs,
        out_shape=out_shape,
        scratch_shapes=scratch,
        compiler_params=_cparams(1),
        name="rglru_ctx" if want_final else "rglru_lat",
    )(of32, of32, w_cat, b_cat, lam_cat, h0)


def _causal_bias(mask_ref):
    lc = mask_ref.shape[1]
    ii = lax.broadcasted_iota(jnp.int32, (lc, lc), 0)
    jj = lax.broadcasted_iota(jnp.int32, (lc, lc), 1)
    mask_ref[0] = jnp.where(jj <= ii, 0.0, -jnp.inf)
    mask_ref[1] = jnp.where(jj >= ii, 0.0, -jnp.inf)


def _with_ones(v):
    return jnp.concatenate([v, jnp.ones((v.shape[0], GATE_LANES), v.dtype)], axis=1)


def _rep(col):
    return jnp.broadcast_to(col, (col.shape[0], GATE_LANES))


def _wide(rep, width):
    return jnp.concatenate([rep] * (width // GATE_LANES), axis=1)


def _mlstm_unit(direction, head, q, k, v, v1, gc, gt_ref, r0, mask_ref, state, scale):
    lc, hd = q.shape
    col_i = direction * ML_HEADS + head
    col_c = 2 * ML_HEADS + col_i
    lane = lax.broadcasted_iota(jnp.int32, (1, GATE_LANES), 1)
    ig_c = _rep(jnp.sum(jnp.where(lane == col_i, gc, 0.0), -1, keepdims=True))
    cum_c = _rep(jnp.sum(jnp.where(lane == col_c, gc, 0.0), -1, keepdims=True))
    ig_row = gt_ref[pl.ds(col_i, 1), pl.ds(r0, lc)]
    cum_row = gt_ref[pl.ds(col_c, 1), pl.ds(r0, lc)]
    dmat = (_wide(cum_c, lc) + (ig_row - cum_row)) + mask_ref[direction]
    if state is None:
        m_prev = 0.0
    else:
        s_prev, m_prev = state
    inter = cum_c + m_prev
    m_row = jnp.maximum(_rep(jnp.max(dmat, -1, keepdims=True)), inter)
    wts = jnp.exp(dmat - _wide(m_row, lc))
    s = lax.dot_general(q, k, (((1,), (1,)), ((), ())), preferred_element_type=F32) * (wts * scale)
    if state is None:
        num = jnp.dot(s.astype(BF16), v, preferred_element_type=F32)
        den = _rep(jnp.sum(s, -1, keepdims=True))
        h = num * _wide(1.0 / jnp.maximum(jnp.abs(den), jnp.exp(-m_row)), hd)
    else:
        w_inter = jnp.exp(inter - m_row) * scale
        nd = (jnp.dot(s.astype(BF16), v1, preferred_element_type=F32)
              + _wide(w_inter, hd + GATE_LANES) * jnp.dot(q, s_prev.astype(BF16), preferred_element_type=F32))
        rden = 1.0 / jnp.maximum(jnp.abs(nd[:, hd:]), jnp.exp(-m_row))
        h = nd[:, :hd] * _wide(rden, hd)
    btot = cum_c[lc - 1:lc] if direction == 0 else cum_c[0:1]
    wk_log = btot - cum_c + ig_c
    m_new = jnp.maximum(btot + m_prev, jnp.max(wk_log, 0, keepdims=True))
    kw = k.astype(F32) * _wide(jnp.exp(wk_log - m_new), hd)
    t_lhs = (((0,), (0,)), ((), ()))
    if state is None:
        c_new = lax.dot_general(kw.astype(BF16), v, t_lhs, preferred_element_type=F32)
        return h, c_new, jnp.sum(kw, 0, keepdims=True), m_new
    decay = _wide(jnp.exp(btot + m_prev - m_new), hd + GATE_LANES)
    s_new = decay * s_prev + lax.dot_general(kw.astype(BF16), v1, t_lhs, preferred_element_type=F32)
    return h, s_new, None, m_new


HEAD_OUT_ROWS = 1024


def _head_out_all(h_acc, og_ref, gn_ref, out_ref):
    gn = gn_ref[...]
    rows = h_acc.shape[0]
    step = min(rows, HEAD_OUT_ROWS)
    for r0 in range(0, rows, step):
        rs = slice(r0, r0 + step)
        out_ref[rs, :] = (og_ref[rs, :] * (_ln(h_acc[rs, :]) * gn)).astype(BF16)


def _mlstm_ctx_kernel(q_ref, k_ref, v_ref, o_ref, g_ref, gt_ref, gn_ref,
                      out_ref, cfin_ref, nfin_ref, mfin_ref, mask_ref, h_acc, *, n_sub, scale):
    head = pl.program_id(1)
    lc = MLSTM_CHUNK
    _causal_bias(mask_ref)

    def body(s, carry):
        r0 = pl.multiple_of(s * lc, lc)
        rs = pl.ds(r0, lc)
        q, k, v, gc = q_ref[rs, :], k_ref[rs, :], v_ref[rs, :], g_ref[rs, :]
        hsum = None
        for direction in (0, 1):
            h, c_new, n_new, m_new = _mlstm_unit(direction, head, q, k, v, None, gc, gt_ref, r0, mask_ref,
                                                 None, scale)
            cfin_ref[s, direction, 0] = c_new
            nfin_ref[s, direction, 0] = n_new
            mfin_ref[s, direction, 0] = m_new
            hsum = h if hsum is None else hsum + h
        h_acc[rs, :] = hsum
        return carry

    lax.fori_loop(0, n_sub, body, 0)
    _head_out_all(h_acc, o_ref, gn_ref, out_ref)


def _mlstm_lat_kernel(q_ref, k_ref, v_ref, o_ref, g_ref, gt_ref, gn_ref, c0_ref, n0_ref, m0_ref,
                      out_ref, s_st, m_st, h_acc, mask_ref, *, n_chunks, scale):
    head = pl.program_id(1)
    lc = MLSTM_CHUNK
    hd = q_ref.shape[1]
    _causal_bias(mask_ref)
    for direction in (0, 1):
        s_st[direction, :, 0:hd] = c0_ref[0, direction, 0]
        s_st[direction, :, hd:] = n0_ref[0, direction, 0]
        m_st[direction] = m0_ref[0, direction, 0]
    h_acc[...] = jnp.zeros(h_acc.shape, F32)

    def body(c, carry):
        for direction in (0, 1):
            cc = c if direction == 0 else n_chunks - 1 - c
            r0 = pl.multiple_of(cc * lc, lc)
            rs = pl.ds(r0, lc)
            state = (s_st[direction], m_st[direction])
            v = v_ref[rs, :]
            h, s_new, _, m_new = _mlstm_unit(direction, head, q_ref[rs, :], k_ref[rs, :], v, _with_ones(v),
                                             g_ref[rs, :], gt_ref, r0, mask_ref, state, scale)
            s_st[direction] = s_new
            m_st[direction] = m_new
            h_acc[rs, :] += h
        return carry

    lax.fori_loop(0, n_chunks, body, 0, unroll=2)
    _head_out_all(h_acc, o_ref, gn_ref, out_ref)


def _mlstm_in_specs(rows, hd):
    slab = lambda idx: pl.BlockSpec((None, rows, hd), lambda s, h: (idx, s, h))
    return [slab(0), slab(1), slab(2), slab(2),
            pl.BlockSpec((rows, GATE_LANES), lambda s, h: (s, 0)),
            pl.BlockSpec((4 * ML_HEADS, rows), lambda s, h: (0, s)),
            pl.BlockSpec((1, hd), lambda s, h: (0, h))]


def _mlstm_ctx(obf, of32, g, gt, gn_g, *, n_seq):
    n = obf.shape[1]
    ml_dim = gn_g.shape[1]
    hd = ml_dim // ML_HEADS
    lc = MLSTM_CHUNK
    assert n == n_seq * lc
    n_sub = 4
    rows = n_sub * lc
    kern = functools.partial(_mlstm_ctx_kernel, n_sub=n_sub, scale=hd ** -0.5)
    return pl.pallas_call(
        kern,
        grid=(n_seq // n_sub, ML_HEADS),
        in_specs=_mlstm_in_specs(rows, hd),
        out_specs=[pl.BlockSpec((rows, hd), lambda s, h: (s, h)),
                   pl.BlockSpec((n_sub, 2, 1, hd, hd), lambda s, h: (s, 0, h, 0, 0)),
                   pl.BlockSpec((n_sub, 2, 1, 1, hd), lambda s, h: (s, 0, h, 0, 0)),
                   pl.BlockSpec((n_sub, 2, 1, 1, GATE_LANES), lambda s, h: (s, 0, h, 0, 0))],
        out_shape=[jax.ShapeDtypeStruct((n, ml_dim), BF16),
                   jax.ShapeDtypeStruct((n_seq, 2, ML_HEADS, hd, hd), F32),
                   jax.ShapeDtypeStruct((n_seq, 2, ML_HEADS, 1, hd), F32),
                   jax.ShapeDtypeStruct((n_seq, 2, ML_HEADS, 1, GATE_LANES), F32)],
        scratch_shapes=[pltpu.VMEM((2, lc, lc), F32), pltpu.VMEM((rows, hd), F32)],
        compiler_params=_cparams(2),
        name="mlstm_ctx",
    )(obf, obf, obf, of32, g, gt, gn_g)


def _mlstm_lat(obf, of32, g, gt, gn_g, c0, n0, m0, *, n_seq):
    n = obf.shape[1]
    ml_dim = gn_g.shape[1]
    hd = ml_dim // ML_HEADS
    lc = MLSTM_CHUNK
    rows = n // n_seq
    n_chunks = rows // lc
    kern = functools.partial(_mlstm_lat_kernel, n_chunks=n_chunks, scale=hd ** -0.5)
    return pl.pallas_call(
        kern,
        grid=(n_seq, ML_HEADS),
        in_specs=_mlstm_in_specs(rows, hd) + [
                  pl.BlockSpec((1, 2, 1, hd, hd), lambda s, h: (s, 0, h, 0, 0)),
                  pl.BlockSpec((1, 2, 1, hd, GATE_LANES), lambda s, h: (s, 0, h, 0, 0)),
                  pl.BlockSpec((1, 2, 1, 1, GATE_LANES), lambda s, h: (s, 0, h, 0, 0))],
        out_specs=pl.BlockSpec((rows, hd), lambda s, h: (s, h)),
        out_shape=jax.ShapeDtypeStruct((n, ml_dim), BF16),
        scratch_shapes=[pltpu.VMEM((2, hd, hd + GATE_LANES), F32),
                        pltpu.VMEM((2, 1, GATE_LANES), F32),
                        pltpu.VMEM((rows, hd), F32),
                        pltpu.VMEM((2, lc, lc), F32)],
        compiler_params=_cparams(2),
        name="mlstm_lat",
    )(obf, obf, obf, of32, g, gt, gn_g, c0, n0, m0)


def _merge_kernel(hrg_ref, hml_ref, gma_ref, gmb_ref, x_ref, mod_ref, wrg_ref, wml_ref, wout_ref,
                  bm_ref, lng_ref, lnb_ref, o_ref, *, d, alpha, row_base, tiles_per_row):
    r = _mod_row(pl.program_id(0), row_base, tiles_per_row)
    gate1 = mod_ref[pl.ds(r, 1), 2 * d:3 * d]
    rows_per_group = x_ref.shape[0] // MERGE_ROW_GROUPS
    for grp in range(MERGE_ROW_GROUPS):
        rows = slice(grp * rows_per_group, (grp + 1) * rows_per_group)
        y_rg = jnp.dot(hrg_ref[rows, :], wrg_ref[...], preferred_element_type=F32)
        y_ml = jnp.dot(hml_ref[rows, :], wml_ref[...], preferred_element_type=F32)
        g_rg = jax.nn.sigmoid(gma_ref[rows, :] + bm_ref[:, 0:d])
        g_ml = jax.nn.sigmoid(gmb_ref[rows, :] + bm_ref[:, d:2 * d])
        merged = (g_rg * y_rg + g_ml * y_ml).astype(BF16)
        mix = jnp.dot(merged, wout_ref[...], preferred_element_type=F32)
        o_ref[rows, :] = _ln(alpha * x_ref[rows, :] + gate1 * mix) * lng_ref[...] + lnb_ref[...]


def _merge(hrg, hml, of32, x, mod, w_rg, w_ml, w_out, b_merge, ln_g, ln_b, *, alpha, row_base,
           rows_per_mod, tm):
    n, d = x.shape
    tiles_per_row = _tiles_per_row(rows_per_mod, tm)
    kern = functools.partial(_merge_kernel, d=d, alpha=alpha, row_base=row_base, tiles_per_row=tiles_per_row)
    full = lambda shape: pl.BlockSpec(shape, lambda i: (0,) * len(shape))
    resident = lambda shape: pl.BlockSpec(shape, lambda i: (0, 0), pipeline_mode=pl.Buffered(1))
    return pl.pallas_call(
        kern,
        grid=(n // tm,),
        in_specs=[pl.BlockSpec((tm, d), lambda i: (i, 0)),
                  pl.BlockSpec((tm, d), lambda i: (i, 0)),
                  pl.BlockSpec((None, tm, d), lambda i: (3, i, 0)),
                  pl.BlockSpec((None, tm, d), lambda i: (4, i, 0)),
                  pl.BlockSpec((tm, d), lambda i: (i, 0)),
                  full(mod.shape), resident((d, d)), resident((d, d)), resident((d, d)),
                  full((1, 2 * d)), full((1, d)), full((1, d))],
        out_specs=pl.BlockSpec((tm, d), lambda i: (i, 0)),
        out_shape=jax.ShapeDtypeStruct((n, d), F32),
        compiler_params=_cparams(1),
        name="merge",
    )(hrg, hml, of32, of32, x, mod, w_rg, w_ml, w_out, b_merge, ln_g, ln_b)


def _mlp_kernel(x_ref, mod_ref, wfc_ref, bfc_ref, wpj_ref, bpj_ref, lng_ref, lnb_ref, o_ref,
                u_ref, *, d, tf, alpha, row_base, tiles_per_row):
    r = _mod_row(pl.program_id(0), row_base, tiles_per_row)
    shift = mod_ref[pl.ds(r, 1), 3 * d:4 * d]
    scale = mod_ref[pl.ds(r, 1), 4 * d:5 * d]
    gate2 = mod_ref[pl.ds(r, 1), 5 * d:6 * d]
    rows_per_group = x_ref.shape[0] // MLP_ROW_GROUPS
    for grp in range(MLP_ROW_GROUPS):
        rows = slice(grp * rows_per_group, (grp + 1) * rows_per_group)
        u_ref[rows, :] = (_ln(x_ref[rows, :]) * (1.0 + scale) + shift).astype(BF16)
        acc = None
        for f in range(wfc_ref.shape[1] // tf):
            cols = slice(f * tf, (f + 1) * tf)
            hid = jnp.dot(u_ref[rows, :], wfc_ref[:, cols], preferred_element_type=F32) + bfc_ref[:, cols]
            hid = jnp.square(jnp.maximum(hid, 0.0)).astype(BF16)
            part = jnp.dot(hid, wpj_ref[cols, :], preferred_element_type=F32)
            acc = part if acc is None else acc + part
        y = alpha * x_ref[rows, :] + gate2 * (acc + bpj_ref[...])
        o_ref[rows, :] = _ln(y) * lng_ref[...] + lnb_ref[...]


def _mlp(x, mod, w_fc, b_fc, w_proj, b_proj, ln_g, ln_b, *, alpha, row_base, rows_per_mod, tm):
    n, d = x.shape
    tiles_per_row = _tiles_per_row(rows_per_mod, tm)
    d_ff = w_fc.shape[1]
    tf = 1024
    assert d_ff % tf == 0
    kern = functools.partial(_mlp_kernel, d=d, tf=tf, alpha=alpha, row_base=row_base,
                             tiles_per_row=tiles_per_row)
    resident = lambda shape: pl.BlockSpec(shape, lambda i: (0, 0), pipeline_mode=pl.Buffered(1))
    return pl.pallas_call(
        kern,
        grid=(n // tm,),
        in_specs=[pl.BlockSpec((tm, d), lambda i: (i, 0)),
                  pl.BlockSpec(mod.shape, lambda i: (0, 0)),
                  resident((d, d_ff)),
                  pl.BlockSpec((1, d_ff), lambda i: (0, 0)),
                  resident((d_ff, d)),
                  pl.BlockSpec((1, d), lambda i: (0, 0)),
                  pl.BlockSpec((1, d), lambda i: (0, 0)),
                  pl.BlockSpec((1, d), lambda i: (0, 0))],
        out_specs=pl.BlockSpec((tm, d), lambda i: (i, 0)),
        out_shape=jax.ShapeDtypeStruct((n, d), F32),
        scratch_shapes=[pltpu.VMEM((tm, d), BF16)],
        compiler_params=_cparams(1),
        name="mlp",
    )(x, mod, w_fc, b_fc, w_proj, b_proj, ln_g, ln_b)


def _rg_gate_weights(wa, ba, wx, bx, lam):
    ct = RG_CT
    d_rnn = ba.shape[-1]
    n_ct = d_rnn // ct
    per = ct // RG_BW

    def tile_blockdiag(w):
        wt = w.reshape(n_ct, per, RG_BW, RG_BW)
        eye = jnp.eye(per, dtype=w.dtype)
        return jnp.einsum('cpkj,pq->cpkqj', wt, eye).reshape(n_ct, ct, ct)

    w_cat = (0.5 * jnp.concatenate([tile_blockdiag(wa[0]), tile_blockdiag(wx[0]),
                                    tile_blockdiag(wa[1]), tile_blockdiag(wx[1])], axis=-1)).astype(BF16)
    tiles = lambda b: b.reshape(n_ct, 1, ct)
    b_cat = 0.5 * jnp.concatenate([tiles(ba[0]), tiles(bx[0]), tiles(ba[1]), tiles(bx[1])], axis=-1)
    lam_cat = jnp.concatenate([tiles(lam[0]), tiles(lam[1])], axis=-1)
    return w_cat, b_cat, lam_cat


def kernel(x_prompt, x_sample, state_rglru_h, state_mlstm_C, state_mlstm_n, state_mlstm_m, c, c_ctx, w_ada, b_ada, w_in, rg_conv_w, rg_conv_b, rg_wa, rg_ba, rg_wx, rg_bx, rg_lambda, w_rg_proj, ml_b_igate, ml_b_fgate, ml_gn_g, w_ml_proj, b_merge, w_out, ln_g, ln_b, w_fc, b_fc, w_proj, b_proj):
    bp, seq, d = x_prompt.shape
    bl, dec_seq, _ = x_sample.shape
    depth = w_in.shape[0]
    d_rnn = rg_conv_w.shape[-1]
    ml_dim = ml_gn_g.shape[-1]
    hd = ml_dim // ML_HEADS
    alpha = (2 * depth) ** 0.25
    d_main = 2 * d_rnn + 4 * ml_dim + 2 * d
    assert seq == MLSTM_CHUNK == SCAN_SEG and dec_seq % MLSTM_CHUNK == 0
    assert d_rnn == d and ml_dim == d and w_in.shape[-1] == d_main + 4 * ML_HEADS

    tm = 1024
    tm_proj = 512
    xp = x_prompt.reshape(bp * seq, d)
    xs = x_sample.reshape(bl * dec_seq, d)
    cond = jnp.concatenate([c_ctx[None, :], c], axis=0)
    zero_h = jnp.zeros((2, bp, d_rnn), F32)
    new_h, new_c, new_n, new_m = [], [], [], []
    for l in range(depth):
        mod = _ada(cond, w_ada[l], b_ada[l])
        w_main = w_in[l].astype(BF16)
        w_gate = jnp.zeros((d, GATE_LANES), BF16).at[:, :4 * ML_HEADS].set(w_in[l][:, d_main:].astype(BF16))
        b_gate = jnp.concatenate([ml_b_igate[l].reshape(-1), ml_b_fgate[l].reshape(-1)]).reshape(-1, 1)
        w_cat, b_cat, lam_cat = _rg_gate_weights(rg_wa[l], rg_ba[l], rg_wx[l], rg_bx[l], rg_lambda[l])
        conv_b = rg_conv_b[l].reshape(1, d_rnn)
        gn_g = ml_gn_g[l].reshape(1, ml_dim)
        w_rg, w_ml, w_o = (w_rg_proj[l].astype(BF16), w_ml_proj[l].astype(BF16), w_out[l].astype(BF16))
        w_fc_b, w_pj_b = w_fc[l].astype(BF16), w_proj[l].astype(BF16)
        bm = b_merge[l].reshape(1, 2 * d)
        lng0, lnb0 = ln_g[l, 0].reshape(1, d), ln_b[l, 0].reshape(1, d)
        lng1, lnb1 = ln_g[l, 1].reshape(1, d), ln_b[l, 1].reshape(1, d)
        bfc, bpj = b_fc[l].reshape(1, -1), b_proj[l].reshape(1, d)

        def tail(x, hrg, hml, of32, row_base, rows_per_mod):
            x1 = _merge(hrg, hml, of32, x, mod, w_rg, w_ml, w_o, bm, lng0, lnb0, alpha=alpha,
                        row_base=row_base, rows_per_mod=rows_per_mod, tm=tm)
            return _mlp(x1, mod, w_fc_b, bfc, w_pj_b, bpj, lng1, lnb1, alpha=alpha,
                        row_base=row_base, rows_per_mod=rows_per_mod, tm=tm)

        of32, obf, g, gt = _in_proj(xp, mod, w_main, w_gate, b_gate, rg_conv_w[l], conv_b, conv_len=seq,
                                    row_base=0, rows_per_mod=None, tm=tm_proj)
        hrg, h_fin = _rglru(of32, w_cat, b_cat, lam_cat, zero_h, n_seq=bp, seq_len=seq, want_final=True)
        hml, c_fin, n_fin, m_fin = _mlstm_ctx(obf, of32, g, gt, gn_g, n_seq=bp)
        xp = tail(xp, hrg, hml, of32, 0, None)
        new_h.append(jnp.transpose(h_fin, (1, 0, 2)))
        new_c.append(c_fin)
        new_n.append(n_fin.reshape(bp, 2, ML_HEADS, hd))
        new_m.append(m_fin[..., 0, 0])

        of32, obf, g, gt = _in_proj(xs, mod, w_main, w_gate, b_gate, rg_conv_w[l], conv_b, conv_len=GRID_W,
                                    row_base=1, rows_per_mod=dec_seq, tm=tm_proj)
        h0 = jnp.transpose(state_rglru_h[:, l], (1, 0, 2))
        (hrg,) = _rglru(of32, w_cat, b_cat, lam_cat, h0, n_seq=bl, seq_len=dec_seq, want_final=False)
        c0 = state_mlstm_C[:, l]
        n0 = jnp.broadcast_to(state_mlstm_n[:, l][..., None], (bl, 2, ML_HEADS, hd, GATE_LANES))
        m0 = jnp.broadcast_to(state_mlstm_m[:, l].reshape(bl, 2, ML_HEADS, 1, 1),
                              (bl, 2, ML_HEADS, 1, GATE_LANES))
        hml = _mlstm_lat(obf, of32, g, gt, gn_g, c0, n0, m0, n_seq=bl)
        xs = tail(xs, hrg, hml, of32, 1, dec_seq)

    def stack(parts):
        return parts[0][:, None] if len(parts) == 1 else jnp.stack(parts, axis=1)

    return (xp.reshape(bp, seq, d), xs.reshape(bl, dec_seq, d),
            stack(new_h), stack(new_c), stack(new_n), stack(new_m))
```

```python
import functools

import jax
import jax.numpy as jnp
from jax import lax
from jax.experimental import pallas as pl
from jax.experimental.pallas import tpu as pltpu

F32 = jnp.float32
BF16 = jnp.bfloat16

LN_EPS = 1e-5
LOG2E = 1.4426950408889634
RG_C = 8.0
RG_BW = 64
CONV_W = 4
GRID_W = 64
ML_HEADS = 4
MLSTM_CHUNK = 256
SCAN_SEG = 256
SUBLANES = 8
SCAN_PITCH = SCAN_SEG + 4
SCAN_GROUP = 8
RG_CT = 128
GATE_LANES = 128
VMEM_LIMIT = 52 * 1024 * 1024
TAIL_ROW_GROUPS = 2
MERGE_ROW_GROUPS = 4
MLP_ROW_GROUPS = 2
IN_PROJ_VMEM_LIMIT = 58 * 1024 * 1024


def _cparams(n_axes):
    return pltpu.CompilerParams(
        dimension_semantics=("arbitrary",) * n_axes, vmem_limit_bytes=VMEM_LIMIT)


def _log_sigmoid(x):
    return jnp.minimum(x, 0.0) - jnp.log1p(jnp.exp(-jnp.abs(x)))


def _ln(x):
    mu = jnp.mean(x, -1, keepdims=True)
    xc = x - mu
    var = jnp.mean(xc * xc, -1, keepdims=True)
    return xc * lax.rsqrt(var + LN_EPS)


def _mod_row(i, row_base, tiles_per_row):
    if tiles_per_row is None:
        return row_base
    return row_base + lax.div(i, jnp.int32(tiles_per_row))


def _ada_kernel(ct_ref, w_ref, b_ref, o_ref, *, n_rows):
    w = w_ref[...]
    ct = ct_ref[...]
    s = ct * jax.nn.sigmoid(ct)
    o_ref[...] = jnp.zeros(o_ref.shape, F32)
    for r in range(n_rows):
        o_ref[r:r + 1, :] = jnp.sum(w * s[:, r:r + 1], axis=0, keepdims=True) + b_ref[...]


def _ada(cond, w_ada, b_ada):
    n_rows, d = cond.shape
    assert n_rows <= 8
    ct = jnp.zeros((d, 8), F32).at[:, :n_rows].set(cond.T)
    n_out = w_ada.shape[1]
    tn = 2048
    assert n_out % tn == 0
    return pl.pallas_call(
        functools.partial(_ada_kernel, n_rows=n_rows),
        grid=(n_out // tn,),
        in_specs=[pl.BlockSpec((d, 8), lambda j: (0, 0)),
                  pl.BlockSpec((d, tn), lambda j: (0, j)),
                  pl.BlockSpec((1, tn), lambda j: (0, j))],
        out_specs=pl.BlockSpec((8, tn), lambda j: (0, j)),
        out_shape=jax.ShapeDtypeStruct((8, n_out), F32),
        compiler_params=_cparams(1),
        name="ada_mod",
    )(ct, w_ada, b_ada.reshape(1, n_out))


def _inproj_tile(u_ref, w_ref, wg_ref, bg_ref, cw_ref, cb_ref, of_ref, ob_ref, g_ref, gt_ref, *, d, tm,
                 conv_len):
    lc = MLSTM_CHUNK
    n_gate = 4 * ML_HEADS
    g = jnp.dot(u_ref[...], wg_ref[...], preferred_element_type=F32)
    gt = g.T[0:n_gate, :] + bg_ref[...]
    gt = jnp.where(lax.broadcasted_iota(jnp.int32, gt.shape, 0) >= 2 * ML_HEADS, _log_sigmoid(gt), gt)
    row = lax.broadcasted_iota(jnp.int32, (n_gate, lc), 0)
    ii = lax.broadcasted_iota(jnp.int32, (lc, lc), 0)
    jj = lax.broadcasted_iota(jnp.int32, (lc, lc), 1)
    incl_before = (ii <= jj).astype(BF16)
    incl_after = (ii >= jj).astype(BF16)
    parts = []
    for c in range(tm // lc):
        x = gt[:, c * lc:(c + 1) * lc]
        hi = x.astype(BF16)
        r1 = x - hi.astype(F32)
        mid = r1.astype(BF16)
        lo = (r1 - mid.astype(F32)).astype(BF16)
        split = jnp.concatenate([hi, mid, lo], axis=0)

        def cumsum(tri):
            res = jnp.dot(split, tri, preferred_element_type=F32)
            return res[0:n_gate] + res[n_gate:2 * n_gate] + res[2 * n_gate:3 * n_gate]

        parts.append(jnp.where(row < 2 * ML_HEADS, x,
                               jnp.where(row < 3 * ML_HEADS, cumsum(incl_before), cumsum(incl_after))))
    gfin = jnp.concatenate(parts, axis=1)
    gt_ref[...] = gfin
    pad = jnp.zeros((GATE_LANES - n_gate, tm), F32)
    g_ref[...] = jnp.concatenate([gfin, pad], axis=0).T

    def col_tile(t):
        return jnp.dot(u_ref[...], w_ref[:, t * d:(t + 1) * d], preferred_element_type=F32)

    xr = col_tile(0)
    pos = jnp.bitwise_and(lax.broadcasted_iota(jnp.int32, (tm, 1), 0), conv_len - 1)
    cw = cw_ref[...]
    of_ref[0] = (cw[0:1] * jnp.where(pos >= 2, pltpu.roll(xr, 2, 0), 0.0)
                 + cw[1:2] * jnp.where(pos >= 1, pltpu.roll(xr, 1, 0), 0.0)
                 + cw[2:3] * xr
                 + cw[3:4] * jnp.where(pos <= conv_len - 2, pltpu.roll(xr, tm - 1, 0), 0.0)
                 + cb_ref[...])
    of_ref[1] = jax.nn.gelu(col_tile(1))
    of_ref[2] = jax.nn.sigmoid(col_tile(5))
    of_ref[3] = col_tile(6)
    of_ref[4] = col_tile(7)
    for slab, t in enumerate((2, 3, 4)):
        ob_ref[slab] = col_tile(t).astype(BF16)


def _inproj_kernel(x0_ref, xn_ref, mod_ref, w_ref, wg_ref, bg_ref, cw_ref, cb_ref,
                   of_ref, ob_ref, g_ref, gt_ref, ua_ref, ub_ref, *, d, tm, conv_len, row_base, tiles_per_row):
    i = pl.program_id(0)
    last = pl.num_programs(0) - 1

    def ln_mod(x_blk, tile):
        r = _mod_row(tile, row_base, tiles_per_row)
        shift = mod_ref[pl.ds(r, 1), 0:d]
        scale = mod_ref[pl.ds(r, 1), d:2 * d]
        return (_ln(x_blk[...]) * (1.0 + scale) + shift).astype(BF16)

    @pl.when(i == 0)
    def _():
        ua_ref[...] = ln_mod(x0_ref, i)

    def step(u_ref, u_next_ref):
        u_next_ref[...] = ln_mod(xn_ref, jnp.minimum(i + 1, last))
        _inproj_tile(u_ref, w_ref, wg_ref, bg_ref, cw_ref, cb_ref, of_ref, ob_ref, g_ref, gt_ref,
                     d=d, tm=tm, conv_len=conv_len)

    parity = jnp.bitwise_and(i, 1)
    pl.when(parity == 0)(lambda: step(ua_ref, ub_ref))
    pl.when(parity == 1)(lambda: step(ub_ref, ua_ref))


def _tiles_per_row(rows_per_mod, tm):
    if rows_per_mod is None:
        return None
    assert rows_per_mod % tm == 0
    return rows_per_mod // tm


def _in_proj(x, mod, w_main, w_gate, b_gate, conv_w, conv_b, *, conv_len, row_base, rows_per_mod, tm):
    n, d = x.shape
    tiles_per_row = _tiles_per_row(rows_per_mod, tm)
    assert w_main.shape[1] >= 8 * d and conv_w.shape == (CONV_W, d)
    assert tm % conv_len == 0 and (conv_len & (conv_len - 1)) == 0
    n_tiles = n // tm
    kern = functools.partial(_inproj_kernel, d=d, tm=tm, conv_len=conv_len, row_base=row_base,
                             tiles_per_row=tiles_per_row)
    return pl.pallas_call(
        kern,
        grid=(n_tiles,),
        in_specs=[pl.BlockSpec((tm, d), lambda i: (0, 0)),
                  pl.BlockSpec((tm, d), lambda i: (jnp.minimum(i + 1, n_tiles - 1), 0)),
                  pl.BlockSpec(mod.shape, lambda i: (0, 0)),
                  pl.BlockSpec(w_main.shape, lambda i: (0, 0), pipeline_mode=pl.Buffered(1)),
                  pl.BlockSpec((d, GATE_LANES), lambda i: (0, 0)),
                  pl.BlockSpec((4 * ML_HEADS, 1), lambda i: (0, 0)),
                  pl.BlockSpec((CONV_W, d), lambda i: (0, 0)),
                  pl.BlockSpec((1, d), lambda i: (0, 0))],
        out_specs=[pl.BlockSpec((5, tm, d), lambda i: (0, i, 0)),
                   pl.BlockSpec((3, tm, d), lambda i: (0, i, 0)),
                   pl.BlockSpec((tm, GATE_LANES), lambda i: (i, 0)),
                   pl.BlockSpec((4 * ML_HEADS, tm), lambda i: (0, i))],
        out_shape=[jax.ShapeDtypeStruct((5, n, d), F32),
                   jax.ShapeDtypeStruct((3, n, d), BF16),
                   jax.ShapeDtypeStruct((n, GATE_LANES), F32),
                   jax.ShapeDtypeStruct((4 * ML_HEADS, n), F32)],
        scratch_shapes=[pltpu.VMEM((tm, d), BF16), pltpu.VMEM((tm, d), BF16)],
        compiler_params=pltpu.CompilerParams(dimension_semantics=("arbitrary",),
                                             vmem_limit_bytes=IN_PROJ_VMEM_LIMIT),
        name="in_proj",
    )(x, x, mod, w_main, w_gate, b_gate, conv_w, conv_b)


def _rglru_kernel(xc_ref, gz_ref, w_ref, b_ref, lam_ref, h0_ref, *rest, n_vseq, n_seg, want_final):
    if want_final:
        out_ref, hfin_ref = rest[:2]
        rest = rest[2:]
    else:
        out_ref = rest[0]
        rest = rest[1:]
    a_f, b_f, a_b, b_b = rest[:4]
    rest = rest[4:]
    seg = SCAN_SEG
    pitch = SCAN_PITCH
    ct = xc_ref.shape[1]
    ch = seg
    la = (0.5 * RG_C * LOG2E) * _log_sigmoid(lam_ref[0])
    la_f, la_b = la[:, :ct], la[:, ct:]
    w = w_ref[0]
    bias = b_ref[0]

    def for_chunk_groups(fn):
        def trip(k, carry):
            base = pl.multiple_of(k * (SCAN_GROUP * pitch), SUBLANES)
            for j in range(SCAN_GROUP):
                fn(k * SCAN_GROUP + j, base + j * pitch)
            return carry

        lax.fori_loop(0, n_vseq // SCAN_GROUP, trip, 0)

    def gate_chunk(ci, srow):
        r0 = pl.multiple_of(ci * ch, ch)
        xc = xc_ref[pl.ds(r0, ch), :]
        gates = jnp.dot(xc.astype(BF16), w, preferred_element_type=F32) + bias

        def a_and_b(r_half, i_half, la_half):
            a = jnp.exp2(la_half + la_half * jnp.tanh(r_half))
            y = jnp.clip(1.0 - a * a, 0.0, 1.0)
            gain = jnp.where(y > 0.0, y * lax.rsqrt(y), 0.0)
            half_gx = (0.5 * gain) * xc
            return a, half_gx + half_gx * jnp.tanh(i_half)

        af, bf = a_and_b(gates[:, 0:ct], gates[:, ct:2 * ct], la_f)
        ab, bb = a_and_b(gates[:, 2 * ct:3 * ct], gates[:, 3 * ct:4 * ct], la_b)
        s0 = pl.ds(srow, ch)
        a_f[s0, :] = af
        b_f[s0, :] = bf
        a_b[s0, :] = ab
        b_b[s0, :] = bb

    for_chunk_groups(gate_chunk)

    def slab_idx(t):
        return pl.ds(t, n_vseq, stride=pitch)

    def slab(ref, t):
        return ref[slab_idx(t), :]

    if n_seg == 1:
        init_f = h0_ref[0]
        init_b = h0_ref[1]
    else:
        e_f, p_f, e_b, p_b, cin_f, cin_b = rest
        n_real = n_vseq // n_seg
        zero = jnp.zeros((n_vseq, ct), F32)
        one = jnp.ones((n_vseq, ct), F32)

        def local_step(t, carry):
            ef, pf, eb, pb = carry
            tb = seg - 1 - t
            a = slab(a_f, t)
            ef = a * ef + slab(b_f, t)
            pf = a * pf
            a2 = slab(a_b, tb)
            eb = a2 * eb + slab(b_b, tb)
            pb = a2 * pb
            return ef, pf, eb, pb

        ef, pf, eb, pb = lax.fori_loop(0, seg, local_step, (zero, one, zero, one), unroll=8)
        e_f[...] = ef
        p_f[...] = pf
        e_b[...] = eb
        p_b[...] = pb
        carry = h0_ref[0]
        for s in range(n_seg):
            idx = pl.ds(s, n_real, stride=n_seg)
            cin_f[idx, :] = carry
            carry = e_f[idx, :] + p_f[idx, :] * carry
        carry = h0_ref[1]
        for s in reversed(range(n_seg)):
            idx = pl.ds(s, n_real, stride=n_seg)
            cin_b[idx, :] = carry
            carry = e_b[idx, :] + p_b[idx, :] * carry
        init_f = cin_f[...]
        init_b = cin_b[...]

    def scan_step(t, carry):
        hf, hb = carry
        tb = seg - 1 - t
        hf = slab(a_f, t) * hf + slab(b_f, t)
        b_f[slab_idx(t), :] = hf
        hb = slab(a_b, tb) * hb + slab(b_b, tb)
        b_b[slab_idx(tb), :] = hb
        return hf, hb

    hf_last, hb_last = lax.fori_loop(0, seg, scan_step, (init_f, init_b), unroll=8)
    if want_final:
        hfin_ref[0] = hf_last
        hfin_ref[1] = hb_last

    def out_chunk(ci, srow):
        r0 = pl.multiple_of(ci * ch, ch)
        h = b_f[pl.ds(srow, ch), :] + b_b[pl.ds(srow, ch), :]
        out_ref[pl.ds(r0, ch), :] = (h * gz_ref[pl.ds(r0, ch), :]).astype(BF16)

    for_chunk_groups(out_chunk)


def _rglru(of32, w_cat, b_cat, lam_cat, h0, *, n_seq, seq_len, want_final):
    n, d_rnn = of32.shape[1:]
    ct = RG_CT
    n_ct = d_rnn // ct
    n_seg = seq_len // SCAN_SEG
    n_vseq = n_seq * n_seg
    assert n_vseq * SCAN_SEG == n and n_vseq % SCAN_GROUP == 0
    kern = functools.partial(_rglru_kernel, n_vseq=n_vseq, n_seg=n_seg, want_final=want_final)
    out_specs = [pl.BlockSpec((n, ct), lambda c: (0, c))]
    out_shape = [jax.ShapeDtypeStruct((n, d_rnn), BF16)]
    if want_final:
        out_specs.append(pl.BlockSpec((2, n_seq, ct), lambda c: (0, 0, c)))
        out_shape.append(jax.ShapeDtypeStruct((2, n_seq, d_rnn), F32))
    scratch = [pltpu.VMEM((n_vseq * SCAN_PITCH, ct), F32) for _ in range(4)]
    if n_seg > 1:
        scratch += [pltpu.VMEM((n_vseq, ct), F32) for _ in range(6)]
    return pl.pallas_call(
        kern,
        grid=(n_ct,),
        in_specs=[pl.BlockSpec((None, n, ct), lambda c: (0, 0, c)),
                  pl.BlockSpec((None, n, ct), lambda c: (1, 0, c)),
                  pl.BlockSpec((1, ct, 4 * ct), lambda c: (c, 0, 0)),
                  pl.BlockSpec((1, 1, 4 * ct), lambda c: (c, 0, 0)),
                  pl.BlockSpec((1, 1, 2 * ct), lambda c: (c, 0, 0)),
                  pl.BlockSpec((2, n_seq, ct), lambda c: (0, 0, c))],
        out_specs=out_specs,
        out_shape=out_shape,
        scratch_shapes=scratch,
        compiler_params=_cparams(1),
        name="rglru_ctx" if want_final else "rglru_lat",
    )(of32, of32, w_cat, b_cat, lam_cat, h0)


def _causal_bias(mask_ref):
    lc = mask_ref.shape[1]
    ii = lax.broadcasted_iota(jnp.int32, (lc, lc), 0)
    jj = lax.broadcasted_iota(jnp.int32, (lc, lc), 1)
    mask_ref[0] = jnp.where(jj <= ii, 0.0, -jnp.inf)
    mask_ref[1] = jnp.where(jj >= ii, 0.0, -jnp.inf)


def _with_ones(v):
    return jnp.concatenate([v, jnp.ones((v.shape[0], GATE_LANES), v.dtype)], axis=1)


def _rep(col):
    return jnp.broadcast_to(col, (col.shape[0], GATE_LANES))


def _wide(rep, width):
    return jnp.concatenate([rep] * (width // GATE_LANES), axis=1)


def _mlstm_unit(direction, head, q, k, v, v1, gc, gt_ref, r0, mask_ref, state, scale):
    lc, hd = q.shape
    col_i = direction * ML_HEADS + head
    col_c = 2 * ML_HEADS + col_i
    lane = lax.broadcasted_iota(jnp.int32, (1, GATE_LANES), 1)
    ig_c = _rep(jnp.sum(jnp.where(lane == col_i, gc, 0.0), -1, keepdims=True))
    cum_c = _rep(jnp.sum(jnp.where(lane == col_c, gc, 0.0), -1, keepdims=True))
    ig_row = gt_ref[pl.ds(col_i, 1), pl.ds(r0, lc)]
    cum_row = gt_ref[pl.ds(col_c, 1), pl.ds(r0, lc)]
    dmat = (_wide(cum_c, lc) + (ig_row - cum_row)) + mask_ref[direction]
    if state is None:
        m_prev = 0.0
    else:
        s_prev, m_prev = state
    inter = cum_c + m_prev
    m_row = jnp.maximum(_rep(jnp.max(dmat, -1, keepdims=True)), inter)
    wts = jnp.exp(dmat - _wide(m_row, lc))
    s = lax.dot_general(q, k, (((1,), (1,)), ((), ())), preferred_element_type=F32) * (wts * scale)
    if state is None:
        num = jnp.dot(s.astype(BF16), v, preferred_element_type=F32)
        den = _rep(jnp.sum(s, -1, keepdims=True))
        h = num * _wide(1.0 / jnp.maximum(jnp.abs(den), jnp.exp(-m_row)), hd)
    else:
        w_inter = jnp.exp(inter - m_row) * scale
        nd = (jnp.dot(s.astype(BF16), v1, preferred_element_type=F32)
              + _wide(w_inter, hd + GATE_LANES) * jnp.dot(q, s_prev.astype(BF16), preferred_element_type=F32))
        rden = 1.0 / jnp.maximum(jnp.abs(nd[:, hd:]), jnp.exp(-m_row))
        h = nd[:, :hd] * _wide(rden, hd)
    btot = cum_c[lc - 1:lc] if direction == 0 else cum_c[0:1]
    wk_log = btot - cum_c + ig_c
    m_new = jnp.maximum(btot + m_prev, jnp.max(wk_log, 0, keepdims=True))
    kw = k.astype(F32) * _wide(jnp.exp(wk_log - m_new), hd)
    t_lhs = (((0,), (0,)), ((), ()))
    if state is None:
        c_new = lax.dot_general(kw.astype(BF16), v, t_lhs, preferred_element_type=F32)
        return h, c_new, jnp.sum(kw, 0, keepdims=True), m_new
    decay = _wide(jnp.exp(btot + m_prev - m_new), hd + GATE_LANES)
    s_new = decay * s_prev + lax.dot_general(kw.astype(BF16), v1, t_lhs, preferred_element_type=F32)
    return h, s_new, None, m_new


HEAD_OUT_ROWS = 1024


def _head_out_all(h_acc, og_ref, gn_ref, out_ref):
    gn = gn_ref[...]
    rows = h_acc.shape[0]
    step = min(rows, HEAD_OUT_ROWS)
    for r0 in range(0, rows, step):
        rs = slice(r0, r0 + step)
        out_ref[rs, :] = (og_ref[rs, :] * (_ln(h_acc[rs, :]) * gn)).astype(BF16)


def _mlstm_ctx_kernel(q_ref, k_ref, v_ref, o_ref, g_ref, gt_ref, gn_ref,
                      out_ref, cfin_ref, nfin_ref, mfin_ref, mask_ref, h_acc, *, n_sub, scale):
    head = pl.program_id(1)
    lc = MLSTM_CHUNK
    _causal_bias(mask_ref)

    def body(s, carry):
        r0 = pl.multiple_of(s * lc, lc)
        rs = pl.ds(r0, lc)
        q, k, v, gc = q_ref[rs, :], k_ref[rs, :], v_ref[rs, :], g_ref[rs, :]
        hsum = None
        for direction in (0, 1):
            h, c_new, n_new, m_new = _mlstm_unit(direction, head, q, k, v, None, gc, gt_ref, r0, mask_ref,
                                                 None, scale)
            cfin_ref[s, direction, 0] = c_new
            nfin_ref[s, direction, 0] = n_new
            mfin_ref[s, direction, 0] = m_new
            hsum = h if hsum is None else hsum + h
        h_acc[rs, :] = hsum
        return carry

    lax.fori_loop(0, n_sub, body, 0)
    _head_out_all(h_acc, o_ref, gn_ref, out_ref)


def _mlstm_lat_kernel(q_ref, k_ref, v_ref, o_ref, g_ref, gt_ref, gn_ref, c0_ref, n0_ref, m0_ref,
                      out_ref, s_st, m_st, h_acc, mask_ref, *, n_chunks, scale):
    head = pl.program_id(1)
    lc = MLSTM_CHUNK
    hd = q_ref.shape[1]
    _causal_bias(mask_ref)
    for direction in (0, 1):
        s_st[direction, :, 0:hd] = c0_ref[0, direction, 0]
        s_st[direction, :, hd:] = n0_ref[0, direction, 0]
        m_st[direction] = m0_ref[0, direction, 0]
    h_acc[...] = jnp.zeros(h_acc.shape, F32)

    def body(c, carry):
        for direction in (0, 1):
            cc = c if direction == 0 else n_chunks - 1 - c
            r0 = pl.multiple_of(cc * lc, lc)
            rs = pl.ds(r0, lc)
            state = (s_st[direction], m_st[direction])
            v = v_ref[rs, :]
            h, s_new, _, m_new = _mlstm_unit(direction, head, q_ref[rs, :], k_ref[rs, :], v, _with_ones(v),
                                             g_ref[rs, :], gt_ref, r0, mask_ref, state, scale)
            s_st[direction] = s_new
            m_st[direction] = m_new
            h_acc[rs, :] += h
        return carry

    lax.fori_loop(0, n_chunks, body, 0, unroll=2)
    _head_out_all(h_acc, o_ref, gn_ref, out_ref)


def _mlstm_in_specs(rows, hd):
    slab = lambda idx: pl.BlockSpec((None, rows, hd), lambda s, h: (idx, s, h))
    return [slab(0), slab(1), slab(2), slab(2),
            pl.BlockSpec((rows, GATE_LANES), lambda s, h: (s, 0)),
            pl.BlockSpec((4 * ML_HEADS, rows), lambda s, h: (0, s)),
            pl.BlockSpec((1, hd), lambda s, h: (0, h))]


def _mlstm_ctx(obf, of32, g, gt, gn_g, *, n_seq):
    n = obf.shape[1]
    ml_dim = gn_g.shape[1]
    hd = ml_dim // ML_HEADS
    lc = MLSTM_CHUNK
    assert n == n_seq * lc
    n_sub = 4
    rows = n_sub * lc
    kern = functools.partial(_mlstm_ctx_kernel, n_sub=n_sub, scale=hd ** -0.5)
    return pl.pallas_call(
        kern,
        grid=(n_seq // n_sub, ML_HEADS),
        in_specs=_mlstm_in_specs(rows, hd),
        out_specs=[pl.BlockSpec((rows, hd), lambda s, h: (s, h)),
                   pl.BlockSpec((n_sub, 2, 1, hd, hd), lambda s, h: (s, 0, h, 0, 0)),
                   pl.BlockSpec((n_sub, 2, 1, 1, hd), lambda s, h: (s, 0, h, 0, 0)),
                   pl.BlockSpec((n_sub, 2, 1, 1, GATE_LANES), lambda s, h: (s, 0, h, 0, 0))],
        out_shape=[jax.ShapeDtypeStruct((n, ml_dim), BF16),
                   jax.ShapeDtypeStruct((n_seq, 2, ML_HEADS, hd, hd), F32),
                   jax.ShapeDtypeStruct((n_seq, 2, ML_HEADS, 1, hd), F32),
                   jax.ShapeDtypeStruct((n_seq, 2, ML_HEADS, 1, GATE_LANES), F32)],
        scratch_shapes=[pltpu.VMEM((2, lc, lc), F32), pltpu.VMEM((rows, hd), F32)],
        compiler_params=_cparams(2),
        name="mlstm_ctx",
    )(obf, obf, obf, of32, g, gt, gn_g)


def _mlstm_lat(obf, of32, g, gt, gn_g, c0, n0, m0, *, n_seq):
    n = obf.shape[1]
    ml_dim = gn_g.shape[1]
    hd = ml_dim // ML_HEADS
    lc = MLSTM_CHUNK
    rows = n // n_seq
    n_chunks = rows // lc
    kern = functools.partial(_mlstm_lat_kernel, n_chunks=n_chunks, scale=hd ** -0.5)
    return pl.pallas_call(
        kern,
        grid=(n_seq, ML_HEADS),
        in_specs=_mlstm_in_specs(rows, hd) + [
                  pl.BlockSpec((1, 2, 1, hd, hd), lambda s, h: (s, 0, h, 0, 0)),
                  pl.BlockSpec((1, 2, 1, hd, GATE_LANES), lambda s, h: (s, 0, h, 0, 0)),
                  pl.BlockSpec((1, 2, 1, 1, GATE_LANES), lambda s, h: (s, 0, h, 0, 0))],
        out_specs=pl.BlockSpec((rows, hd), lambda s, h: (s, h)),
        out_shape=jax.ShapeDtypeStruct((n, ml_dim), BF16),
        scratch_shapes=[pltpu.VMEM((2, hd, hd + GATE_LANES), F32),
                        pltpu.VMEM((2, 1, GATE_LANES), F32),
                        pltpu.VMEM((rows, hd), F32),
                        pltpu.VMEM((2, lc, lc), F32)],
        compiler_params=_cparams(2),
        name="mlstm_lat",
    )(obf, obf, obf, of32, g, gt, gn_g, c0, n0, m0)


def _merge_kernel(hrg_ref, hml_ref, gma_ref, gmb_ref, x_ref, mod_ref, wrg_ref, wml_ref, wout_ref,
                  bm_ref, lng_ref, lnb_ref, o_ref, *, d, alpha, row_base, tiles_per_row):
    r = _mod_row(pl.program_id(0), row_base, tiles_per_row)
    gate1 = mod_ref[pl.ds(r, 1), 2 * d:3 * d]
    rows_per_group = x_ref.shape[0] // MERGE_ROW_GROUPS
    for grp in range(MERGE_ROW_GROUPS):
        rows = slice(grp * rows_per_group, (grp + 1) * rows_per_group)
        y_rg = jnp.dot(hrg_ref[rows, :], wrg_ref[...], preferred_element_type=F32)
        y_ml = jnp.dot(hml_ref[rows, :], wml_ref[...], preferred_element_type=F32)
        g_rg = jax.nn.sigmoid(gma_ref[rows, :] + bm_ref[:, 0:d])
        g_ml = jax.nn.sigmoid(gmb_ref[rows, :] + bm_ref[:, d:2 * d])
        merged = (g_rg * y_rg + g_ml * y_ml).astype(BF16)
        mix = jnp.dot(merged, wout_ref[...], preferred_element_type=F32)
        o_ref[rows, :] = _ln(alpha * x_ref[rows, :] + gate1 * mix) * lng_ref[...] + lnb_ref[...]


def _merge(hrg, hml, of32, x, mod, w_rg, w_ml, w_out, b_merge, ln_g, ln_b, *, alpha, row_base,
           rows_per_mod, tm):
    n, d = x.shape
    tiles_per_row = _tiles_per_row(rows_per_mod, tm)
    kern = functools.partial(_merge_kernel, d=d, alpha=alpha, row_base=row_base, tiles_per_row=tiles_per_row)
    full = lambda shape: pl.BlockSpec(shape, lambda i: (0,) * len(shape))
    resident = lambda shape: pl.BlockSpec(shape, lambda i: (0, 0), pipeline_mode=pl.Buffered(1))
    return pl.pallas_call(
        kern,
        grid=(n // tm,),
        in_specs=[pl.BlockSpec((tm, d), lambda i: (i, 0)),
                  pl.BlockSpec((tm, d), lambda i: (i, 0)),
                  pl.BlockSpec((None, tm, d), lambda i: (3, i, 0)),
                  pl.BlockSpec((None, tm, d), lambda i: (4, i, 0)),
                  pl.BlockSpec((tm, d), lambda i: (i, 0)),
                  full(mod.shape), resident((d, d)), resident((d, d)), resident((d, d)),
                  full((1, 2 * d)), full((1, d)), full((1, d))],
        out_specs=pl.BlockSpec((tm, d), lambda i: (i, 0)),
        out_shape=jax.ShapeDtypeStruct((n, d), F32),
        compiler_params=_cparams(1),
        name="merge",
    )(hrg, hml, of32, of32, x, mod, w_rg, w_ml, w_out, b_merge, ln_g, ln_b)


def _mlp_kernel(x_ref, mod_ref, wfc_ref, bfc_ref, wpj_ref, bpj_ref, lng_ref, lnb_ref, o_ref,
                u_ref, *, d, tf, alpha, row_base, tiles_per_row):
    r = _mod_row(pl.program_id(0), row_base, tiles_per_row)
    shift = mod_ref[pl.ds(r, 1), 3 * d:4 * d]
    scale = mod_ref[pl.ds(r, 1), 4 * d:5 * d]
    gate2 = mod_ref[pl.ds(r, 1), 5 * d:6 * d]
    rows_per_group = x_ref.shape[0] // MLP_ROW_GROUPS
    for grp in range(MLP_ROW_GROUPS):
        rows = slice(grp * rows_per_group, (grp + 1) * rows_per_group)
        u_ref[rows, :] = (_ln(x_ref[rows, :]) * (1.0 + scale) + shift).astype(BF16)
        acc = None
        for f in range(wfc_ref.shape[1] // tf):
            cols = slice(f * tf, (f + 1) * tf)
            hid = jnp.dot(u_ref[rows, :], wfc_ref[:, cols], preferred_element_type=F32) + bfc_ref[:, cols]
            hid = jnp.square(jnp.maximum(hid, 0.0)).astype(BF16)
            part = jnp.dot(hid, wpj_ref[cols, :], preferred_element_type=F32)
            acc = part if acc is None else acc + part
        y = alpha * x_ref[rows, :] + gate2 * (acc + bpj_ref[...])
        o_ref[rows, :] = _ln(y) * lng_ref[...] + lnb_ref[...]


def _mlp(x, mod, w_fc, b_fc, w_proj, b_proj, ln_g, ln_b, *, alpha, row_base, rows_per_mod, tm):
    n, d = x.shape
    tiles_per_row = _tiles_per_row(rows_per_mod, tm)
    d_ff = w_fc.shape[1]
    tf = 1024
    assert d_ff % tf == 0
    kern = functools.partial(_mlp_kernel, d=d, tf=tf, alpha=alpha, row_base=row_base,
                             tiles_per_row=tiles_per_row)
    resident = lambda shape: pl.BlockSpec(shape, lambda i: (0, 0), pipeline_mode=pl.Buffered(1))
    return pl.pallas_call(
        kern,
        grid=(n // tm,),
        in_specs=[pl.BlockSpec((tm, d), lambda i: (i, 0)),
                  pl.BlockSpec(mod.shape, lambda i: (0, 0)),
                  resident((d, d_ff)),
                  pl.BlockSpec((1, d_ff), lambda i: (0, 0)),
                  resident((d_ff, d)),
                  pl.BlockSpec((1, d), lambda i: (0, 0)),
                  pl.BlockSpec((1, d), lambda i: (0, 0)),
                  pl.BlockSpec((1, d), lambda i: (0, 0))],
        out_specs=pl.BlockSpec((tm, d), lambda i: (i, 0)),
        out_shape=jax.ShapeDtypeStruct((n, d), F32),
        scratch_shapes=[pltpu.VMEM((tm, d), BF16)],
        compiler_params=_cparams(1),
        name="mlp",
    )(x, mod, w_fc, b_fc, w_proj, b_proj, ln_g, ln_b)


def _tail_kernel(hrg_ref, hml_ref, gma_ref, gmb_ref, x_ref, mod_ref, wrg_ref, wml_ref, wout_ref, bm_ref,
                 lng0_ref, lnb0_ref, wfc_ref, bfc_ref, wpj_ref, bpj_ref, lng1_ref, lnb1_ref, o_ref,
                 x1_ref, u_ref, *, d, tf, alpha, row_base, tiles_per_row):
    r = _mod_row(pl.program_id(0), row_base, tiles_per_row)
    gate1 = mod_ref[pl.ds(r, 1), 2 * d:3 * d]
    shift2 = mod_ref[pl.ds(r, 1), 3 * d:4 * d]
    scale2 = mod_ref[pl.ds(r, 1), 4 * d:5 * d]
    gate2 = mod_ref[pl.ds(r, 1), 5 * d:6 * d]
    rows_per_group = x_ref.shape[0] // TAIL_ROW_GROUPS
    for grp in range(TAIL_ROW_GROUPS):
        rows = slice(grp * rows_per_group, (grp + 1) * rows_per_group)
        y_rg = jnp.dot(hrg_ref[rows, :], wrg_ref[...], preferred_element_type=F32)
        y_ml = jnp.dot(hml_ref[rows, :], wml_ref[...], preferred_element_type=F32)
        g_rg = jax.nn.sigmoid(gma_ref[rows, :] + bm_ref[:, 0:d])
        g_ml = jax.nn.sigmoid(gmb_ref[rows, :] + bm_ref[:, d:2 * d])
        merged = (g_rg * y_rg + g_ml * y_ml).astype(BF16)
        mix = jnp.dot(merged, wout_ref[...], preferred_element_type=F32)
        x1 = _ln(alpha * x_ref[rows, :] + gate1 * mix) * lng0_ref[...] + lnb0_ref[...]
        x1_ref[rows, :] = x1
        u_ref[rows, :] = (_ln(x1) * (1.0 + scale2) + shift2).astype(BF16)
        acc = None
        for f in range(wfc_ref.shape[1] // tf):
            cols = slice(f * tf, (f + 1) * tf)
            hid = jnp.dot(u_ref[rows, :], wfc_ref[:, cols], preferred_element_type=F32) + bfc_ref[:, cols]
            hid = jnp.square(jnp.maximum(hid, 0.0)).astype(BF16)
            part = jnp.dot(hid, wpj_ref[cols, :], preferred_element_type=F32)
            acc = part if acc is None else acc + part
        y = alpha * x1_ref[rows, :] + gate2 * (acc + bpj_ref[...])
        o_ref[rows, :] = _ln(y) * lng1_ref[...] + lnb1_ref[...]


def _tail(hrg, hml, of32, x, mod, w_rg, w_ml, w_out, b_merge, ln_g0, ln_b0, w_fc, b_fc, w_proj, b_proj,
          ln_g1, ln_b1, *, alpha, row_base, rows_per_mod, tm):
    n, d = x.shape
    tiles_per_row = _tiles_per_row(rows_per_mod, tm)
    d_ff = w_fc.shape[1]
    tf = 1024
    assert d_ff % tf == 0
    kern = functools.partial(_tail_kernel, d=d, tf=tf, alpha=alpha, row_base=row_base,
                             tiles_per_row=tiles_per_row)
    tile = pl.BlockSpec((tm, d), lambda i: (i, 0))
    full = lambda shape: pl.BlockSpec(shape, lambda i: (0,) * len(shape))
    resident = lambda shape: pl.BlockSpec(shape, lambda i: (0, 0), pipeline_mode=pl.Buffered(1))
    return pl.pallas_call(
        kern,
        grid=(n // tm,),
        in_specs=[tile, tile,
                  pl.BlockSpec((None, tm, d), lambda i: (3, i, 0)),
                  pl.BlockSpec((None, tm, d), lambda i: (4, i, 0)),
                  tile, full(mod.shape), resident((d, d)), resident((d, d)), resident((d, d)),
                  full((1, 2 * d)), full((1, d)), full((1, d)),
                  resident((d, d_ff)), full((1, d_ff)), resident((d_ff, d)),
                  full((1, d)), full((1, d)), full((1, d))],
        out_specs=tile,
        out_shape=jax.ShapeDtypeStruct((n, d), F32),
        scratch_shapes=[pltpu.VMEM((tm, d), F32), pltpu.VMEM((tm, d), BF16)],
        compiler_params=_cparams(1),
        name="tail",
    )(hrg, hml, of32, of32, x, mod, w_rg, w_ml, w_out, b_merge, ln_g0, ln_b0, w_fc, b_fc, w_proj, b_proj,
      ln_g1, ln_b1)


def _rg_gate_weights(wa, ba, wx, bx, lam):
    ct = RG_CT
    d_rnn = ba.shape[-1]
    n_ct = d_rnn // ct
    per = ct // RG_BW

    def tile_blockdiag(w):
        wt = w.reshape(n_ct, per, RG_BW, RG_BW)
        eye = jnp.eye(per, dtype=w.dtype)
        return jnp.einsum('cpkj,pq->cpkqj', wt, eye).reshape(n_ct, ct, ct)

    w_cat = (0.5 * jnp.concatenate([tile_blockdiag(wa[0]), tile_blockdiag(wx[0]),
                                    tile_blockdiag(wa[1]), tile_blockdiag(wx[1])], axis=-1)).astype(BF16)
    tiles = lambda b: b.reshape(n_ct, 1, ct)
    b_cat = 0.5 * jnp.concatenate([tiles(ba[0]), tiles(bx[0]), tiles(ba[1]), tiles(bx[1])], axis=-1)
    lam_cat = jnp.concatenate([tiles(lam[0]), tiles(lam[1])], axis=-1)
    return w_cat, b_cat, lam_cat


def kernel(x_prompt, x_sample, state_rglru_h, state_mlstm_C, state_mlstm_n, state_mlstm_m, c, c_ctx, w_ada, b_ada, w_in, rg_conv_w, rg_conv_b, rg_wa, rg_ba, rg_wx, rg_bx, rg_lambda, w_rg_proj, ml_b_igate, ml_b_fgate, ml_gn_g, w_ml_proj, b_merge, w_out, ln_g, ln_b, w_fc, b_fc, w_proj, b_proj):
    bp, seq, d = x_prompt.shape
    bl, dec_seq, _ = x_sample.shape
    depth = w_in.shape[0]
    d_rnn = rg_conv_w.shape[-1]
    ml_dim = ml_gn_g.shape[-1]
    hd = ml_dim // ML_HEADS
    alpha = (2 * depth) ** 0.25
    d_main = 2 * d_rnn + 4 * ml_dim + 2 * d
    assert seq == MLSTM_CHUNK == SCAN_SEG and dec_seq % MLSTM_CHUNK == 0
    assert d_rnn == d and ml_dim == d and w_in.shape[-1] == d_main + 4 * ML_HEADS

    tm = 1024
    tm_proj = 512
    xp = x_prompt.reshape(bp * seq, d)
    xs = x_sample.reshape(bl * dec_seq, d)
    cond = jnp.concatenate([c_ctx[None, :], c], axis=0)
    zero_h = jnp.zeros((2, bp, d_rnn), F32)
    new_h, new_c, new_n, new_m = [], [], [], []
    for l in range(depth):
        mod = _ada(cond, w_ada[l], b_ada[l])
        w_main = w_in[l].astype(BF16)
        w_gate = jnp.zeros((d, GATE_LANES), BF16).at[:, :4 * ML_HEADS].set(w_in[l][:, d_main:].astype(BF16))
        b_gate = jnp.concatenate([ml_b_igate[l].reshape(-1), ml_b_fgate[l].reshape(-1)]).reshape(-1, 1)
        w_cat, b_cat, lam_cat = _rg_gate_weights(rg_wa[l], rg_ba[l], rg_wx[l], rg_bx[l], rg_lambda[l])
        conv_b = rg_conv_b[l].reshape(1, d_rnn)
        gn_g = ml_gn_g[l].reshape(1, ml_dim)
        w_rg, w_ml, w_o = (w_rg_proj[l].astype(BF16), w_ml_proj[l].astype(BF16), w_out[l].astype(BF16))
        w_fc_b, w_pj_b = w_fc[l].astype(BF16), w_proj[l].astype(BF16)
        bm = b_merge[l].reshape(1, 2 * d)
        lng0, lnb0 = ln_g[l, 0].reshape(1, d), ln_b[l, 0].reshape(1, d)
        lng1, lnb1 = ln_g[l, 1].reshape(1, d), ln_b[l, 1].reshape(1, d)
        bfc, bpj = b_fc[l].reshape(1, -1), b_proj[l].reshape(1, d)

        def tail(x, hrg, hml, of32, row_base, rows_per_mod):
            return _tail(hrg, hml, of32, x, mod, w_rg, w_ml, w_o, bm, lng0, lnb0, w_fc_b, bfc, w_pj_b, bpj,
                         lng1, lnb1, alpha=alpha, row_base=row_base, rows_per_mod=rows_per_mod, tm=tm_proj)

        of32, obf, g, gt = _in_proj(xp, mod, w_main, w_gate, b_gate, rg_conv_w[l], conv_b, conv_len=seq,
                                    row_base=0, rows_per_mod=None, tm=tm_proj)
        hrg, h_fin = _rglru(of32, w_cat, b_cat, lam_cat, zero_h, n_seq=bp, seq_len=seq, want_final=True)
        hml, c_fin, n_fin, m_fin = _mlstm_ctx(obf, of32, g, gt, gn_g, n_seq=bp)
        xp = tail(xp, hrg, hml, of32, 0, None)
        new_h.append(jnp.transpose(h_fin, (1, 0, 2)))
        new_c.append(c_fin)
        new_n.append(n_fin.reshape(bp, 2, ML_HEADS, hd))
        new_m.append(m_fin[..., 0, 0])

        of32, obf, g, gt = _in_proj(xs, mod, w_main, w_gate, b_gate, rg_conv_w[l], conv_b, conv_len=GRID_W,
                                    row_base=1, rows_per_mod=dec_seq, tm=tm_proj)
        h0 = jnp.transpose(state_rglru_h[:, l], (1, 0, 2))
        (hrg,) = _rglru(of32, w_cat, b_cat, lam_cat, h0, n_seq=bl, seq_len=dec_seq, want_final=False)
        c0 = state_mlstm_C[:, l]
        n0 = jnp.broadcast_to(state_mlstm_n[:, l][..., None], (bl, 2, ML_HEADS, hd, GATE_LANES))
        m0 = jnp.broadcast_to(state_mlstm_m[:, l].reshape(bl, 2, ML_HEADS, 1, 1),
                              (bl, 2, ML_HEADS, 1, GATE_LANES))
        hml = _mlstm_lat(obf, of32, g, gt, gn_g, c0, n0, m0, n_seq=bl)
        xs = tail(xs, hrg, hml, of32, 1, dec_seq)

    def stack(parts):
        return parts[0][:, None] if len(parts) == 1 else jnp.stack(parts, axis=1)

    return (xp.reshape(bp, seq, d), xs.reshape(bl, dec_seq, d),
            stack(new_h), stack(new_c), stack(new_n), stack(new_m))
```

```python
import functools

import jax
import jax.numpy as jnp
from jax import lax
from jax.experimental import pallas as pl
from jax.experimental.pallas import tpu as pltpu

F32 = jnp.float32
BF16 = jnp.bfloat16

LN_EPS = 1e-5
LOG2E = 1.4426950408889634
LN2 = 0.6931471805599453
RG_C = 8.0
RG_BW = 64
CONV_W = 4
GRID_W = 64
ML_HEADS = 4
MLSTM_CHUNK = 256
SCAN_SEG = 256
SUBLANES = 8
SCAN_PITCH = SCAN_SEG + 4
SCAN_GROUP = 8
RG_CT = 128
GATE_LANES = 128
VMEM_LIMIT = 52 * 1024 * 1024
MERGE_ROW_GROUPS = 4
MLP_ROW_GROUPS = 2
IN_PROJ_VMEM_LIMIT = 58 * 1024 * 1024


def _cparams(n_axes):
    return pltpu.CompilerParams(
        dimension_semantics=("arbitrary",) * n_axes, vmem_limit_bytes=VMEM_LIMIT)


def _log_sigmoid(x):
    return jnp.minimum(x, 0.0) - jnp.log1p(jnp.exp(-jnp.abs(x)))


def _ln(x):
    mu = jnp.mean(x, -1, keepdims=True)
    xc = x - mu
    var = jnp.mean(xc * xc, -1, keepdims=True)
    return xc * lax.rsqrt(var + LN_EPS)


def _mod_row(i, row_base, tiles_per_row):
    if tiles_per_row is None:
        return row_base
    return row_base + lax.div(i, jnp.int32(tiles_per_row))


def _ada_kernel(ct_ref, w_ref, b_ref, o_ref, *, n_rows):
    w = w_ref[...]
    ct = ct_ref[...]
    s = ct * jax.nn.sigmoid(ct)
    o_ref[...] = jnp.zeros(o_ref.shape, F32)
    for r in range(n_rows):
        o_ref[r:r + 1, :] = jnp.sum(w * s[:, r:r + 1], axis=0, keepdims=True) + b_ref[...]


def _ada(cond, w_ada, b_ada):
    n_rows, d = cond.shape
    assert n_rows <= 8
    ct = jnp.zeros((d, 8), F32).at[:, :n_rows].set(cond.T)
    n_out = w_ada.shape[1]
    tn = 2048
    assert n_out % tn == 0
    return pl.pallas_call(
        functools.partial(_ada_kernel, n_rows=n_rows),
        grid=(n_out // tn,),
        in_specs=[pl.BlockSpec((d, 8), lambda j: (0, 0)),
                  pl.BlockSpec((d, tn), lambda j: (0, j)),
                  pl.BlockSpec((1, tn), lambda j: (0, j))],
        out_specs=pl.BlockSpec((8, tn), lambda j: (0, j)),
        out_shape=jax.ShapeDtypeStruct((8, n_out), F32),
        compiler_params=_cparams(1),
        name="ada_mod",
    )(ct, w_ada, b_ada.reshape(1, n_out))


def _inproj_tile(u_ref, w_ref, wg_ref, bg_ref, cw_ref, cb_ref, of_ref, ob_ref, g_ref, gt_ref, *, d, tm,
                 conv_len):
    lc = MLSTM_CHUNK
    n_gate = 4 * ML_HEADS
    g = jnp.dot(u_ref[...], wg_ref[...], preferred_element_type=F32)
    gt = g.T[0:n_gate, :] + bg_ref[...]
    gt = jnp.where(lax.broadcasted_iota(jnp.int32, gt.shape, 0) >= 2 * ML_HEADS, _log_sigmoid(gt), gt)
    gt = gt * LOG2E
    row = lax.broadcasted_iota(jnp.int32, (n_gate, lc), 0)
    ii = lax.broadcasted_iota(jnp.int32, (lc, lc), 0)
    jj = lax.broadcasted_iota(jnp.int32, (lc, lc), 1)
    incl_before = (ii <= jj).astype(BF16)
    incl_after = (ii >= jj).astype(BF16)
    parts = []
    for c in range(tm // lc):
        x = gt[:, c * lc:(c + 1) * lc]
        hi = x.astype(BF16)
        r1 = x - hi.astype(F32)
        mid = r1.astype(BF16)
        lo = (r1 - mid.astype(F32)).astype(BF16)
        split = jnp.concatenate([hi, mid, lo], axis=0)

        def cumsum(tri):
            res = jnp.dot(split, tri, preferred_element_type=F32)
            return res[0:n_gate] + res[n_gate:2 * n_gate] + res[2 * n_gate:3 * n_gate]

        parts.append(jnp.where(row < 2 * ML_HEADS, x,
                               jnp.where(row < 3 * ML_HEADS, cumsum(incl_before), cumsum(incl_after))))
    gfin = jnp.concatenate(parts, axis=1)
    gt_ref[...] = gfin
    pad = jnp.zeros((GATE_LANES - n_gate, tm), F32)
    g_ref[...] = jnp.concatenate([gfin, pad], axis=0).T

    def col_tile(t):
        return jnp.dot(u_ref[...], w_ref[:, t * d:(t + 1) * d], preferred_element_type=F32)

    xr = col_tile(0)
    pos = jnp.bitwise_and(lax.broadcasted_iota(jnp.int32, (tm, 1), 0), conv_len - 1)
    cw = cw_ref[...]
    of_ref[0] = (cw[0:1] * jnp.where(pos >= 2, pltpu.roll(xr, 2, 0), 0.0)
                 + cw[1:2] * jnp.where(pos >= 1, pltpu.roll(xr, 1, 0), 0.0)
                 + cw[2:3] * xr
                 + cw[3:4] * jnp.where(pos <= conv_len - 2, pltpu.roll(xr, tm - 1, 0), 0.0)
                 + cb_ref[...])
    of_ref[1] = jax.nn.gelu(col_tile(1))
    of_ref[2] = jax.nn.sigmoid(col_tile(5))
    of_ref[3] = col_tile(6)
    of_ref[4] = col_tile(7)
    for slab, t in enumerate((2, 3, 4)):
        ob_ref[slab] = col_tile(t).astype(BF16)


def _inproj_kernel(x0_ref, xn_ref, mod_ref, w_ref, wg_ref, bg_ref, cw_ref, cb_ref,
                   of_ref, ob_ref, g_ref, gt_ref, ua_ref, ub_ref, *, d, tm, conv_len, row_base, tiles_per_row):
    i = pl.program_id(0)
    last = pl.num_programs(0) - 1

    def ln_mod(x_blk, tile):
        r = _mod_row(tile, row_base, tiles_per_row)
        shift = mod_ref[pl.ds(r, 1), 0:d]
        scale = mod_ref[pl.ds(r, 1), d:2 * d]
        return (_ln(x_blk[...]) * (1.0 + scale) + shift).astype(BF16)

    @pl.when(i == 0)
    def _():
        ua_ref[...] = ln_mod(x0_ref, i)

    def step(u_ref, u_next_ref):
        u_next_ref[...] = ln_mod(xn_ref, jnp.minimum(i + 1, last))
        _inproj_tile(u_ref, w_ref, wg_ref, bg_ref, cw_ref, cb_ref, of_ref, ob_ref, g_ref, gt_ref,
                     d=d, tm=tm, conv_len=conv_len)

    parity = jnp.bitwise_and(i, 1)
    pl.when(parity == 0)(lambda: step(ua_ref, ub_ref))
    pl.when(parity == 1)(lambda: step(ub_ref, ua_ref))


def _tiles_per_row(rows_per_mod, tm):
    if rows_per_mod is None:
        return None
    assert rows_per_mod % tm == 0
    return rows_per_mod // tm


def _in_proj(x, mod, w_main, w_gate, b_gate, conv_w, conv_b, *, conv_len, row_base, rows_per_mod, tm):
    n, d = x.shape
    tiles_per_row = _tiles_per_row(rows_per_mod, tm)
    assert w_main.shape[1] >= 8 * d and conv_w.shape == (CONV_W, d)
    assert tm % conv_len == 0 and (conv_len & (conv_len - 1)) == 0
    n_tiles = n // tm
    kern = functools.partial(_inproj_kernel, d=d, tm=tm, conv_len=conv_len, row_base=row_base,
                             tiles_per_row=tiles_per_row)
    return pl.pallas_call(
        kern,
        grid=(n_tiles,),
        in_specs=[pl.BlockSpec((tm, d), lambda i: (0, 0)),
                  pl.BlockSpec((tm, d), lambda i: (jnp.minimum(i + 1, n_tiles - 1), 0)),
                  pl.BlockSpec(mod.shape, lambda i: (0, 0)),
                  pl.BlockSpec(w_main.shape, lambda i: (0, 0), pipeline_mode=pl.Buffered(1)),
                  pl.BlockSpec((d, GATE_LANES), lambda i: (0, 0)),
                  pl.BlockSpec((4 * ML_HEADS, 1), lambda i: (0, 0)),
                  pl.BlockSpec((CONV_W, d), lambda i: (0, 0)),
                  pl.BlockSpec((1, d), lambda i: (0, 0))],
        out_specs=[pl.BlockSpec((5, tm, d), lambda i: (0, i, 0)),
                   pl.BlockSpec((3, tm, d), lambda i: (0, i, 0)),
                   pl.BlockSpec((tm, GATE_LANES), lambda i: (i, 0)),
                   pl.BlockSpec((4 * ML_HEADS, tm), lambda i: (0, i))],
        out_shape=[jax.ShapeDtypeStruct((5, n, d), F32),
                   jax.ShapeDtypeStruct((3, n, d), BF16),
                   jax.ShapeDtypeStruct((n, GATE_LANES), F32),
                   jax.ShapeDtypeStruct((4 * ML_HEADS, n), F32)],
        scratch_shapes=[pltpu.VMEM((tm, d), BF16), pltpu.VMEM((tm, d), BF16)],
        compiler_params=pltpu.CompilerParams(dimension_semantics=("arbitrary",),
                                             vmem_limit_bytes=IN_PROJ_VMEM_LIMIT),
        name="in_proj",
    )(x, x, mod, w_main, w_gate, b_gate, conv_w, conv_b)


def _rglru_kernel(xc_ref, gz_ref, w_ref, b_ref, lam_ref, h0_ref, *rest, n_vseq, n_seg, want_final):
    if want_final:
        out_ref, hfin_ref = rest[:2]
        rest = rest[2:]
    else:
        out_ref = rest[0]
        rest = rest[1:]
    a_f, b_f, a_b, b_b = rest[:4]
    rest = rest[4:]
    seg = SCAN_SEG
    pitch = SCAN_PITCH
    ct = xc_ref.shape[1]
    ch = seg
    la = (0.5 * RG_C * LOG2E) * _log_sigmoid(lam_ref[0])
    la_f, la_b = la[:, :ct], la[:, ct:]
    w = w_ref[0]
    bias = b_ref[0]

    def for_chunk_groups(fn):
        def trip(k, carry):
            base = pl.multiple_of(k * (SCAN_GROUP * pitch), SUBLANES)
            for j in range(SCAN_GROUP):
                fn(k * SCAN_GROUP + j, base + j * pitch)
            return carry

        lax.fori_loop(0, n_vseq // SCAN_GROUP, trip, 0)

    def gate_chunk(ci, srow):
        r0 = pl.multiple_of(ci * ch, ch)
        xc = xc_ref[pl.ds(r0, ch), :]
        gates = jnp.dot(xc.astype(BF16), w, preferred_element_type=F32) + bias

        def a_and_b(r_half, i_half, la_half):
            a = jnp.exp2(la_half + la_half * jnp.tanh(r_half))
            y = jnp.clip(1.0 - a * a, 0.0, 1.0)
            gain = jnp.where(y > 0.0, y * lax.rsqrt(y), 0.0)
            half_gx = (0.5 * gain) * xc
            return a, half_gx + half_gx * jnp.tanh(i_half)

        af, bf = a_and_b(gates[:, 0:ct], gates[:, ct:2 * ct], la_f)
        ab, bb = a_and_b(gates[:, 2 * ct:3 * ct], gates[:, 3 * ct:4 * ct], la_b)
        s0 = pl.ds(srow, ch)
        a_f[s0, :] = af
        b_f[s0, :] = bf
        a_b[s0, :] = ab
        b_b[s0, :] = bb

    for_chunk_groups(gate_chunk)

    def slab_idx(t):
        return pl.ds(t, n_vseq, stride=pitch)

    def slab(ref, t):
        return ref[slab_idx(t), :]

    if n_seg == 1:
        init_f = h0_ref[0]
        init_b = h0_ref[1]
    else:
        e_f, p_f, e_b, p_b, cin_f, cin_b = rest
        n_real = n_vseq // n_seg
        zero = jnp.zeros((n_vseq, ct), F32)
        one = jnp.ones((n_vseq, ct), F32)

        def local_step(t, carry):
            ef, pf, eb, pb = carry
            tb = seg - 1 - t
            a = slab(a_f, t)
            ef = a * ef + slab(b_f, t)
            pf = a * pf
            a2 = slab(a_b, tb)
            eb = a2 * eb + slab(b_b, tb)
            pb = a2 * pb
            return ef, pf, eb, pb

        ef, pf, eb, pb = lax.fori_loop(0, seg, local_step, (zero, one, zero, one), unroll=8)
        e_f[...] = ef
        p_f[...] = pf
        e_b[...] = eb
        p_b[...] = pb
        carry = h0_ref[0]
        for s in range(n_seg):
            idx = pl.ds(s, n_real, stride=n_seg)
            cin_f[idx, :] = carry
            carry = e_f[idx, :] + p_f[idx, :] * carry
        carry = h0_ref[1]
        for s in reversed(range(n_seg)):
            idx = pl.ds(s, n_real, stride=n_seg)
            cin_b[idx, :] = carry
            carry = e_b[idx, :] + p_b[idx, :] * carry
        init_f = cin_f[...]
        init_b = cin_b[...]

    def scan_step(t, carry):
        hf, hb = carry
        tb = seg - 1 - t
        hf = slab(a_f, t) * hf + slab(b_f, t)
        b_f[slab_idx(t), :] = hf
        hb = slab(a_b, tb) * hb + slab(b_b, tb)
        b_b[slab_idx(tb), :] = hb
        return hf, hb

    hf_last, hb_last = lax.fori_loop(0, seg, scan_step, (init_f, init_b), unroll=8)
    if want_final:
        hfin_ref[0] = hf_last
        hfin_ref[1] = hb_last

    def out_chunk(ci, srow):
        r0 = pl.multiple_of(ci * ch, ch)
        h = b_f[pl.ds(srow, ch), :] + b_b[pl.ds(srow, ch), :]
        out_ref[pl.ds(r0, ch), :] = (h * gz_ref[pl.ds(r0, ch), :]).astype(BF16)

    for_chunk_groups(out_chunk)


def _rglru(of32, w_cat, b_cat, lam_cat, h0, *, n_seq, seq_len, want_final):
    n, d_rnn = of32.shape[1:]
    ct = RG_CT
    n_ct = d_rnn // ct
    n_seg = seq_len // SCAN_SEG
    n_vseq = n_seq * n_seg
    assert n_vseq * SCAN_SEG == n and n_vseq % SCAN_GROUP == 0
    kern = functools.partial(_rglru_kernel, n_vseq=n_vseq, n_seg=n_seg, want_final=want_final)
    out_specs = [pl.BlockSpec((n, ct), lambda c: (0, c))]
    out_shape = [jax.ShapeDtypeStruct((n, d_rnn), BF16)]
    if want_final:
        out_specs.append(pl.BlockSpec((2, n_seq, ct), lambda c: (0, 0, c)))
        out_shape.append(jax.ShapeDtypeStruct((2, n_seq, d_rnn), F32))
    scratch = [pltpu.VMEM((n_vseq * SCAN_PITCH, ct), F32) for _ in range(4)]
    if n_seg > 1:
        scratch += [pltpu.VMEM((n_vseq, ct), F32) for _ in range(6)]
    return pl.pallas_call(
        kern,
        grid=(n_ct,),
        in_specs=[pl.BlockSpec((None, n, ct), lambda c: (0, 0, c)),
                  pl.BlockSpec((None, n, ct), lambda c: (1, 0, c)),
                  pl.BlockSpec((1, ct, 4 * ct), lambda c: (c, 0, 0)),
                  pl.BlockSpec((1, 1, 4 * ct), lambda c: (c, 0, 0)),
                  pl.BlockSpec((1, 1, 2 * ct), lambda c: (c, 0, 0)),
                  pl.BlockSpec((2, n_seq, ct), lambda c: (0, 0, c))],
        out_specs=out_specs,
        out_shape=out_shape,
        scratch_shapes=scratch,
        compiler_params=_cparams(1),
        name="rglru_ctx" if want_final else "rglru_lat",
    )(of32, of32, w_cat, b_cat, lam_cat, h0)


def _causal_bias(mask_ref):
    lc = mask_ref.shape[1]
    ii = lax.broadcasted_iota(jnp.int32, (lc, lc), 0)
    jj = lax.broadcasted_iota(jnp.int32, (lc, lc), 1)
    mask_ref[0] = jnp.where(jj <= ii, 0.0, -jnp.inf)
    mask_ref[1] = jnp.where(jj >= ii, 0.0, -jnp.inf)


def _with_ones(v):
    return jnp.concatenate([v, jnp.ones((v.shape[0], GATE_LANES), v.dtype)], axis=1)


def _rep(col):
    return jnp.broadcast_to(col, (col.shape[0], GATE_LANES))


def _wide(rep, width):
    return jnp.concatenate([rep] * (width // GATE_LANES), axis=1)


def _mlstm_unit(direction, head, q, k, v, v1, gc, gt_ref, r0, mask_ref, state):
    lc, hd = q.shape
    col_i = direction * ML_HEADS + head
    col_c = 2 * ML_HEADS + col_i
    lane = lax.broadcasted_iota(jnp.int32, (1, GATE_LANES), 1)
    ig_c = _rep(jnp.sum(jnp.where(lane == col_i, gc, 0.0), -1, keepdims=True))
    cum_c = _rep(jnp.sum(jnp.where(lane == col_c, gc, 0.0), -1, keepdims=True))
    ig_row = gt_ref[pl.ds(col_i, 1), pl.ds(r0, lc)]
    cum_row = gt_ref[pl.ds(col_c, 1), pl.ds(r0, lc)]
    dmat = (_wide(cum_c, lc) + (ig_row - cum_row)) + mask_ref[direction]
    if state is None:
        m_prev = 0.0
    else:
        s_prev, m_prev = state
    inter = cum_c + m_prev
    m_row = jnp.maximum(_rep(jnp.max(dmat, -1, keepdims=True)), inter)
    wts = jnp.exp2(dmat - _wide(m_row, lc))
    s = lax.dot_general(q, k, (((1,), (1,)), ((), ())), preferred_element_type=F32) * wts
    if state is None:
        num = jnp.dot(s.astype(BF16), v, preferred_element_type=F32)
        den = _rep(jnp.sum(s, -1, keepdims=True))
        h = num * _wide(1.0 / jnp.maximum(jnp.abs(den), jnp.exp2(-m_row)), hd)
    else:
        w_inter = jnp.exp2(inter - m_row)
        nd = (jnp.dot(s.astype(BF16), v1, preferred_element_type=F32)
              + _wide(w_inter, hd + GATE_LANES) * jnp.dot(q, s_prev.astype(BF16), preferred_element_type=F32))
        rden = 1.0 / jnp.maximum(jnp.abs(nd[:, hd:]), jnp.exp2(-m_row))
        h = nd[:, :hd] * _wide(rden, hd)
    btot = cum_c[lc - 1:lc] if direction == 0 else cum_c[0:1]
    wk_log = btot - cum_c + ig_c
    m_new = jnp.maximum(btot + m_prev, jnp.max(wk_log, 0, keepdims=True))
    kw = k.astype(F32) * _wide(jnp.exp2(wk_log - m_new), hd)
    t_lhs = (((0,), (0,)), ((), ()))
    if state is None:
        c_new = lax.dot_general(kw.astype(BF16), v, t_lhs, preferred_element_type=F32)
        return h, c_new, jnp.sum(kw, 0, keepdims=True), m_new
    decay = _wide(jnp.exp2(btot + m_prev - m_new), hd + GATE_LANES)
    s_new = decay * s_prev + lax.dot_general(kw.astype(BF16), v1, t_lhs, preferred_element_type=F32)
    return h, s_new, None, m_new


HEAD_OUT_ROWS = 1024


def _head_out_all(h_acc, og_ref, gn_ref, out_ref):
    gn = gn_ref[...]
    rows = h_acc.shape[0]
    step = min(rows, HEAD_OUT_ROWS)
    for r0 in range(0, rows, step):
        rs = slice(r0, r0 + step)
        out_ref[rs, :] = (og_ref[rs, :] * (_ln(h_acc[rs, :]) * gn)).astype(BF16)


def _mlstm_ctx_kernel(q_ref, k_ref, v_ref, o_ref, g_ref, gt_ref, gn_ref,
                      out_ref, cfin_ref, nfin_ref, mfin_ref, mask_ref, h_acc, *, n_sub, scale):
    head = pl.program_id(1)
    lc = MLSTM_CHUNK
    _causal_bias(mask_ref)

    def body(s, carry):
        r0 = pl.multiple_of(s * lc, lc)
        rs = pl.ds(r0, lc)
        q, k, v, gc = q_ref[rs, :] * scale, k_ref[rs, :], v_ref[rs, :], g_ref[rs, :]
        hsum = None
        for direction in (0, 1):
            h, c_new, n_new, m_new = _mlstm_unit(direction, head, q, k, v, None, gc, gt_ref, r0, mask_ref, None)
            cfin_ref[s, direction, 0] = c_new
            nfin_ref[s, direction, 0] = n_new
            mfin_ref[s, direction, 0] = m_new * LN2
            hsum = h if hsum is None else hsum + h
        h_acc[rs, :] = hsum
        return carry

    lax.fori_loop(0, n_sub, body, 0, unroll=2)
    _head_out_all(h_acc, o_ref, gn_ref, out_ref)


def _mlstm_lat_kernel(q_ref, k_ref, v_ref, o_ref, g_ref, gt_ref, gn_ref, c0_ref, n0_ref, m0_ref,
                      out_ref, s_st, m_st, h_acc, mask_ref, *, n_chunks, scale):
    head = pl.program_id(1)
    lc = MLSTM_CHUNK
    hd = q_ref.shape[1]
    _causal_bias(mask_ref)
    for direction in (0, 1):
        s_st[direction, :, 0:hd] = c0_ref[0, direction, 0]
        s_st[direction, :, hd:] = n0_ref[0, direction, 0]
        m_st[direction] = m0_ref[0, direction, 0] * LOG2E
    h_acc[...] = jnp.zeros(h_acc.shape, F32)

    def body(c, carry):
        for direction in (0, 1):
            cc = c if direction == 0 else n_chunks - 1 - c
            r0 = pl.multiple_of(cc * lc, lc)
            rs = pl.ds(r0, lc)
            state = (s_st[direction], m_st[direction])
            v = v_ref[rs, :]
            h, s_new, _, m_new = _mlstm_unit(direction, head, q_ref[rs, :] * scale, k_ref[rs, :], v,
                                             _with_ones(v), g_ref[rs, :], gt_ref, r0, mask_ref, state)
            s_st[direction] = s_new
            m_st[direction] = m_new
            h_acc[rs, :] += h
        return carry

    lax.fori_loop(0, n_chunks, body, 0, unroll=2)
    _head_out_all(h_acc, o_ref, gn_ref, out_ref)


def _mlstm_in_specs(rows, hd):
    slab = lambda idx: pl.BlockSpec((None, rows, hd), lambda s, h: (idx, s, h))
    return [slab(0), slab(1), slab(2), slab(2),
            pl.BlockSpec((rows, GATE_LANES), lambda s, h: (s, 0)),
            pl.BlockSpec((4 * ML_HEADS, rows), lambda s, h: (0, s)),
            pl.BlockSpec((1, hd), lambda s, h: (0, h))]


def _mlstm_ctx(obf, of32, g, gt, gn_g, *, n_seq):
    n = obf.shape[1]
    ml_dim = gn_g.shape[1]
    hd = ml_dim // ML_HEADS
    lc = MLSTM_CHUNK
    assert n == n_seq * lc
    n_sub = 8
    assert n_seq % n_sub == 0
    rows = n_sub * lc
    kern = functools.partial(_mlstm_ctx_kernel, n_sub=n_sub, scale=hd ** -0.5)
    return pl.pallas_call(
        kern,
        grid=(n_seq // n_sub, ML_HEADS),
        in_specs=_mlstm_in_specs(rows, hd),
        out_specs=[pl.BlockSpec((rows, hd), lambda s, h: (s, h)),
                   pl.BlockSpec((n_sub, 2, 1, hd, hd), lambda s, h: (s, 0, h, 0, 0)),
                   pl.BlockSpec((n_sub, 2, 1, 1, hd), lambda s, h: (s, 0, h, 0, 0)),
                   pl.BlockSpec((n_sub, 2, 1, 1, GATE_LANES), lambda s, h: (s, 0, h, 0, 0))],
        out_shape=[jax.ShapeDtypeStruct((n, ml_dim), BF16),
                   jax.ShapeDtypeStruct((n_seq, 2, ML_HEADS, hd, hd), F32),
                   jax.ShapeDtypeStruct((n_seq, 2, ML_HEADS, 1, hd), F32),
                   jax.ShapeDtypeStruct((n_seq, 2, ML_HEADS, 1, GATE_LANES), F32)],
        scratch_shapes=[pltpu.VMEM((2, lc, lc), F32), pltpu.VMEM((rows, hd), F32)],
        compiler_params=_cparams(2),
        name="mlstm_ctx",
    )(obf, obf, obf, of32, g, gt, gn_g)


def _mlstm_lat(obf, of32, g, gt, gn_g, c0, n0, m0, *, n_seq):
    n = obf.shape[1]
    ml_dim = gn_g.shape[1]
    hd = ml_dim // ML_HEADS
    lc = MLSTM_CHUNK
    rows = n // n_seq
    n_chunks = rows // lc
    kern = functools.partial(_mlstm_lat_kernel, n_chunks=n_chunks, scale=hd ** -0.5)
    return pl.pallas_call(
        kern,
        grid=(n_seq, ML_HEADS),
        in_specs=_mlstm_in_specs(rows, hd) + [
                  pl.BlockSpec((1, 2, 1, hd, hd), lambda s, h: (s, 0, h, 0, 0)),
                  pl.BlockSpec((1, 2, 1, hd, GATE_LANES), lambda s, h: (s, 0, h, 0, 0)),
                  pl.BlockSpec((1, 2, 1, 1, GATE_LANES), lambda s, h: (s, 0, h, 0, 0))],
        out_specs=pl.BlockSpec((rows, hd), lambda s, h: (s, h)),
        out_shape=jax.ShapeDtypeStruct((n, ml_dim), BF16),
        scratch_shapes=[pltpu.VMEM((2, hd, hd + GATE_LANES), F32),
                        pltpu.VMEM((2, 1, GATE_LANES), F32),
                        pltpu.VMEM((rows, hd), F32),
                        pltpu.VMEM((2, lc, lc), F32)],
        compiler_params=_cparams(2),
        name="mlstm_lat",
    )(obf, obf, obf, of32, g, gt, gn_g, c0, n0, m0)


def _merge_kernel(hrg_ref, hml_ref, gma_ref, gmb_ref, x_ref, mod_ref, wrg_ref, wml_ref, wout_ref,
                  bm_ref, lng_ref, lnb_ref, o_ref, *, d, alpha, row_base, tiles_per_row):
    r = _mod_row(pl.program_id(0), row_base, tiles_per_row)
    gate1 = mod_ref[pl.ds(r, 1), 2 * d:3 * d]
    rows_per_group = x_ref.shape[0] // MERGE_ROW_GROUPS
    for grp in range(MERGE_ROW_GROUPS):
        rows = slice(grp * rows_per_group, (grp + 1) * rows_per_group)
        y_rg = jnp.dot(hrg_ref[rows, :], wrg_ref[...], preferred_element_type=F32)
        y_ml = jnp.dot(hml_ref[rows, :], wml_ref[...], preferred_element_type=F32)
        g_rg = jax.nn.sigmoid(gma_ref[rows, :] + bm_ref[:, 0:d])
        g_ml = jax.nn.sigmoid(gmb_ref[rows, :] + bm_ref[:, d:2 * d])
        merged = (g_rg * y_rg + g_ml * y_ml).astype(BF16)
        mix = jnp.dot(merged, wout_ref[...], preferred_element_type=F32)
        o_ref[rows, :] = _ln(alpha * x_ref[rows, :] + gate1 * mix) * lng_ref[...] + lnb_ref[...]


def _merge(hrg, hml, of32, x, mod, w_rg, w_ml, w_out, b_merge, ln_g, ln_b, *, alpha, row_base,
           rows_per_mod, tm):
    n, d = x.shape
    tiles_per_row = _tiles_per_row(rows_per_mod, tm)
    kern = functools.partial(_merge_kernel, d=d, alpha=alpha, row_base=row_base, tiles_per_row=tiles_per_row)
    full = lambda shape: pl.BlockSpec(shape, lambda i: (0,) * len(shape))
    resident = lambda shape: pl.BlockSpec(shape, lambda i: (0, 0), pipeline_mode=pl.Buffered(1))
    return pl.pallas_call(
        kern,
        grid=(n // tm,),
        in_specs=[pl.BlockSpec((tm, d), lambda i: (i, 0)),
                  pl.BlockSpec((tm, d), lambda i: (i, 0)),
                  pl.BlockSpec((None, tm, d), lambda i: (3, i, 0)),
                  pl.BlockSpec((None, tm, d), lambda i: (4, i, 0)),
                  pl.BlockSpec((tm, d), lambda i: (i, 0)),
                  full(mod.shape), resident((d, d)), resident((d, d)), resident((d, d)),
                  full((1, 2 * d)), full((1, d)), full((1, d))],
        out_specs=pl.BlockSpec((tm, d), lambda i: (i, 0)),
        out_shape=jax.ShapeDtypeStruct((n, d), F32),
        compiler_params=_cparams(1),
        name="merge",
    )(hrg, hml, of32, of32, x, mod, w_rg, w_ml, w_out, b_merge, ln_g, ln_b)


def _mlp_kernel(x_ref, mod_ref, wfc_ref, bfc_ref, wpj_ref, bpj_ref, lng_ref, lnb_ref, o_ref,
                u_ref, *, d, tf, alpha, row_base, tiles_per_row):
    r = _mod_row(pl.program_id(0), row_base, tiles_per_row)
    shift = mod_ref[pl.ds(r, 1), 3 * d:4 * d]
    scale = mod_ref[pl.ds(r, 1), 4 * d:5 * d]
    gate2 = mod_ref[pl.ds(r, 1), 5 * d:6 * d]
    rows_per_group = x_ref.shape[0] // MLP_ROW_GROUPS
    for grp in range(MLP_ROW_GROUPS):
        rows = slice(grp * rows_per_group, (grp + 1) * rows_per_group)
        u_ref[rows, :] = (_ln(x_ref[rows, :]) * (1.0 + scale) + shift).astype(BF16)
        acc = None
        for f in range(wfc_ref.shape[1] // tf):
            cols = slice(f * tf, (f + 1) * tf)
            hid = jnp.dot(u_ref[rows, :], wfc_ref[:, cols], preferred_element_type=F32) + bfc_ref[:, cols]
            hid = jnp.square(jnp.maximum(hid, 0.0)).astype(BF16)
            part = jnp.dot(hid, wpj_ref[cols, :], preferred_element_type=F32)
            acc = part if acc is None else acc + part
        y = alpha * x_ref[rows, :] + gate2 * (acc + bpj_ref[...])
        o_ref[rows, :] = _ln(y) * lng_ref[...] + lnb_ref[...]


def _mlp(x, mod, w_fc, b_fc, w_proj, b_proj, ln_g, ln_b, *, alpha, row_base, rows_per_mod, tm):
    n, d = x.shape
    tiles_per_row = _tiles_per_row(rows_per_mod, tm)
    d_ff = w_fc.shape[1]
    tf = 1024
    assert d_ff % tf == 0
    kern = functools.partial(_mlp_kernel, d=d, tf=tf, alpha=alpha, row_base=row_base,
                             tiles_per_row=tiles_per_row)
    resident = lambda shape: pl.BlockSpec(shape, lambda i: (0, 0), pipeline_mode=pl.Buffered(1))
    return pl.pallas_call(
        kern,
        grid=(n // tm,),
        in_specs=[pl.BlockSpec((tm, d), lambda i: (i, 0)),
                  pl.BlockSpec(mod.shape, lambda i: (0, 0)),
                  resident((d, d_ff)),
                  pl.BlockSpec((1, d_ff), lambda i: (0, 0)),
                  resident((d_ff, d)),
                  pl.BlockSpec((1, d), lambda i: (0, 0)),
                  pl.BlockSpec((1, d), lambda i: (0, 0)),
                  pl.BlockSpec((1, d), lambda i: (0, 0))],
        out_specs=pl.BlockSpec((tm, d), lambda i: (i, 0)),
        out_shape=jax.ShapeDtypeStruct((n, d), F32),
        scratch_shapes=[pltpu.VMEM((tm, d), BF16)],
        compiler_params=_cparams(1),
        name="mlp",
    )(x, mod, w_fc, b_fc, w_proj, b_proj, ln_g, ln_b)


def _rg_gate_weights(wa, ba, wx, bx, lam):
    ct = RG_CT
    d_rnn = ba.shape[-1]
    n_ct = d_rnn // ct
    per = ct // RG_BW

    def tile_blockdiag(w):
        wt = w.reshape(n_ct, per, RG_BW, RG_BW)
        eye = jnp.eye(per, dtype=w.dtype)
        return jnp.einsum('cpkj,pq->cpkqj', wt, eye).reshape(n_ct, ct, ct)

    w_cat = (0.5 * jnp.concatenate([tile_blockdiag(wa[0]), tile_blockdiag(wx[0]),
                                    tile_blockdiag(wa[1]), tile_blockdiag(wx[1])], axis=-1)).astype(BF16)
    tiles = lambda b: b.reshape(n_ct, 1, ct)
    b_cat = 0.5 * jnp.concatenate([tiles(ba[0]), tiles(bx[0]), tiles(ba[1]), tiles(bx[1])], axis=-1)
    lam_cat = jnp.concatenate([tiles(lam[0]), tiles(lam[1])], axis=-1)
    return w_cat, b_cat, lam_cat


def kernel(x_prompt, x_sample, state_rglru_h, state_mlstm_C, state_mlstm_n, state_mlstm_m, c, c_ctx, w_ada, b_ada, w_in, rg_conv_w, rg_conv_b, rg_wa, rg_ba, rg_wx, rg_bx, rg_lambda, w_rg_proj, ml_b_igate, ml_b_fgate, ml_gn_g, w_ml_proj, b_merge, w_out, ln_g, ln_b, w_fc, b_fc, w_proj, b_proj):
    bp, seq, d = x_prompt.shape
    bl, dec_seq, _ = x_sample.shape
    depth = w_in.shape[0]
    d_rnn = rg_conv_w.shape[-1]
    ml_dim = ml_gn_g.shape[-1]
    hd = ml_dim // ML_HEADS
    alpha = (2 * depth) ** 0.25
    d_main = 2 * d_rnn + 4 * ml_dim + 2 * d
    assert seq == MLSTM_CHUNK == SCAN_SEG and dec_seq % MLSTM_CHUNK == 0
    assert d_rnn == d and ml_dim == d and w_in.shape[-1] == d_main + 4 * ML_HEADS

    tm = 1024
    tm_proj = 512
    xp = x_prompt.reshape(bp * seq, d)
    xs = x_sample.reshape(bl * dec_seq, d)
    cond = jnp.concatenate([c_ctx[None, :], c], axis=0)
    zero_h = jnp.zeros((2, bp, d_rnn), F32)
    new_h, new_c, new_n, new_m = [], [], [], []
    for l in range(depth):
        mod = _ada(cond, w_ada[l], b_ada[l])
        w_main = w_in[l].astype(BF16)
        w_gate = jnp.zeros((d, GATE_LANES), BF16).at[:, :4 * ML_HEADS].set(w_in[l][:, d_main:].astype(BF16))
        b_gate = jnp.concatenate([ml_b_igate[l].reshape(-1), ml_b_fgate[l].reshape(-1)]).reshape(-1, 1)
        w_cat, b_cat, lam_cat = _rg_gate_weights(rg_wa[l], rg_ba[l], rg_wx[l], rg_bx[l], rg_lambda[l])
        conv_b = rg_conv_b[l].reshape(1, d_rnn)
        gn_g = ml_gn_g[l].reshape(1, ml_dim)
        w_rg, w_ml, w_o = (w_rg_proj[l].astype(BF16), w_ml_proj[l].astype(BF16), w_out[l].astype(BF16))
        w_fc_b, w_pj_b = w_fc[l].astype(BF16), w_proj[l].astype(BF16)
        bm = b_merge[l].reshape(1, 2 * d)
        lng0, lnb0 = ln_g[l, 0].reshape(1, d), ln_b[l, 0].reshape(1, d)
        lng1, lnb1 = ln_g[l, 1].reshape(1, d), ln_b[l, 1].reshape(1, d)
        bfc, bpj = b_fc[l].reshape(1, -1), b_proj[l].reshape(1, d)

        def tail(x, hrg, hml, of32, row_base, rows_per_mod):
            x1 = _merge(hrg, hml, of32, x, mod, w_rg, w_ml, w_o, bm, lng0, lnb0, alpha=alpha,
                        row_base=row_base, rows_per_mod=rows_per_mod, tm=tm)
            return _mlp(x1, mod, w_fc_b, bfc, w_pj_b, bpj, lng1, lnb1, alpha=alpha,
                        row_base=row_base, rows_per_mod=rows_per_mod, tm=tm)

        of32, obf, g, gt = _in_proj(xp, mod, w_main, w_gate, b_gate, rg_conv_w[l], conv_b, conv_len=seq,
                                    row_base=0, rows_per_mod=None, tm=tm_proj)
        hrg, h_fin = _rglru(of32, w_cat, b_cat, lam_cat, zero_h, n_seq=bp, seq_len=seq, want_final=True)
        hml, c_fin, n_fin, m_fin = _mlstm_ctx(obf, of32, g, gt, gn_g, n_seq=bp)
        xp = tail(xp, hrg, hml, of32, 0, None)
        new_h.append(jnp.transpose(h_fin, (1, 0, 2)))
        new_c.append(c_fin)
        new_n.append(n_fin.reshape(bp, 2, ML_HEADS, hd))
        new_m.append(m_fin[..., 0, 0])

        of32, obf, g, gt = _in_proj(xs, mod, w_main, w_gate, b_gate, rg_conv_w[l], conv_b, conv_len=GRID_W,
                                    row_base=1, rows_per_mod=dec_seq, tm=tm_proj)
        h0 = jnp.transpose(state_rglru_h[:, l], (1, 0, 2))
        (hrg,) = _rglru(of32, w_cat, b_cat, lam_cat, h0, n_seq=bl, seq_len=dec_seq, want_final=False)
        c0 = state_mlstm_C[:, l]
        n0 = jnp.broadcast_to(state_mlstm_n[:, l][..., None], (bl, 2, ML_HEADS, hd, GATE_LANES))
        m0 = jnp.broadcast_to(state_mlstm_m[:, l].reshape(bl, 2, ML_HEADS, 1, 1),
                              (bl, 2, ML_HEADS, 1, GATE_LANES))
        hml = _mlstm_lat(obf, of32, g, gt, gn_g, c0, n0, m0, n_seq=bl)
        xs = tail(xs, hrg, hml, of32, 1, dec_seq)

    def stack(parts):
        return parts[0][:, None] if len(parts) == 1 else jnp.stack(parts, axis=1)

    return (xp.reshape(bp, seq, d), xs.reshape(bl, dec_seq, d),
            stack(new_h), stack(new_c), stack(new_n), stack(new_m))
```

```python
import functools

import jax
import jax.numpy as jnp
from jax import lax
from jax.experimental import pallas as pl
from jax.experimental.pallas import tpu as pltpu

F32 = jnp.float32
BF16 = jnp.bfloat16

LN_EPS = 1e-5
LOG2E = 1.4426950408889634
LN2 = 0.6931471805599453
RG_C = 8.0
RG_BW = 64
CONV_W = 4
GRID_W = 64
ML_HEADS = 4
MLSTM_CHUNK = 256
SCAN_SEG = 256
SUBLANES = 8
SCAN_PITCH = SCAN_SEG + 4
SCAN_GROUP = 8
RG_CT = 128
GATE_LANES = 128
VMEM_LIMIT = 52 * 1024 * 1024
MERGE_ROW_GROUPS = 4
MLP_ROW_GROUPS = 2
IN_PROJ_VMEM_LIMIT = 58 * 1024 * 1024


def _cparams(n_axes):
    return pltpu.CompilerParams(
        dimension_semantics=("arbitrary",) * n_axes, vmem_limit_bytes=VMEM_LIMIT)


def _log_sigmoid(x):
    return jnp.minimum(x, 0.0) - jnp.log1p(jnp.exp(-jnp.abs(x)))


def _ln(x):
    mu = jnp.mean(x, -1, keepdims=True)
    xc = x - mu
    var = jnp.mean(xc * xc, -1, keepdims=True)
    return xc * lax.rsqrt(var + LN_EPS)


def _mod_row(i, row_base, tiles_per_row):
    if tiles_per_row is None:
        return row_base
    return row_base + lax.div(i, jnp.int32(tiles_per_row))


def _ada_kernel(ct_ref, w_ref, b_ref, o_ref, *, n_rows):
    w = w_ref[...]
    ct = ct_ref[...]
    s = ct * jax.nn.sigmoid(ct)
    o_ref[...] = jnp.zeros(o_ref.shape, F32)
    for r in range(n_rows):
        o_ref[r:r + 1, :] = jnp.sum(w * s[:, r:r + 1], axis=0, keepdims=True) + b_ref[...]


def _ada(cond, w_ada, b_ada):
    n_rows, d = cond.shape
    assert n_rows <= 8
    ct = jnp.zeros((d, 8), F32).at[:, :n_rows].set(cond.T)
    n_out = w_ada.shape[1]
    tn = 2048
    assert n_out % tn == 0
    return pl.pallas_call(
        functools.partial(_ada_kernel, n_rows=n_rows),
        grid=(n_out // tn,),
        in_specs=[pl.BlockSpec((d, 8), lambda j: (0, 0)),
                  pl.BlockSpec((d, tn), lambda j: (0, j)),
                  pl.BlockSpec((1, tn), lambda j: (0, j))],
        out_specs=pl.BlockSpec((8, tn), lambda j: (0, j)),
        out_shape=jax.ShapeDtypeStruct((8, n_out), F32),
        compiler_params=_cparams(1),
        name="ada_mod",
    )(ct, w_ada, b_ada.reshape(1, n_out))


def _inproj_tile(u_ref, w_ref, wg_ref, bg_ref, cw_ref, cb_ref, of_ref, ob_ref, g_ref, gt_ref, *, d, tm,
                 conv_len):
    lc = MLSTM_CHUNK
    n_gate = 4 * ML_HEADS
    g = jnp.dot(u_ref[...], wg_ref[...], preferred_element_type=F32)
    gt = g.T[0:n_gate, :] + bg_ref[...]
    gt = jnp.where(lax.broadcasted_iota(jnp.int32, gt.shape, 0) >= 2 * ML_HEADS, _log_sigmoid(gt), gt)
    gt = gt * LOG2E
    row = lax.broadcasted_iota(jnp.int32, (n_gate, lc), 0)
    ii = lax.broadcasted_iota(jnp.int32, (lc, lc), 0)
    jj = lax.broadcasted_iota(jnp.int32, (lc, lc), 1)
    incl_before = (ii <= jj).astype(BF16)
    incl_after = (ii >= jj).astype(BF16)
    parts = []
    for c in range(tm // lc):
        x = gt[:, c * lc:(c + 1) * lc]
        hi = x.astype(BF16)
        r1 = x - hi.astype(F32)
        mid = r1.astype(BF16)
        lo = (r1 - mid.astype(F32)).astype(BF16)
        split = jnp.concatenate([hi, mid, lo], axis=0)

        def cumsum(tri):
            res = jnp.dot(split, tri, preferred_element_type=F32)
            return res[0:n_gate] + res[n_gate:2 * n_gate] + res[2 * n_gate:3 * n_gate]

        parts.append(jnp.where(row < 2 * ML_HEADS, x,
                               jnp.where(row < 3 * ML_HEADS, cumsum(incl_before), cumsum(incl_after))))
    gfin = jnp.concatenate(parts, axis=1)
    gt_ref[...] = gfin
    pad = jnp.zeros((GATE_LANES - n_gate, tm), F32)
    g_ref[...] = jnp.concatenate([gfin, pad], axis=0).T

    def col_tile(t):
        return jnp.dot(u_ref[...], w_ref[:, t * d:(t + 1) * d], preferred_element_type=F32)

    xr = col_tile(0)
    pos = jnp.bitwise_and(lax.broadcasted_iota(jnp.int32, (tm, 1), 0), conv_len - 1)
    cw = cw_ref[...]
    of_ref[0] = (cw[0:1] * jnp.where(pos >= 2, pltpu.roll(xr, 2, 0), 0.0)
                 + cw[1:2] * jnp.where(pos >= 1, pltpu.roll(xr, 1, 0), 0.0)
                 + cw[2:3] * xr
                 + cw[3:4] * jnp.where(pos <= conv_len - 2, pltpu.roll(xr, tm - 1, 0), 0.0)
                 + cb_ref[...])
    of_ref[1] = jax.nn.gelu(col_tile(1))
    of_ref[2] = jax.nn.sigmoid(col_tile(5))
    of_ref[3] = col_tile(6)
    of_ref[4] = col_tile(7)
    for slab, t in enumerate((2, 3, 4)):
        ob_ref[slab] = col_tile(t).astype(BF16)


def _inproj_kernel(x0_ref, xn_ref, mod_ref, w_ref, wg_ref, bg_ref, cw_ref, cb_ref,
                   of_ref, ob_ref, g_ref, gt_ref, ua_ref, ub_ref, *, d, tm, conv_len, row_base, tiles_per_row):
    i = pl.program_id(0)
    last = pl.num_programs(0) - 1

    def ln_mod(x_blk, tile):
        r = _mod_row(tile, row_base, tiles_per_row)
        shift = mod_ref[pl.ds(r, 1), 0:d]
        scale = mod_ref[pl.ds(r, 1), d:2 * d]
        return (_ln(x_blk[...]) * (1.0 + scale) + shift).astype(BF16)

    @pl.when(i == 0)
    def _():
        ua_ref[...] = ln_mod(x0_ref, i)

    def step(u_ref, u_next_ref):
        u_next_ref[...] = ln_mod(xn_ref, jnp.minimum(i + 1, last))
        _inproj_tile(u_ref, w_ref, wg_ref, bg_ref, cw_ref, cb_ref, of_ref, ob_ref, g_ref, gt_ref,
                     d=d, tm=tm, conv_len=conv_len)

    parity = jnp.bitwise_and(i, 1)
    pl.when(parity == 0)(lambda: step(ua_ref, ub_ref))
    pl.when(parity == 1)(lambda: step(ub_ref, ua_ref))


def _tiles_per_row(rows_per_mod, tm):
    if rows_per_mod is None:
        return None
    assert rows_per_mod % tm == 0
    return rows_per_mod // tm


def _in_proj(x, mod, w_main, w_gate, b_gate, conv_w, conv_b, *, conv_len, row_base, rows_per_mod, tm):
    n, d = x.shape
    tiles_per_row = _tiles_per_row(rows_per_mod, tm)
    assert w_main.shape[1] >= 8 * d and conv_w.shape == (CONV_W, d)
    assert tm % conv_len == 0 and (conv_len & (conv_len - 1)) == 0
    n_tiles = n // tm
    kern = functools.partial(_inproj_kernel, d=d, tm=tm, conv_len=conv_len, row_base=row_base,
                             tiles_per_row=tiles_per_row)
    return pl.pallas_call(
        kern,
        grid=(n_tiles,),
        in_specs=[pl.BlockSpec((tm, d), lambda i: (0, 0)),
                  pl.BlockSpec((tm, d), lambda i: (jnp.minimum(i + 1, n_tiles - 1), 0)),
                  pl.BlockSpec(mod.shape, lambda i: (0, 0)),
                  pl.BlockSpec(w_main.shape, lambda i: (0, 0), pipeline_mode=pl.Buffered(1)),
                  pl.BlockSpec((d, GATE_LANES), lambda i: (0, 0)),
                  pl.BlockSpec((4 * ML_HEADS, 1), lambda i: (0, 0)),
                  pl.BlockSpec((CONV_W, d), lambda i: (0, 0)),
                  pl.BlockSpec((1, d), lambda i: (0, 0))],
        out_specs=[pl.BlockSpec((5, tm, d), lambda i: (0, i, 0)),
                   pl.BlockSpec((3, tm, d), lambda i: (0, i, 0)),
                   pl.BlockSpec((tm, GATE_LANES), lambda i: (i, 0)),
                   pl.BlockSpec((4 * ML_HEADS, tm), lambda i: (0, i))],
        out_shape=[jax.ShapeDtypeStruct((5, n, d), F32),
                   jax.ShapeDtypeStruct((3, n, d), BF16),
                   jax.ShapeDtypeStruct((n, GATE_LANES), F32),
                   jax.ShapeDtypeStruct((4 * ML_HEADS, n), F32)],
        scratch_shapes=[pltpu.VMEM((tm, d), BF16), pltpu.VMEM((tm, d), BF16)],
        compiler_params=pltpu.CompilerParams(dimension_semantics=("arbitrary",),
                                             vmem_limit_bytes=IN_PROJ_VMEM_LIMIT),
        name="in_proj",
    )(x, x, mod, w_main, w_gate, b_gate, conv_w, conv_b)


def _rglru_kernel(xc_ref, gz_ref, w_ref, b_ref, lam_ref, h0_ref, *rest, n_vseq, n_seg, want_final):
    if want_final:
        out_ref, hfin_ref = rest[:2]
        rest = rest[2:]
    else:
        out_ref = rest[0]
        rest = rest[1:]
    a_f, b_f, a_b, b_b = rest[:4]
    rest = rest[4:]
    seg = SCAN_SEG
    pitch = SCAN_PITCH
    ct = xc_ref.shape[1]
    ch = seg
    la = (0.5 * RG_C * LOG2E) * _log_sigmoid(lam_ref[0])
    la_f, la_b = la[:, :ct], la[:, ct:]
    w = w_ref[0]
    bias = b_ref[0]

    def for_chunk_groups(fn):
        def trip(k, carry):
            base = pl.multiple_of(k * (SCAN_GROUP * pitch), SUBLANES)
            for j in range(SCAN_GROUP):
                fn(k * SCAN_GROUP + j, base + j * pitch)
            return carry

        lax.fori_loop(0, n_vseq // SCAN_GROUP, trip, 0)

    def gate_chunk(ci, srow):
        r0 = pl.multiple_of(ci * ch, ch)
        xc = xc_ref[pl.ds(r0, ch), :]
        gates = jnp.dot(xc.astype(BF16), w, preferred_element_type=F32) + bias

        def a_and_b(r_half, i_half, la_half):
            a = jnp.exp2(la_half + la_half * jnp.tanh(r_half))
            y = jnp.clip(1.0 - a * a, 0.0, 1.0)
            gain = jnp.where(y > 0.0, y * lax.rsqrt(y), 0.0)
            half_gx = (0.5 * gain) * xc
            return a, half_gx + half_gx * jnp.tanh(i_half)

        af, bf = a_and_b(gates[:, 0:ct], gates[:, ct:2 * ct], la_f)
        ab, bb = a_and_b(gates[:, 2 * ct:3 * ct], gates[:, 3 * ct:4 * ct], la_b)
        s0 = pl.ds(srow, ch)
        a_f[s0, :] = af
        b_f[s0, :] = bf
        a_b[s0, :] = ab
        b_b[s0, :] = bb

    for_chunk_groups(gate_chunk)

    def slab_idx(t):
        return pl.ds(t, n_vseq, stride=pitch)

    def slab(ref, t):
        return ref[slab_idx(t), :]

    if n_seg == 1:
        init_f = h0_ref[0]
        init_b = h0_ref[1]
    else:
        e_f, p_f, e_b, p_b, cin_f, cin_b = rest
        n_real = n_vseq // n_seg
        zero = jnp.zeros((n_vseq, ct), F32)
        one = jnp.ones((n_vseq, ct), F32)

        def local_step(t, carry):
            ef, pf, eb, pb = carry
            tb = seg - 1 - t
            a = slab(a_f, t)
            ef = a * ef + slab(b_f, t)
            pf = a * pf
            a2 = slab(a_b, tb)
            eb = a2 * eb + slab(b_b, tb)
            pb = a2 * pb
            return ef, pf, eb, pb

        ef, pf, eb, pb = lax.fori_loop(0, seg, local_step, (zero, one, zero, one), unroll=8)
        e_f[...] = ef
        p_f[...] = pf
        e_b[...] = eb
        p_b[...] = pb
        carry = h0_ref[0]
        for s in range(n_seg):
            idx = pl.ds(s, n_real, stride=n_seg)
            cin_f[idx, :] = carry
            carry = e_f[idx, :] + p_f[idx, :] * carry
        carry = h0_ref[1]
        for s in reversed(range(n_seg)):
            idx = pl.ds(s, n_real, stride=n_seg)
            cin_b[idx, :] = carry
            carry = e_b[idx, :] + p_b[idx, :] * carry
        init_f = cin_f[...]
        init_b = cin_b[...]

    def scan_step(t, carry):
        hf, hb = carry
        tb = seg - 1 - t
        hf = slab(a_f, t) * hf + slab(b_f, t)
        b_f[slab_idx(t), :] = hf
        hb = slab(a_b, tb) * hb + slab(b_b, tb)
        b_b[slab_idx(tb), :] = hb
        return hf, hb

    hf_last, hb_last = lax.fori_loop(0, seg, scan_step, (init_f, init_b), unroll=8)
    if want_final:
        hfin_ref[0] = hf_last
        hfin_ref[1] = hb_last

    def out_chunk(ci, srow):
        r0 = pl.multiple_of(ci * ch, ch)
        h = b_f[pl.ds(srow, ch), :] + b_b[pl.ds(srow, ch), :]
        out_ref[pl.ds(r0, ch), :] = (h * gz_ref[pl.ds(r0, ch), :]).astype(BF16)

    for_chunk_groups(out_chunk)


def _rglru(of32, w_cat, b_cat, lam_cat, h0, *, n_seq, seq_len, want_final):
    n, d_rnn = of32.shape[1:]
    ct = RG_CT
    n_ct = d_rnn // ct
    n_seg = seq_len // SCAN_SEG
    n_vseq = n_seq * n_seg
    assert n_vseq * SCAN_SEG == n and n_vseq % SCAN_GROUP == 0
    kern = functools.partial(_rglru_kernel, n_vseq=n_vseq, n_seg=n_seg, want_final=want_final)
    out_specs = [pl.BlockSpec((n, ct), lambda c: (0, c))]
    out_shape = [jax.ShapeDtypeStruct((n, d_rnn), BF16)]
    if want_final:
        out_specs.append(pl.BlockSpec((2, n_seq, ct), lambda c: (0, 0, c)))
        out_shape.append(jax.ShapeDtypeStruct((2, n_seq, d_rnn), F32))
    scratch = [pltpu.VMEM((n_vseq * SCAN_PITCH, ct), F32) for _ in range(4)]
    if n_seg > 1:
        scratch += [pltpu.VMEM((n_vseq, ct), F32) for _ in range(6)]
    return pl.pallas_call(
        kern,
        grid=(n_ct,),
        in_specs=[pl.BlockSpec((None, n, ct), lambda c: (0, 0, c)),
                  pl.BlockSpec((None, n, ct), lambda c: (1, 0, c)),
                  pl.BlockSpec((1, ct, 4 * ct), lambda c: (c, 0, 0)),
                  pl.BlockSpec((1, 1, 4 * ct), lambda c: (c, 0, 0)),
                  pl.BlockSpec((1, 1, 2 * ct), lambda c: (c, 0, 0)),
                  pl.BlockSpec((2, n_seq, ct), lambda c: (0, 0, c))],
        out_specs=out_specs,
        out_shape=out_shape,
        scratch_shapes=scratch,
        compiler_params=_cparams(1),
        name="rglru_ctx" if want_final else "rglru_lat",
    )(of32, of32, w_cat, b_cat, lam_cat, h0)


def _causal_bias(mask_ref):
    lc = mask_ref.shape[1]
    ii = lax.broadcasted_iota(jnp.int32, (lc, lc), 0)
    jj = lax.broadcasted_iota(jnp.int32, (lc, lc), 1)
    mask_ref[0] = jnp.where(jj <= ii, 0.0, -jnp.inf)
    mask_ref[1] = jnp.where(jj >= ii, 0.0, -jnp.inf)


def _with_ones(v):
    return jnp.concatenate([v, jnp.ones((v.shape[0], GATE_LANES), v.dtype)], axis=1)


def _rep(col):
    return jnp.broadcast_to(col, (col.shape[0], GATE_LANES))


def _wide(rep, width):
    return jnp.concatenate([rep] * (width // GATE_LANES), axis=1)


def _mlstm_unit(direction, head, q, k, v, v1, gc, gt_ref, r0, mask_ref, state):
    lc, hd = q.shape
    col_i = direction * ML_HEADS + head
    col_c = 2 * ML_HEADS + col_i
    lane = lax.broadcasted_iota(jnp.int32, (1, GATE_LANES), 1)
    ig_c = _rep(jnp.sum(jnp.where(lane == col_i, gc, 0.0), -1, keepdims=True))
    cum_c = _rep(jnp.sum(jnp.where(lane == col_c, gc, 0.0), -1, keepdims=True))
    ig_row = gt_ref[pl.ds(col_i, 1), pl.ds(r0, lc)]
    cum_row = gt_ref[pl.ds(col_c, 1), pl.ds(r0, lc)]
    dmat = (_wide(cum_c, lc) + (ig_row - cum_row)) + mask_ref[direction]
    if state is None:
        m_prev = 0.0
    else:
        s_prev, m_prev = state
    inter = cum_c + m_prev
    m_row = jnp.maximum(_rep(jnp.max(dmat, -1, keepdims=True)), inter)
    wts = jnp.exp2(dmat - _wide(m_row, lc))
    s = lax.dot_general(q, k, (((1,), (1,)), ((), ())), preferred_element_type=F32) * wts
    if state is None:
        num = jnp.dot(s.astype(BF16), v, preferred_element_type=F32)
        den = _rep(jnp.sum(s, -1, keepdims=True))
        h = num * _wide(1.0 / jnp.maximum(jnp.abs(den), jnp.exp2(-m_row)), hd)
    else:
        w_inter = jnp.exp2(inter - m_row)
        nd = (jnp.dot(s.astype(BF16), v1, preferred_element_type=F32)
              + _wide(w_inter, hd + GATE_LANES) * jnp.dot(q, s_prev.astype(BF16), preferred_element_type=F32))
        rden = 1.0 / jnp.maximum(jnp.abs(nd[:, hd:]), jnp.exp2(-m_row))
        h = nd[:, :hd] * _wide(rden, hd)
    btot = cum_c[lc - 1:lc] if direction == 0 else cum_c[0:1]
    wk_log = btot - cum_c + ig_c
    m_new = jnp.maximum(btot + m_prev, jnp.max(wk_log, 0, keepdims=True))
    kw = k.astype(F32) * _wide(jnp.exp2(wk_log - m_new), hd)
    t_lhs = (((0,), (0,)), ((), ()))
    if state is None:
        c_new = lax.dot_general(kw.astype(BF16), v, t_lhs, preferred_element_type=F32)
        return h, c_new, jnp.sum(kw, 0, keepdims=True), m_new
    decay = _wide(jnp.exp2(btot + m_prev - m_new), hd + GATE_LANES)
    s_new = decay * s_prev + lax.dot_general(kw.astype(BF16), v1, t_lhs, preferred_element_type=F32)
    return h, s_new, None, m_new


HEAD_OUT_ROWS = 1024


def _head_out_all(h_acc, og_ref, gn_ref, out_ref):
    gn = gn_ref[...]
    rows = h_acc.shape[0]
    step = min(rows, HEAD_OUT_ROWS)
    for r0 in range(0, rows, step):
        rs = slice(r0, r0 + step)
        out_ref[rs, :] = (og_ref[rs, :] * (_ln(h_acc[rs, :]) * gn)).astype(BF16)


def _mlstm_ctx_kernel(q_ref, k_ref, v_ref, o_ref, g_ref, gt_ref, gn_ref,
                      out_ref, cfin_ref, nfin_ref, mfin_ref, mask_ref, h_acc, *, n_sub, scale):
    head = pl.program_id(1)
    lc = MLSTM_CHUNK
    _causal_bias(mask_ref)

    def body(s, carry):
        r0 = pl.multiple_of(s * lc, lc)
        rs = pl.ds(r0, lc)
        q, k, v, gc = q_ref[rs, :] * scale, k_ref[rs, :], v_ref[rs, :], g_ref[rs, :]
        hsum = None
        for direction in (0, 1):
            h, c_new, n_new, m_new = _mlstm_unit(direction, head, q, k, v, None, gc, gt_ref, r0, mask_ref, None)
            cfin_ref[s, direction, 0] = c_new
            nfin_ref[s, direction, 0] = n_new
            mfin_ref[s, direction, 0] = m_new * LN2
            hsum = h if hsum is None else hsum + h
        h_acc[rs, :] = hsum
        return carry

    lax.fori_loop(0, n_sub, body, 0, unroll=2)
    _head_out_all(h_acc, o_ref, gn_ref, out_ref)


def _mlstm_lat_kernel(q_ref, k_ref, v_ref, o_ref, g_ref, gt_ref, gn_ref, c0_ref, n0_ref, m0_ref,
                      out_ref, s_st, m_st, h_acc, mask_ref, *, n_chunks, scale):
    head = pl.program_id(1)
    lc = MLSTM_CHUNK
    hd = q_ref.shape[1]
    _causal_bias(mask_ref)
    for direction in (0, 1):
        s_st[direction, :, 0:hd] = c0_ref[0, direction, 0]
        s_st[direction, :, hd:] = n0_ref[0, direction, 0]
        m_st[direction] = m0_ref[0, direction, 0] * LOG2E
    h_acc[...] = jnp.zeros(h_acc.shape, F32)

    def body(c, carry):
        for direction in (0, 1):
            cc = c if direction == 0 else n_chunks - 1 - c
            r0 = pl.multiple_of(cc * lc, lc)
            rs = pl.ds(r0, lc)
            state = (s_st[direction], m_st[direction])
            v = v_ref[rs, :]
            h, s_new, _, m_new = _mlstm_unit(direction, head, q_ref[rs, :] * scale, k_ref[rs, :], v,
                                             _with_ones(v), g_ref[rs, :], gt_ref, r0, mask_ref, state)
            s_st[direction] = s_new
            m_st[direction] = m_new
            h_acc[rs, :] += h
        return carry

    lax.fori_loop(0, n_chunks, body, 0, unroll=4)
    _head_out_all(h_acc, o_ref, gn_ref, out_ref)


def _mlstm_in_specs(rows, hd):
    slab = lambda idx: pl.BlockSpec((None, rows, hd), lambda s, h: (idx, s, h))
    return [slab(0), slab(1), slab(2), slab(2),
            pl.BlockSpec((rows, GATE_LANES), lambda s, h: (s, 0)),
            pl.BlockSpec((4 * ML_HEADS, rows), lambda s, h: (0, s)),
            pl.BlockSpec((1, hd), lambda s, h: (0, h))]


def _mlstm_ctx(obf, of32, g, gt, gn_g, *, n_seq):
    n = obf.shape[1]
    ml_dim = gn_g.shape[1]
    hd = ml_dim // ML_HEADS
    lc = MLSTM_CHUNK
    assert n == n_seq * lc
    n_sub = 8
    assert n_seq % n_sub == 0
    rows = n_sub * lc
    kern = functools.partial(_mlstm_ctx_kernel, n_sub=n_sub, scale=hd ** -0.5)
    return pl.pallas_call(
        kern,
        grid=(n_seq // n_sub, ML_HEADS),
        in_specs=_mlstm_in_specs(rows, hd),
        out_specs=[pl.BlockSpec((rows, hd), lambda s, h: (s, h)),
                   pl.BlockSpec((n_sub, 2, 1, hd, hd), lambda s, h: (s, 0, h, 0, 0)),
                   pl.BlockSpec((n_sub, 2, 1, 1, hd), lambda s, h: (s, 0, h, 0, 0)),
                   pl.BlockSpec((n_sub, 2, 1, 1, GATE_LANES), lambda s, h: (s, 0, h, 0, 0))],
        out_shape=[jax.ShapeDtypeStruct((n, ml_dim), BF16),
                   jax.ShapeDtypeStruct((n_seq, 2, ML_HEADS, hd, hd), F32),
                   jax.ShapeDtypeStruct((n_seq, 2, ML_HEADS, 1, hd), F32),
                   jax.ShapeDtypeStruct((n_seq, 2, ML_HEADS, 1, GATE_LANES), F32)],
        scratch_shapes=[pltpu.VMEM((2, lc, lc), F32), pltpu.VMEM((rows, hd), F32)],
        compiler_params=_cparams(2),
        name="mlstm_ctx",
    )(obf, obf, obf, of32, g, gt, gn_g)


def _mlstm_lat(obf, of32, g, gt, gn_g, c0, n0, m0, *, n_seq):
    n = obf.shape[1]
    ml_dim = gn_g.shape[1]
    hd = ml_dim // ML_HEADS
    lc = MLSTM_CHUNK
    rows = n // n_seq
    n_chunks = rows // lc
    kern = functools.partial(_mlstm_lat_kernel, n_chunks=n_chunks, scale=hd ** -0.5)
    return pl.pallas_call(
        kern,
        grid=(n_seq, ML_HEADS),
        in_specs=_mlstm_in_specs(rows, hd) + [
                  pl.BlockSpec((1, 2, 1, hd, hd), lambda s, h: (s, 0, h, 0, 0)),
                  pl.BlockSpec((1, 2, 1, hd, GATE_LANES), lambda s, h: (s, 0, h, 0, 0)),
                  pl.BlockSpec((1, 2, 1, 1, GATE_LANES), lambda s, h: (s, 0, h, 0, 0))],
        out_specs=pl.BlockSpec((rows, hd), lambda s, h: (s, h)),
        out_shape=jax.ShapeDtypeStruct((n, ml_dim), BF16),
        scratch_shapes=[pltpu.VMEM((2, hd, hd + GATE_LANES), F32),
                        pltpu.VMEM((2, 1, GATE_LANES), F32),
                        pltpu.VMEM((rows, hd), F32),
                        pltpu.VMEM((2, lc, lc), F32)],
        compiler_params=_cparams(2),
        name="mlstm_lat",
    )(obf, obf, obf, of32, g, gt, gn_g, c0, n0, m0)


def _merge_kernel(hrg_ref, hml_ref, gma_ref, gmb_ref, x_ref, mod_ref, wrg_ref, wml_ref, wout_ref,
                  bm_ref, lng_ref, lnb_ref, o_ref, *, d, alpha, row_base, tiles_per_row):
    r = _mod_row(pl.program_id(0), row_base, tiles_per_row)
    gate1 = mod_ref[pl.ds(r, 1), 2 * d:3 * d]
    rows_per_group = x_ref.shape[0] // MERGE_ROW_GROUPS
    for grp in range(MERGE_ROW_GROUPS):
        rows = slice(grp * rows_per_group, (grp + 1) * rows_per_group)
        y_rg = jnp.dot(hrg_ref[rows, :], wrg_ref[...], preferred_element_type=F32)
        y_ml = jnp.dot(hml_ref[rows, :], wml_ref[...], preferred_element_type=F32)
        g_rg = jax.nn.sigmoid(gma_ref[rows, :] + bm_ref[:, 0:d])
        g_ml = jax.nn.sigmoid(gmb_ref[rows, :] + bm_ref[:, d:2 * d])
        merged = (g_rg * y_rg + g_ml * y_ml).astype(BF16)
        mix = jnp.dot(merged, wout_ref[...], preferred_element_type=F32)
        o_ref[rows, :] = _ln(alpha * x_ref[rows, :] + gate1 * mix) * lng_ref[...] + lnb_ref[...]


def _merge(hrg, hml, of32, x, mod, w_rg, w_ml, w_out, b_merge, ln_g, ln_b, *, alpha, row_base,
           rows_per_mod, tm):
    n, d = x.shape
    tiles_per_row = _tiles_per_row(rows_per_mod, tm)
    kern = functools.partial(_merge_kernel, d=d, alpha=alpha, row_base=row_base, tiles_per_row=tiles_per_row)
    full = lambda shape: pl.BlockSpec(shape, lambda i: (0,) * len(shape))
    resident = lambda shape: pl.BlockSpec(shape, lambda i: (0, 0), pipeline_mode=pl.Buffered(1))
    return pl.pallas_call(
        kern,
        grid=(n // tm,),
        in_specs=[pl.BlockSpec((tm, d), lambda i: (i, 0)),
                  pl.BlockSpec((tm, d), lambda i: (i, 0)),
                  pl.BlockSpec((None, tm, d), lambda i: (3, i, 0)),
                  pl.BlockSpec((None, tm, d), lambda i: (4, i, 0)),
                  pl.BlockSpec((tm, d), lambda i: (i, 0)),
                  full(mod.shape), resident((d, d)), resident((d, d)), resident((d, d)),
                  full((1, 2 * d)), full((1, d)), full((1, d))],
        out_specs=pl.BlockSpec((tm, d), lambda i: (i, 0)),
        out_shape=jax.ShapeDtypeStruct((n, d), F32),
        compiler_params=_cparams(1),
        name="merge",
    )(hrg, hml, of32, of32, x, mod, w_rg, w_ml, w_out, b_merge, ln_g, ln_b)


def _mlp_kernel(x_ref, mod_ref, wfc_ref, bfc_ref, wpj_ref, bpj_ref, lng_ref, lnb_ref, o_ref,
                u_ref, *, d, tf, alpha, row_base, tiles_per_row):
    r = _mod_row(pl.program_id(0), row_base, tiles_per_row)
    shift = mod_ref[pl.ds(r, 1), 3 * d:4 * d]
    scale = mod_ref[pl.ds(r, 1), 4 * d:5 * d]
    gate2 = mod_ref[pl.ds(r, 1), 5 * d:6 * d]
    rows_per_group = x_ref.shape[0] // MLP_ROW_GROUPS
    for grp in range(MLP_ROW_GROUPS):
        rows = slice(grp * rows_per_group, (grp + 1) * rows_per_group)
        u_ref[rows, :] = (_ln(x_ref[rows, :]) * (1.0 + scale) + shift).astype(BF16)
        acc = None
        for f in range(wfc_ref.shape[1] // tf):
            cols = slice(f * tf, (f + 1) * tf)
            hid = jnp.dot(u_ref[rows, :], wfc_ref[:, cols], preferred_element_type=F32) + bfc_ref[:, cols]
            hid = jnp.square(jnp.maximum(hid, 0.0)).astype(BF16)
            part = jnp.dot(hid, wpj_ref[cols, :], preferred_element_type=F32)
            acc = part if acc is None else acc + part
        y = alpha * x_ref[rows, :] + gate2 * (acc + bpj_ref[...])
        o_ref[rows, :] = _ln(y) * lng_ref[...] + lnb_ref[...]


def _mlp(x, mod, w_fc, b_fc, w_proj, b_proj, ln_g, ln_b, *, alpha, row_base, rows_per_mod, tm):
    n, d = x.shape
    tiles_per_row = _tiles_per_row(rows_per_mod, tm)
    d_ff = w_fc.shape[1]
    tf = 1024
    assert d_ff % tf == 0
    kern = functools.partial(_mlp_kernel, d=d, tf=tf, alpha=alpha, row_base=row_base,
                             tiles_per_row=tiles_per_row)
    resident = lambda shape: pl.BlockSpec(shape, lambda i: (0, 0), pipeline_mode=pl.Buffered(1))
    return pl.pallas_call(
        kern,
        grid=(n // tm,),
        in_specs=[pl.BlockSpec((tm, d), lambda i: (i, 0)),
                  pl.BlockSpec(mod.shape, lambda i: (0, 0)),
                  resident((d, d_ff)),
                  pl.BlockSpec((1, d_ff), lambda i: (0, 0)),
                  resident((d_ff, d)),
                  pl.BlockSpec((1, d), lambda i: (0, 0)),
                  pl.BlockSpec((1, d), lambda i: (0, 0)),
                  pl.BlockSpec((1, d), lambda i: (0, 0))],
        out_specs=pl.BlockSpec((tm, d), lambda i: (i, 0)),
        out_shape=jax.ShapeDtypeStruct((n, d), F32),
        scratch_shapes=[pltpu.VMEM((tm, d), BF16)],
        compiler_params=_cparams(1),
        name="mlp",
    )(x, mod, w_fc, b_fc, w_proj, b_proj, ln_g, ln_b)


def _rg_gate_weights(wa, ba, wx, bx, lam):
    ct = RG_CT
    d_rnn = ba.shape[-1]
    n_ct = d_rnn // ct
    per = ct // RG_BW

    def tile_blockdiag(w):
        wt = w.reshape(n_ct, per, RG_BW, RG_BW)
        eye = jnp.eye(per, dtype=w.dtype)
        return jnp.einsum('cpkj,pq->cpkqj', wt, eye).reshape(n_ct, ct, ct)

    w_cat = (0.5 * jnp.concatenate([tile_blockdiag(wa[0]), tile_blockdiag(wx[0]),
                                    tile_blockdiag(wa[1]), tile_blockdiag(wx[1])], axis=-1)).astype(BF16)
    tiles = lambda b: b.reshape(n_ct, 1, ct)
    b_cat = 0.5 * jnp.concatenate([tiles(ba[0]), tiles(bx[0]), tiles(ba[1]), tiles(bx[1])], axis=-1)
    lam_cat = jnp.concatenate([tiles(lam[0]), tiles(lam[1])], axis=-1)
    return w_cat, b_cat, lam_cat


def kernel(x_prompt, x_sample, state_rglru_h, state_mlstm_C, state_mlstm_n, state_mlstm_m, c, c_ctx, w_ada, b_ada, w_in, rg_conv_w, rg_conv_b, rg_wa, rg_ba, rg_wx, rg_bx, rg_lambda, w_rg_proj, ml_b_igate, ml_b_fgate, ml_gn_g, w_ml_proj, b_merge, w_out, ln_g, ln_b, w_fc, b_fc, w_proj, b_proj):
    bp, seq, d = x_prompt.shape
    bl, dec_seq, _ = x_sample.shape
    depth = w_in.shape[0]
    d_rnn = rg_conv_w.shape[-1]
    ml_dim = ml_gn_g.shape[-1]
    hd = ml_dim // ML_HEADS
    alpha = (2 * depth) ** 0.25
    d_main = 2 * d_rnn + 4 * ml_dim + 2 * d
    assert seq == MLSTM_CHUNK == SCAN_SEG and dec_seq % MLSTM_CHUNK == 0
    assert d_rnn == d and ml_dim == d and w_in.shape[-1] == d_main + 4 * ML_HEADS

    tm = 1024
    tm_proj = 512
    xp = x_prompt.reshape(bp * seq, d)
    xs = x_sample.reshape(bl * dec_seq, d)
    cond = jnp.concatenate([c_ctx[None, :], c], axis=0)
    zero_h = jnp.zeros((2, bp, d_rnn), F32)
    new_h, new_c, new_n, new_m = [], [], [], []
    for l in range(depth):
        mod = _ada(cond, w_ada[l], b_ada[l])
        w_main = w_in[l].astype(BF16)
        w_gate = jnp.zeros((d, GATE_LANES), BF16).at[:, :4 * ML_HEADS].set(w_in[l][:, d_main:].astype(BF16))
        b_gate = jnp.concatenate([ml_b_igate[l].reshape(-1), ml_b_fgate[l].reshape(-1)]).reshape(-1, 1)
        w_cat, b_cat, lam_cat = _rg_gate_weights(rg_wa[l], rg_ba[l], rg_wx[l], rg_bx[l], rg_lambda[l])
        conv_b = rg_conv_b[l].reshape(1, d_rnn)
        gn_g = ml_gn_g[l].reshape(1, ml_dim)
        w_rg, w_ml, w_o = (w_rg_proj[l].astype(BF16), w_ml_proj[l].astype(BF16), w_out[l].astype(BF16))
        w_fc_b, w_pj_b = w_fc[l].astype(BF16), w_proj[l].astype(BF16)
        bm = b_merge[l].reshape(1, 2 * d)
        lng0, lnb0 = ln_g[l, 0].reshape(1, d), ln_b[l, 0].reshape(1, d)
        lng1, lnb1 = ln_g[l, 1].reshape(1, d), ln_b[l, 1].reshape(1, d)
        bfc, bpj = b_fc[l].reshape(1, -1), b_proj[l].reshape(1, d)

        def tail(x, hrg, hml, of32, row_base, rows_per_mod):
            x1 = _merge(hrg, hml, of32, x, mod, w_rg, w_ml, w_o, bm, lng0, lnb0, alpha=alpha,
                        row_base=row_base, rows_per_mod=rows_per_mod, tm=tm)
            return _mlp(x1, mod, w_fc_b, bfc, w_pj_b, bpj, lng1, lnb1, alpha=alpha,
                        row_base=row_base, rows_per_mod=rows_per_mod, tm=tm)

        of32, obf, g, gt = _in_proj(xp, mod, w_main, w_gate, b_gate, rg_conv_w[l], conv_b, conv_len=seq,
                                    row_base=0, rows_per_mod=None, tm=tm_proj)
        hrg, h_fin = _rglru(of32, w_cat, b_cat, lam_cat, zero_h, n_seq=bp, seq_len=seq, want_final=True)
        hml, c_fin, n_fin, m_fin = _mlstm_ctx(obf, of32, g, gt, gn_g, n_seq=bp)
        xp = tail(xp, hrg, hml, of32, 0, None)
        new_h.append(jnp.transpose(h_fin, (1, 0, 2)))
        new_c.append(c_fin)
        new_n.append(n_fin.reshape(bp, 2, ML_HEADS, hd))
        new_m.append(m_fin[..., 0, 0])

        of32, obf, g, gt = _in_proj(xs, mod, w_main, w_gate, b_gate, rg_conv_w[l], conv_b, conv_len=GRID_W,
                                    row_base=1, rows_per_mod=dec_seq, tm=tm_proj)
        h0 = jnp.transpose(state_rglru_h[:, l], (1, 0, 2))
        (hrg,) = _rglru(of32, w_cat, b_cat, lam_cat, h0, n_seq=bl, seq_len=dec_seq, want_final=False)
        c0 = state_mlstm_C[:, l]
        n0 = jnp.broadcast_to(state_mlstm_n[:, l][..., None], (bl, 2, ML_HEADS, hd, GATE_LANES))
        m0 = jnp.broadcast_to(state_mlstm_m[:, l].reshape(bl, 2, ML_HEADS, 1, 1),
                              (bl, 2, ML_HEADS, 1, GATE_LANES))
        hml = _mlstm_lat(obf, of32, g, gt, gn_g, c0, n0, m0, n_seq=bl)
        xs = tail(xs, hrg, hml, of32, 1, dec_seq)

    def stack(parts):
        return parts[0][:, None] if len(parts) == 1 else jnp.stack(parts, axis=1)

    return (xp.reshape(bp, seq, d), xs.reshape(bl, dec_seq, d),
            stack(new_h), stack(new_c), stack(new_n), stack(new_m))
```

```python
import functools

import jax
import jax.numpy as jnp
from jax import lax
from jax.experimental import pallas as pl
from jax.experimental.pallas import tpu as pltpu

F32 = jnp.float32
BF16 = jnp.bfloat16

LN_EPS = 1e-5
LOG2E = 1.4426950408889634
LN2 = 0.6931471805599453
RG_C = 8.0
RG_BW = 64
CONV_W = 4
GRID_W = 64
ML_HEADS = 4
MLSTM_CHUNK = 256
SCAN_SEG = 256
SUBLANES = 8
SCAN_PITCH = SCAN_SEG + 4
SCAN_GROUP = 8
RG_CT = 128
GATE_LANES = 128
VMEM_LIMIT = 52 * 1024 * 1024
MERGE_ROW_GROUPS = 4
MLP_ROW_GROUPS = 2
IN_PROJ_VMEM_LIMIT = 58 * 1024 * 1024


def _cparams(n_axes):
    return pltpu.CompilerParams(
        dimension_semantics=("arbitrary",) * n_axes, vmem_limit_bytes=VMEM_LIMIT)


def _log_sigmoid(x):
    return jnp.minimum(x, 0.0) - jnp.log1p(jnp.exp(-jnp.abs(x)))


def _ln(x):
    mu = jnp.mean(x, -1, keepdims=True)
    xc = x - mu
    var = jnp.mean(xc * xc, -1, keepdims=True)
    return xc * lax.rsqrt(var + LN_EPS)


def _mod_row(i, row_base, tiles_per_row):
    if tiles_per_row is None:
        return row_base
    return row_base + lax.div(i, jnp.int32(tiles_per_row))


def _ada_kernel(ct_ref, w_ref, b_ref, after_ref, o_ref, *, n_rows):
    del after_ref
    w = w_ref[...]
    ct = ct_ref[...]
    s = ct * jax.nn.sigmoid(ct)
    o_ref[...] = jnp.zeros(o_ref.shape, F32)
    for r in range(n_rows):
        o_ref[r:r + 1, :] = jnp.sum(w * s[:, r:r + 1], axis=0, keepdims=True) + b_ref[...]


def _ada(cond, w_ada, b_ada, run_after):
    n_rows, d = cond.shape
    assert n_rows <= 8
    ct = jnp.zeros((d, 8), F32).at[:, :n_rows].set(cond.T)
    n_out = w_ada.shape[1]
    tn = 2048
    assert n_out % tn == 0
    return pl.pallas_call(
        functools.partial(_ada_kernel, n_rows=n_rows),
        grid=(n_out // tn,),
        in_specs=[pl.BlockSpec((d, 8), lambda j: (0, 0)),
                  pl.BlockSpec((d, tn), lambda j: (0, j)),
                  pl.BlockSpec((1, tn), lambda j: (0, j)),
                  pl.BlockSpec((16, GATE_LANES), lambda j: (0, 0))],
        out_specs=pl.BlockSpec((8, tn), lambda j: (0, j)),
        out_shape=jax.ShapeDtypeStruct((8, n_out), F32),
        compiler_params=_cparams(1),
        name="ada_mod",
    )(ct, w_ada, b_ada.reshape(1, n_out), run_after)


def _inproj_tile(u_ref, w_ref, wg_ref, bg_ref, cw_ref, cb_ref, of_ref, ob_ref, g_ref, gt_ref, *, d, tm,
                 conv_len):
    lc = MLSTM_CHUNK
    n_gate = 4 * ML_HEADS
    g = jnp.dot(u_ref[...], wg_ref[...], preferred_element_type=F32)
    gt = g.T[0:n_gate, :] + bg_ref[...]
    gt = jnp.where(lax.broadcasted_iota(jnp.int32, gt.shape, 0) >= 2 * ML_HEADS, _log_sigmoid(gt), gt)
    gt = gt * LOG2E
    row = lax.broadcasted_iota(jnp.int32, (n_gate, lc), 0)
    ii = lax.broadcasted_iota(jnp.int32, (lc, lc), 0)
    jj = lax.broadcasted_iota(jnp.int32, (lc, lc), 1)
    incl_before = (ii <= jj).astype(BF16)
    incl_after = (ii >= jj).astype(BF16)
    parts = []
    for c in range(tm // lc):
        x = gt[:, c * lc:(c + 1) * lc]
        hi = x.astype(BF16)
        r1 = x - hi.astype(F32)
        mid = r1.astype(BF16)
        lo = (r1 - mid.astype(F32)).astype(BF16)
        split = jnp.concatenate([hi, mid, lo], axis=0)

        def cumsum(tri):
            res = jnp.dot(split, tri, preferred_element_type=F32)
            return res[0:n_gate] + res[n_gate:2 * n_gate] + res[2 * n_gate:3 * n_gate]

        parts.append(jnp.where(row < 2 * ML_HEADS, x,
                               jnp.where(row < 3 * ML_HEADS, cumsum(incl_before), cumsum(incl_after))))
    gfin = jnp.concatenate(parts, axis=1)
    gt_ref[...] = gfin
    pad = jnp.zeros((GATE_LANES - n_gate, tm), F32)
    g_ref[...] = jnp.concatenate([gfin, pad], axis=0).T

    def col_tile(t):
        return jnp.dot(u_ref[...], w_ref[:, t * d:(t + 1) * d], preferred_element_type=F32)

    xr = col_tile(0)
    pos = jnp.bitwise_and(lax.broadcasted_iota(jnp.int32, (tm, 1), 0), conv_len - 1)
    cw = cw_ref[...]
    of_ref[0] = (cw[0:1] * jnp.where(pos >= 2, pltpu.roll(xr, 2, 0), 0.0)
                 + cw[1:2] * jnp.where(pos >= 1, pltpu.roll(xr, 1, 0), 0.0)
                 + cw[2:3] * xr
                 + cw[3:4] * jnp.where(pos <= conv_len - 2, pltpu.roll(xr, tm - 1, 0), 0.0)
                 + cb_ref[...])
    of_ref[1] = jax.nn.gelu(col_tile(1))
    of_ref[2] = jax.nn.sigmoid(col_tile(5))
    of_ref[3] = col_tile(6)
    of_ref[4] = col_tile(7)
    for slab, t in enumerate((2, 3, 4)):
        ob_ref[slab] = col_tile(t).astype(BF16)


def _inproj_kernel(x0_ref, xn_ref, mod_ref, w_ref, wg_ref, bg_ref, cw_ref, cb_ref,
                   of_ref, ob_ref, g_ref, gt_ref, ua_ref, ub_ref, *, d, tm, conv_len, row_base, tiles_per_row):
    i = pl.program_id(0)
    last = pl.num_programs(0) - 1

    def ln_mod(x_blk, tile):
        r = _mod_row(tile, row_base, tiles_per_row)
        shift = mod_ref[pl.ds(r, 1), 0:d]
        scale = mod_ref[pl.ds(r, 1), d:2 * d]
        return (_ln(x_blk[...]) * (1.0 + scale) + shift).astype(BF16)

    @pl.when(i == 0)
    def _():
        ua_ref[...] = ln_mod(x0_ref, i)

    def step(u_ref, u_next_ref):
        u_next_ref[...] = ln_mod(xn_ref, jnp.minimum(i + 1, last))
        _inproj_tile(u_ref, w_ref, wg_ref, bg_ref, cw_ref, cb_ref, of_ref, ob_ref, g_ref, gt_ref,
                     d=d, tm=tm, conv_len=conv_len)

    parity = jnp.bitwise_and(i, 1)
    pl.when(parity == 0)(lambda: step(ua_ref, ub_ref))
    pl.when(parity == 1)(lambda: step(ub_ref, ua_ref))


def _tiles_per_row(rows_per_mod, tm):
    if rows_per_mod is None:
        return None
    assert rows_per_mod % tm == 0
    return rows_per_mod // tm


def _in_proj(x, mod, w_main, w_gate, b_gate, conv_w, conv_b, *, conv_len, row_base, rows_per_mod, tm):
    n, d = x.shape
    tiles_per_row = _tiles_per_row(rows_per_mod, tm)
    assert w_main.shape[1] >= 8 * d and conv_w.shape == (CONV_W, d)
    assert tm % conv_len == 0 and (conv_len & (conv_len - 1)) == 0
    n_tiles = n // tm
    kern = functools.partial(_inproj_kernel, d=d, tm=tm, conv_len=conv_len, row_base=row_base,
                             tiles_per_row=tiles_per_row)
    return pl.pallas_call(
        kern,
        grid=(n_tiles,),
        in_specs=[pl.BlockSpec((tm, d), lambda i: (0, 0)),
                  pl.BlockSpec((tm, d), lambda i: (jnp.minimum(i + 1, n_tiles - 1), 0)),
                  pl.BlockSpec(mod.shape, lambda i: (0, 0)),
                  pl.BlockSpec(w_main.shape, lambda i: (0, 0), pipeline_mode=pl.Buffered(1)),
                  pl.BlockSpec((d, GATE_LANES), lambda i: (0, 0)),
                  pl.BlockSpec((4 * ML_HEADS, 1), lambda i: (0, 0)),
                  pl.BlockSpec((CONV_W, d), lambda i: (0, 0)),
                  pl.BlockSpec((1, d), lambda i: (0, 0))],
        out_specs=[pl.BlockSpec((5, tm, d), lambda i: (0, i, 0)),
                   pl.BlockSpec((3, tm, d), lambda i: (0, i, 0)),
                   pl.BlockSpec((tm, GATE_LANES), lambda i: (i, 0)),
                   pl.BlockSpec((4 * ML_HEADS, tm), lambda i: (0, i))],
        out_shape=[jax.ShapeDtypeStruct((5, n, d), F32),
                   jax.ShapeDtypeStruct((3, n, d), BF16),
                   jax.ShapeDtypeStruct((n, GATE_LANES), F32),
                   jax.ShapeDtypeStruct((4 * ML_HEADS, n), F32)],
        scratch_shapes=[pltpu.VMEM((tm, d), BF16), pltpu.VMEM((tm, d), BF16)],
        compiler_params=pltpu.CompilerParams(dimension_semantics=("arbitrary",),
                                             vmem_limit_bytes=IN_PROJ_VMEM_LIMIT),
        name="in_proj",
    )(x, x, mod, w_main, w_gate, b_gate, conv_w, conv_b)


def _rglru_kernel(xc_ref, gz_ref, w_ref, b_ref, lam_ref, h0_ref, *rest, n_vseq, n_seg, want_final):
    if want_final:
        out_ref, hfin_ref = rest[:2]
        rest = rest[2:]
    else:
        out_ref = rest[0]
        rest = rest[1:]
    a_f, b_f, a_b, b_b = rest[:4]
    rest = rest[4:]
    seg = SCAN_SEG
    pitch = SCAN_PITCH
    ct = xc_ref.shape[1]
    ch = seg
    la = (0.5 * RG_C * LOG2E) * _log_sigmoid(lam_ref[0])
    la_f, la_b = la[:, :ct], la[:, ct:]
    w = w_ref[0]
    bias = b_ref[0]

    def for_chunk_groups(fn):
        def trip(k, carry):
            base = pl.multiple_of(k * (SCAN_GROUP * pitch), SUBLANES)
            for j in range(SCAN_GROUP):
                fn(k * SCAN_GROUP + j, base + j * pitch)
            return carry

        lax.fori_loop(0, n_vseq // SCAN_GROUP, trip, 0)

    def gate_chunk(ci, srow):
        r0 = pl.multiple_of(ci * ch, ch)
        xc = xc_ref[pl.ds(r0, ch), :]
        gates = jnp.dot(xc.astype(BF16), w, preferred_element_type=F32) + bias

        def a_and_b(r_half, i_half, la_half):
            a = jnp.exp2(la_half + la_half * jnp.tanh(r_half))
            y = jnp.clip(1.0 - a * a, 0.0, 1.0)
            gain = jnp.where(y > 0.0, y * lax.rsqrt(y), 0.0)
            half_gx = (0.5 * gain) * xc
            return a, half_gx + half_gx * jnp.tanh(i_half)

        af, bf = a_and_b(gates[:, 0:ct], gates[:, ct:2 * ct], la_f)
        ab, bb = a_and_b(gates[:, 2 * ct:3 * ct], gates[:, 3 * ct:4 * ct], la_b)
        s0 = pl.ds(srow, ch)
        a_f[s0, :] = af
        b_f[s0, :] = bf
        a_b[s0, :] = ab
        b_b[s0, :] = bb

    for_chunk_groups(gate_chunk)

    def slab_idx(t):
        return pl.ds(t, n_vseq, stride=pitch)

    def slab(ref, t):
        return ref[slab_idx(t), :]

    if n_seg == 1:
        init_f = h0_ref[0]
        init_b = h0_ref[1]
    else:
        e_f, p_f, e_b, p_b, cin_f, cin_b = rest
        n_real = n_vseq // n_seg
        zero = jnp.zeros((n_vseq, ct), F32)
        one = jnp.ones((n_vseq, ct), F32)

        def local_step(t, carry):
            ef, pf, eb, pb = carry
            tb = seg - 1 - t
            a = slab(a_f, t)
            ef = a * ef + slab(b_f, t)
            pf = a * pf
            a2 = slab(a_b, tb)
            eb = a2 * eb + slab(b_b, tb)
            pb = a2 * pb
            return ef, pf, eb, pb

        ef, pf, eb, pb = lax.fori_loop(0, seg, local_step, (zero, one, zero, one), unroll=8)
        e_f[...] = ef
        p_f[...] = pf
        e_b[...] = eb
        p_b[...] = pb
        carry = h0_ref[0]
        for s in range(n_seg):
            idx = pl.ds(s, n_real, stride=n_seg)
            cin_f[idx, :] = carry
            carry = e_f[idx, :] + p_f[idx, :] * carry
        carry = h0_ref[1]
        for s in reversed(range(n_seg)):
            idx = pl.ds(s, n_real, stride=n_seg)
            cin_b[idx, :] = carry
            carry = e_b[idx, :] + p_b[idx, :] * carry
        init_f = cin_f[...]
        init_b = cin_b[...]

    def scan_step(t, carry):
        hf, hb = carry
        tb = seg - 1 - t
        hf = slab(a_f, t) * hf + slab(b_f, t)
        b_f[slab_idx(t), :] = hf
        hb = slab(a_b, tb) * hb + slab(b_b, tb)
        b_b[slab_idx(tb), :] = hb
        return hf, hb

    hf_last, hb_last = lax.fori_loop(0, seg, scan_step, (init_f, init_b), unroll=8)
    if want_final:
        hfin_ref[0] = hf_last
        hfin_ref[1] = hb_last

    def out_chunk(ci, srow):
        r0 = pl.multiple_of(ci * ch, ch)
        h = b_f[pl.ds(srow, ch), :] + b_b[pl.ds(srow, ch), :]
        out_ref[pl.ds(r0, ch), :] = (h * gz_ref[pl.ds(r0, ch), :]).astype(BF16)

    for_chunk_groups(out_chunk)


def _rglru(of32, w_cat, b_cat, lam_cat, h0, *, n_seq, seq_len, want_final):
    n, d_rnn = of32.shape[1:]
    ct = RG_CT
    n_ct = d_rnn // ct
    n_seg = seq_len // SCAN_SEG
    n_vseq = n_seq * n_seg
    assert n_vseq * SCAN_SEG == n and n_vseq % SCAN_GROUP == 0
    kern = functools.partial(_rglru_kernel, n_vseq=n_vseq, n_seg=n_seg, want_final=want_final)
    out_specs = [pl.BlockSpec((n, ct), lambda c: (0, c))]
    out_shape = [jax.ShapeDtypeStruct((n, d_rnn), BF16)]
    if want_final:
        out_specs.append(pl.BlockSpec((2, n_seq, ct), lambda c: (0, 0, c)))
        out_shape.append(jax.ShapeDtypeStruct((2, n_seq, d_rnn), F32))
    scratch = [pltpu.VMEM((n_vseq * SCAN_PITCH, ct), F32) for _ in range(4)]
    if n_seg > 1:
        scratch += [pltpu.VMEM((n_vseq, ct), F32) for _ in range(6)]
    return pl.pallas_call(
        kern,
        grid=(n_ct,),
        in_specs=[pl.BlockSpec((None, n, ct), lambda c: (0, 0, c)),
                  pl.BlockSpec((None, n, ct), lambda c: (1, 0, c)),
                  pl.BlockSpec((1, ct, 4 * ct), lambda c: (c, 0, 0)),
                  pl.BlockSpec((1, 1, 4 * ct), lambda c: (c, 0, 0)),
                  pl.BlockSpec((1, 1, 2 * ct), lambda c: (c, 0, 0)),
                  pl.BlockSpec((2, n_seq, ct), lambda c: (0, 0, c))],
        out_specs=out_specs,
        out_shape=out_shape,
        scratch_shapes=scratch,
        compiler_params=_cparams(1),
        name="rglru_ctx" if want_final else "rglru_lat",
    )(of32, of32, w_cat, b_cat, lam_cat, h0)


def _causal_bias(mask_ref):
    lc = mask_ref.shape[1]
    ii = lax.broadcasted_iota(jnp.int32, (lc, lc), 0)
    jj = lax.broadcasted_iota(jnp.int32, (lc, lc), 1)
    mask_ref[0] = jnp.where(jj <= ii, 0.0, -jnp.inf)
    mask_ref[1] = jnp.where(jj >= ii, 0.0, -jnp.inf)


def _with_ones(v):
    return jnp.concatenate([v, jnp.ones((v.shape[0], GATE_LANES), v.dtype)], axis=1)


def _rep(col):
    return jnp.broadcast_to(col, (col.shape[0], GATE_LANES))


def _wide(rep, width):
    return jnp.concatenate([rep] * (width // GATE_LANES), axis=1)


def _mlstm_unit(direction, head, q, k, v, v1, gc, gt_ref, r0, mask_ref, state):
    lc, hd = q.shape
    col_i = direction * ML_HEADS + head
    col_c = 2 * ML_HEADS + col_i
    lane = lax.broadcasted_iota(jnp.int32, (1, GATE_LANES), 1)
    ig_c = _rep(jnp.sum(jnp.where(lane == col_i, gc, 0.0), -1, keepdims=True))
    cum_c = _rep(jnp.sum(jnp.where(lane == col_c, gc, 0.0), -1, keepdims=True))
    ig_row = gt_ref[pl.ds(col_i, 1), pl.ds(r0, lc)]
    cum_row = gt_ref[pl.ds(col_c, 1), pl.ds(r0, lc)]
    dmat = (_wide(cum_c, lc) + (ig_row - cum_row)) + mask_ref[direction]
    if state is None:
        m_prev = 0.0
    else:
        s_prev, m_prev = state
    inter = cum_c + m_prev
    m_row = jnp.maximum(_rep(jnp.max(dmat, -1, keepdims=True)), inter)
    wts = jnp.exp2(dmat - _wide(m_row, lc))
    s = lax.dot_general(q, k, (((1,), (1,)), ((), ())), preferred_element_type=F32) * wts
    if state is None:
        num = jnp.dot(s.astype(BF16), v, preferred_element_type=F32)
        den = _rep(jnp.sum(s, -1, keepdims=True))
        h = num * _wide(1.0 / jnp.maximum(jnp.abs(den), jnp.exp2(-m_row)), hd)
    else:
        w_inter = jnp.exp2(inter - m_row)
        nd = (jnp.dot(s.astype(BF16), v1, preferred_element_type=F32)
              + _wide(w_inter, hd + GATE_LANES) * jnp.dot(q, s_prev.astype(BF16), preferred_element_type=F32))
        rden = 1.0 / jnp.maximum(jnp.abs(nd[:, hd:]), jnp.exp2(-m_row))
        h = nd[:, :hd] * _wide(rden, hd)
    btot = cum_c[lc - 1:lc] if direction == 0 else cum_c[0:1]
    wk_log = btot - cum_c + ig_c
    m_new = jnp.maximum(btot + m_prev, jnp.max(wk_log, 0, keepdims=True))
    kw = k.astype(F32) * _wide(jnp.exp2(wk_log - m_new), hd)
    t_lhs = (((0,), (0,)), ((), ()))
    if state is None:
        c_new = lax.dot_general(kw.astype(BF16), v, t_lhs, preferred_element_type=F32)
        return h, c_new, jnp.sum(kw, 0, keepdims=True), m_new
    decay = _wide(jnp.exp2(btot + m_prev - m_new), hd + GATE_LANES)
    s_new = decay * s_prev + lax.dot_general(kw.astype(BF16), v1, t_lhs, preferred_element_type=F32)
    return h, s_new, None, m_new


HEAD_OUT_ROWS = 1024


def _head_out_all(h_acc, og_ref, gn_ref, out_ref):
    gn = gn_ref[...]
    rows = h_acc.shape[0]
    step = min(rows, HEAD_OUT_ROWS)
    for r0 in range(0, rows, step):
        rs = slice(r0, r0 + step)
        out_ref[rs, :] = (og_ref[rs, :] * (_ln(h_acc[rs, :]) * gn)).astype(BF16)


def _mlstm_ctx_kernel(q_ref, k_ref, v_ref, o_ref, g_ref, gt_ref, gn_ref,
                      out_ref, cfin_ref, nfin_ref, mfin_ref, mask_ref, h_acc, *, n_sub, scale):
    head = pl.program_id(1)
    lc = MLSTM_CHUNK
    _causal_bias(mask_ref)

    def body(s, carry):
        r0 = pl.multiple_of(s * lc, lc)
        rs = pl.ds(r0, lc)
        q, k, v, gc = q_ref[rs, :] * scale, k_ref[rs, :], v_ref[rs, :], g_ref[rs, :]
        hsum = None
        for direction in (0, 1):
            h, c_new, n_new, m_new = _mlstm_unit(direction, head, q, k, v, None, gc, gt_ref, r0, mask_ref, None)
            cfin_ref[s, direction, 0] = c_new
            nfin_ref[s, direction, 0] = n_new
            mfin_ref[s, direction, 0] = m_new * LN2
            hsum = h if hsum is None else hsum + h
        h_acc[rs, :] = hsum
        return carry

    lax.fori_loop(0, n_sub, body, 0, unroll=2)
    _head_out_all(h_acc, o_ref, gn_ref, out_ref)


def _mlstm_lat_kernel(q_ref, k_ref, v_ref, o_ref, g_ref, gt_ref, gn_ref, c0_ref, n0_ref, m0_ref,
                      out_ref, s_st, m_st, h_acc, mask_ref, *, n_chunks, scale):
    head = pl.program_id(1)
    lc = MLSTM_CHUNK
    hd = q_ref.shape[1]
    _causal_bias(mask_ref)
    for direction in (0, 1):
        s_st[direction, :, 0:hd] = c0_ref[0, direction, 0]
        s_st[direction, :, hd:] = n0_ref[0, direction, 0]
        m_st[direction] = m0_ref[0, direction, 0] * LOG2E

    half = n_chunks // 2

    def make_body(accumulate):
        def body(c, carry):
            for direction in (0, 1):
                cc = c if direction == 0 else n_chunks - 1 - c
                r0 = pl.multiple_of(cc * lc, lc)
                rs = pl.ds(r0, lc)
                state = (s_st[direction], m_st[direction])
                v = v_ref[rs, :]
                h, s_new, _, m_new = _mlstm_unit(direction, head, q_ref[rs, :] * scale, k_ref[rs, :], v,
                                                 _with_ones(v), g_ref[rs, :], gt_ref, r0, mask_ref, state)
                s_st[direction] = s_new
                m_st[direction] = m_new
                if accumulate:
                    h_acc[rs, :] += h
                else:
                    h_acc[rs, :] = h
            return carry
        return body

    lax.fori_loop(0, half, make_body(False), 0, unroll=min(half, 8))
    lax.fori_loop(half, n_chunks, make_body(True), 0, unroll=min(half, 8))
    _head_out_all(h_acc, o_ref, gn_ref, out_ref)


def _mlstm_in_specs(rows, hd):
    slab = lambda idx: pl.BlockSpec((None, rows, hd), lambda s, h: (idx, s, h))
    return [slab(0), slab(1), slab(2), slab(2),
            pl.BlockSpec((rows, GATE_LANES), lambda s, h: (s, 0)),
            pl.BlockSpec((4 * ML_HEADS, rows), lambda s, h: (0, s)),
            pl.BlockSpec((1, hd), lambda s, h: (0, h))]


def _mlstm_ctx(obf, of32, g, gt, gn_g, *, n_seq):
    n = obf.shape[1]
    ml_dim = gn_g.shape[1]
    hd = ml_dim // ML_HEADS
    lc = MLSTM_CHUNK
    assert n == n_seq * lc
    n_sub = 8
    assert n_seq % n_sub == 0
    rows = n_sub * lc
    kern = functools.partial(_mlstm_ctx_kernel, n_sub=n_sub, scale=hd ** -0.5)
    return pl.pallas_call(
        kern,
        grid=(n_seq // n_sub, ML_HEADS),
        in_specs=_mlstm_in_specs(rows, hd),
        out_specs=[pl.BlockSpec((rows, hd), lambda s, h: (s, h)),
                   pl.BlockSpec((n_sub, 2, 1, hd, hd), lambda s, h: (s, 0, h, 0, 0)),
                   pl.BlockSpec((n_sub, 2, 1, 1, hd), lambda s, h: (s, 0, h, 0, 0)),
                   pl.BlockSpec((n_sub, 2, 1, 1, GATE_LANES), lambda s, h: (s, 0, h, 0, 0))],
        out_shape=[jax.ShapeDtypeStruct((n, ml_dim), BF16),
                   jax.ShapeDtypeStruct((n_seq, 2, ML_HEADS, hd, hd), F32),
                   jax.ShapeDtypeStruct((n_seq, 2, ML_HEADS, 1, hd), F32),
                   jax.ShapeDtypeStruct((n_seq, 2, ML_HEADS, 1, GATE_LANES), F32)],
        scratch_shapes=[pltpu.VMEM((2, lc, lc), F32), pltpu.VMEM((rows, hd), F32)],
        compiler_params=_cparams(2),
        name="mlstm_ctx",
    )(obf, obf, obf, of32, g, gt, gn_g)


def _mlstm_lat(obf, of32, g, gt, gn_g, c0, n0, m0, *, n_seq):
    n = obf.shape[1]
    ml_dim = gn_g.shape[1]
    hd = ml_dim // ML_HEADS
    lc = MLSTM_CHUNK
    rows = n // n_seq
    n_chunks = rows // lc
    assert n_chunks % 2 == 0
    kern = functools.partial(_mlstm_lat_kernel, n_chunks=n_chunks, scale=hd ** -0.5)
    return pl.pallas_call(
        kern,
        grid=(n_seq, ML_HEADS),
        in_specs=_mlstm_in_specs(rows, hd) + [
                  pl.BlockSpec((1, 2, 1, hd, hd), lambda s, h: (s, 0, h, 0, 0)),
                  pl.BlockSpec((1, 2, 1, hd, GATE_LANES), lambda s, h: (s, 0, h, 0, 0)),
                  pl.BlockSpec((1, 2, 1, 1, GATE_LANES), lambda s, h: (s, 0, h, 0, 0))],
        out_specs=pl.BlockSpec((rows, hd), lambda s, h: (s, h)),
        out_shape=jax.ShapeDtypeStruct((n, ml_dim), BF16),
        scratch_shapes=[pltpu.VMEM((2, hd, hd + GATE_LANES), F32),
                        pltpu.VMEM((2, 1, GATE_LANES), F32),
                        pltpu.VMEM((rows, hd), F32),
                        pltpu.VMEM((2, lc, lc), F32)],
        compiler_params=_cparams(2),
        name="mlstm_lat",
    )(obf, obf, obf, of32, g, gt, gn_g, c0, n0, m0)


def _merge_kernel(hrg_ref, hml_ref, gma_ref, gmb_ref, x_ref, mod_ref, wrg_ref, wml_ref, wout_ref,
                  bm_ref, lng_ref, lnb_ref, o_ref, *, d, alpha, row_base, tiles_per_row):
    r = _mod_row(pl.program_id(0), row_base, tiles_per_row)
    gate1 = mod_ref[pl.ds(r, 1), 2 * d:3 * d]
    rows_per_group = x_ref.shape[0] // MERGE_ROW_GROUPS
    for grp in range(MERGE_ROW_GROUPS):
        rows = slice(grp * rows_per_group, (grp + 1) * rows_per_group)
        y_rg = jnp.dot(hrg_ref[rows, :], wrg_ref[...], preferred_element_type=F32)
        y_ml = jnp.dot(hml_ref[rows, :], wml_ref[...], preferred_element_type=F32)
        g_rg = jax.nn.sigmoid(gma_ref[rows, :] + bm_ref[:, 0:d])
        g_ml = jax.nn.sigmoid(gmb_ref[rows, :] + bm_ref[:, d:2 * d])
        merged = (g_rg * y_rg + g_ml * y_ml).astype(BF16)
        mix = jnp.dot(merged, wout_ref[...], preferred_element_type=F32)
        o_ref[rows, :] = _ln(alpha * x_ref[rows, :] + gate1 * mix) * lng_ref[...] + lnb_ref[...]


def _merge(hrg, hml, of32, x, mod, w_rg, w_ml, w_out, b_merge, ln_g, ln_b, *, alpha, row_base,
           rows_per_mod, tm):
    n, d = x.shape
    tiles_per_row = _tiles_per_row(rows_per_mod, tm)
    kern = functools.partial(_merge_kernel, d=d, alpha=alpha, row_base=row_base, tiles_per_row=tiles_per_row)
    full = lambda shape: pl.BlockSpec(shape, lambda i: (0,) * len(shape))
    resident = lambda shape: pl.BlockSpec(shape, lambda i: (0, 0), pipeline_mode=pl.Buffered(1))
    return pl.pallas_call(
        kern,
        grid=(n // tm,),
        in_specs=[pl.BlockSpec((tm, d), lambda i: (i, 0)),
                  pl.BlockSpec((tm, d), lambda i: (i, 0)),
                  pl.BlockSpec((None, tm, d), lambda i: (3, i, 0)),
                  pl.BlockSpec((None, tm, d), lambda i: (4, i, 0)),
                  pl.BlockSpec((tm, d), lambda i: (i, 0)),
                  full(mod.shape), resident((d, d)), resident((d, d)), resident((d, d)),
                  full((1, 2 * d)), full((1, d)), full((1, d))],
        out_specs=pl.BlockSpec((tm, d), lambda i: (i, 0)),
        out_shape=jax.ShapeDtypeStruct((n, d), F32),
        compiler_params=_cparams(1),
        name="merge",
    )(hrg, hml, of32, of32, x, mod, w_rg, w_ml, w_out, b_merge, ln_g, ln_b)


def _mlp_kernel(x_ref, mod_ref, wfc_ref, bfc_ref, wpj_ref, bpj_ref, lng_ref, lnb_ref, o_ref,
                u_ref, *, d, tf, alpha, row_base, tiles_per_row):
    r = _mod_row(pl.program_id(0), row_base, tiles_per_row)
    shift = mod_ref[pl.ds(r, 1), 3 * d:4 * d]
    scale = mod_ref[pl.ds(r, 1), 4 * d:5 * d]
    gate2 = mod_ref[pl.ds(r, 1), 5 * d:6 * d]
    rows_per_group = x_ref.shape[0] // MLP_ROW_GROUPS
    for grp in range(MLP_ROW_GROUPS):
        rows = slice(grp * rows_per_group, (grp + 1) * rows_per_group)
        u_ref[rows, :] = (_ln(x_ref[rows, :]) * (1.0 + scale) + shift).astype(BF16)
        acc = None
        for f in range(wfc_ref.shape[1] // tf):
            cols = slice(f * tf, (f + 1) * tf)
            hid = jnp.dot(u_ref[rows, :], wfc_ref[:, cols], preferred_element_type=F32) + bfc_ref[:, cols]
            hid = jnp.square(jnp.maximum(hid, 0.0)).astype(BF16)
            part = jnp.dot(hid, wpj_ref[cols, :], preferred_element_type=F32)
            acc = part if acc is None else acc + part
        y = alpha * x_ref[rows, :] + gate2 * (acc + bpj_ref[...])
        o_ref[rows, :] = _ln(y) * lng_ref[...] + lnb_ref[...]


def _mlp(x, mod, w_fc, b_fc, w_proj, b_proj, ln_g, ln_b, *, alpha, row_base, rows_per_mod, tm):
    n, d = x.shape
    tiles_per_row = _tiles_per_row(rows_per_mod, tm)
    d_ff = w_fc.shape[1]
    tf = 1024
    assert d_ff % tf == 0
    kern = functools.partial(_mlp_kernel, d=d, tf=tf, alpha=alpha, row_base=row_base,
                             tiles_per_row=tiles_per_row)
    resident = lambda shape: pl.BlockSpec(shape, lambda i: (0, 0), pipeline_mode=pl.Buffered(1))
    return pl.pallas_call(
        kern,
        grid=(n // tm,),
        in_specs=[pl.BlockSpec((tm, d), lambda i: (i, 0)),
                  pl.BlockSpec(mod.shape, lambda i: (0, 0)),
                  resident((d, d_ff)),
                  pl.BlockSpec((1, d_ff), lambda i: (0, 0)),
                  resident((d_ff, d)),
                  pl.BlockSpec((1, d), lambda i: (0, 0)),
                  pl.BlockSpec((1, d), lambda i: (0, 0)),
                  pl.BlockSpec((1, d), lambda i: (0, 0))],
        out_specs=pl.BlockSpec((tm, d), lambda i: (i, 0)),
        out_shape=jax.ShapeDtypeStruct((n, d), F32),
        scratch_shapes=[pltpu.VMEM((tm, d), BF16)],
        compiler_params=_cparams(1),
        name="mlp",
    )(x, mod, w_fc, b_fc, w_proj, b_proj, ln_g, ln_b)


def _rg_gate_weights(wa, ba, wx, bx, lam):
    ct = RG_CT
    d_rnn = ba.shape[-1]
    n_ct = d_rnn // ct
    per = ct // RG_BW

    def tile_blockdiag(w):
        wt = w.reshape(n_ct, per, RG_BW, RG_BW)
        eye = jnp.eye(per, dtype=w.dtype)
        return jnp.einsum('cpkj,pq->cpkqj', wt, eye).reshape(n_ct, ct, ct)

    w_cat = (0.5 * jnp.concatenate([tile_blockdiag(wa[0]), tile_blockdiag(wx[0]),
                                    tile_blockdiag(wa[1]), tile_blockdiag(wx[1])], axis=-1)).astype(BF16)
    tiles = lambda b: b.reshape(n_ct, 1, ct)
    b_cat = 0.5 * jnp.concatenate([tiles(ba[0]), tiles(bx[0]), tiles(ba[1]), tiles(bx[1])], axis=-1)
    lam_cat = jnp.concatenate([tiles(lam[0]), tiles(lam[1])], axis=-1)
    return w_cat, b_cat, lam_cat


def kernel(x_prompt, x_sample, state_rglru_h, state_mlstm_C, state_mlstm_n, state_mlstm_m, c, c_ctx, w_ada, b_ada, w_in, rg_conv_w, rg_conv_b, rg_wa, rg_ba, rg_wx, rg_bx, rg_lambda, w_rg_proj, ml_b_igate, ml_b_fgate, ml_gn_g, w_ml_proj, b_merge, w_out, ln_g, ln_b, w_fc, b_fc, w_proj, b_proj):
    bp, seq, d = x_prompt.shape
    bl, dec_seq, _ = x_sample.shape
    depth = w_in.shape[0]
    d_rnn = rg_conv_w.shape[-1]
    ml_dim = ml_gn_g.shape[-1]
    hd = ml_dim // ML_HEADS
    alpha = (2 * depth) ** 0.25
    d_main = 2 * d_rnn + 4 * ml_dim + 2 * d
    assert seq == MLSTM_CHUNK == SCAN_SEG and dec_seq % MLSTM_CHUNK == 0
    assert d_rnn == d and ml_dim == d and w_in.shape[-1] == d_main + 4 * ML_HEADS

    tm = 1024
    tm_proj = 512
    xp = x_prompt.reshape(bp * seq, d)
    xs = x_sample.reshape(bl * dec_seq, d)
    cond = jnp.concatenate([c_ctx[None, :], c], axis=0)
    zero_h = jnp.zeros((2, bp, d_rnn), F32)
    new_h, new_c, new_n, new_m = [], [], [], []
    for l in range(depth):
        w_main = w_in[l].astype(BF16)
        mod = _ada(cond, w_ada[l], b_ada[l], w_main)
        w_gate = jnp.zeros((d, GATE_LANES), BF16).at[:, :4 * ML_HEADS].set(w_in[l][:, d_main:].astype(BF16))
        b_gate = jnp.concatenate([ml_b_igate[l].reshape(-1), ml_b_fgate[l].reshape(-1)]).reshape(-1, 1)
        w_cat, b_cat, lam_cat = _rg_gate_weights(rg_wa[l], rg_ba[l], rg_wx[l], rg_bx[l], rg_lambda[l])
        conv_b = rg_conv_b[l].reshape(1, d_rnn)
        gn_g = ml_gn_g[l].reshape(1, ml_dim)
        w_rg, w_ml, w_o = (w_rg_proj[l].astype(BF16), w_ml_proj[l].astype(BF16), w_out[l].astype(BF16))
        w_fc_b, w_pj_b = w_fc[l].astype(BF16), w_proj[l].astype(BF16)
        bm = b_merge[l].reshape(1, 2 * d)
        lng0, lnb0 = ln_g[l, 0].reshape(1, d), ln_b[l, 0].reshape(1, d)
        lng1, lnb1 = ln_g[l, 1].reshape(1, d), ln_b[l, 1].reshape(1, d)
        bfc, bpj = b_fc[l].reshape(1, -1), b_proj[l].reshape(1, d)

        def tail(x, hrg, hml, of32, row_base, rows_per_mod):
            x1 = _merge(hrg, hml, of32, x, mod, w_rg, w_ml, w_o, bm, lng0, lnb0, alpha=alpha,
                        row_base=row_base, rows_per_mod=rows_per_mod, tm=tm)
            return _mlp(x1, mod, w_fc_b, bfc, w_pj_b, bpj, lng1, lnb1, alpha=alpha,
                        row_base=row_base, rows_per_mod=rows_per_mod, tm=tm)

        of32, obf, g, gt = _in_proj(xp, mod, w_main, w_gate, b_gate, rg_conv_w[l], conv_b, conv_len=seq,
                                    row_base=0, rows_per_mod=None, tm=tm_proj)
        hrg, h_fin = _rglru(of32, w_cat, b_cat, lam_cat, zero_h, n_seq=bp, seq_len=seq, want_final=True)
        hml, c_fin, n_fin, m_fin = _mlstm_ctx(obf, of32, g, gt, gn_g, n_seq=bp)
        xp = tail(xp, hrg, hml, of32, 0, None)
        new_h.append(jnp.transpose(h_fin, (1, 0, 2)))
        new_c.append(c_fin)
        new_n.append(n_fin.reshape(bp, 2, ML_HEADS, hd))
        new_m.append(m_fin[..., 0, 0])

        of32, obf, g, gt = _in_proj(xs, mod, w_main, w_gate, b_gate, rg_conv_w[l], conv_b, conv_len=GRID_W,
                                    row_base=1, rows_per_mod=dec_seq, tm=tm_proj)
        h0 = jnp.transpose(state_rglru_h[:, l], (1, 0, 2))
        (hrg,) = _rglru(of32, w_cat, b_cat, lam_cat, h0, n_seq=bl, seq_len=dec_seq, want_final=False)
        c0 = state_mlstm_C[:, l]
        n0 = jnp.broadcast_to(state_mlstm_n[:, l][..., None], (bl, 2, ML_HEADS, hd, GATE_LANES))
        m0 = jnp.broadcast_to(state_mlstm_m[:, l].reshape(bl, 2, ML_HEADS, 1, 1),
                              (bl, 2, ML_HEADS, 1, GATE_LANES))
        hml = _mlstm_lat(obf, of32, g, gt, gn_g, c0, n0, m0, n_seq=bl)
        xs = tail(xs, hrg, hml, of32, 1, dec_seq)

    def stack(parts):
        return parts[0][:, None] if len(parts) == 1 else jnp.stack(parts, axis=1)

    return (xp.reshape(bp, seq, d), xs.reshape(bl, dec_seq, d),
            stack(new_h), stack(new_c), stack(new_n), stack(new_m))
```

```python
import functools

import jax
import jax.numpy as jnp
from jax import lax
from jax.experimental import pallas as pl
from jax.experimental.pallas import tpu as pltpu

F32 = jnp.float32
BF16 = jnp.bfloat16

LN_EPS = 1e-5
LOG2E = 1.4426950408889634
LN2 = 0.6931471805599453
RG_C = 8.0
RG_BW = 64
CONV_W = 4
GRID_W = 64
ML_HEADS = 4
MLSTM_CHUNK = 256
SCAN_SEG = 256
SUBLANES = 8
SCAN_PITCH = SCAN_SEG + 4
SCAN_GROUP = 8
RG_CT = 128
GATE_LANES = 128
VMEM_LIMIT = 52 * 1024 * 1024
MERGE_ROW_GROUPS = 4
MLP_ROW_GROUPS = 2
IN_PROJ_VMEM_LIMIT = 58 * 1024 * 1024


def _cparams(n_axes):
    return pltpu.CompilerParams(
        dimension_semantics=("arbitrary",) * n_axes, vmem_limit_bytes=VMEM_LIMIT)


def _log_sigmoid(x):
    return jnp.minimum(x, 0.0) - jnp.log1p(jnp.exp(-jnp.abs(x)))


def _ln(x):
    mu = jnp.mean(x, -1, keepdims=True)
    xc = x - mu
    var = jnp.mean(xc * xc, -1, keepdims=True)
    return xc * lax.rsqrt(var + LN_EPS)


def _mod_row(i, row_base, tiles_per_row):
    if tiles_per_row is None:
        return row_base
    return row_base + lax.div(i, jnp.int32(tiles_per_row))


def _ada_kernel(ct_ref, w_ref, b_ref, o_ref, *, n_rows):
    w = w_ref[...]
    ct = ct_ref[...]
    s = ct * jax.nn.sigmoid(ct)
    o_ref[...] = jnp.zeros(o_ref.shape, F32)
    for r in range(n_rows):
        o_ref[r:r + 1, :] = jnp.sum(w * s[:, r:r + 1], axis=0, keepdims=True) + b_ref[...]


def _ada(cond, w_ada, b_ada):
    n_rows, d = cond.shape
    assert n_rows <= 8
    ct = jnp.zeros((d, 8), F32).at[:, :n_rows].set(cond.T)
    n_out = w_ada.shape[1]
    tn = 2048
    assert n_out % tn == 0
    return pl.pallas_call(
        functools.partial(_ada_kernel, n_rows=n_rows),
        grid=(n_out // tn,),
        in_specs=[pl.BlockSpec((d, 8), lambda j: (0, 0)),
                  pl.BlockSpec((d, tn), lambda j: (0, j)),
                  pl.BlockSpec((1, tn), lambda j: (0, j))],
        out_specs=pl.BlockSpec((8, tn), lambda j: (0, j)),
        out_shape=jax.ShapeDtypeStruct((8, n_out), F32),
        compiler_params=_cparams(1),
        name="ada_mod",
    )(ct, w_ada, b_ada.reshape(1, n_out))


def _inproj_tile(u_ref, w_ref, wg_ref, bg_ref, cw_ref, cb_ref, of_ref, ob_ref, g_ref, gt_ref, *, d, tm,
                 conv_len):
    lc = MLSTM_CHUNK
    n_gate = 4 * ML_HEADS
    g = jnp.dot(u_ref[...], wg_ref[...], preferred_element_type=F32)
    gt = g.T[0:n_gate, :] + bg_ref[...]
    gt = jnp.where(lax.broadcasted_iota(jnp.int32, gt.shape, 0) >= 2 * ML_HEADS, _log_sigmoid(gt), gt)
    gt = gt * LOG2E
    row = lax.broadcasted_iota(jnp.int32, (n_gate, lc), 0)
    ii = lax.broadcasted_iota(jnp.int32, (lc, lc), 0)
    jj = lax.broadcasted_iota(jnp.int32, (lc, lc), 1)
    incl_before = (ii <= jj).astype(BF16)
    incl_after = (ii >= jj).astype(BF16)
    parts = []
    for c in range(tm // lc):
        x = gt[:, c * lc:(c + 1) * lc]
        hi = x.astype(BF16)
        r1 = x - hi.astype(F32)
        mid = r1.astype(BF16)
        lo = (r1 - mid.astype(F32)).astype(BF16)
        split = jnp.concatenate([hi, mid, lo], axis=0)

        def cumsum(tri):
            res = jnp.dot(split, tri, preferred_element_type=F32)
            return res[0:n_gate] + res[n_gate:2 * n_gate] + res[2 * n_gate:3 * n_gate]

        parts.append(jnp.where(row < 2 * ML_HEADS, x,
                               jnp.where(row < 3 * ML_HEADS, cumsum(incl_before), cumsum(incl_after))))
    gfin = jnp.concatenate(parts, axis=1)
    gt_ref[...] = gfin
    pad = jnp.zeros((GATE_LANES - n_gate, tm), F32)
    g_ref[...] = jnp.concatenate([gfin, pad], axis=0).T

    def col_tile(t):
        return jnp.dot(u_ref[...], w_ref[:, t * d:(t + 1) * d], preferred_element_type=F32)

    xr = col_tile(0)
    pos = jnp.bitwise_and(lax.broadcasted_iota(jnp.int32, (tm, 1), 0), conv_len - 1)
    cw = cw_ref[...]
    of_ref[0] = (cw[0:1] * jnp.where(pos >= 2, pltpu.roll(xr, 2, 0), 0.0)
                 + cw[1:2] * jnp.where(pos >= 1, pltpu.roll(xr, 1, 0), 0.0)
                 + cw[2:3] * xr
                 + cw[3:4] * jnp.where(pos <= conv_len - 2, pltpu.roll(xr, tm - 1, 0), 0.0)
                 + cb_ref[...])
    of_ref[1] = jax.nn.gelu(col_tile(1))
    of_ref[2] = jax.nn.sigmoid(col_tile(5))
    of_ref[3] = col_tile(6)
    of_ref[4] = col_tile(7)
    for slab, t in enumerate((2, 3, 4)):
        ob_ref[slab] = col_tile(t).astype(BF16)


def _inproj_kernel(x0_ref, xn_ref, mod_ref, w_ref, wg_ref, bg_ref, cw_ref, cb_ref,
                   of_ref, ob_ref, g_ref, gt_ref, ua_ref, ub_ref, *, d, tm, conv_len, row_base, tiles_per_row):
    i = pl.program_id(0)
    last = pl.num_programs(0) - 1

    def ln_mod(x_blk, tile):
        r = _mod_row(tile, row_base, tiles_per_row)
        shift = mod_ref[pl.ds(r, 1), 0:d]
        scale = mod_ref[pl.ds(r, 1), d:2 * d]
        return (_ln(x_blk[...]) * (1.0 + scale) + shift).astype(BF16)

    @pl.when(i == 0)
    def _():
        ua_ref[...] = ln_mod(x0_ref, i)

    def step(u_ref, u_next_ref):
        u_next_ref[...] = ln_mod(xn_ref, jnp.minimum(i + 1, last))
        _inproj_tile(u_ref, w_ref, wg_ref, bg_ref, cw_ref, cb_ref, of_ref, ob_ref, g_ref, gt_ref,
                     d=d, tm=tm, conv_len=conv_len)

    parity = jnp.bitwise_and(i, 1)
    pl.when(parity == 0)(lambda: step(ua_ref, ub_ref))
    pl.when(parity == 1)(lambda: step(ub_ref, ua_ref))


def _tiles_per_row(rows_per_mod, tm):
    if rows_per_mod is None:
        return None
    assert rows_per_mod % tm == 0
    return rows_per_mod // tm


def _in_proj(x, mod, w_main, w_gate, b_gate, conv_w, conv_b, *, conv_len, row_base, rows_per_mod, tm):
    n, d = x.shape
    tiles_per_row = _tiles_per_row(rows_per_mod, tm)
    assert w_main.shape[1] >= 8 * d and conv_w.shape == (CONV_W, d)
    assert tm % conv_len == 0 and (conv_len & (conv_len - 1)) == 0
    n_tiles = n // tm
    kern = functools.partial(_inproj_kernel, d=d, tm=tm, conv_len=conv_len, row_base=row_base,
                             tiles_per_row=tiles_per_row)
    return pl.pallas_call(
        kern,
        grid=(n_tiles,),
        in_specs=[pl.BlockSpec((tm, d), lambda i: (0, 0)),
                  pl.BlockSpec((tm, d), lambda i: (jnp.minimum(i + 1, n_tiles - 1), 0)),
                  pl.BlockSpec(mod.shape, lambda i: (0, 0)),
                  pl.BlockSpec(w_main.shape, lambda i: (0, 0), pipeline_mode=pl.Buffered(1)),
                  pl.BlockSpec((d, GATE_LANES), lambda i: (0, 0)),
                  pl.BlockSpec((4 * ML_HEADS, 1), lambda i: (0, 0)),
                  pl.BlockSpec((CONV_W, d), lambda i: (0, 0)),
                  pl.BlockSpec((1, d), lambda i: (0, 0))],
        out_specs=[pl.BlockSpec((5, tm, d), lambda i: (0, i, 0)),
                   pl.BlockSpec((3, tm, d), lambda i: (0, i, 0)),
                   pl.BlockSpec((tm, GATE_LANES), lambda i: (i, 0)),
                   pl.BlockSpec((4 * ML_HEADS, tm), lambda i: (0, i))],
        out_shape=[jax.ShapeDtypeStruct((5, n, d), F32),
                   jax.ShapeDtypeStruct((3, n, d), BF16),
                   jax.ShapeDtypeStruct((n, GATE_LANES), F32),
                   jax.ShapeDtypeStruct((4 * ML_HEADS, n), F32)],
        scratch_shapes=[pltpu.VMEM((tm, d), BF16), pltpu.VMEM((tm, d), BF16)],
        compiler_params=pltpu.CompilerParams(dimension_semantics=("arbitrary",),
                                             vmem_limit_bytes=IN_PROJ_VMEM_LIMIT),
        name="in_proj",
    )(x, x, mod, w_main, w_gate, b_gate, conv_w, conv_b)


def _rglru_kernel(xc_ref, gz_ref, w_ref, b_ref, lam_ref, h0_ref, *rest, n_vseq, n_seg, want_final):
    if want_final:
        out_ref, hfin_ref = rest[:2]
        rest = rest[2:]
    else:
        out_ref = rest[0]
        rest = rest[1:]
    a_f, b_f, a_b, b_b = rest[:4]
    rest = rest[4:]
    seg = SCAN_SEG
    pitch = SCAN_PITCH
    ct = xc_ref.shape[1]
    ch = seg
    la = (0.5 * RG_C * LOG2E) * _log_sigmoid(lam_ref[0])
    la_f, la_b = la[:, :ct], la[:, ct:]
    w = w_ref[0]
    bias = b_ref[0]

    def for_chunk_groups(fn):
        def trip(k, carry):
            base = pl.multiple_of(k * (SCAN_GROUP * pitch), SUBLANES)
            for j in range(SCAN_GROUP):
                fn(k * SCAN_GROUP + j, base + j * pitch)
            return carry

        lax.fori_loop(0, n_vseq // SCAN_GROUP, trip, 0)

    def gate_chunk(ci, srow):
        r0 = pl.multiple_of(ci * ch, ch)
        xc = xc_ref[pl.ds(r0, ch), :]
        gates = jnp.dot(xc.astype(BF16), w, preferred_element_type=F32) + bias

        def a_and_b(r_half, i_half, la_half):
            a = jnp.exp2(la_half + la_half * jnp.tanh(r_half))
            y = jnp.clip(1.0 - a * a, 0.0, 1.0)
            gain = jnp.where(y > 0.0, y * lax.rsqrt(y), 0.0)
            half_gx = (0.5 * gain) * xc
            return a, half_gx + half_gx * jnp.tanh(i_half)

        af, bf = a_and_b(gates[:, 0:ct], gates[:, ct:2 * ct], la_f)
        ab, bb = a_and_b(gates[:, 2 * ct:3 * ct], gates[:, 3 * ct:4 * ct], la_b)
        s0 = pl.ds(srow, ch)
        a_f[s0, :] = af
        b_f[s0, :] = bf
        a_b[s0, :] = ab
        b_b[s0, :] = bb

    for_chunk_groups(gate_chunk)

    def slab_idx(t):
        return pl.ds(t, n_vseq, stride=pitch)

    def slab(ref, t):
        return ref[slab_idx(t), :]

    if n_seg == 1:
        init_f = h0_ref[0]
        init_b = h0_ref[1]
    else:
        e_f, p_f, e_b, p_b, cin_f, cin_b = rest
        n_real = n_vseq // n_seg
        zero = jnp.zeros((n_vseq, ct), F32)
        one = jnp.ones((n_vseq, ct), F32)

        def local_step(t, carry):
            ef, pf, eb, pb = carry
            tb = seg - 1 - t
            a = slab(a_f, t)
            ef = a * ef + slab(b_f, t)
            pf = a * pf
            a2 = slab(a_b, tb)
            eb = a2 * eb + slab(b_b, tb)
            pb = a2 * pb
            return ef, pf, eb, pb

        ef, pf, eb, pb = lax.fori_loop(0, seg, local_step, (zero, one, zero, one), unroll=8)
        e_f[...] = ef
        p_f[...] = pf
        e_b[...] = eb
        p_b[...] = pb
        carry = h0_ref[0]
        for s in range(n_seg):
            idx = pl.ds(s, n_real, stride=n_seg)
            cin_f[idx, :] = carry
            carry = e_f[idx, :] + p_f[idx, :] * carry
        carry = h0_ref[1]
        for s in reversed(range(n_seg)):
            idx = pl.ds(s, n_real, stride=n_seg)
            cin_b[idx, :] = carry
            carry = e_b[idx, :] + p_b[idx, :] * carry
        init_f = cin_f[...]
        init_b = cin_b[...]

    def scan_step(t, carry):
        hf, hb = carry
        tb = seg - 1 - t
        hf = slab(a_f, t) * hf + slab(b_f, t)
        b_f[slab_idx(t), :] = hf
        hb = slab(a_b, tb) * hb + slab(b_b, tb)
        b_b[slab_idx(tb), :] = hb
        return hf, hb

    hf_last, hb_last = lax.fori_loop(0, seg, scan_step, (init_f, init_b), unroll=8)
    if want_final:
        hfin_ref[0] = hf_last
        hfin_ref[1] = hb_last

    def out_chunk(ci, srow):
        r0 = pl.multiple_of(ci * ch, ch)
        h = b_f[pl.ds(srow, ch), :] + b_b[pl.ds(srow, ch), :]
        out_ref[pl.ds(r0, ch), :] = (h * gz_ref[pl.ds(r0, ch), :]).astype(BF16)

    for_chunk_groups(out_chunk)


def _rglru(of32, w_cat, b_cat, lam_cat, h0, *, n_seq, seq_len, want_final):
    n, d_rnn = of32.shape[1:]
    ct = RG_CT
    n_ct = d_rnn // ct
    n_seg = seq_len // SCAN_SEG
    n_vseq = n_seq * n_seg
    assert n_vseq * SCAN_SEG == n and n_vseq % SCAN_GROUP == 0
    kern = functools.partial(_rglru_kernel, n_vseq=n_vseq, n_seg=n_seg, want_final=want_final)
    out_specs = [pl.BlockSpec((n, ct), lambda c: (0, c))]
    out_shape = [jax.ShapeDtypeStruct((n, d_rnn), BF16)]
    if want_final:
        out_specs.append(pl.BlockSpec((2, n_seq, ct), lambda c: (0, 0, c)))
        out_shape.append(jax.ShapeDtypeStruct((2, n_seq, d_rnn), F32))
    scratch = [pltpu.VMEM((n_vseq * SCAN_PITCH, ct), F32) for _ in range(4)]
    if n_seg > 1:
        scratch += [pltpu.VMEM((n_vseq, ct), F32) for _ in range(6)]
    return pl.pallas_call(
        kern,
        grid=(n_ct,),
        in_specs=[pl.BlockSpec((None, n, ct), lambda c: (0, 0, c)),
                  pl.BlockSpec((None, n, ct), lambda c: (1, 0, c)),
                  pl.BlockSpec((1, ct, 4 * ct), lambda c: (c, 0, 0)),
                  pl.BlockSpec((1, 1, 4 * ct), lambda c: (c, 0, 0)),
                  pl.BlockSpec((1, 1, 2 * ct), lambda c: (c, 0, 0)),
                  pl.BlockSpec((2, n_seq, ct), lambda c: (0, 0, c))],
        out_specs=out_specs,
        out_shape=out_shape,
        scratch_shapes=scratch,
        compiler_params=_cparams(1),
        name="rglru_ctx" if want_final else "rglru_lat",
    )(of32, of32, w_cat, b_cat, lam_cat, h0)


def _causal_bias(mask_ref):
    lc = mask_ref.shape[1]
    ii = lax.broadcasted_iota(jnp.int32, (lc, lc), 0)
    jj = lax.broadcasted_iota(jnp.int32, (lc, lc), 1)
    mask_ref[0] = jnp.where(jj <= ii, 0.0, -jnp.inf)
    mask_ref[1] = jnp.where(jj >= ii, 0.0, -jnp.inf)


def _with_ones(v):
    return jnp.concatenate([v, jnp.ones((v.shape[0], GATE_LANES), v.dtype)], axis=1)


def _rep(col):
    return jnp.broadcast_to(col, (col.shape[0], GATE_LANES))


def _wide(rep, width):
    return jnp.concatenate([rep] * (width // GATE_LANES), axis=1)


def _mlstm_unit(direction, head, q, k, v, v1, gc, gt_ref, r0, mask_ref, state):
    lc, hd = q.shape
    col_i = direction * ML_HEADS + head
    col_c = 2 * ML_HEADS + col_i
    lane = lax.broadcasted_iota(jnp.int32, (1, GATE_LANES), 1)
    ig_c = _rep(jnp.sum(jnp.where(lane == col_i, gc, 0.0), -1, keepdims=True))
    cum_c = _rep(jnp.sum(jnp.where(lane == col_c, gc, 0.0), -1, keepdims=True))
    ig_row = gt_ref[pl.ds(col_i, 1), pl.ds(r0, lc)]
    cum_row = gt_ref[pl.ds(col_c, 1), pl.ds(r0, lc)]
    dmat = (_wide(cum_c, lc) + (ig_row - cum_row)) + mask_ref[direction]
    if state is None:
        m_prev = 0.0
    else:
        s_prev, m_prev = state
    inter = cum_c + m_prev
    m_row = jnp.maximum(_rep(jnp.max(dmat, -1, keepdims=True)), inter)
    wts = jnp.exp2(dmat - _wide(m_row, lc))
    s = lax.dot_general(q, k, (((1,), (1,)), ((), ())), preferred_element_type=F32) * wts
    if state is None:
        num = jnp.dot(s.astype(BF16), v, preferred_element_type=F32)
        den = _rep(jnp.sum(s, -1, keepdims=True))
        h = num * _wide(1.0 / jnp.maximum(jnp.abs(den), jnp.exp2(-m_row)), hd)
    else:
        w_inter = jnp.exp2(inter - m_row)
        nd = (jnp.dot(s.astype(BF16), v1, preferred_element_type=F32)
              + _wide(w_inter, hd + GATE_LANES) * jnp.dot(q, s_prev.astype(BF16), preferred_element_type=F32))
        rden = 1.0 / jnp.maximum(jnp.abs(nd[:, hd:]), jnp.exp2(-m_row))
        h = nd[:, :hd] * _wide(rden, hd)
    btot = cum_c[lc - 1:lc] if direction == 0 else cum_c[0:1]
    wk_log = btot - cum_c + ig_c
    m_new = jnp.maximum(btot + m_prev, jnp.max(wk_log, 0, keepdims=True))
    kw = k.astype(F32) * _wide(jnp.exp2(wk_log - m_new), hd)
    t_lhs = (((0,), (0,)), ((), ()))
    if state is None:
        c_new = lax.dot_general(kw.astype(BF16), v, t_lhs, preferred_element_type=F32)
        return h, c_new, jnp.sum(kw, 0, keepdims=True), m_new
    decay = _wide(jnp.exp2(btot + m_prev - m_new), hd + GATE_LANES)
    s_new = decay * s_prev + lax.dot_general(kw.astype(BF16), v1, t_lhs, preferred_element_type=F32)
    return h, s_new, None, m_new


HEAD_OUT_ROWS = 1024


def _head_out_all(h_acc, og_ref, gn_ref, out_ref):
    gn = gn_ref[...]
    rows = h_acc.shape[0]
    step = min(rows, HEAD_OUT_ROWS)
    for r0 in range(0, rows, step):
        rs = slice(r0, r0 + step)
        out_ref[rs, :] = (og_ref[rs, :] * (_ln(h_acc[rs, :]) * gn)).astype(BF16)


def _mlstm_ctx_kernel(q_ref, k_ref, v_ref, o_ref, g_ref, gt_ref, gn_ref,
                      out_ref, cfin_ref, nfin_ref, mfin_ref, mask_ref, h_acc, *, n_sub, scale):
    head = pl.program_id(1)
    lc = MLSTM_CHUNK
    _causal_bias(mask_ref)

    def body(s, carry):
        r0 = pl.multiple_of(s * lc, lc)
        rs = pl.ds(r0, lc)
        q, k, v, gc = q_ref[rs, :] * scale, k_ref[rs, :], v_ref[rs, :], g_ref[rs, :]
        hsum = None
        for direction in (0, 1):
            h, c_new, n_new, m_new = _mlstm_unit(direction, head, q, k, v, None, gc, gt_ref, r0, mask_ref, None)
            cfin_ref[s, direction, 0] = c_new
            nfin_ref[s, direction, 0] = n_new
            mfin_ref[s, direction, 0] = m_new * LN2
            hsum = h if hsum is None else hsum + h
        h_acc[rs, :] = hsum
        return carry

    lax.fori_loop(0, n_sub, body, 0, unroll=2)
    _head_out_all(h_acc, o_ref, gn_ref, out_ref)


def _mlstm_lat_kernel(q_ref, k_ref, v_ref, o_ref, g_ref, gt_ref, gn_ref, c0_ref, n0_ref, m0_ref,
                      out_ref, s_st, m_st, h_acc, mask_ref, *, n_chunks, scale):
    head = pl.program_id(1)
    lc = MLSTM_CHUNK
    hd = q_ref.shape[1]
    _causal_bias(mask_ref)
    for direction in (0, 1):
        s_st[direction, :, 0:hd] = c0_ref[0, direction, 0]
        s_st[direction, :, hd:] = n0_ref[0, direction, 0]
        m_st[direction] = m0_ref[0, direction, 0] * LOG2E

    half = n_chunks // 2

    def make_body(accumulate):
        def body(c, carry):
            for direction in (0, 1):
                cc = c if direction == 0 else n_chunks - 1 - c
                r0 = pl.multiple_of(cc * lc, lc)
                rs = pl.ds(r0, lc)
                state = (s_st[direction], m_st[direction])
                v = v_ref[rs, :]
                h, s_new, _, m_new = _mlstm_unit(direction, head, q_ref[rs, :] * scale, k_ref[rs, :], v,
                                                 _with_ones(v), g_ref[rs, :], gt_ref, r0, mask_ref, state)
                s_st[direction] = s_new
                m_st[direction] = m_new
                if accumulate:
                    h_acc[rs, :] += h
                else:
                    h_acc[rs, :] = h
            return carry
        return body

    lax.fori_loop(0, half, make_body(False), 0, unroll=min(half, 8))
    lax.fori_loop(half, n_chunks, make_body(True), 0, unroll=min(half, 8))
    _head_out_all(h_acc, o_ref, gn_ref, out_ref)


def _mlstm_in_specs(rows, hd):
    slab = lambda idx: pl.BlockSpec((None, rows, hd), lambda s, h: (idx, s, h))
    return [slab(0), slab(1), slab(2), slab(2),
            pl.BlockSpec((rows, GATE_LANES), lambda s, h: (s, 0)),
            pl.BlockSpec((4 * ML_HEADS, rows), lambda s, h: (0, s)),
            pl.BlockSpec((1, hd), lambda s, h: (0, h))]


def _mlstm_ctx(obf, of32, g, gt, gn_g, *, n_seq):
    n = obf.shape[1]
    ml_dim = gn_g.shape[1]
    hd = ml_dim // ML_HEADS
    lc = MLSTM_CHUNK
    assert n == n_seq * lc
    n_sub = 8
    assert n_seq % n_sub == 0
    rows = n_sub * lc
    kern = functools.partial(_mlstm_ctx_kernel, n_sub=n_sub, scale=hd ** -0.5)
    return pl.pallas_call(
        kern,
        grid=(n_seq // n_sub, ML_HEADS),
        in_specs=_mlstm_in_specs(rows, hd),
        out_specs=[pl.BlockSpec((rows, hd), lambda s, h: (s, h)),
                   pl.BlockSpec((n_sub, 2, 1, hd, hd), lambda s, h: (s, 0, h, 0, 0)),
                   pl.BlockSpec((n_sub, 2, 1, 1, hd), lambda s, h: (s, 0, h, 0, 0)),
                   pl.BlockSpec((n_sub, 2, 1, 1, GATE_LANES), lambda s, h: (s, 0, h, 0, 0))],
        out_shape=[jax.ShapeDtypeStruct((n, ml_dim), BF16),
                   jax.ShapeDtypeStruct((n_seq, 2, ML_HEADS, hd, hd), F32),
                   jax.ShapeDtypeStruct((n_seq, 2, ML_HEADS, 1, hd), F32),
                   jax.ShapeDtypeStruct((n_seq, 2, ML_HEADS, 1, GATE_LANES), F32)],
        scratch_shapes=[pltpu.VMEM((2, lc, lc), F32), pltpu.VMEM((rows, hd), F32)],
        compiler_params=_cparams(2),
        name="mlstm_ctx",
    )(obf, obf, obf, of32, g, gt, gn_g)


def _mlstm_lat(obf, of32, g, gt, gn_g, c0, n0, m0, *, n_seq):
    n = obf.shape[1]
    ml_dim = gn_g.shape[1]
    hd = ml_dim // ML_HEADS
    lc = MLSTM_CHUNK
    rows = n // n_seq
    n_chunks = rows // lc
    assert n_chunks % 2 == 0
    kern = functools.partial(_mlstm_lat_kernel, n_chunks=n_chunks, scale=hd ** -0.5)
    return pl.pallas_call(
        kern,
        grid=(n_seq, ML_HEADS),
        in_specs=_mlstm_in_specs(rows, hd) + [
                  pl.BlockSpec((1, 2, 1, hd, hd), lambda s, h: (s, 0, h, 0, 0)),
                  pl.BlockSpec((1, 2, 1, hd, GATE_LANES), lambda s, h: (s, 0, h, 0, 0)),
                  pl.BlockSpec((1, 2, 1, 1, GATE_LANES), lambda s, h: (s, 0, h, 0, 0))],
        out_specs=pl.BlockSpec((rows, hd), lambda s, h: (s, h)),
        out_shape=jax.ShapeDtypeStruct((n, ml_dim), BF16),
        scratch_shapes=[pltpu.VMEM((2, hd, hd + GATE_LANES), F32),
                        pltpu.VMEM((2, 1, GATE_LANES), F32),
                        pltpu.VMEM((rows, hd), F32),
                        pltpu.VMEM((2, lc, lc), F32)],
        compiler_params=_cparams(2),
        name="mlstm_lat",
    )(obf, obf, obf, of32, g, gt, gn_g, c0, n0, m0)


def _merge_kernel(hrg_ref, hml_ref, gma_ref, gmb_ref, x_ref, mod_ref, wrg_ref, wml_ref, wout_ref,
                  bm_ref, lng_ref, lnb_ref, o_ref, *, d, alpha, row_base, tiles_per_row):
    r = _mod_row(pl.program_id(0), row_base, tiles_per_row)
    gate1 = mod_ref[pl.ds(r, 1), 2 * d:3 * d]
    rows_per_group = x_ref.shape[0] // MERGE_ROW_GROUPS
    for grp in range(MERGE_ROW_GROUPS):
        rows = slice(grp * rows_per_group, (grp + 1) * rows_per_group)
        y_rg = jnp.dot(hrg_ref[rows, :], wrg_ref[...], preferred_element_type=F32)
        y_ml = jnp.dot(hml_ref[rows, :], wml_ref[...], preferred_element_type=F32)
        g_rg = jax.nn.sigmoid(gma_ref[rows, :] + bm_ref[:, 0:d])
        g_ml = jax.nn.sigmoid(gmb_ref[rows, :] + bm_ref[:, d:2 * d])
        merged = (g_rg * y_rg + g_ml * y_ml).astype(BF16)
        mix = jnp.dot(merged, wout_ref[...], preferred_element_type=F32)
        o_ref[rows, :] = _ln(alpha * x_ref[rows, :] + gate1 * mix) * lng_ref[...] + lnb_ref[...]


def _merge(hrg, hml, of32, x, mod, w_rg, w_ml, w_out, b_merge, ln_g, ln_b, *, alpha, row_base,
           rows_per_mod, tm):
    n, d = x.shape
    tiles_per_row = _tiles_per_row(rows_per_mod, tm)
    kern = functools.partial(_merge_kernel, d=d, alpha=alpha, row_base=row_base, tiles_per_row=tiles_per_row)
    full = lambda shape: pl.BlockSpec(shape, lambda i: (0,) * len(shape))
    resident = lambda shape: pl.BlockSpec(shape, lambda i: (0, 0), pipeline_mode=pl.Buffered(1))
    return pl.pallas_call(
        kern,
        grid=(n // tm,),
        in_specs=[pl.BlockSpec((tm, d), lambda i: (i, 0)),
                  pl.BlockSpec((tm, d), lambda i: (i, 0)),
                  pl.BlockSpec((None, tm, d), lambda i: (3, i, 0)),
                  pl.BlockSpec((None, tm, d), lambda i: (4, i, 0)),
                  pl.BlockSpec((tm, d), lambda i: (i, 0)),
                  full(mod.shape), resident((d, d)), resident((d, d)), resident((d, d)),
                  full((1, 2 * d)), full((1, d)), full((1, d))],
        out_specs=pl.BlockSpec((tm, d), lambda i: (i, 0)),
        out_shape=jax.ShapeDtypeStruct((n, d), F32),
        compiler_params=_cparams(1),
        name="merge",
    )(hrg, hml, of32, of32, x, mod, w_rg, w_ml, w_out, b_merge, ln_g, ln_b)


def _mlp_kernel(x_ref, mod_ref, wfc_ref, bfc_ref, wpj_ref, bpj_ref, lng_ref, lnb_ref, o_ref,
                u_ref, *, d, tf, alpha, row_base, tiles_per_row):
    r = _mod_row(pl.program_id(0), row_base, tiles_per_row)
    shift = mod_ref[pl.ds(r, 1), 3 * d:4 * d]
    scale = mod_ref[pl.ds(r, 1), 4 * d:5 * d]
    gate2 = mod_ref[pl.ds(r, 1), 5 * d:6 * d]
    rows_per_group = x_ref.shape[0] // MLP_ROW_GROUPS
    for grp in range(MLP_ROW_GROUPS):
        rows = slice(grp * rows_per_group, (grp + 1) * rows_per_group)
        u_ref[rows, :] = (_ln(x_ref[rows, :]) * (1.0 + scale) + shift).astype(BF16)
        acc = None
        for f in range(wfc_ref.shape[1] // tf):
            cols = slice(f * tf, (f + 1) * tf)
            hid = jnp.dot(u_ref[rows, :], wfc_ref[:, cols], preferred_element_type=F32) + bfc_ref[:, cols]
            hid = jnp.square(jnp.maximum(hid, 0.0)).astype(BF16)
            part = jnp.dot(hid, wpj_ref[cols, :], preferred_element_type=F32)
            acc = part if acc is None else acc + part
        y = alpha * x_ref[rows, :] + gate2 * (acc + bpj_ref[...])
        o_ref[rows, :] = _ln(y) * lng_ref[...] + lnb_ref[...]


def _mlp(x, mod, w_fc, b_fc, w_proj, b_proj, ln_g, ln_b, *, alpha, row_base, rows_per_mod, tm):
    n, d = x.shape
    tiles_per_row = _tiles_per_row(rows_per_mod, tm)
    d_ff = w_fc.shape[1]
    tf = 1024
    assert d_ff % tf == 0
    kern = functools.partial(_mlp_kernel, d=d, tf=tf, alpha=alpha, row_base=row_base,
                             tiles_per_row=tiles_per_row)
    resident = lambda shape: pl.BlockSpec(shape, lambda i: (0, 0), pipeline_mode=pl.Buffered(1))
    return pl.pallas_call(
        kern,
        grid=(n // tm,),
        in_specs=[pl.BlockSpec((tm, d), lambda i: (i, 0)),
                  pl.BlockSpec(mod.shape, lambda i: (0, 0)),
                  resident((d, d_ff)),
                  pl.BlockSpec((1, d_ff), lambda i: (0, 0)),
                  resident((d_ff, d)),
                  pl.BlockSpec((1, d), lambda i: (0, 0)),
                  pl.BlockSpec((1, d), lambda i: (0, 0)),
                  pl.BlockSpec((1, d), lambda i: (0, 0))],
        out_specs=pl.BlockSpec((tm, d), lambda i: (i, 0)),
        out_shape=jax.ShapeDtypeStruct((n, d), F32),
        scratch_shapes=[pltpu.VMEM((tm, d), BF16)],
        compiler_params=_cparams(1),
        name="mlp",
    )(x, mod, w_fc, b_fc, w_proj, b_proj, ln_g, ln_b)


def _rg_gate_weights(wa, ba, wx, bx, lam):
    ct = RG_CT
    d_rnn = ba.shape[-1]
    n_ct = d_rnn // ct
    per = ct // RG_BW

    def tile_blockdiag(w):
        wt = w.reshape(n_ct, per, RG_BW, RG_BW)
        eye = jnp.eye(per, dtype=w.dtype)
        return jnp.einsum('cpkj,pq->cpkqj', wt, eye).reshape(n_ct, ct, ct)

    w_cat = (0.5 * jnp.concatenate([tile_blockdiag(wa[0]), tile_blockdiag(wx[0]),
                                    tile_blockdiag(wa[1]), tile_blockdiag(wx[1])], axis=-1)).astype(BF16)
    tiles = lambda b: b.reshape(n_ct, 1, ct)
    b_cat = 0.5 * jnp.concatenate([tiles(ba[0]), tiles(bx[0]), tiles(ba[1]), tiles(bx[1])], axis=-1)
    lam_cat = jnp.concatenate([tiles(lam[0]), tiles(lam[1])], axis=-1)
    return w_cat, b_cat, lam_cat


def kernel(x_prompt, x_sample, state_rglru_h, state_mlstm_C, state_mlstm_n, state_mlstm_m, c, c_ctx, w_ada, b_ada, w_in, rg_conv_w, rg_conv_b, rg_wa, rg_ba, rg_wx, rg_bx, rg_lambda, w_rg_proj, ml_b_igate, ml_b_fgate, ml_gn_g, w_ml_proj, b_merge, w_out, ln_g, ln_b, w_fc, b_fc, w_proj, b_proj):
    bp, seq, d = x_prompt.shape
    bl, dec_seq, _ = x_sample.shape
    depth = w_in.shape[0]
    d_rnn = rg_conv_w.shape[-1]
    ml_dim = ml_gn_g.shape[-1]
    hd = ml_dim // ML_HEADS
    alpha = (2 * depth) ** 0.25
    d_main = 2 * d_rnn + 4 * ml_dim + 2 * d
    assert seq == MLSTM_CHUNK == SCAN_SEG and dec_seq % MLSTM_CHUNK == 0
    assert d_rnn == d and ml_dim == d and w_in.shape[-1] == d_main + 4 * ML_HEADS

    tm = 1024
    tm_proj = 512
    xp = x_prompt.reshape(bp * seq, d)
    xs = x_sample.reshape(bl * dec_seq, d)
    cond = jnp.concatenate([c_ctx[None, :], c], axis=0)
    zero_h = jnp.zeros((2, bp, d_rnn), F32)
    new_h, new_c, new_n, new_m = [], [], [], []
    for l in range(depth):
        mod = _ada(cond, w_ada[l], b_ada[l])
        w_main = w_in[l].astype(BF16)
        w_gate = jnp.zeros((d, GATE_LANES), BF16).at[:, :4 * ML_HEADS].set(w_in[l][:, d_main:].astype(BF16))
        b_gate = jnp.concatenate([ml_b_igate[l].reshape(-1), ml_b_fgate[l].reshape(-1)]).reshape(-1, 1)
        w_cat, b_cat, lam_cat = _rg_gate_weights(rg_wa[l], rg_ba[l], rg_wx[l], rg_bx[l], rg_lambda[l])
        conv_b = rg_conv_b[l].reshape(1, d_rnn)
        gn_g = ml_gn_g[l].reshape(1, ml_dim)
        w_rg, w_ml, w_o = (w_rg_proj[l].astype(BF16), w_ml_proj[l].astype(BF16), w_out[l].astype(BF16))
        w_fc_b, w_pj_b = w_fc[l].astype(BF16), w_proj[l].astype(BF16)
        bm = b_merge[l].reshape(1, 2 * d)
        lng0, lnb0 = ln_g[l, 0].reshape(1, d), ln_b[l, 0].reshape(1, d)
        lng1, lnb1 = ln_g[l, 1].reshape(1, d), ln_b[l, 1].reshape(1, d)
        bfc, bpj = b_fc[l].reshape(1, -1), b_proj[l].reshape(1, d)

        def tail(x, hrg, hml, of32, row_base, rows_per_mod):
            x1 = _merge(hrg, hml, of32, x, mod, w_rg, w_ml, w_o, bm, lng0, lnb0, alpha=alpha,
                        row_base=row_base, rows_per_mod=rows_per_mod, tm=tm)
            return _mlp(x1, mod, w_fc_b, bfc, w_pj_b, bpj, lng1, lnb1, alpha=alpha,
                        row_base=row_base, rows_per_mod=rows_per_mod, tm=tm)

        of32, obf, g, gt = _in_proj(xp, mod, w_main, w_gate, b_gate, rg_conv_w[l], conv_b, conv_len=seq,
                                    row_base=0, rows_per_mod=None, tm=tm_proj)
        hrg, h_fin = _rglru(of32, w_cat, b_cat, lam_cat, zero_h, n_seq=bp, seq_len=seq, want_final=True)
        hml, c_fin, n_fin, m_fin = _mlstm_ctx(obf, of32, g, gt, gn_g, n_seq=bp)
        xp = tail(xp, hrg, hml, of32, 0, None)
        new_h.append(jnp.transpose(h_fin, (1, 0, 2)))
        new_c.append(c_fin)
        new_n.append(n_fin.reshape(bp, 2, ML_HEADS, hd))
        new_m.append(m_fin[..., 0, 0])

        of32, obf, g, gt = _in_proj(xs, mod, w_main, w_gate, b_gate, rg_conv_w[l], conv_b, conv_len=GRID_W,
                                    row_base=1, rows_per_mod=dec_seq, tm=tm_proj)
        h0 = jnp.transpose(state_rglru_h[:, l], (1, 0, 2))
        (hrg,) = _rglru(of32, w_cat, b_cat, lam_cat, h0, n_seq=bl, seq_len=dec_seq, want_final=False)
        c0 = state_mlstm_C[:, l]
        n0 = jnp.broadcast_to(state_mlstm_n[:, l][..., None], (bl, 2, ML_HEADS, hd, GATE_LANES))
        m0 = jnp.broadcast_to(state_mlstm_m[:, l].reshape(bl, 2, ML_HEADS, 1, 1),
                              (bl, 2, ML_HEADS, 1, GATE_LANES))
        hml = _mlstm_lat(obf, of32, g, gt, gn_g, c0, n0, m0, n_seq=bl)
        xs = tail(xs, hrg, hml, of32, 1, dec_seq)

    def stack(parts):
        return parts[0][:, None] if len(parts) == 1 else jnp.stack(parts, axis=1)

    return (xp.reshape(bp, seq, d), xs.reshape(bl, dec_seq, d),
            stack(new_h), stack(new_c), stack(new_n), stack(new_m))
```

```python
import functools

import jax
import jax.numpy as jnp
from jax import lax
from jax.experimental import pallas as pl
from jax.experimental.pallas import tpu as pltpu

F32 = jnp.float32
BF16 = jnp.bfloat16

LN_EPS = 1e-5
LOG2E = 1.4426950408889634
LN2 = 0.6931471805599453
RG_C = 8.0
RG_BW = 64
CONV_W = 4
GRID_W = 64
ML_HEADS = 4
MLSTM_CHUNK = 256
SCAN_SEG = 256
SUBLANES = 8
SCAN_PITCH = SCAN_SEG + 4
SCAN_GROUP = 8
RG_CT = 128
GATE_LANES = 128
VMEM_LIMIT = 52 * 1024 * 1024
MERGE_ROW_GROUPS = 4
MLP_ROW_GROUPS = 2
IN_PROJ_VMEM_LIMIT = 58 * 1024 * 1024


def _cparams(n_axes):
    return pltpu.CompilerParams(
        dimension_semantics=("arbitrary",) * n_axes, vmem_limit_bytes=VMEM_LIMIT)


def _log_sigmoid(x):
    return jnp.minimum(x, 0.0) - jnp.log1p(jnp.exp(-jnp.abs(x)))


def _ln(x):
    mu = jnp.mean(x, -1, keepdims=True)
    xc = x - mu
    var = jnp.mean(xc * xc, -1, keepdims=True)
    return xc * lax.rsqrt(var + LN_EPS)


def _mod_row(i, row_base, tiles_per_row):
    if tiles_per_row is None:
        return row_base
    return row_base + lax.div(i, jnp.int32(tiles_per_row))


def _ada_kernel(ct_ref, w_ref, b_ref, o_ref, *, n_rows):
    w = w_ref[...]
    ct = ct_ref[...]
    s = ct * jax.nn.sigmoid(ct)
    o_ref[...] = jnp.zeros(o_ref.shape, F32)
    for r in range(n_rows):
        o_ref[r:r + 1, :] = jnp.sum(w * s[:, r:r + 1], axis=0, keepdims=True) + b_ref[...]


def _ada(cond, w_ada, b_ada):
    n_rows, d = cond.shape
    assert n_rows <= 8
    ct = jnp.zeros((d, 8), F32).at[:, :n_rows].set(cond.T)
    n_out = w_ada.shape[1]
    tn = 2048
    assert n_out % tn == 0
    return pl.pallas_call(
        functools.partial(_ada_kernel, n_rows=n_rows),
        grid=(n_out // tn,),
        in_specs=[pl.BlockSpec((d, 8), lambda j: (0, 0)),
                  pl.BlockSpec((d, tn), lambda j: (0, j)),
                  pl.BlockSpec((1, tn), lambda j: (0, j))],
        out_specs=pl.BlockSpec((8, tn), lambda j: (0, j)),
        out_shape=jax.ShapeDtypeStruct((8, n_out), F32),
        compiler_params=_cparams(1),
        name="ada_mod",
    )(ct, w_ada, b_ada.reshape(1, n_out))


def _inproj_tile(u_ref, w_ref, wg_ref, bg_ref, cw_ref, cb_ref, of_ref, ob_ref, g_ref, gt_ref, *, d, tm,
                 conv_len):
    lc = MLSTM_CHUNK
    n_gate = 4 * ML_HEADS
    g = jnp.dot(u_ref[...], wg_ref[...], preferred_element_type=F32)
    gt = g.T[0:n_gate, :] + bg_ref[...]
    gt = jnp.where(lax.broadcasted_iota(jnp.int32, gt.shape, 0) >= 2 * ML_HEADS, _log_sigmoid(gt), gt)
    gt = gt * LOG2E
    row = lax.broadcasted_iota(jnp.int32, (n_gate, lc), 0)
    ii = lax.broadcasted_iota(jnp.int32, (lc, lc), 0)
    jj = lax.broadcasted_iota(jnp.int32, (lc, lc), 1)
    incl_before = (ii <= jj).astype(BF16)
    incl_after = (ii >= jj).astype(BF16)
    parts = []
    for c in range(tm // lc):
        x = gt[:, c * lc:(c + 1) * lc]
        hi = x.astype(BF16)
        r1 = x - hi.astype(F32)
        mid = r1.astype(BF16)
        lo = (r1 - mid.astype(F32)).astype(BF16)
        split = jnp.concatenate([hi, mid, lo], axis=0)

        def cumsum(tri):
            res = jnp.dot(split, tri, preferred_element_type=F32)
            return res[0:n_gate] + res[n_gate:2 * n_gate] + res[2 * n_gate:3 * n_gate]

        parts.append(jnp.where(row < 2 * ML_HEADS, x,
                               jnp.where(row < 3 * ML_HEADS, cumsum(incl_before), cumsum(incl_after))))
    gfin = jnp.concatenate(parts, axis=1)
    gt_ref[...] = gfin
    pad = jnp.zeros((GATE_LANES - n_gate, tm), F32)
    g_ref[...] = jnp.concatenate([gfin, pad], axis=0).T

    def col_tile(t):
        return jnp.dot(u_ref[...], w_ref[:, t * d:(t + 1) * d], preferred_element_type=F32)

    xr = col_tile(0)
    pos = jnp.bitwise_and(lax.broadcasted_iota(jnp.int32, (tm, 1), 0), conv_len - 1)
    cw = cw_ref[...]
    of_ref[0] = (cw[0:1] * jnp.where(pos >= 2, pltpu.roll(xr, 2, 0), 0.0)
                 + cw[1:2] * jnp.where(pos >= 1, pltpu.roll(xr, 1, 0), 0.0)
                 + cw[2:3] * xr
                 + cw[3:4] * jnp.where(pos <= conv_len - 2, pltpu.roll(xr, tm - 1, 0), 0.0)
                 + cb_ref[...])
    of_ref[1] = jax.nn.gelu(col_tile(1))
    of_ref[2] = jax.nn.sigmoid(col_tile(5))
    of_ref[3] = col_tile(6)
    of_ref[4] = col_tile(7)
    for slab, t in enumerate((2, 3, 4)):
        ob_ref[slab] = col_tile(t).astype(BF16)


def _inproj_kernel(x0_ref, xn_ref, mod_ref, w_ref, wg_ref, bg_ref, cw_ref, cb_ref,
                   of_ref, ob_ref, g_ref, gt_ref, ua_ref, ub_ref, *, d, tm, conv_len, row_base, tiles_per_row):
    i = pl.program_id(0)
    last = pl.num_programs(0) - 1

    def ln_mod(x_blk, tile):
        r = _mod_row(tile, row_base, tiles_per_row)
        shift = mod_ref[pl.ds(r, 1), 0:d]
        scale = mod_ref[pl.ds(r, 1), d:2 * d]
        return (_ln(x_blk[...]) * (1.0 + scale) + shift).astype(BF16)

    @pl.when(i == 0)
    def _():
        ua_ref[...] = ln_mod(x0_ref, i)

    def step(u_ref, u_next_ref):
        u_next_ref[...] = ln_mod(xn_ref, jnp.minimum(i + 1, last))
        _inproj_tile(u_ref, w_ref, wg_ref, bg_ref, cw_ref, cb_ref, of_ref, ob_ref, g_ref, gt_ref,
                     d=d, tm=tm, conv_len=conv_len)

    parity = jnp.bitwise_and(i, 1)
    pl.when(parity == 0)(lambda: step(ua_ref, ub_ref))
    pl.when(parity == 1)(lambda: step(ub_ref, ua_ref))


def _tiles_per_row(rows_per_mod, tm):
    if rows_per_mod is None:
        return None
    assert rows_per_mod % tm == 0
    return rows_per_mod // tm


def _in_proj(x, mod, w_main, w_gate, b_gate, conv_w, conv_b, *, conv_len, row_base, rows_per_mod, tm):
    n, d = x.shape
    tiles_per_row = _tiles_per_row(rows_per_mod, tm)
    assert w_main.shape[1] >= 8 * d and conv_w.shape == (CONV_W, d)
    assert tm % conv_len == 0 and (conv_len & (conv_len - 1)) == 0
    n_tiles = n // tm
    kern = functools.partial(_inproj_kernel, d=d, tm=tm, conv_len=conv_len, row_base=row_base,
                             tiles_per_row=tiles_per_row)
    return pl.pallas_call(
        kern,
        grid=(n_tiles,),
        in_specs=[pl.BlockSpec((tm, d), lambda i: (0, 0)),
                  pl.BlockSpec((tm, d), lambda i: (jnp.minimum(i + 1, n_tiles - 1), 0)),
                  pl.BlockSpec(mod.shape, lambda i: (0, 0)),
                  pl.BlockSpec(w_main.shape, lambda i: (0, 0), pipeline_mode=pl.Buffered(1)),
                  pl.BlockSpec((d, GATE_LANES), lambda i: (0, 0)),
                  pl.BlockSpec((4 * ML_HEADS, 1), lambda i: (0, 0)),
                  pl.BlockSpec((CONV_W, d), lambda i: (0, 0)),
                  pl.BlockSpec((1, d), lambda i: (0, 0))],
        out_specs=[pl.BlockSpec((5, tm, d), lambda i: (0, i, 0)),
                   pl.BlockSpec((3, tm, d), lambda i: (0, i, 0)),
                   pl.BlockSpec((tm, GATE_LANES), lambda i: (i, 0)),
                   pl.BlockSpec((4 * ML_HEADS, tm), lambda i: (0, i))],
        out_shape=[jax.ShapeDtypeStruct((5, n, d), F32),
                   jax.ShapeDtypeStruct((3, n, d), BF16),
                   jax.ShapeDtypeStruct((n, GATE_LANES), F32),
                   jax.ShapeDtypeStruct((4 * ML_HEADS, n), F32)],
        scratch_shapes=[pltpu.VMEM((tm, d), BF16), pltpu.VMEM((tm, d), BF16)],
        compiler_params=pltpu.CompilerParams(dimension_semantics=("arbitrary",),
                                             vmem_limit_bytes=IN_PROJ_VMEM_LIMIT),
        name="in_proj",
    )(x, x, mod, w_main, w_gate, b_gate, conv_w, conv_b)


def _rglru_kernel(xc_ref, gz_ref, w_ref, b_ref, lam_ref, h0_ref, *rest, n_vseq, n_seg, want_final):
    if want_final:
        out_ref, hfin_ref = rest[:2]
        rest = rest[2:]
    else:
        out_ref = rest[0]
        rest = rest[1:]
    a_f, b_f, a_b, b_b = rest[:4]
    rest = rest[4:]
    seg = SCAN_SEG
    pitch = SCAN_PITCH
    ct = xc_ref.shape[1]
    ch = seg
    la = (0.5 * RG_C * LOG2E) * _log_sigmoid(lam_ref[0])
    la_f, la_b = la[:, :ct], la[:, ct:]
    w = w_ref[0]
    bias = b_ref[0]

    def for_chunk_groups(fn):
        def trip(k, carry):
            base = pl.multiple_of(k * (SCAN_GROUP * pitch), SUBLANES)
            for j in range(SCAN_GROUP):
                fn(k * SCAN_GROUP + j, base + j * pitch)
            return carry

        lax.fori_loop(0, n_vseq // SCAN_GROUP, trip, 0)

    def gate_chunk(ci, srow):
        r0 = pl.multiple_of(ci * ch, ch)
        xc = xc_ref[pl.ds(r0, ch), :]
        gates = jnp.dot(xc.astype(BF16), w, preferred_element_type=F32) + bias

        def a_and_b(r_half, i_half, la_half):
            a = jnp.exp2(la_half + la_half * jnp.tanh(r_half))
            y = jnp.clip(1.0 - a * a, 0.0, 1.0)
            gain = jnp.where(y > 0.0, y * lax.rsqrt(y), 0.0)
            half_gx = (0.5 * gain) * xc
            return a, half_gx + half_gx * jnp.tanh(i_half)

        af, bf = a_and_b(gates[:, 0:ct], gates[:, ct:2 * ct], la_f)
        ab, bb = a_and_b(gates[:, 2 * ct:3 * ct], gates[:, 3 * ct:4 * ct], la_b)
        s0 = pl.ds(srow, ch)
        a_f[s0, :] = af
        b_f[s0, :] = bf
        a_b[s0, :] = ab
        b_b[s0, :] = bb

    for_chunk_groups(gate_chunk)

    def slab_idx(t):
        return pl.ds(t, n_vseq, stride=pitch)

    def slab(ref, t):
        return ref[slab_idx(t), :]

    if n_seg == 1:
        init_f = h0_ref[0]
        init_b = h0_ref[1]
    else:
        e_f, p_f, e_b, p_b, cin_f, cin_b = rest
        n_real = n_vseq // n_seg
        zero = jnp.zeros((n_vseq, ct), F32)
        one = jnp.ones((n_vseq, ct), F32)

        def local_step(t, carry):
            ef, pf, eb, pb = carry
            tb = seg - 1 - t
            a = slab(a_f, t)
            ef = a * ef + slab(b_f, t)
            pf = a * pf
            a2 = slab(a_b, tb)
            eb = a2 * eb + slab(b_b, tb)
            pb = a2 * pb
            return ef, pf, eb, pb

        ef, pf, eb, pb = lax.fori_loop(0, seg, local_step, (zero, one, zero, one), unroll=8)
        e_f[...] = ef
        p_f[...] = pf
        e_b[...] = eb
        p_b[...] = pb
        carry = h0_ref[0]
        for s in range(n_seg):
            idx = pl.ds(s, n_real, stride=n_seg)
            cin_f[idx, :] = carry
            carry = e_f[idx, :] + p_f[idx, :] * carry
        carry = h0_ref[1]
        for s in reversed(range(n_seg)):
            idx = pl.ds(s, n_real, stride=n_seg)
            cin_b[idx, :] = carry
            carry = e_b[idx, :] + p_b[idx, :] * carry
        init_f = cin_f[...]
        init_b = cin_b[...]

    def scan_step(t, carry):
        hf, hb = carry
        tb = seg - 1 - t
        hf = slab(a_f, t) * hf + slab(b_f, t)
        b_f[slab_idx(t), :] = hf
        hb = slab(a_b, tb) * hb + slab(b_b, tb)
        b_b[slab_idx(tb), :] = hb
        return hf, hb

    hf_last, hb_last = lax.fori_loop(0, seg, scan_step, (init_f, init_b), unroll=8)
    if want_final:
        hfin_ref[0] = hf_last
        hfin_ref[1] = hb_last

    def out_chunk(ci, srow):
        r0 = pl.multiple_of(ci * ch, ch)
        h = b_f[pl.ds(srow, ch), :] + b_b[pl.ds(srow, ch), :]
        out_ref[pl.ds(r0, ch), :] = (h * gz_ref[pl.ds(r0, ch), :]).astype(BF16)

    for_chunk_groups(out_chunk)


def _rglru(of32, w_cat, b_cat, lam_cat, h0, *, n_seq, seq_len, want_final):
    n, d_rnn = of32.shape[1:]
    ct = RG_CT
    n_ct = d_rnn // ct
    n_seg = seq_len // SCAN_SEG
    n_vseq = n_seq * n_seg
    assert n_vseq * SCAN_SEG == n and n_vseq % SCAN_GROUP == 0
    kern = functools.partial(_rglru_kernel, n_vseq=n_vseq, n_seg=n_seg, want_final=want_final)
    out_specs = [pl.BlockSpec((n, ct), lambda c: (0, c))]
    out_shape = [jax.ShapeDtypeStruct((n, d_rnn), BF16)]
    if want_final:
        out_specs.append(pl.BlockSpec((2, n_seq, ct), lambda c: (0, 0, c)))
        out_shape.append(jax.ShapeDtypeStruct((2, n_seq, d_rnn), F32))
    scratch = [pltpu.VMEM((n_vseq * SCAN_PITCH, ct), F32) for _ in range(4)]
    if n_seg > 1:
        scratch += [pltpu.VMEM((n_vseq, ct), F32) for _ in range(6)]
    return pl.pallas_call(
        kern,
        grid=(n_ct,),
        in_specs=[pl.BlockSpec((None, n, ct), lambda c: (0, 0, c)),
                  pl.BlockSpec((None, n, ct), lambda c: (1, 0, c)),
                  pl.BlockSpec((1, ct, 4 * ct), lambda c: (c, 0, 0)),
                  pl.BlockSpec((1, 1, 4 * ct), lambda c: (c, 0, 0)),
                  pl.BlockSpec((1, 1, 2 * ct), lambda c: (c, 0, 0)),
                  pl.BlockSpec((2, n_seq, ct), lambda c: (0, 0, c))],
        out_specs=out_specs,
        out_shape=out_shape,
        scratch_shapes=scratch,
        compiler_params=_cparams(1),
        name="rglru_ctx" if want_final else "rglru_lat",
    )(of32, of32, w_cat, b_cat, lam_cat, h0)


def _causal_bias(mask_ref):
    lc = mask_ref.shape[1]
    ii = lax.broadcasted_iota(jnp.int32, (lc, lc), 0)
    jj = lax.broadcasted_iota(jnp.int32, (lc, lc), 1)
    mask_ref[0] = jnp.where(jj <= ii, 0.0, -jnp.inf)
    mask_ref[1] = jnp.where(jj >= ii, 0.0, -jnp.inf)


def _with_ones(v):
    return jnp.concatenate([v, jnp.ones((v.shape[0], GATE_LANES), v.dtype)], axis=1)


def _rep(col):
    return jnp.broadcast_to(col, (col.shape[0], GATE_LANES))


def _wide(rep, width):
    return jnp.concatenate([rep] * (width // GATE_LANES), axis=1)


def _mlstm_unit(direction, head, q, k, v, v1, gc, gt_ref, r0, mask_ref, state):
    lc, hd = q.shape
    col_i = direction * ML_HEADS + head
    col_c = 2 * ML_HEADS + col_i
    lane = lax.broadcasted_iota(jnp.int32, (1, GATE_LANES), 1)
    ig_c = _rep(jnp.sum(jnp.where(lane == col_i, gc, 0.0), -1, keepdims=True))
    cum_c = _rep(jnp.sum(jnp.where(lane == col_c, gc, 0.0), -1, keepdims=True))
    ig_row = gt_ref[pl.ds(col_i, 1), pl.ds(r0, lc)]
    cum_row = gt_ref[pl.ds(col_c, 1), pl.ds(r0, lc)]
    dmat = (_wide(cum_c, lc) + (ig_row - cum_row)) + mask_ref[direction]
    if state is None:
        m_prev = 0.0
    else:
        s_prev, m_prev = state
    inter = cum_c + m_prev
    m_row = jnp.maximum(_rep(jnp.max(dmat, -1, keepdims=True)), inter)
    wts = jnp.exp2(dmat - _wide(m_row, lc))
    s = lax.dot_general(q, k, (((1,), (1,)), ((), ())), preferred_element_type=F32) * wts
    if state is None:
        num = jnp.dot(s.astype(BF16), v, preferred_element_type=F32)
        den = _rep(jnp.sum(s, -1, keepdims=True))
        h = num * _wide(1.0 / jnp.maximum(jnp.abs(den), jnp.exp2(-m_row)), hd)
    else:
        w_inter = jnp.exp2(inter - m_row)
        nd = (jnp.dot(s.astype(BF16), v1, preferred_element_type=F32)
              + _wide(w_inter, hd + GATE_LANES) * jnp.dot(q, s_prev.astype(BF16), preferred_element_type=F32))
        rden = 1.0 / jnp.maximum(jnp.abs(nd[:, hd:]), jnp.exp2(-m_row))
        h = nd[:, :hd] * _wide(rden, hd)
    btot = cum_c[lc - 1:lc] if direction == 0 else cum_c[0:1]
    wk_log = btot - cum_c + ig_c
    m_new = jnp.maximum(btot + m_prev, jnp.max(wk_log, 0, keepdims=True))
    kw = k.astype(F32) * _wide(jnp.exp2(wk_log - m_new), hd)
    t_lhs = (((0,), (0,)), ((), ()))
    if state is None:
        c_new = lax.dot_general(kw.astype(BF16), v, t_lhs, preferred_element_type=F32)
        return h, c_new, jnp.sum(kw, 0, keepdims=True), m_new
    decay = _wide(jnp.exp2(btot + m_prev - m_new), hd + GATE_LANES)
    s_new = decay * s_prev + lax.dot_general(kw.astype(BF16), v1, t_lhs, preferred_element_type=F32)
    return h, s_new, None, m_new


HEAD_OUT_ROWS = 1024


def _head_out_all(h_acc, og_ref, gn_ref, out_ref):
    gn = gn_ref[...]
    rows = h_acc.shape[0]
    step = min(rows, HEAD_OUT_ROWS)
    for r0 in range(0, rows, step):
        rs = slice(r0, r0 + step)
        out_ref[rs, :] = (og_ref[rs, :] * (_ln(h_acc[rs, :]) * gn)).astype(BF16)


def _mlstm_ctx_kernel(q_ref, k_ref, v_ref, o_ref, g_ref, gt_ref, gn_ref,
                      out_ref, cfin_ref, nfin_ref, mfin_ref, mask_ref, h_acc, *, n_sub, scale):
    head = pl.program_id(1)
    lc = MLSTM_CHUNK
    _causal_bias(mask_ref)

    def body(s, carry):
        r0 = pl.multiple_of(s * lc, lc)
        rs = pl.ds(r0, lc)
        q, k, v, gc = q_ref[rs, :] * scale, k_ref[rs, :], v_ref[rs, :], g_ref[rs, :]
        hsum = None
        for direction in (0, 1):
            h, c_new, n_new, m_new = _mlstm_unit(direction, head, q, k, v, None, gc, gt_ref, r0, mask_ref, None)
            cfin_ref[s, direction, 0] = c_new
            nfin_ref[s, direction, 0] = n_new
            mfin_ref[s, direction, 0] = m_new * LN2
            hsum = h if hsum is None else hsum + h
        h_acc[rs, :] = hsum
        return carry

    lax.fori_loop(0, n_sub, body, 0, unroll=2)
    _head_out_all(h_acc, o_ref, gn_ref, out_ref)


def _mlstm_lat_kernel(q_ref, k_ref, v_ref, o_ref, g_ref, gt_ref, gn_ref, c0_ref, n0_ref, m0_ref,
                      out_ref, s_st, m_st, h_acc, mask_ref, *, n_chunks, scale):
    head = pl.program_id(1)
    lc = MLSTM_CHUNK
    hd = q_ref.shape[1]
    _causal_bias(mask_ref)
    for direction in (0, 1):
        s_st[direction, :, 0:hd] = c0_ref[0, direction, 0]
        s_st[direction, :, hd:] = n0_ref[0, direction, 0]
        m_st[direction] = m0_ref[0, direction, 0] * LOG2E

    half = n_chunks // 2

    def make_body(accumulate):
        def body(c, carry):
            for direction in (0, 1):
                cc = c if direction == 0 else n_chunks - 1 - c
                r0 = pl.multiple_of(cc * lc, lc)
                rs = pl.ds(r0, lc)
                state = (s_st[direction], m_st[direction])
                v = v_ref[rs, :]
                h, s_new, _, m_new = _mlstm_unit(direction, head, q_ref[rs, :] * scale, k_ref[rs, :], v,
                                                 _with_ones(v), g_ref[rs, :], gt_ref, r0, mask_ref, state)
                s_st[direction] = s_new
                m_st[direction] = m_new
                if accumulate:
                    h_acc[rs, :] += h
                else:
                    h_acc[rs, :] = h
            return carry
        return body

    lax.fori_loop(0, half, make_body(False), 0, unroll=min(half, 8))
    lax.fori_loop(half, n_chunks, make_body(True), 0, unroll=min(half, 8))
    _head_out_all(h_acc, o_ref, gn_ref, out_ref)


def _mlstm_in_specs(rows, hd):
    slab = lambda idx: pl.BlockSpec((None, rows, hd), lambda s, h: (idx, s, h))
    return [slab(0), slab(1), slab(2), slab(2),
            pl.BlockSpec((rows, GATE_LANES), lambda s, h: (s, 0)),
            pl.BlockSpec((4 * ML_HEADS, rows), lambda s, h: (0, s)),
            pl.BlockSpec((1, hd), lambda s, h: (0, h))]


def _mlstm_ctx(obf, of32, g, gt, gn_g, *, n_seq):
    n = obf.shape[1]
    ml_dim = gn_g.shape[1]
    hd = ml_dim // ML_HEADS
    lc = MLSTM_CHUNK
    assert n == n_seq * lc
    n_sub = 8
    assert n_seq % n_sub == 0
    rows = n_sub * lc
    kern = functools.partial(_mlstm_ctx_kernel, n_sub=n_sub, scale=hd ** -0.5)
    return pl.pallas_call(
        kern,
        grid=(n_seq // n_sub, ML_HEADS),
        in_specs=_mlstm_in_specs(rows, hd),
        out_specs=[pl.BlockSpec((rows, hd), lambda s, h: (s, h)),
                   pl.BlockSpec((n_sub, 2, 1, hd, hd), lambda s, h: (s, 0, h, 0, 0)),
                   pl.BlockSpec((n_sub, 2, 1, 1, hd), lambda s, h: (s, 0, h, 0, 0)),
                   pl.BlockSpec((n_sub, 2, 1, 1, GATE_LANES), lambda s, h: (s, 0, h, 0, 0))],
        out_shape=[jax.ShapeDtypeStruct((n, ml_dim), BF16),
                   jax.ShapeDtypeStruct((n_seq, 2, ML_HEADS, hd, hd), F32),
                   jax.ShapeDtypeStruct((n_seq, 2, ML_HEADS, 1, hd), F32),
                   jax.ShapeDtypeStruct((n_seq, 2, ML_HEADS, 1, GATE_LANES), F32)],
        scratch_shapes=[pltpu.VMEM((2, lc, lc), F32), pltpu.VMEM((rows, hd), F32)],
        compiler_params=_cparams(2),
        name="mlstm_ctx",
    )(obf, obf, obf, of32, g, gt, gn_g)


def _mlstm_lat(obf, of32, g, gt, gn_g, c0, n0, m0, *, n_seq):
    n = obf.shape[1]
    ml_dim = gn_g.shape[1]
    hd = ml_dim // ML_HEADS
    lc = MLSTM_CHUNK
    rows = n // n_seq
    n_chunks = rows // lc
    assert n_chunks % 2 == 0
    kern = functools.partial(_mlstm_lat_kernel, n_chunks=n_chunks, scale=hd ** -0.5)
    return pl.pallas_call(
        kern,
        grid=(n_seq, ML_HEADS),
        in_specs=_mlstm_in_specs(rows, hd) + [
                  pl.BlockSpec((1, 2, 1, hd, hd), lambda s, h: (s, 0, h, 0, 0)),
                  pl.BlockSpec((1, 2, 1, hd, GATE_LANES), lambda s, h: (s, 0, h, 0, 0)),
                  pl.BlockSpec((1, 2, 1, 1, GATE_LANES), lambda s, h: (s, 0, h, 0, 0))],
        out_specs=pl.BlockSpec((rows, hd), lambda s, h: (s, h)),
        out_shape=jax.ShapeDtypeStruct((n, ml_dim), BF16),
        scratch_shapes=[pltpu.VMEM((2, hd, hd + GATE_LANES), F32),
                        pltpu.VMEM((2, 1, GATE_LANES), F32),
                        pltpu.VMEM((rows, hd), F32),
                        pltpu.VMEM((2, lc, lc), F32)],
        compiler_params=_cparams(2),
        name="mlstm_lat",
    )(obf, obf, obf, of32, g, gt, gn_g, c0, n0, m0)


def _merge_kernel(hrg_ref, hml_ref, gma_ref, gmb_ref, x_ref, mod_ref, wrg_ref, wml_ref, wout_ref,
                  bm_ref, lng_ref, lnb_ref, o_ref, *, d, alpha, row_base, tiles_per_row):
    r = _mod_row(pl.program_id(0), row_base, tiles_per_row)
    gate1 = mod_ref[pl.ds(r, 1), 2 * d:3 * d]
    rows_per_group = x_ref.shape[0] // MERGE_ROW_GROUPS
    for grp in range(MERGE_ROW_GROUPS):
        rows = slice(grp * rows_per_group, (grp + 1) * rows_per_group)
        y_rg = jnp.dot(hrg_ref[rows, :], wrg_ref[...], preferred_element_type=F32)
        y_ml = jnp.dot(hml_ref[rows, :], wml_ref[...], preferred_element_type=F32)
        g_rg = jax.nn.sigmoid(gma_ref[rows, :] + bm_ref[:, 0:d])
        g_ml = jax.nn.sigmoid(gmb_ref[rows, :] + bm_ref[:, d:2 * d])
        merged = (g_rg * y_rg + g_ml * y_ml).astype(BF16)
        mix = jnp.dot(merged, wout_ref[...], preferred_element_type=F32)
        o_ref[rows, :] = _ln(alpha * x_ref[rows, :] + gate1 * mix) * lng_ref[...] + lnb_ref[...]


def _merge(hrg, hml, of32, x, mod, w_rg, w_ml, w_out, b_merge, ln_g, ln_b, *, alpha, row_base,
           rows_per_mod, tm):
    n, d = x.shape
    tiles_per_row = _tiles_per_row(rows_per_mod, tm)
    kern = functools.partial(_merge_kernel, d=d, alpha=alpha, row_base=row_base, tiles_per_row=tiles_per_row)
    full = lambda shape: pl.BlockSpec(shape, lambda i: (0,) * len(shape))
    resident = lambda shape: pl.BlockSpec(shape, lambda i: (0, 0), pipeline_mode=pl.Buffered(1))
    return pl.pallas_call(
        kern,
        grid=(n // tm,),
        in_specs=[pl.BlockSpec((tm, d), lambda i: (i, 0)),
                  pl.BlockSpec((tm, d), lambda i: (i, 0)),
                  pl.BlockSpec((None, tm, d), lambda i: (3, i, 0)),
                  pl.BlockSpec((None, tm, d), lambda i: (4, i, 0)),
                  pl.BlockSpec((tm, d), lambda i: (i, 0)),
                  full(mod.shape), resident((d, d)), resident((d, d)), resident((d, d)),
                  full((1, 2 * d)), full((1, d)), full((1, d))],
        out_specs=pl.BlockSpec((tm, d), lambda i: (i, 0)),
        out_shape=jax.ShapeDtypeStruct((n, d), F32),
        compiler_params=_cparams(1),
        name="merge",
    )(hrg, hml, of32, of32, x, mod, w_rg, w_ml, w_out, b_merge, ln_g, ln_b)


def _mlp_kernel(x_ref, mod_ref, wfc_ref, bfc_ref, wpj_ref, bpj_ref, lng_ref, lnb_ref, o_ref,
                u_ref, *, d, alpha, row_base, tiles_per_row):
    r = _mod_row(pl.program_id(0), row_base, tiles_per_row)
    shift = mod_ref[pl.ds(r, 1), 3 * d:4 * d]
    scale = mod_ref[pl.ds(r, 1), 4 * d:5 * d]
    gate2 = mod_ref[pl.ds(r, 1), 5 * d:6 * d]
    rows_per_group = x_ref.shape[0] // MLP_ROW_GROUPS
    for grp in range(MLP_ROW_GROUPS):
        rows = slice(grp * rows_per_group, (grp + 1) * rows_per_group)
        u_ref[rows, :] = (_ln(x_ref[rows, :]) * (1.0 + scale) + shift).astype(BF16)
        hid = jnp.dot(u_ref[rows, :], wfc_ref[...], preferred_element_type=F32) + bfc_ref[...]
        hid = jnp.square(jnp.maximum(hid, 0.0)).astype(BF16)
        acc = jnp.dot(hid, wpj_ref[...], preferred_element_type=F32)
        y = alpha * x_ref[rows, :] + gate2 * (acc + bpj_ref[...])
        o_ref[rows, :] = _ln(y) * lng_ref[...] + lnb_ref[...]


def _mlp(x, mod, w_fc, b_fc, w_proj, b_proj, ln_g, ln_b, *, alpha, row_base, rows_per_mod, tm):
    n, d = x.shape
    tiles_per_row = _tiles_per_row(rows_per_mod, tm)
    d_ff = w_fc.shape[1]
    kern = functools.partial(_mlp_kernel, d=d, alpha=alpha, row_base=row_base, tiles_per_row=tiles_per_row)
    resident = lambda shape: pl.BlockSpec(shape, lambda i: (0, 0), pipeline_mode=pl.Buffered(1))
    return pl.pallas_call(
        kern,
        grid=(n // tm,),
        in_specs=[pl.BlockSpec((tm, d), lambda i: (i, 0)),
                  pl.BlockSpec(mod.shape, lambda i: (0, 0)),
                  resident((d, d_ff)),
                  pl.BlockSpec((1, d_ff), lambda i: (0, 0)),
                  resident((d_ff, d)),
                  pl.BlockSpec((1, d), lambda i: (0, 0)),
                  pl.BlockSpec((1, d), lambda i: (0, 0)),
                  pl.BlockSpec((1, d), lambda i: (0, 0))],
        out_specs=pl.BlockSpec((tm, d), lambda i: (i, 0)),
        out_shape=jax.ShapeDtypeStruct((n, d), F32),
        scratch_shapes=[pltpu.VMEM((tm, d), BF16)],
        compiler_params=_cparams(1),
        name="mlp",
    )(x, mod, w_fc, b_fc, w_proj, b_proj, ln_g, ln_b)


def _rg_gate_weights(wa, ba, wx, bx, lam):
    ct = RG_CT
    d_rnn = ba.shape[-1]
    n_ct = d_rnn // ct
    per = ct // RG_BW

    def tile_blockdiag(w):
        wt = w.reshape(n_ct, per, RG_BW, RG_BW)
        eye = jnp.eye(per, dtype=w.dtype)
        return jnp.einsum('cpkj,pq->cpkqj', wt, eye).reshape(n_ct, ct, ct)

    w_cat = (0.5 * jnp.concatenate([tile_blockdiag(wa[0]), tile_blockdiag(wx[0]),
                                    tile_blockdiag(wa[1]), tile_blockdiag(wx[1])], axis=-1)).astype(BF16)
    tiles = lambda b: b.reshape(n_ct, 1, ct)
    b_cat = 0.5 * jnp.concatenate([tiles(ba[0]), tiles(bx[0]), tiles(ba[1]), tiles(bx[1])], axis=-1)
    lam_cat = jnp.concatenate([tiles(lam[0]), tiles(lam[1])], axis=-1)
    return w_cat, b_cat, lam_cat


def kernel(x_prompt, x_sample, state_rglru_h, state_mlstm_C, state_mlstm_n, state_mlstm_m, c, c_ctx, w_ada, b_ada, w_in, rg_conv_w, rg_conv_b, rg_wa, rg_ba, rg_wx, rg_bx, rg_lambda, w_rg_proj, ml_b_igate, ml_b_fgate, ml_gn_g, w_ml_proj, b_merge, w_out, ln_g, ln_b, w_fc, b_fc, w_proj, b_proj):
    bp, seq, d = x_prompt.shape
    bl, dec_seq, _ = x_sample.shape
    depth = w_in.shape[0]
    d_rnn = rg_conv_w.shape[-1]
    ml_dim = ml_gn_g.shape[-1]
    hd = ml_dim // ML_HEADS
    alpha = (2 * depth) ** 0.25
    d_main = 2 * d_rnn + 4 * ml_dim + 2 * d
    assert seq == MLSTM_CHUNK == SCAN_SEG and dec_seq % MLSTM_CHUNK == 0
    assert d_rnn == d and ml_dim == d and w_in.shape[-1] == d_main + 4 * ML_HEADS

    tm = 1024
    tm_proj = 512
    xp = x_prompt.reshape(bp * seq, d)
    xs = x_sample.reshape(bl * dec_seq, d)
    cond = jnp.concatenate([c_ctx[None, :], c], axis=0)
    zero_h = jnp.zeros((2, bp, d_rnn), F32)
    new_h, new_c, new_n, new_m = [], [], [], []
    for l in range(depth):
        mod = _ada(cond, w_ada[l], b_ada[l])
        w_main = w_in[l].astype(BF16)
        w_gate = jnp.zeros((d, GATE_LANES), BF16).at[:, :4 * ML_HEADS].set(w_in[l][:, d_main:].astype(BF16))
        b_gate = jnp.concatenate([ml_b_igate[l].reshape(-1), ml_b_fgate[l].reshape(-1)]).reshape(-1, 1)
        w_cat, b_cat, lam_cat = _rg_gate_weights(rg_wa[l], rg_ba[l], rg_wx[l], rg_bx[l], rg_lambda[l])
        conv_b = rg_conv_b[l].reshape(1, d_rnn)
        gn_g = ml_gn_g[l].reshape(1, ml_dim)
        w_rg, w_ml, w_o = (w_rg_proj[l].astype(BF16), w_ml_proj[l].astype(BF16), w_out[l].astype(BF16))
        w_fc_b, w_pj_b = w_fc[l].astype(BF16), w_proj[l].astype(BF16)
        bm = b_merge[l].reshape(1, 2 * d)
        lng0, lnb0 = ln_g[l, 0].reshape(1, d), ln_b[l, 0].reshape(1, d)
        lng1, lnb1 = ln_g[l, 1].reshape(1, d), ln_b[l, 1].reshape(1, d)
        bfc, bpj = b_fc[l].reshape(1, -1), b_proj[l].reshape(1, d)

        def tail(x, hrg, hml, of32, row_base, rows_per_mod):
            x1 = _merge(hrg, hml, of32, x, mod, w_rg, w_ml, w_o, bm, lng0, lnb0, alpha=alpha,
                        row_base=row_base, rows_per_mod=rows_per_mod, tm=tm)
            return _mlp(x1, mod, w_fc_b, bfc, w_pj_b, bpj, lng1, lnb1, alpha=alpha,
                        row_base=row_base, rows_per_mod=rows_per_mod, tm=tm)

        of32, obf, g, gt = _in_proj(xp, mod, w_main, w_gate, b_gate, rg_conv_w[l], conv_b, conv_len=seq,
                                    row_base=0, rows_per_mod=None, tm=tm_proj)
        hrg, h_fin = _rglru(of32, w_cat, b_cat, lam_cat, zero_h, n_seq=bp, seq_len=seq, want_final=True)
        hml, c_fin, n_fin, m_fin = _mlstm_ctx(obf, of32, g, gt, gn_g, n_seq=bp)
        xp = tail(xp, hrg, hml, of32, 0, None)
        new_h.append(jnp.transpose(h_fin, (1, 0, 2)))
        new_c.append(c_fin)
        new_n.append(n_fin.reshape(bp, 2, ML_HEADS, hd))
        new_m.append(m_fin[..., 0, 0])

        of32, obf, g, gt = _in_proj(xs, mod, w_main, w_gate, b_gate, rg_conv_w[l], conv_b, conv_len=GRID_W,
                                    row_base=1, rows_per_mod=dec_seq, tm=tm_proj)
        h0 = jnp.transpose(state_rglru_h[:, l], (1, 0, 2))
        (hrg,) = _rglru(of32, w_cat, b_cat, lam_cat, h0, n_seq=bl, seq_len=dec_seq, want_final=False)
        c0 = state_mlstm_C[:, l]
        n0 = jnp.broadcast_to(state_mlstm_n[:, l][..., None], (bl, 2, ML_HEADS, hd, GATE_LANES))
        m0 = jnp.broadcast_to(state_mlstm_m[:, l].reshape(bl, 2, ML_HEADS, 1, 1),
                              (bl, 2, ML_HEADS, 1, GATE_LANES))
        hml = _mlstm_lat(obf, of32, g, gt, gn_g, c0, n0, m0, n_seq=bl)
        xs = tail(xs, hrg, hml, of32, 1, dec_seq)

    def stack(parts):
        return parts[0][:, None] if len(parts) == 1 else jnp.stack(parts, axis=1)

    return (xp.reshape(bp, seq, d), xs.reshape(bl, dec_seq, d),
            stack(new_h), stack(new_c), stack(new_n), stack(new_m))
```

```python
import functools

import jax
import jax.numpy as jnp
from jax import lax
from jax.experimental import pallas as pl
from jax.experimental.pallas import tpu as pltpu

F32 = jnp.float32
BF16 = jnp.bfloat16

LN_EPS = 1e-5
LOG2E = 1.4426950408889634
LN2 = 0.6931471805599453
RG_C = 8.0
RG_BW = 64
CONV_W = 4
GRID_W = 64
ML_HEADS = 4
MLSTM_CHUNK = 256
SCAN_SEG = 256
SUBLANES = 8
SCAN_PITCH = SCAN_SEG + 4
SCAN_GROUP = 8
RG_CT = 128
GATE_LANES = 128
VMEM_LIMIT = 52 * 1024 * 1024
MERGE_ROW_GROUPS = 4
MLP_ROW_GROUPS = 2
IN_PROJ_VMEM_LIMIT = 58 * 1024 * 1024


def _cparams(n_axes):
    return pltpu.CompilerParams(
        dimension_semantics=("arbitrary",) * n_axes, vmem_limit_bytes=VMEM_LIMIT)


def _log_sigmoid(x):
    return jnp.minimum(x, 0.0) - jnp.log1p(jnp.exp(-jnp.abs(x)))


def _ln(x):
    mu = jnp.mean(x, -1, keepdims=True)
    xc = x - mu
    var = jnp.mean(xc * xc, -1, keepdims=True)
    return xc * lax.rsqrt(var + LN_EPS)


def _mod_row(i, row_base, tiles_per_row):
    if tiles_per_row is None:
        return row_base
    return row_base + lax.div(i, jnp.int32(tiles_per_row))


def _ada_kernel(ct_ref, w_ref, b_ref, after_ref, o_ref, *, n_rows):
    del after_ref
    w = w_ref[...]
    ct = ct_ref[...]
    s = ct * jax.nn.sigmoid(ct)
    o_ref[...] = jnp.zeros(o_ref.shape, F32)
    for r in range(n_rows):
        o_ref[r:r + 1, :] = jnp.sum(w * s[:, r:r + 1], axis=0, keepdims=True) + b_ref[...]


def _ada(cond, w_ada, b_ada, run_after):
    n_rows, d = cond.shape
    assert n_rows <= 8
    ct = jnp.zeros((d, 8), F32).at[:, :n_rows].set(cond.T)
    n_out = w_ada.shape[1]
    tn = 2048
    assert n_out % tn == 0
    return pl.pallas_call(
        functools.partial(_ada_kernel, n_rows=n_rows),
        grid=(n_out // tn,),
        in_specs=[pl.BlockSpec((d, 8), lambda j: (0, 0)),
                  pl.BlockSpec((d, tn), lambda j: (0, j)),
                  pl.BlockSpec((1, tn), lambda j: (0, j)),
                  pl.BlockSpec((16, GATE_LANES), lambda j: (0, 0))],
        out_specs=pl.BlockSpec((8, tn), lambda j: (0, j)),
        out_shape=jax.ShapeDtypeStruct((8, n_out), F32),
        compiler_params=_cparams(1),
        name="ada_mod",
    )(ct, w_ada, b_ada.reshape(1, n_out), run_after)


def _inproj_tile(u_ref, w_ref, wg_ref, bg_ref, cw_ref, cb_ref, of_ref, ob_ref, g_ref, gt_ref, *, d, tm,
                 conv_len):
    lc = MLSTM_CHUNK
    n_gate = 4 * ML_HEADS
    g = jnp.dot(u_ref[...], wg_ref[...], preferred_element_type=F32)
    gt = g.T[0:n_gate, :] + bg_ref[...]
    gt = jnp.where(lax.broadcasted_iota(jnp.int32, gt.shape, 0) >= 2 * ML_HEADS, _log_sigmoid(gt), gt)
    gt = gt * LOG2E
    row = lax.broadcasted_iota(jnp.int32, (n_gate, lc), 0)
    ii = lax.broadcasted_iota(jnp.int32, (lc, lc), 0)
    jj = lax.broadcasted_iota(jnp.int32, (lc, lc), 1)
    incl_before = (ii <= jj).astype(BF16)
    incl_after = (ii >= jj).astype(BF16)
    parts = []
    for c in range(tm // lc):
        x = gt[:, c * lc:(c + 1) * lc]
        hi = x.astype(BF16)
        r1 = x - hi.astype(F32)
        mid = r1.astype(BF16)
        lo = (r1 - mid.astype(F32)).astype(BF16)
        split = jnp.concatenate([hi, mid, lo], axis=0)

        def cumsum(tri):
            res = jnp.dot(split, tri, preferred_element_type=F32)
            return res[0:n_gate] + res[n_gate:2 * n_gate] + res[2 * n_gate:3 * n_gate]

        parts.append(jnp.where(row < 2 * ML_HEADS, x,
                               jnp.where(row < 3 * ML_HEADS, cumsum(incl_before), cumsum(incl_after))))
    gfin = jnp.concatenate(parts, axis=1)
    gt_ref[...] = gfin
    pad = jnp.zeros((GATE_LANES - n_gate, tm), F32)
    g_ref[...] = jnp.concatenate([gfin, pad], axis=0).T

    def col_tile(t):
        return jnp.dot(u_ref[...], w_ref[:, t * d:(t + 1) * d], preferred_element_type=F32)

    xr = col_tile(0)
    pos = jnp.bitwise_and(lax.broadcasted_iota(jnp.int32, (tm, 1), 0), conv_len - 1)
    cw = cw_ref[...]
    of_ref[0] = (cw[0:1] * jnp.where(pos >= 2, pltpu.roll(xr, 2, 0), 0.0)
                 + cw[1:2] * jnp.where(pos >= 1, pltpu.roll(xr, 1, 0), 0.0)
                 + cw[2:3] * xr
                 + cw[3:4] * jnp.where(pos <= conv_len - 2, pltpu.roll(xr, tm - 1, 0), 0.0)
                 + cb_ref[...])
    of_ref[1] = jax.nn.gelu(col_tile(1))
    of_ref[2] = jax.nn.sigmoid(col_tile(5))
    of_ref[3] = col_tile(6)
    of_ref[4] = col_tile(7)
    for slab, t in enumerate((2, 3, 4)):
        ob_ref[slab] = col_tile(t).astype(BF16)


def _inproj_kernel(x0_ref, xn_ref, mod_ref, w_ref, wg_ref, bg_ref, cw_ref, cb_ref,
                   of_ref, ob_ref, g_ref, gt_ref, ua_ref, ub_ref, *, d, tm, conv_len, row_base, tiles_per_row):
    i = pl.program_id(0)
    last = pl.num_programs(0) - 1

    def ln_mod(x_blk, tile):
        r = _mod_row(tile, row_base, tiles_per_row)
        shift = mod_ref[pl.ds(r, 1), 0:d]
        scale = mod_ref[pl.ds(r, 1), d:2 * d]
        return (_ln(x_blk[...]) * (1.0 + scale) + shift).astype(BF16)

    @pl.when(i == 0)
    def _():
        ua_ref[...] = ln_mod(x0_ref, i)

    def step(u_ref, u_next_ref):
        u_next_ref[...] = ln_mod(xn_ref, jnp.minimum(i + 1, last))
        _inproj_tile(u_ref, w_ref, wg_ref, bg_ref, cw_ref, cb_ref, of_ref, ob_ref, g_ref, gt_ref,
                     d=d, tm=tm, conv_len=conv_len)

    parity = jnp.bitwise_and(i, 1)
    pl.when(parity == 0)(lambda: step(ua_ref, ub_ref))
    pl.when(parity == 1)(lambda: step(ub_ref, ua_ref))


def _tiles_per_row(rows_per_mod, tm):
    if rows_per_mod is None:
        return None
    assert rows_per_mod % tm == 0
    return rows_per_mod // tm


def _in_proj(x, mod, w_main, w_gate, b_gate, conv_w, conv_b, *, conv_len, row_base, rows_per_mod, tm):
    n, d = x.shape
    tiles_per_row = _tiles_per_row(rows_per_mod, tm)
    assert w_main.shape[1] >= 8 * d and conv_w.shape == (CONV_W, d)
    assert tm % conv_len == 0 and (conv_len & (conv_len - 1)) == 0
    n_tiles = n // tm
    kern = functools.partial(_inproj_kernel, d=d, tm=tm, conv_len=conv_len, row_base=row_base,
                             tiles_per_row=tiles_per_row)
    return pl.pallas_call(
        kern,
        grid=(n_tiles,),
        in_specs=[pl.BlockSpec((tm, d), lambda i: (0, 0)),
                  pl.BlockSpec((tm, d), lambda i: (jnp.minimum(i + 1, n_tiles - 1), 0)),
                  pl.BlockSpec(mod.shape, lambda i: (0, 0)),
                  pl.BlockSpec(w_main.shape, lambda i: (0, 0), pipeline_mode=pl.Buffered(1)),
                  pl.BlockSpec((d, GATE_LANES), lambda i: (0, 0)),
                  pl.BlockSpec((4 * ML_HEADS, 1), lambda i: (0, 0)),
                  pl.BlockSpec((CONV_W, d), lambda i: (0, 0)),
                  pl.BlockSpec((1, d), lambda i: (0, 0))],
        out_specs=[pl.BlockSpec((5, tm, d), lambda i: (0, i, 0)),
                   pl.BlockSpec((3, tm, d), lambda i: (0, i, 0)),
                   pl.BlockSpec((tm, GATE_LANES), lambda i: (i, 0)),
                   pl.BlockSpec((4 * ML_HEADS, tm), lambda i: (0, i))],
        out_shape=[jax.ShapeDtypeStruct((5, n, d), F32),
                   jax.ShapeDtypeStruct((3, n, d), BF16),
                   jax.ShapeDtypeStruct((n, GATE_LANES), F32),
                   jax.ShapeDtypeStruct((4 * ML_HEADS, n), F32)],
        scratch_shapes=[pltpu.VMEM((tm, d), BF16), pltpu.VMEM((tm, d), BF16)],
        compiler_params=pltpu.CompilerParams(dimension_semantics=("arbitrary",),
                                             vmem_limit_bytes=IN_PROJ_VMEM_LIMIT),
        name="in_proj",
    )(x, x, mod, w_main, w_gate, b_gate, conv_w, conv_b)


def _rglru_kernel(xc_ref, gz_ref, w_ref, b_ref, lam_ref, h0_ref, *rest, n_vseq, n_seg, want_final):
    if want_final:
        out_ref, hfin_ref = rest[:2]
        rest = rest[2:]
    else:
        out_ref = rest[0]
        rest = rest[1:]
    a_f, b_f, a_b, b_b = rest[:4]
    rest = rest[4:]
    seg = SCAN_SEG
    pitch = SCAN_PITCH
    ct = xc_ref.shape[1]
    ch = seg
    la = (0.5 * RG_C * LOG2E) * _log_sigmoid(lam_ref[0])
    la_f, la_b = la[:, :ct], la[:, ct:]
    w = w_ref[0]
    bias = b_ref[0]

    def for_chunk_groups(fn):
        def trip(k, carry):
            base = pl.multiple_of(k * (SCAN_GROUP * pitch), SUBLANES)
            for j in range(SCAN_GROUP):
                fn(k * SCAN_GROUP + j, base + j * pitch)
            return carry

        lax.fori_loop(0, n_vseq // SCAN_GROUP, trip, 0)

    def gate_chunk(ci, srow):
        r0 = pl.multiple_of(ci * ch, ch)
        xc = xc_ref[pl.ds(r0, ch), :]
        gates = jnp.dot(xc.astype(BF16), w, preferred_element_type=F32) + bias

        def a_and_b(r_half, i_half, la_half):
            a = jnp.exp2(la_half + la_half * jnp.tanh(r_half))
            y = jnp.clip(1.0 - a * a, 0.0, 1.0)
            gain = jnp.where(y > 0.0, y * lax.rsqrt(y), 0.0)
            half_gx = (0.5 * gain) * xc
            return a, half_gx + half_gx * jnp.tanh(i_half)

        af, bf = a_and_b(gates[:, 0:ct], gates[:, ct:2 * ct], la_f)
        ab, bb = a_and_b(gates[:, 2 * ct:3 * ct], gates[:, 3 * ct:4 * ct], la_b)
        s0 = pl.ds(srow, ch)
        a_f[s0, :] = af
        b_f[s0, :] = bf
        a_b[s0, :] = ab
        b_b[s0, :] = bb

    for_chunk_groups(gate_chunk)

    def slab_idx(t):
        return pl.ds(t, n_vseq, stride=pitch)

    def slab(ref, t):
        return ref[slab_idx(t), :]

    if n_seg == 1:
        init_f = h0_ref[0]
        init_b = h0_ref[1]
    else:
        e_f, p_f, e_b, p_b, cin_f, cin_b = rest
        n_real = n_vseq // n_seg
        zero = jnp.zeros((n_vseq, ct), F32)
        one = jnp.ones((n_vseq, ct), F32)

        def local_step(t, carry):
            ef, pf, eb, pb = carry
            tb = seg - 1 - t
            a = slab(a_f, t)
            ef = a * ef + slab(b_f, t)
            pf = a * pf
            a2 = slab(a_b, tb)
            eb = a2 * eb + slab(b_b, tb)
            pb = a2 * pb
            return ef, pf, eb, pb

        ef, pf, eb, pb = lax.fori_loop(0, seg, local_step, (zero, one, zero, one), unroll=8)
        e_f[...] = ef
        p_f[...] = pf
        e_b[...] = eb
        p_b[...] = pb
        carry = h0_ref[0]
        for s in range(n_seg):
            idx = pl.ds(s, n_real, stride=n_seg)
            cin_f[idx, :] = carry
            carry = e_f[idx, :] + p_f[idx, :] * carry
        carry = h0_ref[1]
        for s in reversed(range(n_seg)):
            idx = pl.ds(s, n_real, stride=n_seg)
            cin_b[idx, :] = carry
            carry = e_b[idx, :] + p_b[idx, :] * carry
        init_f = cin_f[...]
        init_b = cin_b[...]

    def scan_step(t, carry):
        hf, hb = carry
        tb = seg - 1 - t
        hf = slab(a_f, t) * hf + slab(b_f, t)
        b_f[slab_idx(t), :] = hf
        hb = slab(a_b, tb) * hb + slab(b_b, tb)
        b_b[slab_idx(tb), :] = hb
        return hf, hb

    hf_last, hb_last = lax.fori_loop(0, seg, scan_step, (init_f, init_b), unroll=8)
    if want_final:
        hfin_ref[0] = hf_last
        hfin_ref[1] = hb_last

    def out_chunk(ci, srow):
        r0 = pl.multiple_of(ci * ch, ch)
        h = b_f[pl.ds(srow, ch), :] + b_b[pl.ds(srow, ch), :]
        out_ref[pl.ds(r0, ch), :] = (h * gz_ref[pl.ds(r0, ch), :]).astype(BF16)

    for_chunk_groups(out_chunk)


def _rglru(of32, w_cat, b_cat, lam_cat, h0, *, n_seq, seq_len, want_final):
    n, d_rnn = of32.shape[1:]
    ct = RG_CT
    n_ct = d_rnn // ct
    n_seg = seq_len // SCAN_SEG
    n_vseq = n_seq * n_seg
    assert n_vseq * SCAN_SEG == n and n_vseq % SCAN_GROUP == 0
    kern = functools.partial(_rglru_kernel, n_vseq=n_vseq, n_seg=n_seg, want_final=want_final)
    out_specs = [pl.BlockSpec((n, ct), lambda c: (0, c))]
    out_shape = [jax.ShapeDtypeStruct((n, d_rnn), BF16)]
    if want_final:
        out_specs.append(pl.BlockSpec((2, n_seq, ct), lambda c: (0, 0, c)))
        out_shape.append(jax.ShapeDtypeStruct((2, n_seq, d_rnn), F32))
    scratch = [pltpu.VMEM((n_vseq * SCAN_PITCH, ct), F32) for _ in range(4)]
    if n_seg > 1:
        scratch += [pltpu.VMEM((n_vseq, ct), F32) for _ in range(6)]
    return pl.pallas_call(
        kern,
        grid=(n_ct,),
        in_specs=[pl.BlockSpec((None, n, ct), lambda c: (0, 0, c)),
                  pl.BlockSpec((None, n, ct), lambda c: (1, 0, c)),
                  pl.BlockSpec((1, ct, 4 * ct), lambda c: (c, 0, 0)),
                  pl.BlockSpec((1, 1, 4 * ct), lambda c: (c, 0, 0)),
                  pl.BlockSpec((1, 1, 2 * ct), lambda c: (c, 0, 0)),
                  pl.BlockSpec((2, n_seq, ct), lambda c: (0, 0, c))],
        out_specs=out_specs,
        out_shape=out_shape,
        scratch_shapes=scratch,
        compiler_params=_cparams(1),
        name="rglru_ctx" if want_final else "rglru_lat",
    )(of32, of32, w_cat, b_cat, lam_cat, h0)


def _causal_bias(mask_ref):
    lc = mask_ref.shape[1]
    ii = lax.broadcasted_iota(jnp.int32, (lc, lc), 0)
    jj = lax.broadcasted_iota(jnp.int32, (lc, lc), 1)
    mask_ref[0] = jnp.where(jj <= ii, 0.0, -jnp.inf)
    mask_ref[1] = jnp.where(jj >= ii, 0.0, -jnp.inf)


def _with_ones(v):
    return jnp.concatenate([v, jnp.ones((v.shape[0], GATE_LANES), v.dtype)], axis=1)


def _rep(col):
    return jnp.broadcast_to(col, (col.shape[0], GATE_LANES))


def _wide(rep, width):
    return jnp.concatenate([rep] * (width // GATE_LANES), axis=1)


def _mlstm_unit(direction, head, q, k, v, v1, gc, gt_ref, r0, mask_ref, state):
    lc, hd = q.shape
    col_i = direction * ML_HEADS + head
    col_c = 2 * ML_HEADS + col_i
    lane = lax.broadcasted_iota(jnp.int32, (1, GATE_LANES), 1)
    ig_c = _rep(jnp.sum(jnp.where(lane == col_i, gc, 0.0), -1, keepdims=True))
    cum_c = _rep(jnp.sum(jnp.where(lane == col_c, gc, 0.0), -1, keepdims=True))
    ig_row = gt_ref[pl.ds(col_i, 1), pl.ds(r0, lc)]
    cum_row = gt_ref[pl.ds(col_c, 1), pl.ds(r0, lc)]
    dmat = (_wide(cum_c, lc) + (ig_row - cum_row)) + mask_ref[direction]
    if state is None:
        m_prev = 0.0
    else:
        s_prev, m_prev = state
    inter = cum_c + m_prev
    m_row = jnp.maximum(_rep(jnp.max(dmat, -1, keepdims=True)), inter)
    wts = jnp.exp2(dmat - _wide(m_row, lc))
    s = lax.dot_general(q, k, (((1,), (1,)), ((), ())), preferred_element_type=F32) * wts
    if state is None:
        num = jnp.dot(s.astype(BF16), v, preferred_element_type=F32)
        den = _rep(jnp.sum(s, -1, keepdims=True))
        h = num * _wide(1.0 / jnp.maximum(jnp.abs(den), jnp.exp2(-m_row)), hd)
    else:
        w_inter = jnp.exp2(inter - m_row)
        nd = (jnp.dot(s.astype(BF16), v1, preferred_element_type=F32)
              + _wide(w_inter, hd + GATE_LANES) * jnp.dot(q, s_prev.astype(BF16), preferred_element_type=F32))
        rden = 1.0 / jnp.maximum(jnp.abs(nd[:, hd:]), jnp.exp2(-m_row))
        h = nd[:, :hd] * _wide(rden, hd)
    btot = cum_c[lc - 1:lc] if direction == 0 else cum_c[0:1]
    wk_log = btot - cum_c + ig_c
    m_new = jnp.maximum(btot + m_prev, jnp.max(wk_log, 0, keepdims=True))
    kw = k.astype(F32) * _wide(jnp.exp2(wk_log - m_new), hd)
    t_lhs = (((0,), (0,)), ((), ()))
    if state is None:
        c_new = lax.dot_general(kw.astype(BF16), v, t_lhs, preferred_element_type=F32)
        return h, c_new, jnp.sum(kw, 0, keepdims=True), m_new
    decay = _wide(jnp.exp2(btot + m_prev - m_new), hd + GATE_LANES)
    s_new = decay * s_prev + lax.dot_general(kw.astype(BF16), v1, t_lhs, preferred_element_type=F32)
    return h, s_new, None, m_new


HEAD_OUT_ROWS = 1024


def _head_out_all(h_acc, og_ref, gn_ref, out_ref):
    gn = gn_ref[...]
    rows = h_acc.shape[0]
    step = min(rows, HEAD_OUT_ROWS)
    for r0 in range(0, rows, step):
        rs = slice(r0, r0 + step)
        out_ref[rs, :] = (og_ref[rs, :] * (_ln(h_acc[rs, :]) * gn)).astype(BF16)


def _mlstm_ctx_kernel(q_ref, k_ref, v_ref, o_ref, g_ref, gt_ref, gn_ref,
                      out_ref, cfin_ref, nfin_ref, mfin_ref, mask_ref, h_acc, *, n_sub, scale):
    head = pl.program_id(1)
    lc = MLSTM_CHUNK
    _causal_bias(mask_ref)

    def body(s, carry):
        r0 = pl.multiple_of(s * lc, lc)
        rs = pl.ds(r0, lc)
        q, k, v, gc = q_ref[rs, :] * scale, k_ref[rs, :], v_ref[rs, :], g_ref[rs, :]
        hsum = None
        for direction in (0, 1):
            h, c_new, n_new, m_new = _mlstm_unit(direction, head, q, k, v, None, gc, gt_ref, r0, mask_ref, None)
            cfin_ref[s, direction, 0] = c_new
            nfin_ref[s, direction, 0] = n_new
            mfin_ref[s, direction, 0] = m_new * LN2
            hsum = h if hsum is None else hsum + h
        h_acc[rs, :] = hsum
        return carry

    lax.fori_loop(0, n_sub, body, 0, unroll=2)
    _head_out_all(h_acc, o_ref, gn_ref, out_ref)


def _mlstm_lat_kernel(q_ref, k_ref, v_ref, o_ref, g_ref, gt_ref, gn_ref, c0_ref, n0_ref, m0_ref,
                      out_ref, s_st, m_st, h_acc, mask_ref, *, n_chunks, scale):
    head = pl.program_id(1)
    lc = MLSTM_CHUNK
    hd = q_ref.shape[1]
    _causal_bias(mask_ref)
    for direction in (0, 1):
        s_st[direction, :, 0:hd] = c0_ref[0, direction, 0]
        s_st[direction, :, hd:] = n0_ref[0, direction, 0]
        m_st[direction] = m0_ref[0, direction, 0] * LOG2E

    half = n_chunks // 2

    def make_body(accumulate):
        def body(c, carry):
            for direction in (0, 1):
                cc = c if direction == 0 else n_chunks - 1 - c
                r0 = pl.multiple_of(cc * lc, lc)
                rs = pl.ds(r0, lc)
                state = (s_st[direction], m_st[direction])
                v = v_ref[rs, :]
                h, s_new, _, m_new = _mlstm_unit(direction, head, q_ref[rs, :] * scale, k_ref[rs, :], v,
                                                 _with_ones(v), g_ref[rs, :], gt_ref, r0, mask_ref, state)
                s_st[direction] = s_new
                m_st[direction] = m_new
                if accumulate:
                    h_acc[rs, :] += h
                else:
                    h_acc[rs, :] = h
            return carry
        return body

    lax.fori_loop(0, half, make_body(False), 0, unroll=min(half, 8))
    lax.fori_loop(half, n_chunks, make_body(True), 0, unroll=min(half, 8))
    _head_out_all(h_acc, o_ref, gn_ref, out_ref)


def _mlstm_in_specs(rows, hd):
    slab = lambda idx: pl.BlockSpec((None, rows, hd), lambda s, h: (idx, s, h))
    return [slab(0), slab(1), slab(2), slab(2),
            pl.BlockSpec((rows, GATE_LANES), lambda s, h: (s, 0)),
            pl.BlockSpec((4 * ML_HEADS, rows), lambda s, h: (0, s)),
            pl.BlockSpec((1, hd), lambda s, h: (0, h))]


def _mlstm_ctx(obf, of32, g, gt, gn_g, *, n_seq):
    n = obf.shape[1]
    ml_dim = gn_g.shape[1]
    hd = ml_dim // ML_HEADS
    lc = MLSTM_CHUNK
    assert n == n_seq * lc
    n_sub = 8
    assert n_seq % n_sub == 0
    rows = n_sub * lc
    kern = functools.partial(_mlstm_ctx_kernel, n_sub=n_sub, scale=hd ** -0.5)
    return pl.pallas_call(
        kern,
        grid=(n_seq // n_sub, ML_HEADS),
        in_specs=_mlstm_in_specs(rows, hd),
        out_specs=[pl.BlockSpec((rows, hd), lambda s, h: (s, h)),
                   pl.BlockSpec((n_sub, 2, 1, hd, hd), lambda s, h: (s, 0, h, 0, 0)),
                   pl.BlockSpec((n_sub, 2, 1, 1, hd), lambda s, h: (s, 0, h, 0, 0)),
                   pl.BlockSpec((n_sub, 2, 1, 1, GATE_LANES), lambda s, h: (s, 0, h, 0, 0))],
        out_shape=[jax.ShapeDtypeStruct((n, ml_dim), BF16),
                   jax.ShapeDtypeStruct((n_seq, 2, ML_HEADS, hd, hd), F32),
                   jax.ShapeDtypeStruct((n_seq, 2, ML_HEADS, 1, hd), F32),
                   jax.ShapeDtypeStruct((n_seq, 2, ML_HEADS, 1, GATE_LANES), F32)],
        scratch_shapes=[pltpu.VMEM((2, lc, lc), F32), pltpu.VMEM((rows, hd), F32)],
        compiler_params=_cparams(2),
        name="mlstm_ctx",
    )(obf, obf, obf, of32, g, gt, gn_g)


def _mlstm_lat(obf, of32, g, gt, gn_g, c0, n0, m0, *, n_seq):
    n = obf.shape[1]
    ml_dim = gn_g.shape[1]
    hd = ml_dim // ML_HEADS
    lc = MLSTM_CHUNK
    rows = n // n_seq
    n_chunks = rows // lc
    assert n_chunks % 2 == 0
    kern = functools.partial(_mlstm_lat_kernel, n_chunks=n_chunks, scale=hd ** -0.5)
    return pl.pallas_call(
        kern,
        grid=(n_seq, ML_HEADS),
        in_specs=_mlstm_in_specs(rows, hd) + [
                  pl.BlockSpec((1, 2, 1, hd, hd), lambda s, h: (s, 0, h, 0, 0)),
                  pl.BlockSpec((1, 2, 1, hd, GATE_LANES), lambda s, h: (s, 0, h, 0, 0)),
                  pl.BlockSpec((1, 2, 1, 1, GATE_LANES), lambda s, h: (s, 0, h, 0, 0))],
        out_specs=pl.BlockSpec((rows, hd), lambda s, h: (s, h)),
        out_shape=jax.ShapeDtypeStruct((n, ml_dim), BF16),
        scratch_shapes=[pltpu.VMEM((2, hd, hd + GATE_LANES), F32),
                        pltpu.VMEM((2, 1, GATE_LANES), F32),
                        pltpu.VMEM((rows, hd), F32),
                        pltpu.VMEM((2, lc, lc), F32)],
        compiler_params=_cparams(2),
        name="mlstm_lat",
    )(obf, obf, obf, of32, g, gt, gn_g, c0, n0, m0)


def _merge_kernel(hrg_ref, hml_ref, gma_ref, gmb_ref, x_ref, mod_ref, wrg_ref, wml_ref, wout_ref,
                  bm_ref, lng_ref, lnb_ref, o_ref, *, d, alpha, row_base, tiles_per_row):
    r = _mod_row(pl.program_id(0), row_base, tiles_per_row)
    gate1 = mod_ref[pl.ds(r, 1), 2 * d:3 * d]
    rows_per_group = x_ref.shape[0] // MERGE_ROW_GROUPS
    for grp in range(MERGE_ROW_GROUPS):
        rows = slice(grp * rows_per_group, (grp + 1) * rows_per_group)
        y_rg = jnp.dot(hrg_ref[rows, :], wrg_ref[...], preferred_element_type=F32)
        y_ml = jnp.dot(hml_ref[rows, :], wml_ref[...], preferred_element_type=F32)
        g_rg = jax.nn.sigmoid(gma_ref[rows, :] + bm_ref[:, 0:d])
        g_ml = jax.nn.sigmoid(gmb_ref[rows, :] + bm_ref[:, d:2 * d])
        merged = (g_rg * y_rg + g_ml * y_ml).astype(BF16)
        mix = jnp.dot(merged, wout_ref[...], preferred_element_type=F32)
        o_ref[rows, :] = _ln(alpha * x_ref[rows, :] + gate1 * mix) * lng_ref[...] + lnb_ref[...]


def _merge(hrg, hml, of32, x, mod, w_rg, w_ml, w_out, b_merge, ln_g, ln_b, *, alpha, row_base,
           rows_per_mod, tm):
    n, d = x.shape
    tiles_per_row = _tiles_per_row(rows_per_mod, tm)
    kern = functools.partial(_merge_kernel, d=d, alpha=alpha, row_base=row_base, tiles_per_row=tiles_per_row)
    full = lambda shape: pl.BlockSpec(shape, lambda i: (0,) * len(shape))
    resident = lambda shape: pl.BlockSpec(shape, lambda i: (0, 0), pipeline_mode=pl.Buffered(1))
    return pl.pallas_call(
        kern,
        grid=(n // tm,),
        in_specs=[pl.BlockSpec((tm, d), lambda i: (i, 0)),
                  pl.BlockSpec((tm, d), lambda i: (i, 0)),
                  pl.BlockSpec((None, tm, d), lambda i: (3, i, 0)),
                  pl.BlockSpec((None, tm, d), lambda i: (4, i, 0)),
                  pl.BlockSpec((tm, d), lambda i: (i, 0)),
                  full(mod.shape), resident((d, d)), resident((d, d)), resident((d, d)),
                  full((1, 2 * d)), full((1, d)), full((1, d))],
        out_specs=pl.BlockSpec((tm, d), lambda i: (i, 0)),
        out_shape=jax.ShapeDtypeStruct((n, d), F32),
        compiler_params=_cparams(1),
        name="merge",
    )(hrg, hml, of32, of32, x, mod, w_rg, w_ml, w_out, b_merge, ln_g, ln_b)


def _mlp_kernel(x_ref, mod_ref, wfc_ref, bfc_ref, wpj_ref, bpj_ref, lng_ref, lnb_ref, o_ref,
                u_ref, *, d, alpha, row_base, tiles_per_row):
    r = _mod_row(pl.program_id(0), row_base, tiles_per_row)
    shift = mod_ref[pl.ds(r, 1), 3 * d:4 * d]
    scale = mod_ref[pl.ds(r, 1), 4 * d:5 * d]
    gate2 = mod_ref[pl.ds(r, 1), 5 * d:6 * d]
    rows_per_group = x_ref.shape[0] // MLP_ROW_GROUPS
    for grp in range(MLP_ROW_GROUPS):
        rows = slice(grp * rows_per_group, (grp + 1) * rows_per_group)
        u_ref[rows, :] = (_ln(x_ref[rows, :]) * (1.0 + scale) + shift).astype(BF16)
        hid = jnp.dot(u_ref[rows, :], wfc_ref[...], preferred_element_type=F32) + bfc_ref[...]
        hid = jnp.square(jnp.maximum(hid, 0.0)).astype(BF16)
        acc = jnp.dot(hid, wpj_ref[...], preferred_element_type=F32)
        y = alpha * x_ref[rows, :] + gate2 * (acc + bpj_ref[...])
        o_ref[rows, :] = _ln(y) * lng_ref[...] + lnb_ref[...]


def _mlp(x, mod, w_fc, b_fc, w_proj, b_proj, ln_g, ln_b, *, alpha, row_base, rows_per_mod, tm):
    n, d = x.shape
    tiles_per_row = _tiles_per_row(rows_per_mod, tm)
    d_ff = w_fc.shape[1]
    kern = functools.partial(_mlp_kernel, d=d, alpha=alpha, row_base=row_base, tiles_per_row=tiles_per_row)
    resident = lambda shape: pl.BlockSpec(shape, lambda i: (0, 0), pipeline_mode=pl.Buffered(1))
    return pl.pallas_call(
        kern,
        grid=(n // tm,),
        in_specs=[pl.BlockSpec((tm, d), lambda i: (i, 0)),
                  pl.BlockSpec(mod.shape, lambda i: (0, 0)),
                  resident((d, d_ff)),
                  pl.BlockSpec((1, d_ff), lambda i: (0, 0)),
                  resident((d_ff, d)),
                  pl.BlockSpec((1, d), lambda i: (0, 0)),
                  pl.BlockSpec((1, d), lambda i: (0, 0)),
                  pl.BlockSpec((1, d), lambda i: (0, 0))],
        out_specs=pl.BlockSpec((tm, d), lambda i: (i, 0)),
        out_shape=jax.ShapeDtypeStruct((n, d), F32),
        scratch_shapes=[pltpu.VMEM((tm, d), BF16)],
        compiler_params=_cparams(1),
        name="mlp",
    )(x, mod, w_fc, b_fc, w_proj, b_proj, ln_g, ln_b)


def _rg_gate_weights(wa, ba, wx, bx, lam):
    ct = RG_CT
    d_rnn = ba.shape[-1]
    n_ct = d_rnn // ct
    per = ct // RG_BW

    def tile_blockdiag(w):
        wt = w.reshape(n_ct, per, RG_BW, RG_BW)
        eye = jnp.eye(per, dtype=w.dtype)
        return jnp.einsum('cpkj,pq->cpkqj', wt, eye).reshape(n_ct, ct, ct)

    w_cat = (0.5 * jnp.concatenate([tile_blockdiag(wa[0]), tile_blockdiag(wx[0]),
                                    tile_blockdiag(wa[1]), tile_blockdiag(wx[1])], axis=-1)).astype(BF16)
    tiles = lambda b: b.reshape(n_ct, 1, ct)
    b_cat = 0.5 * jnp.concatenate([tiles(ba[0]), tiles(bx[0]), tiles(ba[1]), tiles(bx[1])], axis=-1)
    lam_cat = jnp.concatenate([tiles(lam[0]), tiles(lam[1])], axis=-1)
    return w_cat, b_cat, lam_cat


def kernel(x_prompt, x_sample, state_rglru_h, state_mlstm_C, state_mlstm_n, state_mlstm_m, c, c_ctx, w_ada, b_ada, w_in, rg_conv_w, rg_conv_b, rg_wa, rg_ba, rg_wx, rg_bx, rg_lambda, w_rg_proj, ml_b_igate, ml_b_fgate, ml_gn_g, w_ml_proj, b_merge, w_out, ln_g, ln_b, w_fc, b_fc, w_proj, b_proj):
    bp, seq, d = x_prompt.shape
    bl, dec_seq, _ = x_sample.shape
    depth = w_in.shape[0]
    d_rnn = rg_conv_w.shape[-1]
    ml_dim = ml_gn_g.shape[-1]
    hd = ml_dim // ML_HEADS
    alpha = (2 * depth) ** 0.25
    d_main = 2 * d_rnn + 4 * ml_dim + 2 * d
    assert seq == MLSTM_CHUNK == SCAN_SEG and dec_seq % MLSTM_CHUNK == 0
    assert d_rnn == d and ml_dim == d and w_in.shape[-1] == d_main + 4 * ML_HEADS

    tm = 1024
    tm_proj = 512
    xp = x_prompt.reshape(bp * seq, d)
    xs = x_sample.reshape(bl * dec_seq, d)
    cond = jnp.concatenate([c_ctx[None, :], c], axis=0)
    zero_h = jnp.zeros((2, bp, d_rnn), F32)
    new_h, new_c, new_n, new_m = [], [], [], []
    for l in range(depth):
        w_main = w_in[l].astype(BF16)
        mod = _ada(cond, w_ada[l], b_ada[l], w_main)
        w_gate = jnp.zeros((d, GATE_LANES), BF16).at[:, :4 * ML_HEADS].set(w_in[l][:, d_main:].astype(BF16))
        b_gate = jnp.concatenate([ml_b_igate[l].reshape(-1), ml_b_fgate[l].reshape(-1)]).reshape(-1, 1)
        w_cat, b_cat, lam_cat = _rg_gate_weights(rg_wa[l], rg_ba[l], rg_wx[l], rg_bx[l], rg_lambda[l])
        conv_b = rg_conv_b[l].reshape(1, d_rnn)
        gn_g = ml_gn_g[l].reshape(1, ml_dim)
        w_rg, w_ml, w_o = (w_rg_proj[l].astype(BF16), w_ml_proj[l].astype(BF16), w_out[l].astype(BF16))
        w_fc_b, w_pj_b = w_fc[l].astype(BF16), w_proj[l].astype(BF16)
        bm = b_merge[l].reshape(1, 2 * d)
        lng0, lnb0 = ln_g[l, 0].reshape(1, d), ln_b[l, 0].reshape(1, d)
        lng1, lnb1 = ln_g[l, 1].reshape(1, d), ln_b[l, 1].reshape(1, d)
        bfc, bpj = b_fc[l].reshape(1, -1), b_proj[l].reshape(1, d)

        def tail(x, hrg, hml, of32, row_base, rows_per_mod):
            x1 = _merge(hrg, hml, of32, x, mod, w_rg, w_ml, w_o, bm, lng0, lnb0, alpha=alpha,
                        row_base=row_base, rows_per_mod=rows_per_mod, tm=tm)
            return _mlp(x1, mod, w_fc_b, bfc, w_pj_b, bpj, lng1, lnb1, alpha=alpha,
                        row_base=row_base, rows_per_mod=rows_per_mod, tm=tm)

        of32, obf, g, gt = _in_proj(xp, mod, w_main, w_gate, b_gate, rg_conv_w[l], conv_b, conv_len=seq,
                                    row_base=0, rows_per_mod=None, tm=tm_proj)
        hrg, h_fin = _rglru(of32, w_cat, b_cat, lam_cat, zero_h, n_seq=bp, seq_len=seq, want_final=True)
        hml, c_fin, n_fin, m_fin = _mlstm_ctx(obf, of32, g, gt, gn_g, n_seq=bp)
        xp = tail(xp, hrg, hml, of32, 0, None)
        new_h.append(jnp.transpose(h_fin, (1, 0, 2)))
        new_c.append(c_fin)
        new_n.append(n_fin.reshape(bp, 2, ML_HEADS, hd))
        new_m.append(m_fin[..., 0, 0])

        of32, obf, g, gt = _in_proj(xs, mod, w_main, w_gate, b_gate, rg_conv_w[l], conv_b, conv_len=GRID_W,
                                    row_base=1, rows_per_mod=dec_seq, tm=tm_proj)
        h0 = jnp.transpose(state_rglru_h[:, l], (1, 0, 2))
        (hrg,) = _rglru(of32, w_cat, b_cat, lam_cat, h0, n_seq=bl, seq_len=dec_seq, want_final=False)
        c0 = state_mlstm_C[:, l]
        n0 = jnp.broadcast_to(state_mlstm_n[:, l][..., None], (bl, 2, ML_HEADS, hd, GATE_LANES))
        m0 = jnp.broadcast_to(state_mlstm_m[:, l].reshape(bl, 2, ML_HEADS, 1, 1),
                              (bl, 2, ML_HEADS, 1, GATE_LANES))
        hml = _mlstm_lat(obf, of32, g, gt, gn_g, c0, n0, m0, n_seq=bl)
        xs = tail(xs, hrg, hml, of32, 1, dec_seq)

    def stack(parts):
        return parts[0][:, None] if len(parts) == 1 else jnp.stack(parts, axis=1)

    return (xp.reshape(bp, seq, d), xs.reshape(bl, dec_seq, d),
            stack(new_h), stack(new_c), stack(new_n), stack(new_m))
```

```python
import functools

import jax
import jax.numpy as jnp
from jax import lax
from jax.experimental import pallas as pl
from jax.experimental.pallas import tpu as pltpu

F32 = jnp.float32
BF16 = jnp.bfloat16

LN_EPS = 1e-5
LOG2E = 1.4426950408889634
LN2 = 0.6931471805599453
RG_C = 8.0
RG_BW = 64
CONV_W = 4
GRID_W = 64
ML_HEADS = 4
MLSTM_CHUNK = 256
SCAN_SEG = 256
SUBLANES = 8
SCAN_PITCH = SCAN_SEG + 4
SCAN_GROUP = 8
RG_CT = 128
GATE_LANES = 128
VMEM_LIMIT = 52 * 1024 * 1024
MERGE_ROW_GROUPS = 4
MLP_ROW_GROUPS = 2
IN_PROJ_VMEM_LIMIT = 58 * 1024 * 1024


def _cparams(n_axes):
    return pltpu.CompilerParams(
        dimension_semantics=("arbitrary",) * n_axes, vmem_limit_bytes=VMEM_LIMIT)


def _log_sigmoid(x):
    return jnp.minimum(x, 0.0) - jnp.log1p(jnp.exp(-jnp.abs(x)))


def _ln(x):
    mu = jnp.mean(x, -1, keepdims=True)
    xc = x - mu
    var = jnp.mean(xc * xc, -1, keepdims=True)
    return xc * lax.rsqrt(var + LN_EPS)


def _mod_row(i, row_base, tiles_per_row):
    if tiles_per_row is None:
        return row_base
    return row_base + lax.div(i, jnp.int32(tiles_per_row))


def _ada_kernel(ct_ref, w_ref, b_ref, after_ref, o_ref, *, n_rows):
    del after_ref
    w = w_ref[...]
    ct = ct_ref[...]
    s = ct * jax.nn.sigmoid(ct)
    o_ref[...] = jnp.zeros(o_ref.shape, F32)
    for r in range(n_rows):
        o_ref[r:r + 1, :] = jnp.sum(w * s[:, r:r + 1], axis=0, keepdims=True) + b_ref[...]


def _ada(cond, w_ada, b_ada, run_after):
    n_rows, d = cond.shape
    assert n_rows <= 8
    ct = jnp.zeros((d, 8), F32).at[:, :n_rows].set(cond.T)
    n_out = w_ada.shape[1]
    tn = 2048
    assert n_out % tn == 0
    return pl.pallas_call(
        functools.partial(_ada_kernel, n_rows=n_rows),
        grid=(n_out // tn,),
        in_specs=[pl.BlockSpec((d, 8), lambda j: (0, 0)),
                  pl.BlockSpec((d, tn), lambda j: (0, j)),
                  pl.BlockSpec((1, tn), lambda j: (0, j)),
                  pl.BlockSpec((16, GATE_LANES), lambda j: (0, 0))],
        out_specs=pl.BlockSpec((8, tn), lambda j: (0, j)),
        out_shape=jax.ShapeDtypeStruct((8, n_out), F32),
        compiler_params=_cparams(1),
        name="ada_mod",
    )(ct, w_ada, b_ada.reshape(1, n_out), run_after)


def _inproj_tile(u_ref, w_ref, wg_ref, bg_ref, cw_ref, cb_ref, of_ref, ob_ref, g_ref, gt_ref, *, d, tm,
                 conv_len):
    lc = MLSTM_CHUNK
    n_gate = 4 * ML_HEADS
    g = jnp.dot(u_ref[...], wg_ref[...], preferred_element_type=F32)
    gt = g.T[0:n_gate, :] + bg_ref[...]
    gt = jnp.where(lax.broadcasted_iota(jnp.int32, gt.shape, 0) >= 2 * ML_HEADS, _log_sigmoid(gt), gt)
    gt = gt * LOG2E
    row = lax.broadcasted_iota(jnp.int32, (n_gate, lc), 0)
    ii = lax.broadcasted_iota(jnp.int32, (lc, lc), 0)
    jj = lax.broadcasted_iota(jnp.int32, (lc, lc), 1)
    incl_before = (ii <= jj).astype(BF16)
    incl_after = (ii >= jj).astype(BF16)
    parts = []
    for c in range(tm // lc):
        x = gt[:, c * lc:(c + 1) * lc]
        hi = x.astype(BF16)
        r1 = x - hi.astype(F32)
        mid = r1.astype(BF16)
        lo = (r1 - mid.astype(F32)).astype(BF16)
        split = jnp.concatenate([hi, mid, lo], axis=0)

        def cumsum(tri):
            res = jnp.dot(split, tri, preferred_element_type=F32)
            return res[0:n_gate] + res[n_gate:2 * n_gate] + res[2 * n_gate:3 * n_gate]

        parts.append(jnp.where(row < 2 * ML_HEADS, x,
                               jnp.where(row < 3 * ML_HEADS, cumsum(incl_before), cumsum(incl_after))))
    gfin = jnp.concatenate(parts, axis=1)
    gt_ref[...] = gfin
    pad = jnp.zeros((GATE_LANES - n_gate, tm), F32)
    g_ref[...] = jnp.concatenate([gfin, pad], axis=0).T

    def col_tile(t):
        return jnp.dot(u_ref[...], w_ref[:, t * d:(t + 1) * d], preferred_element_type=F32)

    xr = col_tile(0)
    pos = jnp.bitwise_and(lax.broadcasted_iota(jnp.int32, (tm, 1), 0), conv_len - 1)
    cw = cw_ref[...]
    of_ref[0] = (cw[0:1] * jnp.where(pos >= 2, pltpu.roll(xr, 2, 0), 0.0)
                 + cw[1:2] * jnp.where(pos >= 1, pltpu.roll(xr, 1, 0), 0.0)
                 + cw[2:3] * xr
                 + cw[3:4] * jnp.where(pos <= conv_len - 2, pltpu.roll(xr, tm - 1, 0), 0.0)
                 + cb_ref[...])
    of_ref[1] = jax.nn.gelu(col_tile(1))
    of_ref[2] = jax.nn.sigmoid(col_tile(5))
    of_ref[3] = col_tile(6)
    of_ref[4] = col_tile(7)
    for slab, t in enumerate((2, 3, 4)):
        ob_ref[slab] = col_tile(t).astype(BF16)


def _inproj_kernel(x0_ref, xn_ref, mod_ref, w_ref, wg_ref, bg_ref, cw_ref, cb_ref,
                   of_ref, ob_ref, g_ref, gt_ref, ua_ref, ub_ref, *, d, tm, conv_len, row_base, tiles_per_row):
    i = pl.program_id(0)
    last = pl.num_programs(0) - 1

    def ln_mod(x_blk, tile):
        r = _mod_row(tile, row_base, tiles_per_row)
        shift = mod_ref[pl.ds(r, 1), 0:d]
        scale = mod_ref[pl.ds(r, 1), d:2 * d]
        return (_ln(x_blk[...]) * (1.0 + scale) + shift).astype(BF16)

    @pl.when(i == 0)
    def _():
        ua_ref[...] = ln_mod(x0_ref, i)

    def step(u_ref, u_next_ref):
        u_next_ref[...] = ln_mod(xn_ref, jnp.minimum(i + 1, last))
        _inproj_tile(u_ref, w_ref, wg_ref, bg_ref, cw_ref, cb_ref, of_ref, ob_ref, g_ref, gt_ref,
                     d=d, tm=tm, conv_len=conv_len)

    parity = jnp.bitwise_and(i, 1)
    pl.when(parity == 0)(lambda: step(ua_ref, ub_ref))
    pl.when(parity == 1)(lambda: step(ub_ref, ua_ref))


def _tiles_per_row(rows_per_mod, tm):
    if rows_per_mod is None:
        return None
    assert rows_per_mod % tm == 0
    return rows_per_mod // tm


def _in_proj(x, mod, w_main, w_gate, b_gate, conv_w, conv_b, *, conv_len, row_base, rows_per_mod, tm):
    n, d = x.shape
    tiles_per_row = _tiles_per_row(rows_per_mod, tm)
    assert w_main.shape[1] >= 8 * d and conv_w.shape == (CONV_W, d)
    assert tm % conv_len == 0 and (conv_len & (conv_len - 1)) == 0
    n_tiles = n // tm
    kern = functools.partial(_inproj_kernel, d=d, tm=tm, conv_len=conv_len, row_base=row_base,
                             tiles_per_row=tiles_per_row)
    return pl.pallas_call(
        kern,
        grid=(n_tiles,),
        in_specs=[pl.BlockSpec((tm, d), lambda i: (0, 0)),
                  pl.BlockSpec((tm, d), lambda i: (jnp.minimum(i + 1, n_tiles - 1), 0)),
                  pl.BlockSpec(mod.shape, lambda i: (0, 0)),
                  pl.BlockSpec(w_main.shape, lambda i: (0, 0), pipeline_mode=pl.Buffered(1)),
                  pl.BlockSpec((d, GATE_LANES), lambda i: (0, 0)),
                  pl.BlockSpec((4 * ML_HEADS, 1), lambda i: (0, 0)),
                  pl.BlockSpec((CONV_W, d), lambda i: (0, 0)),
                  pl.BlockSpec((1, d), lambda i: (0, 0))],
        out_specs=[pl.BlockSpec((5, tm, d), lambda i: (0, i, 0)),
                   pl.BlockSpec((3, tm, d), lambda i: (0, i, 0)),
                   pl.BlockSpec((tm, GATE_LANES), lambda i: (i, 0)),
                   pl.BlockSpec((4 * ML_HEADS, tm), lambda i: (0, i))],
        out_shape=[jax.ShapeDtypeStruct((5, n, d), F32),
                   jax.ShapeDtypeStruct((3, n, d), BF16),
                   jax.ShapeDtypeStruct((n, GATE_LANES), F32),
                   jax.ShapeDtypeStruct((4 * ML_HEADS, n), F32)],
        scratch_shapes=[pltpu.VMEM((tm, d), BF16), pltpu.VMEM((tm, d), BF16)],
        compiler_params=pltpu.CompilerParams(dimension_semantics=("arbitrary",),
                                             vmem_limit_bytes=IN_PROJ_VMEM_LIMIT),
        name="in_proj",
    )(x, x, mod, w_main, w_gate, b_gate, conv_w, conv_b)


def _rglru_kernel(xc_ref, gz_ref, w_ref, b_ref, lam_ref, h0_ref, *rest, n_vseq, n_seg, want_final):
    if want_final:
        out_ref, hfin_ref = rest[:2]
        rest = rest[2:]
    else:
        out_ref = rest[0]
        rest = rest[1:]
    a_f, b_f, a_b, b_b = rest[:4]
    rest = rest[4:]
    seg = SCAN_SEG
    pitch = SCAN_PITCH
    ct = xc_ref.shape[1]
    ch = seg
    la = (0.5 * RG_C * LOG2E) * _log_sigmoid(lam_ref[0])
    la_f, la_b = la[:, :ct], la[:, ct:]
    w = w_ref[0]
    bias = b_ref[0]

    def for_chunk_groups(fn):
        def trip(k, carry):
            base = pl.multiple_of(k * (SCAN_GROUP * pitch), SUBLANES)
            for j in range(SCAN_GROUP):
                fn(k * SCAN_GROUP + j, base + j * pitch)
            return carry

        lax.fori_loop(0, n_vseq // SCAN_GROUP, trip, 0)

    def gate_chunk(ci, srow):
        r0 = pl.multiple_of(ci * ch, ch)
        xc = xc_ref[pl.ds(r0, ch), :]
        gates = jnp.dot(xc.astype(BF16), w, preferred_element_type=F32) + bias

        def a_and_b(r_half, i_half, la_half):
            a = jnp.exp2(la_half + la_half * jnp.tanh(r_half))
            y = jnp.clip(1.0 - a * a, 0.0, 1.0)
            gain = jnp.where(y > 0.0, y * lax.rsqrt(y), 0.0)
            half_gx = (0.5 * gain) * xc
            return a, half_gx + half_gx * jnp.tanh(i_half)

        af, bf = a_and_b(gates[:, 0:ct], gates[:, ct:2 * ct], la_f)
        ab, bb = a_and_b(gates[:, 2 * ct:3 * ct], gates[:, 3 * ct:4 * ct], la_b)
        s0 = pl.ds(srow, ch)
        a_f[s0, :] = af
        b_f[s0, :] = bf
        a_b[s0, :] = ab
        b_b[s0, :] = bb

    for_chunk_groups(gate_chunk)

    def slab_idx(t):
        return pl.ds(t, n_vseq, stride=pitch)

    def slab(ref, t):
        return ref[slab_idx(t), :]

    if n_seg == 1:
        init_f = h0_ref[0]
        init_b = h0_ref[1]
    else:
        e_f, p_f, e_b, p_b, cin_f, cin_b = rest
        n_real = n_vseq // n_seg
        zero = jnp.zeros((n_vseq, ct), F32)
        one = jnp.ones((n_vseq, ct), F32)

        def local_step(t, carry):
            ef, pf, eb, pb = carry
            tb = seg - 1 - t
            a = slab(a_f, t)
            ef = a * ef + slab(b_f, t)
            pf = a * pf
            a2 = slab(a_b, tb)
            eb = a2 * eb + slab(b_b, tb)
            pb = a2 * pb
            return ef, pf, eb, pb

        ef, pf, eb, pb = lax.fori_loop(0, seg, local_step, (zero, one, zero, one), unroll=8)
        e_f[...] = ef
        p_f[...] = pf
        e_b[...] = eb
        p_b[...] = pb
        carry = h0_ref[0]
        for s in range(n_seg):
            idx = pl.ds(s, n_real, stride=n_seg)
            cin_f[idx, :] = carry
            carry = e_f[idx, :] + p_f[idx, :] * carry
        carry = h0_ref[1]
        for s in reversed(range(n_seg)):
            idx = pl.ds(s, n_real, stride=n_seg)
            cin_b[idx, :] = carry
            carry = e_b[idx, :] + p_b[idx, :] * carry
        init_f = cin_f[...]
        init_b = cin_b[...]

    def scan_step(t, carry):
        hf, hb = carry
        tb = seg - 1 - t
        hf = slab(a_f, t) * hf + slab(b_f, t)
        b_f[slab_idx(t), :] = hf
        hb = slab(a_b, tb) * hb + slab(b_b, tb)
        b_b[slab_idx(tb), :] = hb
        return hf, hb

    hf_last, hb_last = lax.fori_loop(0, seg, scan_step, (init_f, init_b), unroll=8)
    if want_final:
        hfin_ref[0] = hf_last
        hfin_ref[1] = hb_last

    def out_chunk(ci, srow):
        r0 = pl.multiple_of(ci * ch, ch)
        h = b_f[pl.ds(srow, ch), :] + b_b[pl.ds(srow, ch), :]
        out_ref[pl.ds(r0, ch), :] = (h * gz_ref[pl.ds(r0, ch), :]).astype(BF16)

    for_chunk_groups(out_chunk)


def _rglru(of32, w_cat, b_cat, lam_cat, h0, *, n_seq, seq_len, want_final):
    n, d_rnn = of32.shape[1:]
    ct = RG_CT
    n_ct = d_rnn // ct
    n_seg = seq_len // SCAN_SEG
    n_vseq = n_seq * n_seg
    assert n_vseq * SCAN_SEG == n and n_vseq % SCAN_GROUP == 0
    kern = functools.partial(_rglru_kernel, n_vseq=n_vseq, n_seg=n_seg, want_final=want_final)
    out_specs = [pl.BlockSpec((n, ct), lambda c: (0, c))]
    out_shape = [jax.ShapeDtypeStruct((n, d_rnn), BF16)]
    if want_final:
        out_specs.append(pl.BlockSpec((2, n_seq, ct), lambda c: (0, 0, c)))
        out_shape.append(jax.ShapeDtypeStruct((2, n_seq, d_rnn), F32))
    scratch = [pltpu.VMEM((n_vseq * SCAN_PITCH, ct), F32) for _ in range(4)]
    if n_seg > 1:
        scratch += [pltpu.VMEM((n_vseq, ct), F32) for _ in range(6)]
    return pl.pallas_call(
        kern,
        grid=(n_ct,),
        in_specs=[pl.BlockSpec((None, n, ct), lambda c: (0, 0, c)),
                  pl.BlockSpec((None, n, ct), lambda c: (1, 0, c)),
                  pl.BlockSpec((1, ct, 4 * ct), lambda c: (c, 0, 0)),
                  pl.BlockSpec((1, 1, 4 * ct), lambda c: (c, 0, 0)),
                  pl.BlockSpec((1, 1, 2 * ct), lambda c: (c, 0, 0)),
                  pl.BlockSpec((2, n_seq, ct), lambda c: (0, 0, c))],
        out_specs=out_specs,
        out_shape=out_shape,
        scratch_shapes=scratch,
        compiler_params=_cparams(1),
        name="rglru_ctx" if want_final else "rglru_lat",
    )(of32, of32, w_cat, b_cat, lam_cat, h0)


def _causal_bias(mask_ref):
    lc = mask_ref.shape[1]
    ii = lax.broadcasted_iota(jnp.int32, (lc, lc), 0)
    jj = lax.broadcasted_iota(jnp.int32, (lc, lc), 1)
    mask_ref[0] = jnp.where(jj <= ii, 0.0, -jnp.inf)
    mask_ref[1] = jnp.where(jj >= ii, 0.0, -jnp.inf)


def _with_ones(v):
    return jnp.concatenate([v, jnp.ones((v.shape[0], GATE_LANES), v.dtype)], axis=1)


def _rep(col):
    return jnp.broadcast_to(col, (col.shape[0], GATE_LANES))


def _wide(rep, width):
    return jnp.concatenate([rep] * (width // GATE_LANES), axis=1)


def _mlstm_unit(direction, head, q, k, v, v1, gc, gt_ref, r0, mask_ref, state):
    lc, hd = q.shape
    col_i = direction * ML_HEADS + head
    col_c = 2 * ML_HEADS + col_i
    lane = lax.broadcasted_iota(jnp.int32, (1, GATE_LANES), 1)
    ig_c = _rep(jnp.sum(jnp.where(lane == col_i, gc, 0.0), -1, keepdims=True))
    cum_c = _rep(jnp.sum(jnp.where(lane == col_c, gc, 0.0), -1, keepdims=True))
    ig_row = gt_ref[pl.ds(col_i, 1), pl.ds(r0, lc)]
    cum_row = gt_ref[pl.ds(col_c, 1), pl.ds(r0, lc)]
    dmat = (_wide(cum_c, lc) + (ig_row - cum_row)) + mask_ref[direction]
    if state is None:
        m_prev = 0.0
    else:
        s_prev, m_prev = state
    inter = cum_c + m_prev
    m_row = jnp.maximum(_rep(jnp.max(dmat, -1, keepdims=True)), inter)
    wts = jnp.exp2(dmat - _wide(m_row, lc))
    s = lax.dot_general(q, k, (((1,), (1,)), ((), ())), preferred_element_type=F32) * wts
    if state is None:
        num = jnp.dot(s.astype(BF16), v, preferred_element_type=F32)
        den = _rep(jnp.sum(s, -1, keepdims=True))
        h = num * _wide(1.0 / jnp.maximum(jnp.abs(den), jnp.exp2(-m_row)), hd)
    else:
        w_inter = jnp.exp2(inter - m_row)
        nd = (jnp.dot(s.astype(BF16), v1, preferred_element_type=F32)
              + _wide(w_inter, hd + GATE_LANES) * jnp.dot(q, s_prev.astype(BF16), preferred_element_type=F32))
        rden = 1.0 / jnp.maximum(jnp.abs(nd[:, hd:]), jnp.exp2(-m_row))
        h = nd[:, :hd] * _wide(rden, hd)
    btot = cum_c[lc - 1:lc] if direction == 0 else cum_c[0:1]
    wk_log = btot - cum_c + ig_c
    m_new = jnp.maximum(btot + m_prev, jnp.max(wk_log, 0, keepdims=True))
    kw = k.astype(F32) * _wide(jnp.exp2(wk_log - m_new), hd)
    t_lhs = (((0,), (0,)), ((), ()))
    if state is None:
        c_new = lax.dot_general(kw.astype(BF16), v, t_lhs, preferred_element_type=F32)
        return h, c_new, jnp.sum(kw, 0, keepdims=True), m_new
    decay = _wide(jnp.exp2(btot + m_prev - m_new), hd + GATE_LANES)
    s_new = decay * s_prev + lax.dot_general(kw.astype(BF16), v1, t_lhs, preferred_element_type=F32)
    return h, s_new, None, m_new


HEAD_OUT_ROWS = 1024


def _head_out_all(h_acc, og_ref, gn_ref, out_ref):
    gn = gn_ref[...]
    rows = h_acc.shape[0]
    step = min(rows, HEAD_OUT_ROWS)
    for r0 in range(0, rows, step):
        rs = slice(r0, r0 + step)
        out_ref[rs, :] = (og_ref[rs, :] * (_ln(h_acc[rs, :]) * gn)).astype(BF16)


def _mlstm_ctx_kernel(q_ref, k_ref, v_ref, o_ref, g_ref, gt_ref, gn_ref,
                      out_ref, cfin_ref, nfin_ref, mfin_ref, mask_ref, h_acc, *, n_sub, scale):
    head = pl.program_id(1)
    lc = MLSTM_CHUNK
    _causal_bias(mask_ref)

    def body(s, carry):
        r0 = pl.multiple_of(s * lc, lc)
        rs = pl.ds(r0, lc)
        q, k, v, gc = q_ref[rs, :] * scale, k_ref[rs, :], v_ref[rs, :], g_ref[rs, :]
        hsum = None
        for direction in (0, 1):
            h, c_new, n_new, m_new = _mlstm_unit(direction, head, q, k, v, None, gc, gt_ref, r0, mask_ref, None)
            cfin_ref[s, direction, 0] = c_new
            nfin_ref[s, direction, 0] = n_new
            mfin_ref[s, direction, 0] = m_new * LN2
            hsum = h if hsum is None else hsum + h
        h_acc[rs, :] = hsum
        return carry

    lax.fori_loop(0, n_sub, body, 0, unroll=2)
    _head_out_all(h_acc, o_ref, gn_ref, out_ref)


def _mlstm_lat_kernel(q_ref, k_ref, v_ref, o_ref, g_ref, gt_ref, gn_ref, c0_ref, n0_ref, m0_ref,
                      out_ref, s_st, m_st, h_acc, mask_ref, *, n_chunks, scale):
    head = pl.program_id(1)
    lc = MLSTM_CHUNK
    hd = q_ref.shape[1]
    _causal_bias(mask_ref)
    for direction in (0, 1):
        s_st[direction, :, 0:hd] = c0_ref[0, direction, 0]
        s_st[direction, :, hd:] = n0_ref[0, direction, 0]
        m_st[direction] = m0_ref[0, direction, 0] * LOG2E

    half = n_chunks // 2

    def make_body(accumulate):
        def body(c, carry):
            for direction in (0, 1):
                cc = c if direction == 0 else n_chunks - 1 - c
                r0 = pl.multiple_of(cc * lc, lc)
                rs = pl.ds(r0, lc)
                state = (s_st[direction], m_st[direction])
                v = v_ref[rs, :]
                h, s_new, _, m_new = _mlstm_unit(direction, head, q_ref[rs, :] * scale, k_ref[rs, :], v,
                                                 _with_ones(v), g_ref[rs, :], gt_ref, r0, mask_ref, state)
                s_st[direction] = s_new
                m_st[direction] = m_new
                if accumulate:
                    h_acc[rs, :] += h
                else:
                    h_acc[rs, :] = h
            return carry
        return body

    lax.fori_loop(0, half, make_body(False), 0, unroll=min(half, 8))
    lax.fori_loop(half, n_chunks, make_body(True), 0, unroll=min(half, 8))
    _head_out_all(h_acc, o_ref, gn_ref, out_ref)


def _mlstm_in_specs(rows, hd):
    slab = lambda idx: pl.BlockSpec((None, rows, hd), lambda s, h: (idx, s, h))
    return [slab(0), slab(1), slab(2), slab(2),
            pl.BlockSpec((rows, GATE_LANES), lambda s, h: (s, 0)),
            pl.BlockSpec((4 * ML_HEADS, rows), lambda s, h: (0, s)),
            pl.BlockSpec((1, hd), lambda s, h: (0, h))]


def _mlstm_ctx(obf, of32, g, gt, gn_g, *, n_seq):
    n = obf.shape[1]
    ml_dim = gn_g.shape[1]
    hd = ml_dim // ML_HEADS
    lc = MLSTM_CHUNK
    assert n == n_seq * lc
    n_sub = 8
    assert n_seq % n_sub == 0
    rows = n_sub * lc
    kern = functools.partial(_mlstm_ctx_kernel, n_sub=n_sub, scale=hd ** -0.5)
    return pl.pallas_call(
        kern,
        grid=(n_seq // n_sub, ML_HEADS),
        in_specs=_mlstm_in_specs(rows, hd),
        out_specs=[pl.BlockSpec((rows, hd), lambda s, h: (s, h)),
                   pl.BlockSpec((n_sub, 2, 1, hd, hd), lambda s, h: (s, 0, h, 0, 0)),
                   pl.BlockSpec((n_sub, 2, 1, 1, hd), lambda s, h: (s, 0, h, 0, 0)),
                   pl.BlockSpec((n_sub, 2, 1, 1, GATE_LANES), lambda s, h: (s, 0, h, 0, 0))],
        out_shape=[jax.ShapeDtypeStruct((n, ml_dim), BF16),
                   jax.ShapeDtypeStruct((n_seq, 2, ML_HEADS, hd, hd), F32),
                   jax.ShapeDtypeStruct((n_seq, 2, ML_HEADS, 1, hd), F32),
                   jax.ShapeDtypeStruct((n_seq, 2, ML_HEADS, 1, GATE_LANES), F32)],
        scratch_shapes=[pltpu.VMEM((2, lc, lc), F32), pltpu.VMEM((rows, hd), F32)],
        compiler_params=_cparams(2),
        name="mlstm_ctx",
    )(obf, obf, obf, of32, g, gt, gn_g)


def _mlstm_lat(obf, of32, g, gt, gn_g, c0, n0, m0, *, n_seq):
    n = obf.shape[1]
    ml_dim = gn_g.shape[1]
    hd = ml_dim // ML_HEADS
    lc = MLSTM_CHUNK
    rows = n // n_seq
    n_chunks = rows // lc
    assert n_chunks % 2 == 0
    kern = functools.partial(_mlstm_lat_kernel, n_chunks=n_chunks, scale=hd ** -0.5)
    return pl.pallas_call(
        kern,
        grid=(n_seq, ML_HEADS),
        in_specs=_mlstm_in_specs(rows, hd) + [
                  pl.BlockSpec((1, 2, 1, hd, hd), lambda s, h: (s, 0, h, 0, 0)),
                  pl.BlockSpec((1, 2, 1, hd, GATE_LANES), lambda s, h: (s, 0, h, 0, 0)),
                  pl.BlockSpec((1, 2, 1, 1, GATE_LANES), lambda s, h: (s, 0, h, 0, 0))],
        out_specs=pl.BlockSpec((rows, hd), lambda s, h: (s, h)),
        out_shape=jax.ShapeDtypeStruct((n, ml_dim), BF16),
        scratch_shapes=[pltpu.VMEM((2, hd, hd + GATE_LANES), F32),
                        pltpu.VMEM((2, 1, GATE_LANES), F32),
                        pltpu.VMEM((rows, hd), F32),
                        pltpu.VMEM((2, lc, lc), F32)],
        compiler_params=_cparams(2),
        name="mlstm_lat",
    )(obf, obf, obf, of32, g, gt, gn_g, c0, n0, m0)


def _merge_kernel(hrg_ref, hml_ref, gma_ref, gmb_ref, x_ref, mod_ref, wrg_ref, wml_ref, wout_ref,
                  bm_ref, lng_ref, lnb_ref, o_ref, *, d, alpha, row_base, tiles_per_row):
    r = _mod_row(pl.program_id(0), row_base, tiles_per_row)
    gate1 = mod_ref[pl.ds(r, 1), 2 * d:3 * d]
    rows_per_group = x_ref.shape[0] // MERGE_ROW_GROUPS
    for grp in range(MERGE_ROW_GROUPS):
        rows = slice(grp * rows_per_group, (grp + 1) * rows_per_group)
        y_rg = jnp.dot(hrg_ref[rows, :], wrg_ref[...], preferred_element_type=F32)
        y_ml = jnp.dot(hml_ref[rows, :], wml_ref[...], preferred_element_type=F32)
        g_rg = jax.nn.sigmoid(gma_ref[rows, :] + bm_ref[:, 0:d])
        g_ml = jax.nn.sigmoid(gmb_ref[rows, :] + bm_ref[:, d:2 * d])
        merged = (g_rg * y_rg + g_ml * y_ml).astype(BF16)
        mix = jnp.dot(merged, wout_ref[...], preferred_element_type=F32)
        o_ref[rows, :] = _ln(alpha * x_ref[rows, :] + gate1 * mix) * lng_ref[...] + lnb_ref[...]


def _merge(hrg, hml, of32, x, mod, w_rg, w_ml, w_out, b_merge, ln_g, ln_b, *, alpha, row_base,
           rows_per_mod, tm):
    n, d = x.shape
    tiles_per_row = _tiles_per_row(rows_per_mod, tm)
    kern = functools.partial(_merge_kernel, d=d, alpha=alpha, row_base=row_base, tiles_per_row=tiles_per_row)
    full = lambda shape: pl.BlockSpec(shape, lambda i: (0,) * len(shape))
    resident = lambda shape: pl.BlockSpec(shape, lambda i: (0, 0), pipeline_mode=pl.Buffered(1))
    return pl.pallas_call(
        kern,
        grid=(n // tm,),
        in_specs=[pl.BlockSpec((tm, d), lambda i: (i, 0)),
                  pl.BlockSpec((tm, d), lambda i: (i, 0)),
                  pl.BlockSpec((None, tm, d), lambda i: (3, i, 0)),
                  pl.BlockSpec((None, tm, d), lambda i: (4, i, 0)),
                  pl.BlockSpec((tm, d), lambda i: (i, 0)),
                  full(mod.shape), resident((d, d)), resident((d, d)), resident((d, d)),
                  full((1, 2 * d)), full((1, d)), full((1, d))],
        out_specs=pl.BlockSpec((tm, d), lambda i: (i, 0)),
        out_shape=jax.ShapeDtypeStruct((n, d), F32),
        compiler_params=_cparams(1),
        name="merge",
    )(hrg, hml, of32, of32, x, mod, w_rg, w_ml, w_out, b_merge, ln_g, ln_b)


def _mlp_kernel(x_ref, mod_ref, wfc_ref, bfc_ref, wpj_ref, bpj_ref, lng_ref, lnb_ref, o_ref,
                u_ref, *, d, alpha, row_base, tiles_per_row):
    r = _mod_row(pl.program_id(0), row_base, tiles_per_row)
    shift = mod_ref[pl.ds(r, 1), 3 * d:4 * d]
    scale = mod_ref[pl.ds(r, 1), 4 * d:5 * d]
    gate2 = mod_ref[pl.ds(r, 1), 5 * d:6 * d]
    rows_per_group = x_ref.shape[0] // MLP_ROW_GROUPS
    for grp in range(MLP_ROW_GROUPS):
        rows = slice(grp * rows_per_group, (grp + 1) * rows_per_group)
        u_ref[rows, :] = (_ln(x_ref[rows, :]) * (1.0 + scale) + shift).astype(BF16)
        hid = jnp.dot(u_ref[rows, :], wfc_ref[...], preferred_element_type=F32) + bfc_ref[...]
        hid = jnp.square(jnp.maximum(hid, 0.0)).astype(BF16)
        acc = jnp.dot(hid, wpj_ref[...], preferred_element_type=F32)
        y = alpha * x_ref[rows, :] + gate2 * (acc + bpj_ref[...])
        o_ref[rows, :] = _ln(y) * lng_ref[...] + lnb_ref[...]


def _mlp(x, mod, w_fc, b_fc, w_proj, b_proj, ln_g, ln_b, *, alpha, row_base, rows_per_mod, tm):
    n, d = x.shape
    tiles_per_row = _tiles_per_row(rows_per_mod, tm)
    d_ff = w_fc.shape[1]
    kern = functools.partial(_mlp_kernel, d=d, alpha=alpha, row_base=row_base, tiles_per_row=tiles_per_row)
    resident = lambda shape: pl.BlockSpec(shape, lambda i: (0, 0), pipeline_mode=pl.Buffered(1))
    return pl.pallas_call(
        kern,
        grid=(n // tm,),
        in_specs=[pl.BlockSpec((tm, d), lambda i: (i, 0)),
                  pl.BlockSpec(mod.shape, lambda i: (0, 0)),
                  resident((d, d_ff)),
                  pl.BlockSpec((1, d_ff), lambda i: (0, 0)),
                  resident((d_ff, d)),
                  pl.BlockSpec((1, d), lambda i: (0, 0)),
                  pl.BlockSpec((1, d), lambda i: (0, 0)),
                  pl.BlockSpec((1, d), lambda i: (0, 0))],
        out_specs=pl.BlockSpec((tm, d), lambda i: (i, 0)),
        out_shape=jax.ShapeDtypeStruct((n, d), F32),
        scratch_shapes=[pltpu.VMEM((tm, d), BF16)],
        compiler_params=_cparams(1),
        name="mlp",
    )(x, mod, w_fc, b_fc, w_proj, b_proj, ln_g, ln_b)


def _rg_gate_weights(wa, ba, wx, bx, lam):
    ct = RG_CT
    d_rnn = ba.shape[-1]
    n_ct = d_rnn // ct
    per = ct // RG_BW

    def tile_blockdiag(w):
        wt = w.reshape(n_ct, per, RG_BW, RG_BW)
        eye = jnp.eye(per, dtype=w.dtype)
        return jnp.einsum('cpkj,pq->cpkqj', wt, eye).reshape(n_ct, ct, ct)

    w_cat = (0.5 * jnp.concatenate([tile_blockdiag(wa[0]), tile_blockdiag(wx[0]),
                                    tile_blockdiag(wa[1]), tile_blockdiag(wx[1])], axis=-1)).astype(BF16)
    tiles = lambda b: b.reshape(n_ct, 1, ct)
    b_cat = 0.5 * jnp.concatenate([tiles(ba[0]), tiles(bx[0]), tiles(ba[1]), tiles(bx[1])], axis=-1)
    lam_cat = jnp.concatenate([tiles(lam[0]), tiles(lam[1])], axis=-1)
    return w_cat, b_cat, lam_cat


def kernel(x_prompt, x_sample, state_rglru_h, state_mlstm_C, state_mlstm_n, state_mlstm_m, c, c_ctx, w_ada, b_ada, w_in, rg_conv_w, rg_conv_b, rg_wa, rg_ba, rg_wx, rg_bx, rg_lambda, w_rg_proj, ml_b_igate, ml_b_fgate, ml_gn_g, w_ml_proj, b_merge, w_out, ln_g, ln_b, w_fc, b_fc, w_proj, b_proj):
    bp, seq, d = x_prompt.shape
    bl, dec_seq, _ = x_sample.shape
    depth = w_in.shape[0]
    d_rnn = rg_conv_w.shape[-1]
    ml_dim = ml_gn_g.shape[-1]
    hd = ml_dim // ML_HEADS
    alpha = (2 * depth) ** 0.25
    d_main = 2 * d_rnn + 4 * ml_dim + 2 * d
    assert seq == MLSTM_CHUNK == SCAN_SEG and dec_seq % MLSTM_CHUNK == 0
    assert d_rnn == d and ml_dim == d and w_in.shape[-1] == d_main + 4 * ML_HEADS

    tm = 1024
    tm_proj = 512
    xp = x_prompt.reshape(bp * seq, d)
    xs = x_sample.reshape(bl * dec_seq, d)
    cond = jnp.concatenate([c_ctx[None, :], c], axis=0)
    zero_h = jnp.zeros((2, bp, d_rnn), F32)
    new_h, new_c, new_n, new_m = [], [], [], []
    for l in range(depth):
        w_main = w_in[l].astype(BF16)
        mod = _ada(cond, w_ada[l], b_ada[l], w_main)
        w_gate = jnp.zeros((d, GATE_LANES), BF16).at[:, :4 * ML_HEADS].set(w_in[l][:, d_main:].astype(BF16))
        b_gate = jnp.concatenate([ml_b_igate[l].reshape(-1), ml_b_fgate[l].reshape(-1)]).reshape(-1, 1)
        w_cat, b_cat, lam_cat = _rg_gate_weights(rg_wa[l], rg_ba[l], rg_wx[l], rg_bx[l], rg_lambda[l])
        conv_b = rg_conv_b[l].reshape(1, d_rnn)
        gn_g = ml_gn_g[l].reshape(1, ml_dim)
        w_rg, w_ml, w_o = (w_rg_proj[l].astype(BF16), w_ml_proj[l].astype(BF16), w_out[l].astype(BF16))
        w_fc_b, w_pj_b = w_fc[l].astype(BF16), w_proj[l].astype(BF16)
        bm = b_merge[l].reshape(1, 2 * d)
        lng0, lnb0 = ln_g[l, 0].reshape(1, d), ln_b[l, 0].reshape(1, d)
        lng1, lnb1 = ln_g[l, 1].reshape(1, d), ln_b[l, 1].reshape(1, d)
        bfc, bpj = b_fc[l].reshape(1, -1), b_proj[l].reshape(1, d)

        def tail(x, hrg, hml, of32, row_base, rows_per_mod):
            x1 = _merge(hrg, hml, of32, x, mod, w_rg, w_ml, w_o, bm, lng0, lnb0, alpha=alpha,
                        row_base=row_base, rows_per_mod=rows_per_mod, tm=tm)
            return _mlp(x1, mod, w_fc_b, bfc, w_pj_b, bpj, lng1, lnb1, alpha=alpha,
                        row_base=row_base, rows_per_mod=rows_per_mod, tm=tm)

        of32, obf, g, gt = _in_proj(xp, mod, w_main, w_gate, b_gate, rg_conv_w[l], conv_b, conv_len=seq,
                                    row_base=0, rows_per_mod=None, tm=tm_proj)
        hrg, h_fin = _rglru(of32, w_cat, b_cat, lam_cat, zero_h, n_seq=bp, seq_len=seq, want_final=True)
        hml, c_fin, n_fin, m_fin = _mlstm_ctx(obf, of32, g, gt, gn_g, n_seq=bp)
        xp = tail(xp, hrg, hml, of32, 0, None)
        new_h.append(jnp.transpose(h_fin, (1, 0, 2)))
        new_c.append(c_fin)
        new_n.append(n_fin.reshape(bp, 2, ML_HEADS, hd))
        new_m.append(m_fin[..., 0, 0])

        xp, b_gate_lat = lax.optimization_barrier((xp, b_gate))
        of32, obf, g, gt = _in_proj(xs, mod, w_main, w_gate, b_gate_lat, rg_conv_w[l], conv_b, conv_len=GRID_W,
                                    row_base=1, rows_per_mod=dec_seq, tm=tm_proj)
        h0 = jnp.transpose(state_rglru_h[:, l], (1, 0, 2))
        (hrg,) = _rglru(of32, w_cat, b_cat, lam_cat, h0, n_seq=bl, seq_len=dec_seq, want_final=False)
        c0 = state_mlstm_C[:, l]
        n0 = jnp.broadcast_to(state_mlstm_n[:, l][..., None], (bl, 2, ML_HEADS, hd, GATE_LANES))
        m0 = jnp.broadcast_to(state_mlstm_m[:, l].reshape(bl, 2, ML_HEADS, 1, 1),
                              (bl, 2, ML_HEADS, 1, GATE_LANES))
        hml = _mlstm_lat(obf, of32, g, gt, gn_g, c0, n0, m0, n_seq=bl)
        xs = tail(xs, hrg, hml, of32, 1, dec_seq)

    def stack(parts):
        return parts[0][:, None] if len(parts) == 1 else jnp.stack(parts, axis=1)

    return (xp.reshape(bp, seq, d), xs.reshape(bl, dec_seq, d),
            stack(new_h), stack(new_c), stack(new_n), stack(new_m))
```
